```python
import math
import jax, jax.numpy as jnp
from jax import lax
import numpy as np

D_MODEL = 1024
BATCH = 4
SEQ = 4096
DEPTH = 1
DEC_BATCH = 128
DEC_SEQ = 4
PAST_LEN = 8192
PAGE_SIZE = 128

D_MIX = D_MODEL
N_HEADS = 8
NOPE_DIM = 64
ROPE_DIM = 32
V_DIM = 64
Q_LORA = 384
KV_LORA = 256
ATTN_WIDTH = N_HEADS * V_DIM
CONV_DIM = D_MIX - ATTN_WIDTH
CONV_GROUPS = 8
CONV_GROUP_DIM = CONV_DIM // CONV_GROUPS
CONV_W = 3
ROPE_THETA = 10000.0
Q_BLOCK = 128
IN_SIZES = (Q_LORA, KV_LORA, ROPE_DIM, CONV_DIM, CONV_DIM, CONV_DIM)
IN_COLS = sum(IN_SIZES)
IN_SPLITS = tuple(int(v) for v in np.cumsum(IN_SIZES)[:-1])
N_EXPERTS = 256
TOP_K = 8
N_GROUPS = 8
TOPK_GROUPS = 4
E_FF = 256
SHARED_FF = 256
ROUTED_SCALE = 2.5
MOE_BLOCK = 128
EPS = 1e-6

kernel_name = "hymba_mla_shortconv_moe_adaln_step"


def rmsnorm(x, g):
    xf = x.astype(jnp.float32)
    r = lax.rsqrt(jnp.mean(xf * xf, axis=-1, keepdims=True) + EPS)
    return (xf * r).astype(x.dtype) * g


def rope(x, pos):
    half = ROPE_DIM // 2
    inv = 1.0 / (ROPE_THETA ** (jnp.arange(half, dtype=jnp.float32) / half))
    ang = pos.astype(jnp.float32)[:, None] * inv[None, :]
    ang = ang.reshape((ang.shape[0],) + (1,) * (x.ndim - 3) + (half,))
    cos, sin = jnp.cos(ang), jnp.sin(ang)
    xf = x.astype(jnp.float32)
    x1, x2 = xf[..., :half], xf[..., half:]
    return jnp.concatenate([x1 * cos - x2 * sin, x1 * sin + x2 * cos], axis=-1).astype(x.dtype)


def swiglu(h, wg, wu, wd):
    return (jax.nn.silu(h @ wg) * (h @ wu)) @ wd


def mla_attend(q_lat, q_pe, ckv, kpe, q_pos, k_pos):
    B, T = q_lat.shape[0], q_lat.shape[1]
    blk = Q_BLOCK if T % Q_BLOCK == 0 else T
    nb = T // blk
    scale = (NOPE_DIM + ROPE_DIM) ** -0.5

    def to_blocks(a):
        return jnp.moveaxis(a.reshape((B, nb, blk) + a.shape[2:]), 1, 0)

    def one_block(args):
        ql, qp, qpos = args
        s = (jnp.einsum('bqhr,bkr->bhqk', ql, ckv)
             + jnp.einsum('bqhp,bkp->bhqk', qp, kpe)).astype(jnp.float32) * scale
        s = jnp.where(k_pos[None, :] <= qpos[:, None], s, -jnp.inf)
        p = jax.nn.softmax(s, axis=-1).astype(ckv.dtype)
        return jnp.einsum('bhqk,bkr->bqhr', p, ckv)

    o = lax.map(one_block, (to_blocks(q_lat), to_blocks(q_pe), q_pos.reshape(nb, blk)))
    return jnp.moveaxis(o, 0, 1).reshape(q_lat.shape)


def short_conv(u, prev, w_conv):
    T = u.shape[1]
    full = jnp.concatenate([prev.astype(u.dtype), u], axis=1)
    y = full[:, 0:T] * w_conv[0]
    for j in range(1, CONV_W):
        y = y + full[:, j:j + T] * w_conv[j]
    return y, full[:, -(CONV_W - 1):]


def route(h, w_router, e_bias):
    n = h.shape[0]
    s = jax.nn.sigmoid((h @ w_router).astype(jnp.float32))
    biased = s + e_bias.astype(jnp.float32)
    gscore = lax.top_k(biased.reshape(n, N_GROUPS, N_EXPERTS // N_GROUPS), 2)[0].sum(-1)
    _, gidx = lax.top_k(gscore, TOPK_GROUPS)
    gmask = jnp.sum(jax.nn.one_hot(gidx, N_GROUPS, dtype=jnp.int32), axis=1) > 0
    emask = jnp.repeat(gmask, N_EXPERTS // N_GROUPS, axis=1)
    _, idx = lax.top_k(jnp.where(emask, biased, -jnp.inf), TOP_K)
    w = jnp.take_along_axis(s, idx, axis=-1)
    w = w / jnp.sum(w, axis=-1, keepdims=True) * ROUTED_SCALE
    return idx, w.astype(h.dtype)


def moe_routed(h, idx, w, w_e_gate, w_e_up, w_e_down):
    n, d = h.shape
    m = n * TOP_K
    flat_e = idx.reshape(m).astype(jnp.int32)
    flat_tok = jnp.repeat(jnp.arange(n, dtype=jnp.int32), TOP_K)
    flat_w = w.reshape(m)
    order = jnp.argsort(flat_e)
    e_sorted = flat_e[order]
    tok_sorted = flat_tok[order]
    counts = jnp.bincount(flat_e, length=N_EXPERTS).astype(jnp.int32)
    padded = (counts + MOE_BLOCK - 1) // MOE_BLOCK * MOE_BLOCK
    start = jnp.cumsum(counts) - counts
    pend = jnp.cumsum(padded)
    pstart = pend - padded
    dest = pstart[e_sorted] + jnp.arange(m, dtype=jnp.int32) - start[e_sorted]
    n_blocks = -(-m // MOE_BLOCK) + N_EXPERTS
    tok_buf = jnp.full((n_blocks * MOE_BLOCK,), n, jnp.int32).at[dest].set(tok_sorted)
    block_start = jnp.arange(n_blocks, dtype=jnp.int32) * MOE_BLOCK
    block_e = jnp.minimum(jnp.searchsorted(pend, block_start, side='right'), N_EXPERTS - 1)
    h_pad = jnp.concatenate([h, jnp.zeros((1, d), h.dtype)], axis=0)

    def expert_block(args):
        toks, e = args
        return swiglu(h_pad[toks], w_e_gate[e], w_e_up[e], w_e_down[e])

    y_buf = lax.map(expert_block, (tok_buf.reshape(n_blocks, MOE_BLOCK), block_e)).reshape(-1, d)
    y_assign = y_buf[dest] * flat_w[order][:, None]
    return jax.ops.segment_sum(y_assign, tok_sorted, num_segments=n)


def decoder_layer(x, c, pos, conv_prev, attend, w_ada, b_ada, g_attn_norm, w_in, g_q,
                  w_q_up, g_kv, w_uk, w_uv, w_conv, g_attn_out, g_conv_out, w_out,
                  g_ffn_norm, w_router, e_bias, w_e_gate, w_e_up, w_e_down,
                  w_s_gate, w_s_up, w_s_down):
    B, T, D = x.shape
    mod = (jax.nn.silu(c) @ w_ada + b_ada)[:, None, :]
    sh1, sc1, gt1, sh2, sc2, gt2 = jnp.split(mod, 6, axis=-1)
    h = rmsnorm(x, g_attn_norm) * (1 + sc1) + sh1
    q_a, kv_a, k_pe, b_g, c_g, u_in = jnp.split(h @ w_in, IN_SPLITS, axis=-1)
    q = (rmsnorm(q_a, g_q) @ w_q_up).reshape(B, T, N_HEADS, NOPE_DIM + ROPE_DIM)
    q_nope = q[..., :NOPE_DIM]
    q_pe = rope(q[..., NOPE_DIM:], pos)
    ckv = rmsnorm(kv_a, g_kv)
    kpe = rope(k_pe, pos)
    q_lat = jnp.einsum('bthn,rhn->bthr', q_nope, w_uk)
    o_lat = attend(q_lat, q_pe, ckv, kpe)
    o = jnp.einsum('bthr,rhv->bthv', o_lat, w_uv)
    o = rmsnorm(o, g_attn_out.reshape(N_HEADS, V_DIM)).reshape(B, T, ATTN_WIDTH)
    z, conv_new = short_conv(c_g * u_in, conv_prev, w_conv)
    z = (b_g * z).reshape(B, T, CONV_GROUPS, CONV_GROUP_DIM)
    z = rmsnorm(z, g_conv_out.reshape(CONV_GROUPS, CONV_GROUP_DIM)).reshape(B, T, CONV_DIM)
    x = x + gt1 * (jnp.concatenate([o, z], axis=-1) @ w_out)
    h = rmsnorm(x, g_ffn_norm) * (1 + sc2) + sh2
    hf = h.reshape(B * T, D)
    idx, gw = route(hf, w_router, e_bias)
    f = moe_routed(hf, idx, gw, w_e_gate, w_e_up, w_e_down) + swiglu(hf, w_s_gate, w_s_up, w_s_down)
    x = x + gt2 * f.reshape(B, T, D)
    return x, ckv, kpe, conv_new


def setup_inputs(seed: int = 0) -> dict:
    key = jax.random.key(seed)
    kit = iter(list(jax.random.split(key, 48)))

    def nrm(shape, scale):
        return jax.random.normal(next(kit), shape, jnp.float32) * scale

    def gain(shape):
        return 1.0 + nrm(shape, 0.02)

    n_pages = PAST_LEN // PAGE_SIZE
    n_used = DEC_BATCH * n_pages
    n_pool = n_used + n_used // 4 + 1
    page_table = jax.random.permutation(next(kit), n_pool)[:n_used].reshape(DEC_BATCH, n_pages).astype(jnp.int32)
    L, D = DEPTH, D_MODEL
    return {
        "x_prompt": nrm((BATCH, SEQ, D), 1.0),
        "x_sample": nrm((DEC_BATCH, DEC_SEQ, D), 1.0),
        "c_prompt": nrm((BATCH, D), 1.0),
        "c_sample": nrm((DEC_BATCH, D), 1.0),
        "cache_ckv": nrm((L, n_pool, PAGE_SIZE, KV_LORA), 1.0),
        "cache_kpe": nrm((L, n_pool, PAGE_SIZE, ROPE_DIM), 1.0),
        "state_conv": nrm((L, DEC_BATCH, CONV_W - 1, CONV_DIM), 1.0),
        "page_table": page_table,
        "w_ada": nrm((L, D, 6 * D), 0.5 * D ** -0.5),
        "b_ada": nrm((L, 6 * D), 0.02),
        "g_attn_norm": gain((L, D)),
        "w_in": nrm((L, D, IN_COLS), D ** -0.5),
        "g_q": gain((L, Q_LORA)),
        "w_q_up": nrm((L, Q_LORA, N_HEADS * (NOPE_DIM + ROPE_DIM)), Q_LORA ** -0.5),
        "g_kv": gain((L, KV_LORA)),
        "w_uk": nrm((L, KV_LORA, N_HEADS, NOPE_DIM), KV_LORA ** -0.5),
        "w_uv": nrm((L, KV_LORA, N_HEADS, V_DIM), KV_LORA ** -0.5),
        "w_conv": nrm((L, CONV_W, CONV_DIM), CONV_W ** -0.5),
        "g_attn_out": gain((L, ATTN_WIDTH)),
        "g_conv_out": gain((L, CONV_DIM)),
        "w_out": nrm((L, D_MIX, D), D_MIX ** -0.5),
        "g_ffn_norm": gain((L, D)),
        "w_router": nrm((L, D, N_EXPERTS), D ** -0.5),
        "e_bias": nrm((L, N_EXPERTS), 0.01),
        "w_e_gate": nrm((L, N_EXPERTS, D, E_FF), D ** -0.5),
        "w_e_up": nrm((L, N_EXPERTS, D, E_FF), D ** -0.5),
        "w_e_down": nrm((L, N_EXPERTS, E_FF, D), E_FF ** -0.5),
        "w_s_gate": nrm((L, D, SHARED_FF), D ** -0.5),
        "w_s_up": nrm((L, D, SHARED_FF), D ** -0.5),
        "w_s_down": nrm((L, SHARED_FF, D), SHARED_FF ** -0.5),
        "w_ada_final": nrm((D, 2 * D), 0.5 * D ** -0.5),
        "b_ada_final": nrm((2 * D,), 0.02),
        "g_final": gain((D,)),
    }


def reference(x_prompt, x_sample, c_prompt, c_sample, cache_ckv, cache_kpe, state_conv,
              page_table, w_ada, b_ada, g_attn_norm, w_in, g_q, w_q_up, g_kv, w_uk, w_uv,
              w_conv, g_attn_out, g_conv_out, w_out, g_ffn_norm, w_router, e_bias,
              w_e_gate, w_e_up, w_e_down, w_s_gate, w_s_up, w_s_down,
              w_ada_final, b_ada_final, g_final):
    S = x_prompt.shape[1]
    T = x_sample.shape[1]
    past = page_table.shape[1] * PAGE_SIZE
    pos_p = jnp.arange(S, dtype=jnp.int32)
    pos_s = past + jnp.arange(T, dtype=jnp.int32)
    k_pos_s = jnp.arange(past + T, dtype=jnp.int32)
    xp, xs = x_prompt, x_sample
    ckv_p_l, kpe_p_l, conv_p_l, ckv_s_l, kpe_s_l, conv_s_l = [], [], [], [], [], []
    for l in range(DEPTH):
        def attend_prompt(q_lat, q_pe, ckv, kpe):
            return mla_attend(q_lat, q_pe, ckv, kpe, pos_p, pos_p)

        def attend_sample(q_lat, q_pe, ckv, kpe, l=l):
            def one_seq(args):
                pt, ql, qp, ck, kp = args
                ck_all = jnp.concatenate([cache_ckv[l, pt].reshape(-1, KV_LORA).astype(ck.dtype), ck], axis=0)
                kp_all = jnp.concatenate([cache_kpe[l, pt].reshape(-1, ROPE_DIM).astype(kp.dtype), kp], axis=0)
                return mla_attend(ql[None], qp[None], ck_all[None], kp_all[None], pos_s, k_pos_s)[0]
            return lax.map(one_seq, (page_table, q_lat, q_pe, ckv, kpe))

        weights = (w_ada[l], b_ada[l], g_attn_norm[l], w_in[l], g_q[l], w_q_up[l], g_kv[l],
                   w_uk[l], w_uv[l], w_conv[l], g_attn_out[l], g_conv_out[l], w_out[l],
                   g_ffn_norm[l], w_router[l], e_bias[l], w_e_gate[l], w_e_up[l], w_e_down[l],
                   w_s_gate[l], w_s_up[l], w_s_down[l])
        conv0 = jnp.zeros((xp.shape[0], CONV_W - 1, CONV_DIM), xp.dtype)
        xp, ckv_p, kpe_p, conv_p = decoder_layer(xp, c_prompt, pos_p, conv0, attend_prompt, *weights)
        xs, ckv_s, kpe_s, conv_s = decoder_layer(xs, c_sample, pos_s, state_conv[l], attend_sample, *weights)
        ckv_p_l.append(ckv_p); kpe_p_l.append(kpe_p); conv_p_l.append(conv_p)
        ckv_s_l.append(ckv_s); kpe_s_l.append(kpe_s); conv_s_l.append(conv_s)

    def final_norm(x, c):
        shift, scale = jnp.split(jax.nn.silu(c) @ w_ada_final + b_ada_final, 2, axis=-1)
        return rmsnorm(x, g_final) * (1 + scale[:, None, :]) + shift[:, None, :]

    y_prompt = final_norm(xp, c_prompt)
    y_sample = final_norm(xs, c_sample)
    return (y_prompt, y_sample,
            jnp.stack(ckv_p_l), jnp.stack(kpe_p_l), jnp.stack(conv_p_l),
            jnp.stack(ckv_s_l), jnp.stack(kpe_s_l), jnp.stack(conv_s_l))
```

```python
import functools
import math

import jax
import jax.numpy as jnp
from jax import lax
from jax.experimental import pallas as pl
from jax.experimental.pallas import tpu as pltpu

F32 = jnp.float32
BF16 = jnp.bfloat16

D_MODEL = 1024
N_HEADS = 8
NOPE_DIM = 64
ROPE_DIM = 32
V_DIM = 64
Q_LORA = 384
KV_LORA = 256
ATTN_WIDTH = N_HEADS * V_DIM
CONV_DIM = 512
CONV_GROUPS = 8
CONV_GROUP_DIM = CONV_DIM // CONV_GROUPS
CONV_W = 3
ROPE_THETA = 10000.0
PAGE_SIZE = 128
N_EXPERTS = 256
TOP_K = 8
N_GROUPS = 8
GROUP_SIZE = N_EXPERTS // N_GROUPS
TOPK_GROUPS = 4
E_FF = 256
SHARED_FF = 256
ROUTED_SCALE = 2.5
EPS = 1e-6

LANES = 128
HEAD_PAD = LANES
QK_WIDTH = N_HEADS * HEAD_PAD
PE_LO = NOPE_DIM
PE_HALF = ROPE_DIM // 2
IN_PAD_COLS = Q_LORA + KV_LORA + HEAD_PAD + 3 * CONV_DIM
VMEM_LIMIT = 48 * 1024 * 1024

TILE_PROMPT = 256
TILE_SAMPLE = 128
TQ = 512
ROUTE_CHUNK = 256
COMBINE_TILE = 128


def _cparams(sem, vmem=VMEM_LIMIT):
    return pltpu.CompilerParams(dimension_semantics=sem, vmem_limit_bytes=vmem)


def _dot(a, b):
    return jnp.dot(a, b, preferred_element_type=F32)


def _dot_nt(a, b):
    return lax.dot_general(a, b, (((1,), (1,)), ((), ())), preferred_element_type=F32)


def _rms(x, g):
    r = lax.rsqrt(jnp.mean(x * x, axis=-1, keepdims=True) + EPS)
    return (x * r) * g


def _ada_body(c_ref, w_ref, b_ref, o_ref):
    c = c_ref[...]
    a = (c * jax.nn.sigmoid(c)).astype(BF16)
    o_ref[...] = _dot(a, w_ref[...].astype(BF16)) + b_ref[...]


def _ada(c_all, w, b):
    m, d = c_all.shape
    n = w.shape[1]
    tn = 1024
    return pl.pallas_call(
        _ada_body,
        grid=(n // tn,),
        in_specs=[pl.BlockSpec((m, d), lambda j: (0, 0)),
                  pl.BlockSpec((d, tn), lambda j: (0, j)),
                  pl.BlockSpec((1, tn), lambda j: (0, j))],
        out_specs=pl.BlockSpec((m, tn), lambda j: (0, j)),
        out_shape=jax.ShapeDtypeStruct((m, n), F32),
        compiler_params=_cparams(("arbitrary",)),
        name="ada",
    )(c_all, w, b.reshape(1, n))


def _rope_lanes(x, cos, sin_up, sin_dn):
    w = x.shape[1]
    up = pltpu.roll(x, PE_HALF, 1)
    dn = pltpu.roll(x, w - PE_HALF, 1)
    return x * cos + up * sin_up + dn * sin_dn


def _mix_in_body(*refs, prompt, tm, tiles_per_seq, t_new):
    if prompt:
        (x_ref, sc_ref, sh_ref, g1_ref, win_ref, gq_ref, wq_ref, gkv_ref, wuk_ref, wuv_ref,
         cos_ref, sup_ref, sdn_ref, wconv_ref, gconv_ref, gmat_ref,
         q_ref, k_ref, v_ref, ckv_ref, kpe_ref, z_ref, cst_ref, carry_ref) = refs
    else:
        (x_ref, sc_ref, sh_ref, g1_ref, win_ref, gq_ref, wq_ref, gkv_ref,
         cos_ref, sup_ref, sdn_ref, wconv_ref, gconv_ref, gmat_ref, pa_ref, pb_ref,
         q_ref, ckv_ref, kpe_ref, z_ref, u_ref) = refs

    x = x_ref[...]
    h = _rms(x, g1_ref[...]) * (1.0 + sc_ref[...]) + sh_ref[...]
    proj = _dot(h.astype(BF16), win_ref[...])
    o0 = Q_LORA
    o1 = o0 + KV_LORA
    o2 = o1 + HEAD_PAD
    o3 = o2 + CONV_DIM
    o4 = o3 + CONV_DIM
    q_a, kv_a, kpe_blk = proj[:, :o0], proj[:, o0:o1], proj[:, o1:o2]
    b_g, c_g, u_in = proj[:, o2:o3], proj[:, o3:o4], proj[:, o4:]

    cos, sup, sdn = cos_ref[...], sup_ref[...], sdn_ref[...]
    cos8 = jnp.concatenate([cos] * N_HEADS, axis=1)
    sup8 = jnp.concatenate([sup] * N_HEADS, axis=1)
    sdn8 = jnp.concatenate([sdn] * N_HEADS, axis=1)

    qn = _rms(q_a, gq_ref[...]).astype(BF16)
    q = _dot(qn, wq_ref[...]) * ((NOPE_DIM + ROPE_DIM) ** -0.5)
    q_ref[...] = _rope_lanes(q, cos8, sup8, sdn8).astype(BF16)

    ckv = _rms(kv_a, gkv_ref[...])
    ckv_ref[...] = ckv
    kpe = _rope_lanes(kpe_blk, cos, sup, sdn)
    kpe_ref[...] = kpe[:, PE_LO:PE_LO + ROPE_DIM]

    if prompt:
        ckvb = ckv.astype(BF16)
        k = _dot(ckvb, wuk_ref[...]) + jnp.concatenate([kpe] * N_HEADS, axis=1)
        k_ref[...] = k.astype(BF16)
        v_ref[...] = _dot(ckvb, wuv_ref[...]).astype(BF16)

    u = c_g * u_in
    row = lax.broadcasted_iota(jnp.int32, (tm, 1), 0)
    r1 = pltpu.roll(u, 1, 0)
    r2 = pltpu.roll(u, 2, 0)
    if prompt:
        @pl.when(pl.program_id(0) % tiles_per_seq == 0)
        def _():
            carry_ref[...] = jnp.zeros_like(carry_ref)
        c6 = carry_ref[6:7, :]
        c7 = carry_ref[7:8, :]
        um1 = jnp.where(row == 0, c7, r1)
        um2 = jnp.where(row == 0, c6, jnp.where(row == 1, c7, r2))
        carry_ref[...] = u[tm - 8:, :]
        cst_ref[...] = u[tm - (CONV_W - 1):, :]
    else:
        t = row % t_new
        um1 = jnp.where(t == 0, pa_ref[...], r1)
        um2 = jnp.where(t == 0, pb_ref[...], jnp.where(t == 1, pa_ref[...], r2))
        u_ref[...] = u
    wc = wconv_ref[...]
    y = um2 * wc[0:1, :] + um1 * wc[1:2, :] + u * wc[2:3, :]
    zz = b_g * y
    ms = _dot((zz * zz).astype(BF16), gmat_ref[...]) * (1.0 / CONV_GROUP_DIM)
    z_ref[...] = (zz * lax.rsqrt(ms + EPS) * gconv_ref[...]).astype(BF16)


def _mix_in(x2d, sc, sh, p, rope_tabs, *, prompt, seq_len=None, prev_a=None, prev_b=None, t_new=1):
    n = x2d.shape[0]
    tm = TILE_PROMPT if prompt else TILE_SAMPLE
    nt = n // tm
    cos_t, sup_t, sdn_t = rope_tabs
    full = lambda a: pl.BlockSpec(a.shape, lambda i: (0,) * a.ndim)
    row_tile = lambda w: pl.BlockSpec((tm, w), lambda i: (i, 0))
    if prompt:
        tps = seq_len // tm
        mod_spec = pl.BlockSpec((None, 1, D_MODEL), lambda i: (i // tps, 0, 0))
        tab_spec = pl.BlockSpec((tm, LANES), lambda i: (i % tps, 0))
        nb = n // seq_len
        ins = [x2d, sc, sh, p["g1"], p["w_in"], p["g_q"], p["wq"], p["g_kv"], p["wuk"], p["wuv"],
               cos_t, sup_t, sdn_t, p["w_conv"], p["g_conv"], p["gmat"]]
        in_specs = [row_tile(D_MODEL), mod_spec, mod_spec, full(p["g1"]), full(p["w_in"]), full(p["g_q"]),
                    full(p["wq"]), full(p["g_kv"]), full(p["wuk"]), full(p["wuv"]),
                    tab_spec, tab_spec, tab_spec, full(p["w_conv"]), full(p["g_conv"]), full(p["gmat"])]
        out_shape = [jax.ShapeDtypeStruct((n, QK_WIDTH), BF16), jax.ShapeDtypeStruct((n, QK_WIDTH), BF16),
                     jax.ShapeDtypeStruct((n, ATTN_WIDTH), BF16), jax.ShapeDtypeStruct((n, KV_LORA), F32),
                     jax.ShapeDtypeStruct((n, ROPE_DIM), F32), jax.ShapeDtypeStruct((n, CONV_DIM), BF16),
                     jax.ShapeDtypeStruct((nb, CONV_W - 1, CONV_DIM), F32)]
        out_specs = [row_tile(QK_WIDTH), row_tile(QK_WIDTH), row_tile(ATTN_WIDTH), row_tile(KV_LORA),
                     row_tile(ROPE_DIM), row_tile(CONV_DIM),
                     pl.BlockSpec((None, CONV_W - 1, CONV_DIM), lambda i: (i // tps, 0, 0))]
        scratch = [pltpu.VMEM((8, CONV_DIM), F32)]
    else:
        tps = 1
        ins = [x2d, sc, sh, p["g1"], p["w_in"], p["g_q"], p["wq"], p["g_kv"],
               cos_t, sup_t, sdn_t, p["w_conv"], p["g_conv"], p["gmat"], prev_a, prev_b]
        in_specs = [row_tile(D_MODEL), row_tile(D_MODEL), row_tile(D_MODEL), full(p["g1"]), full(p["w_in"]),
                    full(p["g_q"]), full(p["wq"]), full(p["g_kv"]),
                    row_tile(LANES), row_tile(LANES), row_tile(LANES), full(p["w_conv"]), full(p["g_conv"]),
                    full(p["gmat"]), row_tile(CONV_DIM), row_tile(CONV_DIM)]
        out_shape = [jax.ShapeDtypeStruct((n, QK_WIDTH), BF16), jax.ShapeDtypeStruct((n, KV_LORA), F32),
                     jax.ShapeDtypeStruct((n, ROPE_DIM), F32), jax.ShapeDtypeStruct((n, CONV_DIM), BF16),
                     jax.ShapeDtypeStruct((n, CONV_DIM), F32)]
        out_specs = [row_tile(QK_WIDTH), row_tile(KV_LORA), row_tile(ROPE_DIM), row_tile(CONV_DIM),
                     row_tile(CONV_DIM)]
        scratch = []
    return pl.pallas_call(
        functools.partial(_mix_in_body, prompt=prompt, tm=tm, tiles_per_seq=tps, t_new=t_new),
        grid=(nt,), in_specs=in_specs, out_specs=out_specs, out_shape=out_shape,
        scratch_shapes=scratch, compiler_params=_cparams(("arbitrary",)),
        name="mix_in_prompt" if prompt else "mix_in_sample",
    )(*ins)


def _flash_body(q_ref, k_ref, v_ref, g_ref, gm_ref, o_ref, *, tq):
    i = pl.program_id(2)
    rowi = lax.broadcasted_iota(jnp.int32, (tq, tq), 0)
    coli = lax.broadcasted_iota(jnp.int32, (tq, tq), 1)
    causal = coli <= rowi
    outs = []
    for hh in range(2):
        qh = q_ref[:, hh * HEAD_PAD:(hh + 1) * HEAD_PAD]

        def step(j, carry, masked, hh=hh, qh=qh):
            m, l, acc = carry
            ks = pl.multiple_of(j * tq, tq)
            kt = k_ref[pl.ds(ks, tq), hh * HEAD_PAD:(hh + 1) * HEAD_PAD]
            vt = v_ref[pl.ds(ks, tq), :]
            s = _dot_nt(qh, kt)
            if masked:
                s = jnp.where(causal, s, -jnp.inf)
            m_new = jnp.maximum(m, jnp.max(s, axis=1, keepdims=True))
            alpha = jnp.exp(m - m_new)
            pr = jnp.exp(s - m_new)
            l = alpha * l + jnp.sum(pr, axis=1, keepdims=True)
            acc = alpha * acc + _dot(pr.astype(BF16), vt)
            return m_new, l, acc

        init = (jnp.full((tq, 1), -jnp.inf, F32), jnp.zeros((tq, 1), F32), jnp.zeros((tq, LANES), F32))
        carry = lax.fori_loop(0, i, functools.partial(step, masked=False), init)
        m, l, acc = step(i, carry, True)
        outs.append(acc / l)
    lane = lax.broadcasted_iota(jnp.int32, (tq, LANES), 1)
    o = jnp.where(lane < V_DIM, outs[0], outs[1])
    ms = _dot((o * o).astype(BF16), gm_ref[...]) * (1.0 / V_DIM)
    o_ref[...] = (o * lax.rsqrt(ms + EPS) * g_ref[...]).astype(BF16)


def _flash(q, k, v, g_attn, gm_head, batch, seq_len):
    n = q.shape[0]
    nq = seq_len // TQ
    return pl.pallas_call(
        functools.partial(_flash_body, tq=TQ),
        grid=(batch, N_HEADS // 2, nq),
        in_specs=[pl.BlockSpec((TQ, 2 * HEAD_PAD), lambda b, p, i: (b * nq + i, p)),
                  pl.BlockSpec((seq_len, 2 * HEAD_PAD), lambda b, p, i: (b, p)),
                  pl.BlockSpec((seq_len, 2 * V_DIM), lambda b, p, i: (b, p)),
                  pl.BlockSpec((1, 2 * V_DIM), lambda b, p, i: (0, p)),
                  pl.BlockSpec((LANES, LANES), lambda b, p, i: (0, 0))],
        out_specs=pl.BlockSpec((TQ, 2 * V_DIM), lambda b, p, i: (b * nq + i, p)),
        out_shape=jax.ShapeDtypeStruct((n, ATTN_WIDTH), BF16),
        compiler_params=_cparams(("arbitrary", "arbitrary", "arbitrary")),
        name="flash_prompt",
    )(q, k, v, g_attn, gm_head)


def _absorb_body(q_ref, w_ref, ql_ref, qp_ref):
    qb = q_ref[...]
    ql_ref[...] = _dot(qb, w_ref[...]).astype(BF16)
    qp_ref[...] = qb[:, PE_LO:PE_LO + ROPE_DIM]


def _absorb(q_s, wuk_t):
    n = q_s.shape[0]
    return pl.pallas_call(
        _absorb_body,
        grid=(N_HEADS,),
        in_specs=[pl.BlockSpec((n, HEAD_PAD), lambda h: (0, h)),
                  pl.BlockSpec((None, HEAD_PAD, KV_LORA), lambda h: (h, 0, 0))],
        out_specs=[pl.BlockSpec((None, n, KV_LORA), lambda h: (h, 0, 0)),
                   pl.BlockSpec((None, n, ROPE_DIM), lambda h: (h, 0, 0))],
        out_shape=[jax.ShapeDtypeStruct((N_HEADS, n, KV_LORA), BF16),
                   jax.ShapeDtypeStruct((N_HEADS, n, ROPE_DIM), BF16)],
        compiler_params=_cparams(("arbitrary",)),
        name="absorb",
    )(q_s, wuk_t)


PAGES_PER_CHUNK = 8


def _paged_body(pt_ref, ql_ref, qp_ref, cn_ref, kn_ref, cckv_ref, ckpe_ref, o_ref,
                ckv_buf, kpe_buf, sems, *, n_pages, n_seq, t_new):
    b = pl.program_id(0)
    slot = b % 2

    def fetch(seq, sl):
        def body(pg, _):
            page = pt_ref[seq, pg]
            pltpu.make_async_copy(cckv_ref.at[0, page], ckv_buf.at[sl, pg], sems.at[0, sl]).start()
            pltpu.make_async_copy(ckpe_ref.at[0, page], kpe_buf.at[sl, pg], sems.at[1, sl]).start()
            return 0
        lax.fori_loop(0, n_pages, body, 0)

    @pl.when(b == 0)
    def _():
        fetch(0, 0)

    @pl.when(b + 1 < n_seq)
    def _():
        fetch(b + 1, 1 - slot)

    pltpu.make_async_copy(cckv_ref.at[0, pl.ds(0, n_pages)], ckv_buf.at[slot], sems.at[0, slot]).wait()
    pltpu.make_async_copy(ckpe_ref.at[0, pl.ds(0, n_pages)], kpe_buf.at[slot], sems.at[1, slot]).wait()

    ql = ql_ref[...]
    qp = qp_ref[...]
    rows = ql.shape[0]
    ck = PAGES_PER_CHUNK * PAGE_SIZE

    def step(c, carry):
        m, l, acc = carry
        p0 = pl.multiple_of(c * PAGES_PER_CHUNK, PAGES_PER_CHUNK)
        kc = ckv_buf[slot, pl.ds(p0, PAGES_PER_CHUNK)].reshape(ck, KV_LORA).astype(BF16)
        pc = kpe_buf[slot, pl.ds(p0, PAGES_PER_CHUNK)].reshape(ck, ROPE_DIM).astype(BF16)
        s = _dot_nt(ql, kc) + _dot_nt(qp, pc)
        m_new = jnp.maximum(m, jnp.max(s, axis=1, keepdims=True))
        alpha = jnp.exp(m - m_new)
        pr = jnp.exp(s - m_new)
        l = alpha * l + jnp.sum(pr, axis=1, keepdims=True)
        acc = alpha * acc + _dot(pr.astype(BF16), kc)
        return m_new, l, acc

    init = (jnp.full((rows, 1), -jnp.inf, F32), jnp.zeros((rows, 1), F32), jnp.zeros((rows, KV_LORA), F32))
    m, l, acc = lax.fori_loop(0, n_pages // PAGES_PER_CHUNK, step, init)

    qlf = ql.astype(F32)
    qpf = qp.astype(F32)
    cn = cn_ref[...]
    kn = kn_ref[...]
    trow = lax.broadcasted_iota(jnp.int32, (rows, 1), 0) % t_new
    s_new = []
    for j in range(t_new):
        sj = (jnp.sum(qlf * cn[j:j + 1, :], axis=1, keepdims=True)
              + jnp.sum(qpf * kn[j:j + 1, :], axis=1, keepdims=True))
        s_new.append(jnp.where(trow >= j, sj, -jnp.inf))
    m_new = m
    for sj in s_new:
        m_new = jnp.maximum(m_new, sj)
    alpha = jnp.exp(m - m_new)
    l = alpha * l
    acc = alpha * acc
    for j in range(t_new):
        pj = jnp.exp(s_new[j] - m_new)
        l = l + pj
        acc = acc + pj * cn[j:j + 1, :]
    o_ref[...] = acc / l


def _paged(page_table, ql, qp, ckv_new, kpe_new, cache_ckv, cache_kpe):
    n_seq, n_pages = page_table.shape
    rows = ql.shape[1]
    t_new = ckv_new.shape[1]
    grid_spec = pltpu.PrefetchScalarGridSpec(
        num_scalar_prefetch=1,
        grid=(n_seq,),
        in_specs=[pl.BlockSpec((None, rows, KV_LORA), lambda b, pt: (b, 0, 0)),
                  pl.BlockSpec((None, rows, ROPE_DIM), lambda b, pt: (b, 0, 0)),
                  pl.BlockSpec((None, t_new, KV_LORA), lambda b, pt: (b, 0, 0)),
                  pl.BlockSpec((None, t_new, ROPE_DIM), lambda b, pt: (b, 0, 0)),
                  pl.BlockSpec(memory_space=pl.ANY),
                  pl.BlockSpec(memory_space=pl.ANY)],
        out_specs=pl.BlockSpec((None, rows, KV_LORA), lambda b, pt: (b, 0, 0)),
        scratch_shapes=[pltpu.VMEM((2, n_pages, PAGE_SIZE, KV_LORA), F32),
                        pltpu.VMEM((2, n_pages, PAGE_SIZE, ROPE_DIM), F32),
                        pltpu.SemaphoreType.DMA((2, 2))],
    )
    return pl.pallas_call(
        functools.partial(_paged_body, n_pages=n_pages, n_seq=n_seq, t_new=t_new),
        grid_spec=grid_spec,
        out_shape=jax.ShapeDtypeStruct((n_seq, rows, KV_LORA), F32),
        compiler_params=_cparams(("arbitrary",)),
        name="paged_attn",
    )(page_table, ql, qp, ckv_new, kpe_new, cache_ckv, cache_kpe)


def _unabsorb_body(o_ref, w_ref, g_ref, out_ref):
    o = _dot(o_ref[...].astype(BF16), w_ref[...])
    r = lax.rsqrt(jnp.sum(o * o, axis=-1, keepdims=True) * (1.0 / V_DIM) + EPS)
    out_ref[...] = (o * r * g_ref[...]).astype(BF16)


def _unabsorb(o_lat_h, wuv_h, g_h):
    n = o_lat_h.shape[1]
    return pl.pallas_call(
        _unabsorb_body,
        grid=(N_HEADS,),
        in_specs=[pl.BlockSpec((None, n, KV_LORA), lambda h: (h, 0, 0)),
                  pl.BlockSpec((None, KV_LORA, HEAD_PAD), lambda h: (h, 0, 0)),
                  pl.BlockSpec((None, 1, HEAD_PAD), lambda h: (h, 0, 0))],
        out_specs=pl.BlockSpec((None, n, HEAD_PAD), lambda h: (h, 0, 0)),
        out_shape=jax.ShapeDtypeStruct((N_HEADS, n, HEAD_PAD), BF16),
        compiler_params=_cparams(("arbitrary",)),
        name="unabsorb",
    )(o_lat_h, wuv_h, g_h)


def _post_mix_body(x_ref, o_ref, z_ref, gt1_ref, sc_ref, sh_ref, gt2_ref, wo_ref, g2_ref,
                   wr_ref, eb_ref, wsgu_ref, wsd_ref, tri_ref, cin_ref,
                   xs_ref, h2_ref, idx_ref, gw_ref, rank_ref, cout_ref, cnt_ref, *, tm):
    i = pl.program_id(0)

    @pl.when(i == 0)
    def _():
        cnt_ref[...] = cin_ref[...]

    wo = wo_ref[...]
    mix = _dot(o_ref[...], wo[:ATTN_WIDTH, :]) + _dot(z_ref[...], wo[ATTN_WIDTH:, :])
    x1 = x_ref[...] + gt1_ref[...] * mix
    h2 = _rms(x1, g2_ref[...]) * (1.0 + sc_ref[...]) + sh_ref[...]
    h2_ref[...] = h2
    hb = h2.astype(BF16)

    gu = _dot(hb, wsgu_ref[...])
    gate, up = gu[:, :SHARED_FF], gu[:, SHARED_FF:]
    shared = _dot((gate * jax.nn.sigmoid(gate) * up).astype(BF16), wsd_ref[...])
    xs_ref[...] = x1 + gt2_ref[...] * shared

    s = jax.nn.sigmoid(_dot_nt(wr_ref[...], hb))
    biased = s + eb_ref[...]
    ninf = -jnp.inf
    gi = lax.broadcasted_iota(jnp.int32, (GROUP_SIZE, tm), 0).astype(F32)
    gs = []
    for g in range(N_GROUPS):
        blk = biased[g * GROUP_SIZE:(g + 1) * GROUP_SIZE, :]
        m1 = jnp.max(blk, axis=0, keepdims=True)
        i1 = jnp.min(jnp.where(blk == m1, gi, float(GROUP_SIZE)), axis=0, keepdims=True)
        m2 = jnp.max(jnp.where(gi == i1, ninf, blk), axis=0, keepdims=True)
        gs.append(m1 + m2)
    gscore = jnp.concatenate(gs, axis=0)
    giota = lax.broadcasted_iota(jnp.int32, (N_GROUPS, tm), 0).astype(F32)
    gsel = jnp.zeros((N_GROUPS, tm), F32)
    for _ in range(TOPK_GROUPS):
        gm = jnp.max(gscore, axis=0, keepdims=True)
        gidx = jnp.min(jnp.where(gscore == gm, giota, float(N_GROUPS)), axis=0, keepdims=True)
        hit = giota == gidx
        gsel = jnp.where(hit, 1.0, gsel)
        gscore = jnp.where(hit, ninf, gscore)
    masked = jnp.concatenate(
        [jnp.where(gsel[g:g + 1, :] > 0.0, biased[g * GROUP_SIZE:(g + 1) * GROUP_SIZE, :], ninf)
         for g in range(N_GROUPS)], axis=0)
    eiota = lax.broadcasted_iota(jnp.int32, (N_EXPERTS, tm), 0).astype(F32)
    idxs, ws = [], []
    selall = jnp.zeros((N_EXPERTS, tm), F32)
    for _ in range(TOP_K):
        mx = jnp.max(masked, axis=0, keepdims=True)
        ei = jnp.min(jnp.where(masked == mx, eiota, float(N_EXPERTS)), axis=0, keepdims=True)
        hit = eiota == ei
        ws.append(jnp.sum(jnp.where(hit, s, 0.0), axis=0, keepdims=True))
        idxs.append(ei)
        selall = jnp.where(hit, 1.0, selall)
        masked = jnp.where(hit, ninf, masked)
    wsum = ws[0]
    for w in ws[1:]:
        wsum = wsum + w
    gw_ref[...] = jnp.concatenate(ws, axis=0) / wsum * ROUTED_SCALE
    idx_ref[...] = jnp.concatenate(idxs, axis=0).astype(jnp.int32)

    before = _dot(selall.astype(BF16), tri_ref[...]) + cnt_ref[:, 0:1]
    ranks = [jnp.sum(jnp.where(eiota == ei, before, 0.0), axis=0, keepdims=True) for ei in idxs]
    rank_ref[...] = jnp.concatenate(ranks, axis=0).astype(jnp.int32)
    cnt_ref[...] = cnt_ref[...] + jnp.sum(selall, axis=1, keepdims=True)
    cout_ref[...] = cnt_ref[...]


def _post_mix(x2d, o, z, gt1, sc2, sh2, gt2, p, cnt_in, h2_buf, *, prompt, seq_len, row_off, n_total):
    n = x2d.shape[0]
    tm = TILE_PROMPT if prompt else TILE_SAMPLE
    nt = n // tm
    full = lambda a: pl.BlockSpec(a.shape, lambda i: (0,) * a.ndim)
    row_tile = lambda w: pl.BlockSpec((tm, w), lambda i: (i, 0))
    if prompt:
        tps = seq_len // tm
        mod_spec = pl.BlockSpec((None, 1, D_MODEL), lambda i: (i // tps, 0, 0))
    else:
        mod_spec = row_tile(D_MODEL)
    tri = p["tri_p"] if prompt else p["tri_s"]
    boff = row_off // tm
    ins = [x2d, o, z, gt1, sc2, sh2, gt2, p["w_out"], p["g2"], p["wr_t"], p["e_bias"], p["ws_gu"], p["ws_d"],
           tri, cnt_in]
    in_specs = [row_tile(D_MODEL), row_tile(ATTN_WIDTH), row_tile(CONV_DIM), mod_spec, mod_spec, mod_spec,
                mod_spec, full(p["w_out"]), full(p["g2"]), full(p["wr_t"]), full(p["e_bias"]), full(p["ws_gu"]),
                full(p["ws_d"]), full(tri), full(cnt_in)]
    out_shape = [jax.ShapeDtypeStruct((n, D_MODEL), F32), jax.ShapeDtypeStruct((n_total, D_MODEL), F32),
                 jax.ShapeDtypeStruct((TOP_K, n), jnp.int32), jax.ShapeDtypeStruct((TOP_K, n), F32),
                 jax.ShapeDtypeStruct((TOP_K, n), jnp.int32), jax.ShapeDtypeStruct((N_EXPERTS, LANES), F32)]
    col_tile = pl.BlockSpec((TOP_K, tm), lambda i: (0, i))
    out_specs = [row_tile(D_MODEL), pl.BlockSpec((tm, D_MODEL), lambda i: (i + boff, 0)),
                 col_tile, col_tile, col_tile, pl.BlockSpec((N_EXPERTS, LANES), lambda i: (0, 0))]
    aliases = {}
    if h2_buf is not None:
        ins.append(h2_buf)
        in_specs.append(pl.BlockSpec(memory_space=pl.ANY))
        aliases = {len(ins) - 1: 1}
        body = lambda *refs: _post_mix_body(*refs[:15], *refs[16:], tm=tm)
    else:
        body = functools.partial(_post_mix_body, tm=tm)
    return pl.pallas_call(
        body, grid=(nt,), in_specs=in_specs, out_specs=out_specs, out_shape=out_shape,
        scratch_shapes=[pltpu.VMEM((N_EXPERTS, LANES), F32)],
        input_output_aliases=aliases,
        compiler_params=_cparams(("arbitrary",)),
        name="post_mix_prompt" if prompt else "post_mix_sample",
    )(*ins)


def _experts_body(be_ref, nu_ref, tokc_ref, tokn_ref, h_ref, wg_ref, wu_ref, wd_ref, y_ref,
                  xbuf, wgb, wub, wdb, sem, *, ch):
    c = pl.program_id(0)
    n_used = nu_ref[0]
    slot = c % 2

    def issue(tok_ref, sl):
        def body(r, _):
            t = tok_ref[0, 0, r]
            pltpu.make_async_copy(h_ref.at[pl.ds(t, 1)], xbuf.at[sl, pl.ds(r, 1)], sem.at[sl]).start()
            return 0
        lax.fori_loop(0, ch, body, 0, unroll=8)

    @pl.when(c == 0)
    def _():
        issue(tokc_ref, 0)

    @pl.when(c + 1 < n_used)
    def _():
        issue(tokn_ref, 1 - slot)

    prev_e = be_ref[jnp.maximum(c - 1, 0)]

    @pl.when(jnp.logical_or(c == 0, be_ref[c] != prev_e))
    def _():
        wgb[...] = wg_ref[...].astype(BF16)
        wub[...] = wu_ref[...].astype(BF16)
        wdb[...] = wd_ref[...].astype(BF16)

    @pl.when(c < n_used)
    def _():
        pltpu.make_async_copy(h_ref.at[pl.ds(0, ch)], xbuf.at[slot], sem.at[slot]).wait()
        xb = xbuf[slot].astype(BF16)
        gate = _dot(xb, wgb[...])
        up = _dot(xb, wub[...])
        act = (gate * jax.nn.sigmoid(gate) * up).astype(BF16)
        y_ref[...] = _dot(act, wdb[...])


def _experts(block_e, n_used, tok_buf3, h2, w_e_gate, w_e_up, w_e_down):
    nch, _, ch = tok_buf3.shape
    grid_spec = pltpu.PrefetchScalarGridSpec(
        num_scalar_prefetch=2,
        grid=(nch,),
        in_specs=[pl.BlockSpec((1, 1, ch), lambda c, be, nu: (c, 0, 0), memory_space=pltpu.SMEM),
                  pl.BlockSpec((1, 1, ch), lambda c, be, nu: (jnp.minimum(c + 1, nch - 1), 0, 0),
                               memory_space=pltpu.SMEM),
                  pl.BlockSpec(memory_space=pl.ANY),
                  pl.BlockSpec((None, D_MODEL, E_FF), lambda c, be, nu: (be[c], 0, 0)),
                  pl.BlockSpec((None, D_MODEL, E_FF), lambda c, be, nu: (be[c], 0, 0)),
                  pl.BlockSpec((None, E_FF, D_MODEL), lambda c, be, nu: (be[c], 0, 0))],
        out_specs=pl.BlockSpec((ch, D_MODEL), lambda c, be, nu: (c, 0)),
        scratch_shapes=[pltpu.VMEM((2, ch, D_MODEL), F32),
                        pltpu.VMEM((D_MODEL, E_FF), BF16), pltpu.VMEM((D_MODEL, E_FF), BF16),
                        pltpu.VMEM((E_FF, D_MODEL), BF16),
                        pltpu.SemaphoreType.DMA((2,))],
    )
    return pl.pallas_call(
        functools.partial(_experts_body, ch=ch),
        grid_spec=grid_spec,
        out_shape=jax.ShapeDtypeStruct((nch * ch, D_MODEL), F32),
        compiler_params=_cparams(("arbitrary",)),
        name="experts",
    )(block_e, n_used, tok_buf3, tok_buf3, h2, w_e_gate, w_e_up, w_e_down)


def _combine_body(dc_ref, dn_ref, y_ref, xs_ref, gw_ref, gt2_ref, gf_ref, fsc_ref, fsh_ref, out_ref,
                  gbuf, sem, *, tm, nt):
    i = pl.program_id(0)
    slot = i % 2

    def issue(d_ref, sl):
        def body(r, _):
            for k in range(TOP_K):
                d = d_ref[0, 0, k * tm + r]
                pltpu.make_async_copy(y_ref.at[pl.ds(d, 1)], gbuf.at[sl, k, pl.ds(r, 1)], sem.at[sl]).start()
            return 0
        lax.fori_loop(0, tm, body, 0)

    @pl.when(i == 0)
    def _():
        issue(dc_ref, 0)

    @pl.when(i + 1 < nt)
    def _():
        issue(dn_ref, 1 - slot)

    for k in range(TOP_K):
        pltpu.make_async_copy(y_ref.at[pl.ds(0, tm)], gbuf.at[slot, k], sem.at[slot]).wait()
    gw = gw_ref[...]
    f = gw[:, 0:1] * gbuf[slot, 0]
    for k in range(1, TOP_K):
        f = f + gw[:, k:k + 1] * gbuf[slot, k]
    x2 = xs_ref[...] + gt2_ref[...] * f
    out_ref[...] = _rms(x2, gf_ref[...]) * (1.0 + fsc_ref[...]) + fsh_ref[...]


def _combine(dest_tiles, y_sorted, xs, gw, gt2, g_final, fsc, fsh, *, prompt, seq_len):
    n = xs.shape[0]
    tm = COMBINE_TILE
    nt = n // tm
    row_tile = lambda w: pl.BlockSpec((tm, w), lambda i: (i, 0))
    if prompt:
        tps = seq_len // tm
        mod_spec = pl.BlockSpec((None, 1, D_MODEL), lambda i: (i // tps, 0, 0))
    else:
        mod_spec = row_tile(D_MODEL)
    return pl.pallas_call(
        functools.partial(_combine_body, tm=tm, nt=nt),
        grid=(nt,),
        in_specs=[pl.BlockSpec((1, 1, TOP_K * tm), lambda i: (i, 0, 0), memory_space=pltpu.SMEM),
                  pl.BlockSpec((1, 1, TOP_K * tm), lambda i: (jnp.minimum(i + 1, nt - 1), 0, 0),
                               memory_space=pltpu.SMEM),
                  pl.BlockSpec(memory_space=pl.ANY),
                  row_tile(D_MODEL), row_tile(TOP_K), mod_spec,
                  pl.BlockSpec((1, D_MODEL), lambda i: (0, 0)), mod_spec, mod_spec],
        out_specs=row_tile(D_MODEL),
        out_shape=jax.ShapeDtypeStruct((n, D_MODEL), F32),
        scratch_shapes=[pltpu.VMEM((2, TOP_K, tm, D_MODEL), F32), pltpu.SemaphoreType.DMA((2,))],
        compiler_params=_cparams(("arbitrary",)),
        name="combine_prompt" if prompt else "combine_sample",
    )(dest_tiles, dest_tiles, y_sorted, xs, gw, gt2, g_final, fsc, fsh)


def _rope_tables(pos):
    inv = 1.0 / (ROPE_THETA ** (jnp.arange(PE_HALF, dtype=F32) / PE_HALF))
    ang = pos.astype(F32)[:, None] * inv[None, :]
    cos, sin = jnp.cos(ang), jnp.sin(ang)
    n = pos.shape[0]
    ones = jnp.ones((n, PE_LO), F32)
    zeros = jnp.zeros((n, PE_LO), F32)
    zh = jnp.zeros((n, PE_HALF), F32)
    tail1 = jnp.ones((n, LANES - PE_LO - ROPE_DIM), F32)
    tail0 = jnp.zeros((n, LANES - PE_LO - ROPE_DIM), F32)
    cos_t = jnp.concatenate([ones, cos, cos, tail1], axis=1)
    sin_up = jnp.concatenate([zeros, zh, sin, tail0], axis=1)
    sin_dn = jnp.concatenate([zeros, -sin, zh, tail0], axis=1)
    return cos_t, sin_up, sin_dn


def _pad_heads(w, width):
    pad = [(0, 0)] * (w.ndim - 1) + [(0, HEAD_PAD - width)]
    w = jnp.pad(w, pad)
    return w.reshape(w.shape[:-2] + (N_HEADS * HEAD_PAD,))


def _block_diag_ones(n, blk):
    r = jnp.arange(n) // blk
    return (r[:, None] == r[None, :]).astype(BF16)


def _prep_weights(w_in, g_attn_norm, g_q, w_q_up, g_kv, w_uk, w_uv, w_conv, g_attn_out, g_conv_out, w_out,
                  g_ffn_norm, w_router, e_bias, w_s_gate, w_s_up, w_s_down):
    o0 = Q_LORA
    o1 = o0 + KV_LORA
    o2 = o1 + ROPE_DIM
    kpe_cols = jnp.pad(w_in[:, o1:o2], ((0, 0), (PE_LO, HEAD_PAD - PE_LO - ROPE_DIM)))
    w_in_p = jnp.concatenate([w_in[:, :o1], kpe_cols, w_in[:, o2:]], axis=1).astype(BF16)
    wq = _pad_heads(w_q_up.reshape(Q_LORA, N_HEADS, NOPE_DIM + ROPE_DIM), NOPE_DIM + ROPE_DIM).astype(BF16)
    wuk = _pad_heads(w_uk, NOPE_DIM).astype(BF16)
    wuv = w_uv.reshape(KV_LORA, ATTN_WIDTH).astype(BF16)
    wuk_t = jnp.pad(jnp.transpose(w_uk, (1, 2, 0)), ((0, 0), (0, HEAD_PAD - NOPE_DIM), (0, 0))).astype(BF16)
    wuv_h = jnp.pad(jnp.transpose(w_uv, (1, 0, 2)), ((0, 0), (0, 0), (0, HEAD_PAD - V_DIM))).astype(BF16)
    g_attn_h = jnp.pad(g_attn_out.reshape(N_HEADS, 1, V_DIM), ((0, 0), (0, 0), (0, HEAD_PAD - V_DIM)))
    tri = lambda t: (jnp.arange(t)[:, None] < jnp.arange(t)[None, :]).astype(BF16)
    return {
        "g1": g_attn_norm.reshape(1, -1), "w_in": w_in_p, "g_q": g_q.reshape(1, -1), "wq": wq,
        "g_kv": g_kv.reshape(1, -1), "wuk": wuk, "wuv": wuv, "wuk_t": wuk_t, "wuv_h": wuv_h,
        "w_conv": w_conv, "g_conv": g_conv_out.reshape(1, -1), "gmat": _block_diag_ones(CONV_DIM, CONV_GROUP_DIM),
        "g_attn": g_attn_out.reshape(1, -1), "g_attn_h": g_attn_h, "gm_head": _block_diag_ones(LANES, V_DIM),
        "w_out": w_out.astype(BF16), "g2": g_ffn_norm.reshape(1, -1), "wr_t": w_router.T.astype(BF16),
        "e_bias": e_bias.reshape(-1, 1), "ws_gu": jnp.concatenate([w_s_gate, w_s_up], axis=1).astype(BF16),
        "ws_d": w_s_down.astype(BF16), "tri_p": tri(TILE_PROMPT), "tri_s": tri(TILE_SAMPLE),
    }


def kernel(x_prompt, x_sample, c_prompt, c_sample, cache_ckv, cache_kpe, state_conv, page_table, w_ada, b_ada, g_attn_norm, w_in, g_q, w_q_up, g_kv, w_uk, w_uv, w_conv, g_attn_out, g_conv_out, w_out, g_ffn_norm, w_router, e_bias, w_e_gate, w_e_up, w_e_down, w_s_gate, w_s_up, w_s_down, w_ada_final, b_ada_final, g_final):
    assert w_ada.shape[0] == 1, "one layer"
    bsz, seq, d = x_prompt.shape
    nseq, t_new, _ = x_sample.shape
    n_p, n_s = bsz * seq, nseq * t_new
    n_tot = n_p + n_s
    past = page_table.shape[1] * PAGE_SIZE

    p = _prep_weights(w_in[0], g_attn_norm[0], g_q[0], w_q_up[0], g_kv[0], w_uk[0], w_uv[0], w_conv[0],
                      g_attn_out[0], g_conv_out[0], w_out[0], g_ffn_norm[0], w_router[0], e_bias[0],
                      w_s_gate[0], w_s_up[0], w_s_down[0])

    c_all = jnp.concatenate([c_prompt, c_sample], axis=0)
    mod = _ada(c_all, w_ada[0], b_ada[0])
    modf = _ada(c_all, w_ada_final, b_ada_final)
    sh1, sc1, gt1, sh2, sc2, gt2 = [mod[:, j * d:(j + 1) * d] for j in range(6)]
    fsh, fsc = modf[:, :d], modf[:, d:]
    per_batch = lambda a: a[:bsz].reshape(bsz, 1, d)
    per_token = lambda a: jnp.repeat(a[bsz:], t_new, axis=0)

    xp = x_prompt.reshape(n_p, d)
    xs_in = x_sample.reshape(n_s, d)
    tabs_p = _rope_tables(jnp.arange(seq, dtype=jnp.int32))
    tabs_s = tuple(jnp.tile(t, (nseq, 1)) for t in _rope_tables(past + jnp.arange(t_new, dtype=jnp.int32)))

    q_p, k_p, v_p, ckv_p, kpe_p, z_p, conv_p = _mix_in(
        xp, per_batch(sc1), per_batch(sh1), p, tabs_p, prompt=True, seq_len=seq)
    o_p = _flash(q_p, k_p, v_p, p["g_attn"], p["gm_head"], bsz, seq)

    prev = state_conv[0]
    prev_a = jnp.repeat(prev[:, 1, :], t_new, axis=0)
    prev_b = jnp.repeat(prev[:, 0, :], t_new, axis=0)
    q_s, ckv_s, kpe_s, z_s, u_s = _mix_in(
        xs_in, per_token(sc1), per_token(sh1), p, tabs_s, prompt=False, prev_a=prev_a, prev_b=prev_b,
        t_new=t_new)
    ql_h, qp_h = _absorb(q_s, p["wuk_t"])
    rows = N_HEADS * t_new
    to_seq = lambda a: jnp.transpose(a.reshape(N_HEADS, nseq, t_new, a.shape[-1]), (1, 0, 2, 3)).reshape(
        nseq, rows, a.shape[-1])
    o_lat = _paged(page_table, to_seq(ql_h), to_seq(qp_h), ckv_s.reshape(nseq, t_new, KV_LORA),
                   kpe_s.reshape(nseq, t_new, ROPE_DIM), cache_ckv, cache_kpe)
    o_lat_h = jnp.transpose(o_lat.reshape(nseq, N_HEADS, t_new, KV_LORA), (1, 0, 2, 3)).reshape(
        N_HEADS, n_s, KV_LORA)
    o_s_h = _unabsorb(o_lat_h, p["wuv_h"], p["g_attn_h"])
    o_s = jnp.transpose(o_s_h[:, :, :V_DIM], (1, 0, 2)).reshape(n_s, ATTN_WIDTH)

    cnt0 = jnp.zeros((N_EXPERTS, LANES), F32)
    xs_p, h2, idx_p, gw_p, rank_p, cnt_p = _post_mix(
        xp, o_p, z_p, per_batch(gt1), per_batch(sc2), per_batch(sh2), per_batch(gt2), p, cnt0, None,
        prompt=True, seq_len=seq, row_off=0, n_total=n_tot)
    xs_s, h2, idx_s, gw_s, rank_s, cnt_s = _post_mix(
        xs_in, o_s, z_s, per_token(gt1), per_token(sc2), per_token(sh2), per_token(gt2), p, cnt_p, h2,
        prompt=False, seq_len=None, row_off=n_p, n_total=n_tot)

    ch = ROUTE_CHUNK
    counts = cnt_s[:, 0].astype(jnp.int32)
    padded = (counts + ch - 1) // ch * ch
    pend = jnp.cumsum(padded)
    pstart = pend - padded
    idx_all = jnp.concatenate([idx_p, idx_s], axis=1)
    rank_all = jnp.concatenate([rank_p, rank_s], axis=1)
    dest = pstart[idx_all] + rank_all
    nch = -(-(n_tot * TOP_K) // ch) + N_EXPERTS
    tok = jnp.broadcast_to(jnp.arange(n_tot, dtype=jnp.int32)[None, :], dest.shape)
    tok_buf = jnp.zeros((nch * ch,), jnp.int32).at[dest.reshape(-1)].set(tok.reshape(-1))
    block_e = jnp.minimum(jnp.searchsorted(pend, jnp.arange(nch, dtype=jnp.int32) * ch, side="right"),
                          N_EXPERTS - 1).astype(jnp.int32)
    n_used = (pend[-1] // ch).astype(jnp.int32).reshape(1)
    y_sorted = _experts(block_e, n_used, tok_buf.reshape(nch, 1, ch), h2, w_e_gate[0], w_e_up[0], w_e_down[0])

    tmc = COMBINE_TILE

    def dest_tiles(dst):
        n = dst.shape[1]
        return jnp.transpose(dst.reshape(TOP_K, n // tmc, tmc), (1, 0, 2)).reshape(n // tmc, 1, TOP_K * tmc)

    gfin = g_final.reshape(1, d)
    y_p = _combine(dest_tiles(dest[:, :n_p]), y_sorted, xs_p, gw_p.T, per_batch(gt2), gfin,
                   per_batch(fsc), per_batch(fsh), prompt=True, seq_len=seq)
    y_s = _combine(dest_tiles(dest[:, n_p:]), y_sorted, xs_s, gw_s.T, per_token(gt2), gfin,
                   per_token(fsc), per_token(fsh), prompt=False, seq_len=None)

    return (y_p.reshape(bsz, seq, d), y_s.reshape(nseq, t_new, d),
            ckv_p.reshape(1, bsz, seq, KV_LORA), kpe_p.reshape(1, bsz, seq, ROPE_DIM),
            conv_p.reshape(1, bsz, CONV_W - 1, CONV_DIM),
            ckv_s.reshape(1, nseq, t_new, KV_LORA), kpe_s.reshape(1, nseq, t_new, ROPE_DIM),
            u_s.reshape(nseq, t_new, CONV_DIM)[:, t_new - (CONV_W - 1):, :].reshape(1, nseq, CONV_W - 1, CONV_DIM))
```

```python
import functools
import math

import jax
import jax.numpy as jnp
from jax import lax
from jax.experimental import pallas as pl
from jax.experimental.pallas import tpu as pltpu

F32 = jnp.float32
BF16 = jnp.bfloat16

D_MODEL = 1024
N_HEADS = 8
NOPE_DIM = 64
ROPE_DIM = 32
V_DIM = 64
Q_LORA = 384
KV_LORA = 256
ATTN_WIDTH = N_HEADS * V_DIM
CONV_DIM = 512
CONV_GROUPS = 8
CONV_GROUP_DIM = CONV_DIM // CONV_GROUPS
CONV_W = 3
ROPE_THETA = 10000.0
PAGE_SIZE = 128
N_EXPERTS = 256
TOP_K = 8
N_GROUPS = 8
GROUP_SIZE = N_EXPERTS // N_GROUPS
TOPK_GROUPS = 4
E_FF = 256
SHARED_FF = 256
ROUTED_SCALE = 2.5
EPS = 1e-6

LANES = 128
HEAD_PAD = LANES
QK_WIDTH = N_HEADS * HEAD_PAD
PE_LO = NOPE_DIM
PE_HALF = ROPE_DIM // 2
IN_PAD_COLS = Q_LORA + KV_LORA + HEAD_PAD + 3 * CONV_DIM
VMEM_LIMIT = 48 * 1024 * 1024

TILE_PROMPT = 256
TILE_SAMPLE = 128
TQ = 512
ROUTE_CHUNK = 256
COMBINE_TILE = 128


def _cparams(sem, vmem=VMEM_LIMIT):
    return pltpu.CompilerParams(dimension_semantics=sem, vmem_limit_bytes=vmem)


def _dot(a, b):
    return jnp.dot(a, b, preferred_element_type=F32)


def _dot_nt(a, b):
    return lax.dot_general(a, b, (((1,), (1,)), ((), ())), preferred_element_type=F32)


def _rms(x, g):
    r = lax.rsqrt(jnp.mean(x * x, axis=-1, keepdims=True) + EPS)
    return (x * r) * g


def _ada_body(c_ref, w_ref, b_ref, o_ref):
    c = c_ref[...]
    a = (c * jax.nn.sigmoid(c)).astype(BF16)
    o_ref[...] = _dot(a, w_ref[...].astype(BF16)) + b_ref[...]


def _ada(c_all, w, b):
    m, d = c_all.shape
    n = w.shape[1]
    tn = 1024
    return pl.pallas_call(
        _ada_body,
        grid=(n // tn,),
        in_specs=[pl.BlockSpec((m, d), lambda j: (0, 0)),
                  pl.BlockSpec((d, tn), lambda j: (0, j)),
                  pl.BlockSpec((1, tn), lambda j: (0, j))],
        out_specs=pl.BlockSpec((m, tn), lambda j: (0, j)),
        out_shape=jax.ShapeDtypeStruct((m, n), F32),
        compiler_params=_cparams(("arbitrary",)),
        name="ada",
    )(c_all, w, b.reshape(1, n))


def _rope_lanes(x, cos, sin_up, sin_dn):
    w = x.shape[1]
    up = pltpu.roll(x, PE_HALF, 1)
    dn = pltpu.roll(x, w - PE_HALF, 1)
    return x * cos + up * sin_up + dn * sin_dn


def _mix_in_body(*refs, prompt, tm, tiles_per_seq, t_new):
    if prompt:
        (x_ref, sc_ref, sh_ref, g1_ref, win_ref, gq_ref, wq_ref, gkv_ref, wuk_ref, wuv_ref,
         cos_ref, sup_ref, sdn_ref, wconv_ref, gconv_ref, gmat_ref,
         q_ref, k_ref, v_ref, ckv_ref, kpe_ref, z_ref, cst_ref, carry_ref) = refs
    else:
        (x_ref, sc_ref, sh_ref, g1_ref, win_ref, gq_ref, wq_ref, gkv_ref,
         cos_ref, sup_ref, sdn_ref, wconv_ref, gconv_ref, gmat_ref, pa_ref, pb_ref,
         q_ref, ckv_ref, kpe_ref, z_ref, u_ref) = refs

    x = x_ref[...]
    h = _rms(x, g1_ref[...]) * (1.0 + sc_ref[...]) + sh_ref[...]
    proj = _dot(h.astype(BF16), win_ref[...])
    o0 = Q_LORA
    o1 = o0 + KV_LORA
    o2 = o1 + HEAD_PAD
    o3 = o2 + CONV_DIM
    o4 = o3 + CONV_DIM
    q_a, kv_a, kpe_blk = proj[:, :o0], proj[:, o0:o1], proj[:, o1:o2]
    b_g, c_g, u_in = proj[:, o2:o3], proj[:, o3:o4], proj[:, o4:]

    cos, sup, sdn = cos_ref[...], sup_ref[...], sdn_ref[...]
    cos8 = jnp.concatenate([cos] * N_HEADS, axis=1)
    sup8 = jnp.concatenate([sup] * N_HEADS, axis=1)
    sdn8 = jnp.concatenate([sdn] * N_HEADS, axis=1)

    qn = _rms(q_a, gq_ref[...]).astype(BF16)
    q = _dot(qn, wq_ref[...]) * ((NOPE_DIM + ROPE_DIM) ** -0.5)
    q_ref[...] = _rope_lanes(q, cos8, sup8, sdn8).astype(BF16)

    ckv = _rms(kv_a, gkv_ref[...])
    ckv_ref[...] = ckv
    kpe = _rope_lanes(kpe_blk, cos, sup, sdn)
    kpe_ref[...] = kpe[:, PE_LO:PE_LO + ROPE_DIM]

    if prompt:
        ckvb = ckv.astype(BF16)
        k = _dot(ckvb, wuk_ref[...]) + jnp.concatenate([kpe] * N_HEADS, axis=1)
        k_ref[...] = k.astype(BF16)
        v_ref[...] = _dot(ckvb, wuv_ref[...]).astype(BF16)

    u = c_g * u_in
    row = lax.broadcasted_iota(jnp.int32, (tm, 1), 0)
    r1 = pltpu.roll(u, 1, 0)
    r2 = pltpu.roll(u, 2, 0)
    if prompt:
        @pl.when(pl.program_id(0) % tiles_per_seq == 0)
        def _():
            carry_ref[...] = jnp.zeros_like(carry_ref)
        c6 = carry_ref[6:7, :]
        c7 = carry_ref[7:8, :]
        um1 = jnp.where(row == 0, c7, r1)
        um2 = jnp.where(row == 0, c6, jnp.where(row == 1, c7, r2))
        carry_ref[...] = u[tm - 8:, :]
        cst_ref[...] = u[tm - (CONV_W - 1):, :]
    else:
        t = row % t_new
        um1 = jnp.where(t == 0, pa_ref[...], r1)
        um2 = jnp.where(t == 0, pb_ref[...], jnp.where(t == 1, pa_ref[...], r2))
        u_ref[...] = u
    wc = wconv_ref[...]
    y = um2 * wc[0:1, :] + um1 * wc[1:2, :] + u * wc[2:3, :]
    zz = b_g * y
    ms = _dot((zz * zz).astype(BF16), gmat_ref[...]) * (1.0 / CONV_GROUP_DIM)
    z_ref[...] = (zz * lax.rsqrt(ms + EPS) * gconv_ref[...]).astype(BF16)


def _mix_in(x2d, sc, sh, p, rope_tabs, *, prompt, seq_len=None, prev_a=None, prev_b=None, t_new=1):
    n = x2d.shape[0]
    tm = TILE_PROMPT if prompt else TILE_SAMPLE
    nt = n // tm
    cos_t, sup_t, sdn_t = rope_tabs
    full = lambda a: pl.BlockSpec(a.shape, lambda i: (0,) * a.ndim)
    row_tile = lambda w: pl.BlockSpec((tm, w), lambda i: (i, 0))
    if prompt:
        tps = seq_len // tm
        mod_spec = pl.BlockSpec((None, 1, D_MODEL), lambda i: (i // tps, 0, 0))
        tab_spec = pl.BlockSpec((tm, LANES), lambda i: (i % tps, 0))
        nb = n // seq_len
        ins = [x2d, sc, sh, p["g1"], p["w_in"], p["g_q"], p["wq"], p["g_kv"], p["wuk"], p["wuv"],
               cos_t, sup_t, sdn_t, p["w_conv"], p["g_conv"], p["gmat"]]
        in_specs = [row_tile(D_MODEL), mod_spec, mod_spec, full(p["g1"]), full(p["w_in"]), full(p["g_q"]),
                    full(p["wq"]), full(p["g_kv"]), full(p["wuk"]), full(p["wuv"]),
                    tab_spec, tab_spec, tab_spec, full(p["w_conv"]), full(p["g_conv"]), full(p["gmat"])]
        out_shape = [jax.ShapeDtypeStruct((n, QK_WIDTH), BF16), jax.ShapeDtypeStruct((n, QK_WIDTH), BF16),
                     jax.ShapeDtypeStruct((n, ATTN_WIDTH), BF16), jax.ShapeDtypeStruct((n, KV_LORA), F32),
                     jax.ShapeDtypeStruct((n, ROPE_DIM), F32), jax.ShapeDtypeStruct((n, CONV_DIM), BF16),
                     jax.ShapeDtypeStruct((nb, CONV_W - 1, CONV_DIM), F32)]
        out_specs = [row_tile(QK_WIDTH), row_tile(QK_WIDTH), row_tile(ATTN_WIDTH), row_tile(KV_LORA),
                     row_tile(ROPE_DIM), row_tile(CONV_DIM),
                     pl.BlockSpec((None, CONV_W - 1, CONV_DIM), lambda i: (i // tps, 0, 0))]
        scratch = [pltpu.VMEM((8, CONV_DIM), F32)]
    else:
        tps = 1
        ins = [x2d, sc, sh, p["g1"], p["w_in"], p["g_q"], p["wq"], p["g_kv"],
               cos_t, sup_t, sdn_t, p["w_conv"], p["g_conv"], p["gmat"], prev_a, prev_b]
        in_specs = [row_tile(D_MODEL), row_tile(D_MODEL), row_tile(D_MODEL), full(p["g1"]), full(p["w_in"]),
                    full(p["g_q"]), full(p["wq"]), full(p["g_kv"]),
                    row_tile(LANES), row_tile(LANES), row_tile(LANES), full(p["w_conv"]), full(p["g_conv"]),
                    full(p["gmat"]), row_tile(CONV_DIM), row_tile(CONV_DIM)]
        out_shape = [jax.ShapeDtypeStruct((n, QK_WIDTH), BF16), jax.ShapeDtypeStruct((n, KV_LORA), F32),
                     jax.ShapeDtypeStruct((n, ROPE_DIM), F32), jax.ShapeDtypeStruct((n, CONV_DIM), BF16),
                     jax.ShapeDtypeStruct((n, CONV_DIM), F32)]
        out_specs = [row_tile(QK_WIDTH), row_tile(KV_LORA), row_tile(ROPE_DIM), row_tile(CONV_DIM),
                     row_tile(CONV_DIM)]
        scratch = []
    return pl.pallas_call(
        functools.partial(_mix_in_body, prompt=prompt, tm=tm, tiles_per_seq=tps, t_new=t_new),
        grid=(nt,), in_specs=in_specs, out_specs=out_specs, out_shape=out_shape,
        scratch_shapes=scratch, compiler_params=_cparams(("arbitrary",)),
        name="mix_in_prompt" if prompt else "mix_in_sample",
    )(*ins)


def _flash_body(q_ref, k_ref, v_ref, g_ref, gm_ref, o_ref, *, tq):
    i = pl.program_id(2)
    rowi = lax.broadcasted_iota(jnp.int32, (tq, tq), 0)
    coli = lax.broadcasted_iota(jnp.int32, (tq, tq), 1)
    causal = coli <= rowi
    outs = []
    for hh in range(2):
        qh = q_ref[:, hh * HEAD_PAD:(hh + 1) * HEAD_PAD]

        def step(j, carry, masked, hh=hh, qh=qh):
            m, l, acc = carry
            ks = pl.multiple_of(j * tq, tq)
            kt = k_ref[pl.ds(ks, tq), hh * HEAD_PAD:(hh + 1) * HEAD_PAD]
            vt = v_ref[pl.ds(ks, tq), :]
            s = _dot_nt(qh, kt)
            if masked:
                s = jnp.where(causal, s, -jnp.inf)
            m_new = jnp.maximum(m, jnp.max(s, axis=1, keepdims=True))
            alpha = jnp.exp(m - m_new)
            pr = jnp.exp(s - m_new)
            l = alpha * l + jnp.sum(pr, axis=1, keepdims=True)
            acc = alpha * acc + _dot(pr.astype(BF16), vt)
            return m_new, l, acc

        init = (jnp.full((tq, 1), -jnp.inf, F32), jnp.zeros((tq, 1), F32), jnp.zeros((tq, LANES), F32))
        carry = lax.fori_loop(0, i, functools.partial(step, masked=False), init)
        m, l, acc = step(i, carry, True)
        outs.append(acc / l)
    lane = lax.broadcasted_iota(jnp.int32, (tq, LANES), 1)
    o = jnp.where(lane < V_DIM, outs[0], outs[1])
    ms = _dot((o * o).astype(BF16), gm_ref[...]) * (1.0 / V_DIM)
    o_ref[...] = (o * lax.rsqrt(ms + EPS) * g_ref[...]).astype(BF16)


def _flash(q, k, v, g_attn, gm_head, batch, seq_len):
    n = q.shape[0]
    nq = seq_len // TQ
    return pl.pallas_call(
        functools.partial(_flash_body, tq=TQ),
        grid=(batch, N_HEADS // 2, nq),
        in_specs=[pl.BlockSpec((TQ, 2 * HEAD_PAD), lambda b, p, i: (b * nq + i, p)),
                  pl.BlockSpec((seq_len, 2 * HEAD_PAD), lambda b, p, i: (b, p)),
                  pl.BlockSpec((seq_len, 2 * V_DIM), lambda b, p, i: (b, p)),
                  pl.BlockSpec((1, 2 * V_DIM), lambda b, p, i: (0, p)),
                  pl.BlockSpec((LANES, LANES), lambda b, p, i: (0, 0))],
        out_specs=pl.BlockSpec((TQ, 2 * V_DIM), lambda b, p, i: (b * nq + i, p)),
        out_shape=jax.ShapeDtypeStruct((n, ATTN_WIDTH), BF16),
        compiler_params=_cparams(("arbitrary", "arbitrary", "arbitrary")),
        name="flash_prompt",
    )(q, k, v, g_attn, gm_head)


def _absorb_body(q_ref, w_ref, ql_ref, qp_ref):
    qb = q_ref[...]
    ql_ref[...] = _dot(qb, w_ref[...]).astype(BF16)
    qp_ref[...] = qb[:, PE_LO:PE_LO + ROPE_DIM]


def _absorb(q_s, wuk_t):
    n = q_s.shape[0]
    return pl.pallas_call(
        _absorb_body,
        grid=(N_HEADS,),
        in_specs=[pl.BlockSpec((n, HEAD_PAD), lambda h: (0, h)),
                  pl.BlockSpec((None, HEAD_PAD, KV_LORA), lambda h: (h, 0, 0))],
        out_specs=[pl.BlockSpec((None, n, KV_LORA), lambda h: (h, 0, 0)),
                   pl.BlockSpec((None, n, ROPE_DIM), lambda h: (h, 0, 0))],
        out_shape=[jax.ShapeDtypeStruct((N_HEADS, n, KV_LORA), BF16),
                   jax.ShapeDtypeStruct((N_HEADS, n, ROPE_DIM), BF16)],
        compiler_params=_cparams(("arbitrary",)),
        name="absorb",
    )(q_s, wuk_t)


PAGES_PER_CHUNK = 8


def _paged_body(pt_ref, ql_ref, qp_ref, cn_ref, kn_ref, cckv_ref, ckpe_ref, o_ref,
                ckv_buf, kpe_buf, sems, *, n_pages, n_seq, t_new):
    b = pl.program_id(0)
    slot = b % 2

    def fetch(seq, sl):
        def body(pg, _):
            page = pt_ref[seq, pg]
            pltpu.make_async_copy(cckv_ref.at[0, page], ckv_buf.at[sl, pg], sems.at[0, sl]).start()
            pltpu.make_async_copy(ckpe_ref.at[0, page], kpe_buf.at[sl, pg], sems.at[1, sl]).start()
            return 0
        lax.fori_loop(0, n_pages, body, 0)

    @pl.when(b == 0)
    def _():
        fetch(0, 0)

    @pl.when(b + 1 < n_seq)
    def _():
        fetch(b + 1, 1 - slot)

    pltpu.make_async_copy(cckv_ref.at[0, pl.ds(0, n_pages)], ckv_buf.at[slot], sems.at[0, slot]).wait()
    pltpu.make_async_copy(ckpe_ref.at[0, pl.ds(0, n_pages)], kpe_buf.at[slot], sems.at[1, slot]).wait()

    ql = ql_ref[...]
    qp = qp_ref[...]
    rows = ql.shape[0]
    ck = PAGES_PER_CHUNK * PAGE_SIZE

    def step(c, carry):
        m, l, acc = carry
        p0 = pl.multiple_of(c * PAGES_PER_CHUNK, PAGES_PER_CHUNK)
        kc = ckv_buf[slot, pl.ds(p0, PAGES_PER_CHUNK)].reshape(ck, KV_LORA).astype(BF16)
        pc = kpe_buf[slot, pl.ds(p0, PAGES_PER_CHUNK)].astype(BF16)
        s_pe = jnp.concatenate([_dot(qp, pc[j]) for j in range(PAGES_PER_CHUNK)], axis=1)
        s = _dot_nt(ql, kc) + s_pe
        m_new = jnp.maximum(m, jnp.max(s, axis=1, keepdims=True))
        alpha = jnp.exp(m - m_new)
        pr = jnp.exp(s - m_new)
        l = alpha * l + jnp.sum(pr, axis=1, keepdims=True)
        acc = alpha * acc + _dot(pr.astype(BF16), kc)
        return m_new, l, acc

    init = (jnp.full((rows, 1), -jnp.inf, F32), jnp.zeros((rows, 1), F32), jnp.zeros((rows, KV_LORA), F32))
    m, l, acc = lax.fori_loop(0, n_pages // PAGES_PER_CHUNK, step, init)

    qlf = ql.astype(F32)
    qpf = qp.astype(F32)
    cn = cn_ref[...]
    kn = kn_ref[...]
    trow = lax.broadcasted_iota(jnp.int32, (rows, 1), 0) % t_new
    s_new = []
    for j in range(t_new):
        sj = (jnp.sum(qlf * cn[j:j + 1, :], axis=1, keepdims=True)
              + jnp.sum(qpf * kn[j:j + 1, :], axis=1, keepdims=True))
        s_new.append(jnp.where(trow >= j, sj, -jnp.inf))
    m_new = m
    for sj in s_new:
        m_new = jnp.maximum(m_new, sj)
    alpha = jnp.exp(m - m_new)
    l = alpha * l
    acc = alpha * acc
    for j in range(t_new):
        pj = jnp.exp(s_new[j] - m_new)
        l = l + pj
        acc = acc + pj * cn[j:j + 1, :]
    o_ref[...] = acc / l


def _paged(page_table, ql, qp, ckv_new, kpe_new, cache_ckv, cache_kpe):
    n_seq, n_pages = page_table.shape
    rows = ql.shape[1]
    t_new = ckv_new.shape[1]
    grid_spec = pltpu.PrefetchScalarGridSpec(
        num_scalar_prefetch=1,
        grid=(n_seq,),
        in_specs=[pl.BlockSpec((None, rows, KV_LORA), lambda b, pt: (b, 0, 0)),
                  pl.BlockSpec((None, rows, ROPE_DIM), lambda b, pt: (b, 0, 0)),
                  pl.BlockSpec((None, t_new, KV_LORA), lambda b, pt: (b, 0, 0)),
                  pl.BlockSpec((None, t_new, ROPE_DIM), lambda b, pt: (b, 0, 0)),
                  pl.BlockSpec(memory_space=pl.ANY),
                  pl.BlockSpec(memory_space=pl.ANY)],
        out_specs=pl.BlockSpec((None, rows, KV_LORA), lambda b, pt: (b, 0, 0)),
        scratch_shapes=[pltpu.VMEM((2, n_pages, PAGE_SIZE, KV_LORA), F32),
                        pltpu.VMEM((2, n_pages, ROPE_DIM, PAGE_SIZE), F32),
                        pltpu.SemaphoreType.DMA((2, 2))],
    )
    return pl.pallas_call(
        functools.partial(_paged_body, n_pages=n_pages, n_seq=n_seq, t_new=t_new),
        grid_spec=grid_spec,
        out_shape=jax.ShapeDtypeStruct((n_seq, rows, KV_LORA), F32),
        compiler_params=_cparams(("arbitrary",)),
        name="paged_attn",
    )(page_table, ql, qp, ckv_new, kpe_new, cache_ckv, cache_kpe)


def _unabsorb_body(o_ref, w_ref, g_ref, out_ref):
    o = _dot(o_ref[...].astype(BF16), w_ref[...])
    r = lax.rsqrt(jnp.sum(o * o, axis=-1, keepdims=True) * (1.0 / V_DIM) + EPS)
    out_ref[...] = (o * r * g_ref[...]).astype(BF16)


def _unabsorb(o_lat_h, wuv_h, g_h):
    n = o_lat_h.shape[1]
    return pl.pallas_call(
        _unabsorb_body,
        grid=(N_HEADS,),
        in_specs=[pl.BlockSpec((None, n, KV_LORA), lambda h: (h, 0, 0)),
                  pl.BlockSpec((None, KV_LORA, HEAD_PAD), lambda h: (h, 0, 0)),
                  pl.BlockSpec((None, 1, HEAD_PAD), lambda h: (h, 0, 0))],
        out_specs=pl.BlockSpec((None, n, HEAD_PAD), lambda h: (h, 0, 0)),
        out_shape=jax.ShapeDtypeStruct((N_HEADS, n, HEAD_PAD), BF16),
        compiler_params=_cparams(("arbitrary",)),
        name="unabsorb",
    )(o_lat_h, wuv_h, g_h)


def _post_mix_body(x_ref, o_ref, z_ref, gt1_ref, sc_ref, sh_ref, gt2_ref, wo_ref, g2_ref,
                   wr_ref, eb_ref, wsgu_ref, wsd_ref, tri_ref, cin_ref,
                   xs_ref, h2_ref, idx_ref, gw_ref, rank_ref, cout_ref, cnt_ref, *, tm):
    i = pl.program_id(0)

    @pl.when(i == 0)
    def _():
        cnt_ref[...] = cin_ref[...]

    wo = wo_ref[...]
    mix = _dot(o_ref[...], wo[:ATTN_WIDTH, :]) + _dot(z_ref[...], wo[ATTN_WIDTH:, :])
    x1 = x_ref[...] + gt1_ref[...] * mix
    h2 = _rms(x1, g2_ref[...]) * (1.0 + sc_ref[...]) + sh_ref[...]
    h2_ref[...] = h2
    hb = h2.astype(BF16)

    gu = _dot(hb, wsgu_ref[...])
    gate, up = gu[:, :SHARED_FF], gu[:, SHARED_FF:]
    shared = _dot((gate * jax.nn.sigmoid(gate) * up).astype(BF16), wsd_ref[...])
    xs_ref[...] = x1 + gt2_ref[...] * shared

    s = jax.nn.sigmoid(_dot_nt(wr_ref[...], hb))
    biased = s + eb_ref[...]
    ninf = -jnp.inf
    gi = lax.broadcasted_iota(jnp.int32, (GROUP_SIZE, tm), 0).astype(F32)
    gs = []
    for g in range(N_GROUPS):
        blk = biased[g * GROUP_SIZE:(g + 1) * GROUP_SIZE, :]
        m1 = jnp.max(blk, axis=0, keepdims=True)
        i1 = jnp.min(jnp.where(blk == m1, gi, float(GROUP_SIZE)), axis=0, keepdims=True)
        m2 = jnp.max(jnp.where(gi == i1, ninf, blk), axis=0, keepdims=True)
        gs.append(m1 + m2)
    gscore = jnp.concatenate(gs, axis=0)
    giota = lax.broadcasted_iota(jnp.int32, (N_GROUPS, tm), 0).astype(F32)
    gsel = jnp.zeros((N_GROUPS, tm), F32)
    for _ in range(TOPK_GROUPS):
        gm = jnp.max(gscore, axis=0, keepdims=True)
        gidx = jnp.min(jnp.where(gscore == gm, giota, float(N_GROUPS)), axis=0, keepdims=True)
        hit = giota == gidx
        gsel = jnp.where(hit, 1.0, gsel)
        gscore = jnp.where(hit, ninf, gscore)
    masked = jnp.concatenate(
        [jnp.where(gsel[g:g + 1, :] > 0.0, biased[g * GROUP_SIZE:(g + 1) * GROUP_SIZE, :], ninf)
         for g in range(N_GROUPS)], axis=0)
    eiota = lax.broadcasted_iota(jnp.int32, (N_EXPERTS, tm), 0).astype(F32)
    idxs, ws = [], []
    selall = jnp.zeros((N_EXPERTS, tm), F32)
    for _ in range(TOP_K):
        mx = jnp.max(masked, axis=0, keepdims=True)
        ei = jnp.min(jnp.where(masked == mx, eiota, float(N_EXPERTS)), axis=0, keepdims=True)
        hit = eiota == ei
        ws.append(jnp.sum(jnp.where(hit, s, 0.0), axis=0, keepdims=True))
        idxs.append(ei)
        selall = jnp.where(hit, 1.0, selall)
        masked = jnp.where(hit, ninf, masked)
    wsum = ws[0]
    for w in ws[1:]:
        wsum = wsum + w
    gw_ref[...] = jnp.concatenate(ws, axis=0) / wsum * ROUTED_SCALE
    idx_ref[...] = jnp.concatenate(idxs, axis=0).astype(jnp.int32)

    before = _dot(selall.astype(BF16), tri_ref[...]) + cnt_ref[:, 0:1]
    ranks = [jnp.sum(jnp.where(eiota == ei, before, 0.0), axis=0, keepdims=True) for ei in idxs]
    rank_ref[...] = jnp.concatenate(ranks, axis=0).astype(jnp.int32)
    cnt_ref[...] = cnt_ref[...] + jnp.sum(selall, axis=1, keepdims=True)
    cout_ref[...] = cnt_ref[...]


def _post_mix(x2d, o, z, gt1, sc2, sh2, gt2, p, cnt_in, h2_buf, *, prompt, seq_len, row_off, n_total):
    n = x2d.shape[0]
    tm = TILE_PROMPT if prompt else TILE_SAMPLE
    nt = n // tm
    full = lambda a: pl.BlockSpec(a.shape, lambda i: (0,) * a.ndim)
    row_tile = lambda w: pl.BlockSpec((tm, w), lambda i: (i, 0))
    if prompt:
        tps = seq_len // tm
        mod_spec = pl.BlockSpec((None, 1, D_MODEL), lambda i: (i // tps, 0, 0))
    else:
        mod_spec = row_tile(D_MODEL)
    tri = p["tri_p"] if prompt else p["tri_s"]
    boff = row_off // tm
    ins = [x2d, o, z, gt1, sc2, sh2, gt2, p["w_out"], p["g2"], p["wr_t"], p["e_bias"], p["ws_gu"], p["ws_d"],
           tri, cnt_in]
    in_specs = [row_tile(D_MODEL), row_tile(ATTN_WIDTH), row_tile(CONV_DIM), mod_spec, mod_spec, mod_spec,
                mod_spec, full(p["w_out"]), full(p["g2"]), full(p["wr_t"]), full(p["e_bias"]), full(p["ws_gu"]),
                full(p["ws_d"]), full(tri), full(cnt_in)]
    out_shape = [jax.ShapeDtypeStruct((n, D_MODEL), F32), jax.ShapeDtypeStruct((n_total, D_MODEL), F32),
                 jax.ShapeDtypeStruct((TOP_K, n), jnp.int32), jax.ShapeDtypeStruct((TOP_K, n), F32),
                 jax.ShapeDtypeStruct((TOP_K, n), jnp.int32), jax.ShapeDtypeStruct((N_EXPERTS, LANES), F32)]
    col_tile = pl.BlockSpec((TOP_K, tm), lambda i: (0, i))
    out_specs = [row_tile(D_MODEL), pl.BlockSpec((tm, D_MODEL), lambda i: (i + boff, 0)),
                 col_tile, col_tile, col_tile, pl.BlockSpec((N_EXPERTS, LANES), lambda i: (0, 0))]
    aliases = {}
    if h2_buf is not None:
        ins.append(h2_buf)
        in_specs.append(pl.BlockSpec(memory_space=pl.ANY))
        aliases = {len(ins) - 1: 1}
        body = lambda *refs: _post_mix_body(*refs[:15], *refs[16:], tm=tm)
    else:
        body = functools.partial(_post_mix_body, tm=tm)
    return pl.pallas_call(
        body, grid=(nt,), in_specs=in_specs, out_specs=out_specs, out_shape=out_shape,
        scratch_shapes=[pltpu.VMEM((N_EXPERTS, LANES), F32)],
        input_output_aliases=aliases,
        compiler_params=_cparams(("arbitrary",)),
        name="post_mix_prompt" if prompt else "post_mix_sample",
    )(*ins)


def _dispatch_body(d_ref, h_ref, xs_ref, sem, *, tm, nt):
    i = pl.program_id(0)
    slot = i % 2
    t0 = i * tm

    def body(r, _):
        for k in range(TOP_K):
            d = d_ref[0, 0, k * tm + r]
            pltpu.make_async_copy(h_ref.at[pl.ds(t0 + r, 1)], xs_ref.at[pl.ds(d, 1)], sem.at[slot]).start()
        return 0
    lax.fori_loop(0, tm, body, 0)

    def drain(sl):
        for _ in range(TOP_K):
            pltpu.make_async_copy(h_ref.at[pl.ds(0, tm)], xs_ref.at[pl.ds(0, tm)], sem.at[sl]).wait()

    @pl.when(i > 0)
    def _():
        drain(1 - slot)

    @pl.when(i == nt - 1)
    def _():
        drain(slot)


def _dispatch(dest_tiles, h2, n_rows):
    nt, _, w = dest_tiles.shape
    tm = w // TOP_K
    return pl.pallas_call(
        functools.partial(_dispatch_body, tm=tm, nt=nt),
        grid=(nt,),
        in_specs=[pl.BlockSpec((1, 1, w), lambda i: (i, 0, 0), memory_space=pltpu.SMEM),
                  pl.BlockSpec(memory_space=pl.ANY)],
        out_specs=pl.BlockSpec(memory_space=pl.ANY),
        out_shape=jax.ShapeDtypeStruct((n_rows, D_MODEL), F32),
        scratch_shapes=[pltpu.SemaphoreType.DMA((2,))],
        compiler_params=_cparams(("arbitrary",)),
        name="dispatch",
    )(dest_tiles, h2)


def _experts_body(be_ref, nv_ref, nu_ref, x_ref, wg_ref, wu_ref, wd_ref, y_ref, wgb, wub, wdb, *, ch):
    c = pl.program_id(0)
    prev_e = be_ref[jnp.maximum(c - 1, 0)]

    @pl.when(jnp.logical_or(c == 0, be_ref[c] != prev_e))
    def _():
        wgb[...] = wg_ref[...].astype(BF16)
        wub[...] = wu_ref[...].astype(BF16)
        wdb[...] = wd_ref[...].astype(BF16)

    @pl.when(c < nu_ref[0])
    def _():
        row = lax.broadcasted_iota(jnp.int32, (ch, 1), 0)
        xb = jnp.where(row < nv_ref[c], x_ref[...], 0.0).astype(BF16)
        gate = _dot(xb, wgb[...])
        up = _dot(xb, wub[...])
        act = (gate * jax.nn.sigmoid(gate) * up).astype(BF16)
        y_ref[...] = _dot(act, wdb[...])


def _experts(block_e, n_valid, n_used, x_sorted, w_e_gate, w_e_up, w_e_down):
    ch = ROUTE_CHUNK
    nch = x_sorted.shape[0] // ch
    last = lambda c, nu: jnp.minimum(c, nu[0] - 1)
    grid_spec = pltpu.PrefetchScalarGridSpec(
        num_scalar_prefetch=3,
        grid=(nch,),
        in_specs=[pl.BlockSpec((ch, D_MODEL), lambda c, be, nv, nu: (last(c, nu), 0)),
                  pl.BlockSpec((None, D_MODEL, E_FF), lambda c, be, nv, nu: (be[c], 0, 0)),
                  pl.BlockSpec((None, D_MODEL, E_FF), lambda c, be, nv, nu: (be[c], 0, 0)),
                  pl.BlockSpec((None, E_FF, D_MODEL), lambda c, be, nv, nu: (be[c], 0, 0))],
        out_specs=pl.BlockSpec((ch, D_MODEL), lambda c, be, nv, nu: (last(c, nu), 0)),
        scratch_shapes=[pltpu.VMEM((D_MODEL, E_FF), BF16), pltpu.VMEM((D_MODEL, E_FF), BF16),
                        pltpu.VMEM((E_FF, D_MODEL), BF16)],
    )
    return pl.pallas_call(
        functools.partial(_experts_body, ch=ch),
        grid_spec=grid_spec,
        out_shape=jax.ShapeDtypeStruct((nch * ch, D_MODEL), F32),
        compiler_params=_cparams(("arbitrary",)),
        name="experts",
    )(block_e, n_valid, n_used, x_sorted, w_e_gate, w_e_up, w_e_down)


def _combine_body(dc_ref, dn_ref, y_ref, xs_ref, gw_ref, gt2_ref, gf_ref, fsc_ref, fsh_ref, out_ref,
                  gbuf, sem, *, tm, nt):
    i = pl.program_id(0)
    slot = i % 2

    def issue(d_ref, sl):
        def body(r, _):
            for k in range(TOP_K):
                d = d_ref[0, 0, k * tm + r]
                pltpu.make_async_copy(y_ref.at[pl.ds(d, 1)], gbuf.at[sl, k, pl.ds(r, 1)], sem.at[sl]).start()
            return 0
        lax.fori_loop(0, tm, body, 0)

    @pl.when(i == 0)
    def _():
        issue(dc_ref, 0)

    @pl.when(i + 1 < nt)
    def _():
        issue(dn_ref, 1 - slot)

    for k in range(TOP_K):
        pltpu.make_async_copy(y_ref.at[pl.ds(0, tm)], gbuf.at[slot, k], sem.at[slot]).wait()
    gw = gw_ref[...]
    f = gw[:, 0:1] * gbuf[slot, 0]
    for k in range(1, TOP_K):
        f = f + gw[:, k:k + 1] * gbuf[slot, k]
    x2 = xs_ref[...] + gt2_ref[...] * f
    out_ref[...] = _rms(x2, gf_ref[...]) * (1.0 + fsc_ref[...]) + fsh_ref[...]


def _combine(dest_tiles, y_sorted, xs, gw, gt2, g_final, fsc, fsh, *, prompt, seq_len):
    n = xs.shape[0]
    tm = COMBINE_TILE
    nt = n // tm
    row_tile = lambda w: pl.BlockSpec((tm, w), lambda i: (i, 0))
    if prompt:
        tps = seq_len // tm
        mod_spec = pl.BlockSpec((None, 1, D_MODEL), lambda i: (i // tps, 0, 0))
    else:
        mod_spec = row_tile(D_MODEL)
    return pl.pallas_call(
        functools.partial(_combine_body, tm=tm, nt=nt),
        grid=(nt,),
        in_specs=[pl.BlockSpec((1, 1, TOP_K * tm), lambda i: (i, 0, 0), memory_space=pltpu.SMEM),
                  pl.BlockSpec((1, 1, TOP_K * tm), lambda i: (jnp.minimum(i + 1, nt - 1), 0, 0),
                               memory_space=pltpu.SMEM),
                  pl.BlockSpec(memory_space=pl.ANY),
                  row_tile(D_MODEL), row_tile(TOP_K), mod_spec,
                  pl.BlockSpec((1, D_MODEL), lambda i: (0, 0)), mod_spec, mod_spec],
        out_specs=row_tile(D_MODEL),
        out_shape=jax.ShapeDtypeStruct((n, D_MODEL), F32),
        scratch_shapes=[pltpu.VMEM((2, TOP_K, tm, D_MODEL), F32), pltpu.SemaphoreType.DMA((2,))],
        compiler_params=_cparams(("arbitrary",)),
        name="combine_prompt" if prompt else "combine_sample",
    )(dest_tiles, dest_tiles, y_sorted, xs, gw, gt2, g_final, fsc, fsh)


def _rope_tables(pos):
    inv = 1.0 / (ROPE_THETA ** (jnp.arange(PE_HALF, dtype=F32) / PE_HALF))
    ang = pos.astype(F32)[:, None] * inv[None, :]
    cos, sin = jnp.cos(ang), jnp.sin(ang)
    n = pos.shape[0]
    ones = jnp.ones((n, PE_LO), F32)
    zeros = jnp.zeros((n, PE_LO), F32)
    zh = jnp.zeros((n, PE_HALF), F32)
    tail1 = jnp.ones((n, LANES - PE_LO - ROPE_DIM), F32)
    tail0 = jnp.zeros((n, LANES - PE_LO - ROPE_DIM), F32)
    cos_t = jnp.concatenate([ones, cos, cos, tail1], axis=1)
    sin_up = jnp.concatenate([zeros, zh, sin, tail0], axis=1)
    sin_dn = jnp.concatenate([zeros, -sin, zh, tail0], axis=1)
    return cos_t, sin_up, sin_dn


def _pad_heads(w, width):
    pad = [(0, 0)] * (w.ndim - 1) + [(0, HEAD_PAD - width)]
    w = jnp.pad(w, pad)
    return w.reshape(w.shape[:-2] + (N_HEADS * HEAD_PAD,))


def _block_diag_ones(n, blk):
    r = jnp.arange(n) // blk
    return (r[:, None] == r[None, :]).astype(BF16)


def _prep_weights(w_in, g_attn_norm, g_q, w_q_up, g_kv, w_uk, w_uv, w_conv, g_attn_out, g_conv_out, w_out,
                  g_ffn_norm, w_router, e_bias, w_s_gate, w_s_up, w_s_down):
    o0 = Q_LORA
    o1 = o0 + KV_LORA
    o2 = o1 + ROPE_DIM
    kpe_cols = jnp.pad(w_in[:, o1:o2], ((0, 0), (PE_LO, HEAD_PAD - PE_LO - ROPE_DIM)))
    w_in_p = jnp.concatenate([w_in[:, :o1], kpe_cols, w_in[:, o2:]], axis=1).astype(BF16)
    wq = _pad_heads(w_q_up.reshape(Q_LORA, N_HEADS, NOPE_DIM + ROPE_DIM), NOPE_DIM + ROPE_DIM).astype(BF16)
    wuk = _pad_heads(w_uk, NOPE_DIM).astype(BF16)
    wuv = w_uv.reshape(KV_LORA, ATTN_WIDTH).astype(BF16)
    wuk_t = jnp.pad(jnp.transpose(w_uk, (1, 2, 0)), ((0, 0), (0, HEAD_PAD - NOPE_DIM), (0, 0))).astype(BF16)
    wuv_h = jnp.pad(jnp.transpose(w_uv, (1, 0, 2)), ((0, 0), (0, 0), (0, HEAD_PAD - V_DIM))).astype(BF16)
    g_attn_h = jnp.pad(g_attn_out.reshape(N_HEADS, 1, V_DIM), ((0, 0), (0, 0), (0, HEAD_PAD - V_DIM)))
    tri = lambda t: (jnp.arange(t)[:, None] < jnp.arange(t)[None, :]).astype(BF16)
    return {
        "g1": g_attn_norm.reshape(1, -1), "w_in": w_in_p, "g_q": g_q.reshape(1, -1), "wq": wq,
        "g_kv": g_kv.reshape(1, -1), "wuk": wuk, "wuv": wuv, "wuk_t": wuk_t, "wuv_h": wuv_h,
        "w_conv": w_conv, "g_conv": g_conv_out.reshape(1, -1), "gmat": _block_diag_ones(CONV_DIM, CONV_GROUP_DIM),
        "g_attn": g_attn_out.reshape(1, -1), "g_attn_h": g_attn_h, "gm_head": _block_diag_ones(LANES, V_DIM),
        "w_out": w_out.astype(BF16), "g2": g_ffn_norm.reshape(1, -1), "wr_t": w_router.T.astype(BF16),
        "e_bias": e_bias.reshape(-1, 1), "ws_gu": jnp.concatenate([w_s_gate, w_s_up], axis=1).astype(BF16),
        "ws_d": w_s_down.astype(BF16), "tri_p": tri(TILE_PROMPT), "tri_s": tri(TILE_SAMPLE),
    }


def kernel(x_prompt, x_sample, c_prompt, c_sample, cache_ckv, cache_kpe, state_conv, page_table, w_ada, b_ada, g_attn_norm, w_in, g_q, w_q_up, g_kv, w_uk, w_uv, w_conv, g_attn_out, g_conv_out, w_out, g_ffn_norm, w_router, e_bias, w_e_gate, w_e_up, w_e_down, w_s_gate, w_s_up, w_s_down, w_ada_final, b_ada_final, g_final):
    assert w_ada.shape[0] == 1, "one layer"
    bsz, seq, d = x_prompt.shape
    nseq, t_new, _ = x_sample.shape
    n_p, n_s = bsz * seq, nseq * t_new
    n_tot = n_p + n_s
    past = page_table.shape[1] * PAGE_SIZE

    p = _prep_weights(w_in[0], g_attn_norm[0], g_q[0], w_q_up[0], g_kv[0], w_uk[0], w_uv[0], w_conv[0],
                      g_attn_out[0], g_conv_out[0], w_out[0], g_ffn_norm[0], w_router[0], e_bias[0],
                      w_s_gate[0], w_s_up[0], w_s_down[0])

    c_all = jnp.concatenate([c_prompt, c_sample], axis=0)
    mod = _ada(c_all, w_ada[0], b_ada[0])
    modf = _ada(c_all, w_ada_final, b_ada_final)
    sh1, sc1, gt1, sh2, sc2, gt2 = [mod[:, j * d:(j + 1) * d] for j in range(6)]
    fsh, fsc = modf[:, :d], modf[:, d:]
    per_batch = lambda a: a[:bsz].reshape(bsz, 1, d)
    per_token = lambda a: jnp.repeat(a[bsz:], t_new, axis=0)

    xp = x_prompt.reshape(n_p, d)
    xs_in = x_sample.reshape(n_s, d)
    tabs_p = _rope_tables(jnp.arange(seq, dtype=jnp.int32))
    tabs_s = tuple(jnp.tile(t, (nseq, 1)) for t in _rope_tables(past + jnp.arange(t_new, dtype=jnp.int32)))

    q_p, k_p, v_p, ckv_p, kpe_p, z_p, conv_p = _mix_in(
        xp, per_batch(sc1), per_batch(sh1), p, tabs_p, prompt=True, seq_len=seq)
    o_p = _flash(q_p, k_p, v_p, p["g_attn"], p["gm_head"], bsz, seq)

    prev = state_conv[0]
    prev_a = jnp.repeat(prev[:, 1, :], t_new, axis=0)
    prev_b = jnp.repeat(prev[:, 0, :], t_new, axis=0)
    q_s, ckv_s, kpe_s, z_s, u_s = _mix_in(
        xs_in, per_token(sc1), per_token(sh1), p, tabs_s, prompt=False, prev_a=prev_a, prev_b=prev_b,
        t_new=t_new)
    ql_h, qp_h = _absorb(q_s, p["wuk_t"])
    rows = N_HEADS * t_new
    to_seq = lambda a: jnp.transpose(a.reshape(N_HEADS, nseq, t_new, a.shape[-1]), (1, 0, 2, 3)).reshape(
        nseq, rows, a.shape[-1])
    o_lat = _paged(page_table, to_seq(ql_h), to_seq(qp_h), ckv_s.reshape(nseq, t_new, KV_LORA),
                   kpe_s.reshape(nseq, t_new, ROPE_DIM), cache_ckv, jnp.swapaxes(cache_kpe, 2, 3))
    o_lat_h = jnp.transpose(o_lat.reshape(nseq, N_HEADS, t_new, KV_LORA), (1, 0, 2, 3)).reshape(
        N_HEADS, n_s, KV_LORA)
    o_s_h = _unabsorb(o_lat_h, p["wuv_h"], p["g_attn_h"])
    o_s = jnp.transpose(o_s_h[:, :, :V_DIM], (1, 0, 2)).reshape(n_s, ATTN_WIDTH)

    cnt0 = jnp.zeros((N_EXPERTS, LANES), F32)
    xs_p, h2, idx_p, gw_p, rank_p, cnt_p = _post_mix(
        xp, o_p, z_p, per_batch(gt1), per_batch(sc2), per_batch(sh2), per_batch(gt2), p, cnt0, None,
        prompt=True, seq_len=seq, row_off=0, n_total=n_tot)
    xs_s, h2, idx_s, gw_s, rank_s, cnt_s = _post_mix(
        xs_in, o_s, z_s, per_token(gt1), per_token(sc2), per_token(sh2), per_token(gt2), p, cnt_p, h2,
        prompt=False, seq_len=None, row_off=n_p, n_total=n_tot)

    ch = ROUTE_CHUNK
    counts = cnt_s[:, 0].astype(jnp.int32)
    padded = (counts + ch - 1) // ch * ch
    pend = jnp.cumsum(padded)
    pstart = pend - padded
    idx_all = jnp.concatenate([idx_p, idx_s], axis=1)
    rank_all = jnp.concatenate([rank_p, rank_s], axis=1)
    eids = jnp.arange(N_EXPERTS, dtype=jnp.int32)
    lookup = lambda table, keys: jnp.sum(jnp.where(keys[..., None] == eids, table, 0), axis=-1)
    dest = lookup(pstart, idx_all) + rank_all
    nch = -(-(n_tot * TOP_K) // ch) + N_EXPERTS
    chunk_start = jnp.arange(nch, dtype=jnp.int32) * ch
    block_e = jnp.minimum(jnp.sum((pend[None, :] <= chunk_start[:, None]).astype(jnp.int32), axis=1),
                          N_EXPERTS - 1)
    n_valid = jnp.clip(lookup(pstart + counts, block_e) - chunk_start, 0, ch).astype(jnp.int32)
    n_used = (pend[-1] // ch).astype(jnp.int32).reshape(1)

    tmc = COMBINE_TILE

    def dest_tiles(dst):
        n = dst.shape[1]
        return jnp.transpose(dst.reshape(TOP_K, n // tmc, tmc), (1, 0, 2)).reshape(n // tmc, 1, TOP_K * tmc)

    x_sorted = _dispatch(dest_tiles(dest), h2, nch * ch)
    y_sorted = _experts(block_e, n_valid, n_used, x_sorted, w_e_gate[0], w_e_up[0], w_e_down[0])


    gfin = g_final.reshape(1, d)
    y_p = _combine(dest_tiles(dest[:, :n_p]), y_sorted, xs_p, gw_p.T, per_batch(gt2), gfin,
                   per_batch(fsc), per_batch(fsh), prompt=True, seq_len=seq)
    y_s = _combine(dest_tiles(dest[:, n_p:]), y_sorted, xs_s, gw_s.T, per_token(gt2), gfin,
                   per_token(fsc), per_token(fsh), prompt=False, seq_len=None)

    return (y_p.reshape(bsz, seq, d), y_s.reshape(nseq, t_new, d),
            ckv_p.reshape(1, bsz, seq, KV_LORA), kpe_p.reshape(1, bsz, seq, ROPE_DIM),
            conv_p.reshape(1, bsz, CONV_W - 1, CONV_DIM),
            ckv_s.reshape(1, nseq, t_new, KV_LORA), kpe_s.reshape(1, nseq, t_new, ROPE_DIM),
            u_s.reshape(nseq, t_new, CONV_DIM)[:, t_new - (CONV_W - 1):, :].reshape(1, nseq, CONV_W - 1, CONV_DIM))
```

```python
import functools
import math

import jax
import jax.numpy as jnp
from jax import lax
from jax.experimental import pallas as pl
from jax.experimental.pallas import tpu as pltpu

F32 = jnp.float32
BF16 = jnp.bfloat16

D_MODEL = 1024
N_HEADS = 8
NOPE_DIM = 64
ROPE_DIM = 32
V_DIM = 64
Q_LORA = 384
KV_LORA = 256
ATTN_WIDTH = N_HEADS * V_DIM
CONV_DIM = 512
CONV_GROUPS = 8
CONV_GROUP_DIM = CONV_DIM // CONV_GROUPS
CONV_W = 3
ROPE_THETA = 10000.0
PAGE_SIZE = 128
N_EXPERTS = 256
TOP_K = 8
N_GROUPS = 8
GROUP_SIZE = N_EXPERTS // N_GROUPS
TOPK_GROUPS = 4
E_FF = 256
SHARED_FF = 256
ROUTED_SCALE = 2.5
EPS = 1e-6

LANES = 128
HEAD_PAD = LANES
QK_WIDTH = N_HEADS * HEAD_PAD
PE_LO = NOPE_DIM
PE_HALF = ROPE_DIM // 2
IN_PAD_COLS = Q_LORA + KV_LORA + HEAD_PAD + 3 * CONV_DIM
VMEM_LIMIT = 48 * 1024 * 1024

TILE_PROMPT = 256
TILE_SAMPLE = 128
TQ = 512
ROUTE_CHUNK = 256
COMBINE_TILE = 128


def _cparams(sem, vmem=VMEM_LIMIT):
    return pltpu.CompilerParams(dimension_semantics=sem, vmem_limit_bytes=vmem)


def _dot(a, b):
    return jnp.dot(a, b, preferred_element_type=F32)


def _dot_nt(a, b):
    return lax.dot_general(a, b, (((1,), (1,)), ((), ())), preferred_element_type=F32)


def _rms(x, g):
    r = lax.rsqrt(jnp.mean(x * x, axis=-1, keepdims=True) + EPS)
    return (x * r) * g


def _ada_body(c_ref, w_ref, b_ref, o_ref):
    c = c_ref[...]
    a = (c * jax.nn.sigmoid(c)).astype(BF16)
    o_ref[...] = _dot(a, w_ref[...].astype(BF16)) + b_ref[...]


def _ada(c_all, w, b):
    m, d = c_all.shape
    n = w.shape[1]
    tn = 1024
    return pl.pallas_call(
        _ada_body,
        grid=(n // tn,),
        in_specs=[pl.BlockSpec((m, d), lambda j: (0, 0)),
                  pl.BlockSpec((d, tn), lambda j: (0, j)),
                  pl.BlockSpec((1, tn), lambda j: (0, j))],
        out_specs=pl.BlockSpec((m, tn), lambda j: (0, j)),
        out_shape=jax.ShapeDtypeStruct((m, n), F32),
        compiler_params=_cparams(("arbitrary",)),
        name="ada",
    )(c_all, w, b.reshape(1, n))


def _rope_lanes(x, cos, sin_up, sin_dn):
    w = x.shape[1]
    up = pltpu.roll(x, PE_HALF, 1)
    dn = pltpu.roll(x, w - PE_HALF, 1)
    return x * cos + up * sin_up + dn * sin_dn


def _mix_in_body(*refs, prompt, tm, tiles_per_seq, t_new):
    if prompt:
        (x_ref, sc_ref, sh_ref, g1_ref, win_ref, gq_ref, wq_ref, gkv_ref, wuk_ref, wuv_ref,
         cos_ref, sup_ref, sdn_ref, wconv_ref, gconv_ref, gmat_ref,
         q_ref, k_ref, v_ref, ckv_ref, kpe_ref, z_ref, cst_ref, carry_ref) = refs
    else:
        (x_ref, sc_ref, sh_ref, g1_ref, win_ref, gq_ref, wq_ref, gkv_ref,
         cos_ref, sup_ref, sdn_ref, wconv_ref, gconv_ref, gmat_ref, pa_ref, pb_ref,
         q_ref, ckv_ref, kpe_ref, z_ref, u_ref) = refs

    x = x_ref[...]
    h = _rms(x, g1_ref[...]) * (1.0 + sc_ref[...]) + sh_ref[...]
    proj = _dot(h.astype(BF16), win_ref[...])
    o0 = Q_LORA
    o1 = o0 + KV_LORA
    o2 = o1 + HEAD_PAD
    o3 = o2 + CONV_DIM
    o4 = o3 + CONV_DIM
    q_a, kv_a, kpe_blk = proj[:, :o0], proj[:, o0:o1], proj[:, o1:o2]
    b_g, c_g, u_in = proj[:, o2:o3], proj[:, o3:o4], proj[:, o4:]

    cos, sup, sdn = cos_ref[...], sup_ref[...], sdn_ref[...]
    cos8 = jnp.concatenate([cos] * N_HEADS, axis=1)
    sup8 = jnp.concatenate([sup] * N_HEADS, axis=1)
    sdn8 = jnp.concatenate([sdn] * N_HEADS, axis=1)

    qn = _rms(q_a, gq_ref[...]).astype(BF16)
    q = _dot(qn, wq_ref[...]) * ((NOPE_DIM + ROPE_DIM) ** -0.5)
    q_ref[...] = _rope_lanes(q, cos8, sup8, sdn8).astype(BF16)

    ckv = _rms(kv_a, gkv_ref[...])
    ckv_ref[...] = ckv
    kpe = _rope_lanes(kpe_blk, cos, sup, sdn)
    kpe_ref[...] = kpe[:, PE_LO:PE_LO + ROPE_DIM]

    if prompt:
        ckvb = ckv.astype(BF16)
        k = _dot(ckvb, wuk_ref[...]) + jnp.concatenate([kpe] * N_HEADS, axis=1)
        k_ref[...] = k.astype(BF16)
        v_ref[...] = _dot(ckvb, wuv_ref[...]).astype(BF16)

    u = c_g * u_in
    row = lax.broadcasted_iota(jnp.int32, (tm, 1), 0)
    r1 = pltpu.roll(u, 1, 0)
    r2 = pltpu.roll(u, 2, 0)
    if prompt:
        @pl.when(pl.program_id(0) % tiles_per_seq == 0)
        def _():
            carry_ref[...] = jnp.zeros_like(carry_ref)
        c6 = carry_ref[6:7, :]
        c7 = carry_ref[7:8, :]
        um1 = jnp.where(row == 0, c7, r1)
        um2 = jnp.where(row == 0, c6, jnp.where(row == 1, c7, r2))
        carry_ref[...] = u[tm - 8:, :]
        cst_ref[...] = u[tm - (CONV_W - 1):, :]
    else:
        t = row % t_new
        um1 = jnp.where(t == 0, pa_ref[...], r1)
        um2 = jnp.where(t == 0, pb_ref[...], jnp.where(t == 1, pa_ref[...], r2))
        u_ref[...] = u
    wc = wconv_ref[...]
    y = um2 * wc[0:1, :] + um1 * wc[1:2, :] + u * wc[2:3, :]
    zz = b_g * y
    ms = _dot((zz * zz).astype(BF16), gmat_ref[...]) * (1.0 / CONV_GROUP_DIM)
    z_ref[...] = (zz * lax.rsqrt(ms + EPS) * gconv_ref[...]).astype(BF16)


def _mix_in(x2d, sc, sh, p, rope_tabs, *, prompt, seq_len=None, prev_a=None, prev_b=None, t_new=1):
    n = x2d.shape[0]
    tm = TILE_PROMPT if prompt else TILE_SAMPLE
    nt = n // tm
    cos_t, sup_t, sdn_t = rope_tabs
    full = lambda a: pl.BlockSpec(a.shape, lambda i: (0,) * a.ndim)
    row_tile = lambda w: pl.BlockSpec((tm, w), lambda i: (i, 0))
    if prompt:
        tps = seq_len // tm
        mod_spec = pl.BlockSpec((None, 1, D_MODEL), lambda i: (i // tps, 0, 0))
        tab_spec = pl.BlockSpec((tm, LANES), lambda i: (i % tps, 0))
        nb = n // seq_len
        ins = [x2d, sc, sh, p["g1"], p["w_in"], p["g_q"], p["wq"], p["g_kv"], p["wuk"], p["wuv"],
               cos_t, sup_t, sdn_t, p["w_conv"], p["g_conv"], p["gmat"]]
        in_specs = [row_tile(D_MODEL), mod_spec, mod_spec, full(p["g1"]), full(p["w_in"]), full(p["g_q"]),
                    full(p["wq"]), full(p["g_kv"]), full(p["wuk"]), full(p["wuv"]),
                    tab_spec, tab_spec, tab_spec, full(p["w_conv"]), full(p["g_conv"]), full(p["gmat"])]
        out_shape = [jax.ShapeDtypeStruct((n, QK_WIDTH), BF16), jax.ShapeDtypeStruct((n, QK_WIDTH), BF16),
                     jax.ShapeDtypeStruct((n, ATTN_WIDTH), BF16), jax.ShapeDtypeStruct((n, KV_LORA), F32),
                     jax.ShapeDtypeStruct((n, ROPE_DIM), F32), jax.ShapeDtypeStruct((n, CONV_DIM), BF16),
                     jax.ShapeDtypeStruct((nb, CONV_W - 1, CONV_DIM), F32)]
        out_specs = [row_tile(QK_WIDTH), row_tile(QK_WIDTH), row_tile(ATTN_WIDTH), row_tile(KV_LORA),
                     row_tile(ROPE_DIM), row_tile(CONV_DIM),
                     pl.BlockSpec((None, CONV_W - 1, CONV_DIM), lambda i: (i // tps, 0, 0))]
        scratch = [pltpu.VMEM((8, CONV_DIM), F32)]
    else:
        tps = 1
        ins = [x2d, sc, sh, p["g1"], p["w_in"], p["g_q"], p["wq"], p["g_kv"],
               cos_t, sup_t, sdn_t, p["w_conv"], p["g_conv"], p["gmat"], prev_a, prev_b]
        in_specs = [row_tile(D_MODEL), row_tile(D_MODEL), row_tile(D_MODEL), full(p["g1"]), full(p["w_in"]),
                    full(p["g_q"]), full(p["wq"]), full(p["g_kv"]),
                    row_tile(LANES), row_tile(LANES), row_tile(LANES), full(p["w_conv"]), full(p["g_conv"]),
                    full(p["gmat"]), row_tile(CONV_DIM), row_tile(CONV_DIM)]
        out_shape = [jax.ShapeDtypeStruct((n, QK_WIDTH), BF16), jax.ShapeDtypeStruct((n, KV_LORA), F32),
                     jax.ShapeDtypeStruct((n, ROPE_DIM), F32), jax.ShapeDtypeStruct((n, CONV_DIM), BF16),
                     jax.ShapeDtypeStruct((n, CONV_DIM), F32)]
        out_specs = [row_tile(QK_WIDTH), row_tile(KV_LORA), row_tile(ROPE_DIM), row_tile(CONV_DIM),
                     row_tile(CONV_DIM)]
        scratch = []
    return pl.pallas_call(
        functools.partial(_mix_in_body, prompt=prompt, tm=tm, tiles_per_seq=tps, t_new=t_new),
        grid=(nt,), in_specs=in_specs, out_specs=out_specs, out_shape=out_shape,
        scratch_shapes=scratch, compiler_params=_cparams(("arbitrary",)),
        name="mix_in_prompt" if prompt else "mix_in_sample",
    )(*ins)


def _flash_body(q_ref, k_ref, v_ref, g_ref, gm_ref, o_ref, *, tq):
    i = pl.program_id(2)
    rowi = lax.broadcasted_iota(jnp.int32, (tq, tq), 0)
    coli = lax.broadcasted_iota(jnp.int32, (tq, tq), 1)
    causal = coli <= rowi
    outs = []
    for hh in range(2):
        qh = q_ref[:, hh * HEAD_PAD:(hh + 1) * HEAD_PAD]

        def step(j, carry, masked, hh=hh, qh=qh):
            m, l, acc = carry
            ks = pl.multiple_of(j * tq, tq)
            kt = k_ref[pl.ds(ks, tq), hh * HEAD_PAD:(hh + 1) * HEAD_PAD]
            vt = v_ref[pl.ds(ks, tq), :]
            s = _dot_nt(qh, kt)
            if masked:
                s = jnp.where(causal, s, -jnp.inf)
            m_new = jnp.maximum(m, jnp.max(s, axis=1, keepdims=True))
            alpha = jnp.exp(m - m_new)
            pr = jnp.exp(s - m_new)
            l = alpha * l + jnp.sum(pr, axis=1, keepdims=True)
            acc = alpha * acc + _dot(pr.astype(BF16), vt)
            return m_new, l, acc

        init = (jnp.full((tq, 1), -jnp.inf, F32), jnp.zeros((tq, 1), F32), jnp.zeros((tq, LANES), F32))
        carry = lax.fori_loop(0, i, functools.partial(step, masked=False), init)
        m, l, acc = step(i, carry, True)
        outs.append(acc / l)
    lane = lax.broadcasted_iota(jnp.int32, (tq, LANES), 1)
    o = jnp.where(lane < V_DIM, outs[0], outs[1])
    ms = _dot((o * o).astype(BF16), gm_ref[...]) * (1.0 / V_DIM)
    o_ref[...] = (o * lax.rsqrt(ms + EPS) * g_ref[...]).astype(BF16)


def _flash(q, k, v, g_attn, gm_head, batch, seq_len):
    n = q.shape[0]
    nq = seq_len // TQ
    return pl.pallas_call(
        functools.partial(_flash_body, tq=TQ),
        grid=(batch, N_HEADS // 2, nq),
        in_specs=[pl.BlockSpec((TQ, 2 * HEAD_PAD), lambda b, p, i: (b * nq + i, p)),
                  pl.BlockSpec((seq_len, 2 * HEAD_PAD), lambda b, p, i: (b, p)),
                  pl.BlockSpec((seq_len, 2 * V_DIM), lambda b, p, i: (b, p)),
                  pl.BlockSpec((1, 2 * V_DIM), lambda b, p, i: (0, p)),
                  pl.BlockSpec((LANES, LANES), lambda b, p, i: (0, 0))],
        out_specs=pl.BlockSpec((TQ, 2 * V_DIM), lambda b, p, i: (b * nq + i, p)),
        out_shape=jax.ShapeDtypeStruct((n, ATTN_WIDTH), BF16),
        compiler_params=_cparams(("arbitrary", "arbitrary", "arbitrary")),
        name="flash_prompt",
    )(q, k, v, g_attn, gm_head)


def _absorb_body(q_ref, w_ref, ql_ref, qp_ref):
    qb = q_ref[...]
    ql_ref[...] = _dot(qb, w_ref[...]).astype(BF16)
    qp_ref[...] = qb[:, PE_LO:PE_LO + ROPE_DIM]


def _absorb(q_s, wuk_t):
    n = q_s.shape[0]
    return pl.pallas_call(
        _absorb_body,
        grid=(N_HEADS,),
        in_specs=[pl.BlockSpec((n, HEAD_PAD), lambda h: (0, h)),
                  pl.BlockSpec((None, HEAD_PAD, KV_LORA), lambda h: (h, 0, 0))],
        out_specs=[pl.BlockSpec((None, n, KV_LORA), lambda h: (h, 0, 0)),
                   pl.BlockSpec((None, n, ROPE_DIM), lambda h: (h, 0, 0))],
        out_shape=[jax.ShapeDtypeStruct((N_HEADS, n, KV_LORA), BF16),
                   jax.ShapeDtypeStruct((N_HEADS, n, ROPE_DIM), BF16)],
        compiler_params=_cparams(("arbitrary",)),
        name="absorb",
    )(q_s, wuk_t)


PAGES_PER_CHUNK = 8


def _paged_body(pt_ref, ql_ref, qp_ref, cn_ref, kn_ref, cckv_ref, ckpe_ref, o_ref,
                ckv_buf, kpe_buf, sems, *, n_pages, n_seq, t_new):
    b = pl.program_id(0)
    slot = b % 2

    def fetch(seq, sl):
        def body(pg, _):
            page = pt_ref[seq, pg]
            pltpu.make_async_copy(cckv_ref.at[0, page], ckv_buf.at[sl, pg], sems.at[0, sl]).start()
            pltpu.make_async_copy(ckpe_ref.at[0, page], kpe_buf.at[sl, pg], sems.at[1, sl]).start()
            return 0
        lax.fori_loop(0, n_pages, body, 0)

    @pl.when(b == 0)
    def _():
        fetch(0, 0)

    @pl.when(b + 1 < n_seq)
    def _():
        fetch(b + 1, 1 - slot)

    pltpu.make_async_copy(cckv_ref.at[0, pl.ds(0, n_pages)], ckv_buf.at[slot], sems.at[0, slot]).wait()
    pltpu.make_async_copy(ckpe_ref.at[0, pl.ds(0, n_pages)], kpe_buf.at[slot], sems.at[1, slot]).wait()

    ql = ql_ref[...]
    qp = qp_ref[...]
    rows = ql.shape[0]
    ck = PAGES_PER_CHUNK * PAGE_SIZE

    def step(c, carry):
        m, l, acc = carry
        p0 = pl.multiple_of(c * PAGES_PER_CHUNK, PAGES_PER_CHUNK)
        kc = ckv_buf[slot, pl.ds(p0, PAGES_PER_CHUNK)].reshape(ck, KV_LORA).astype(BF16)
        pc = kpe_buf[slot, pl.ds(p0, PAGES_PER_CHUNK)].astype(BF16)
        s_pe = jnp.concatenate([_dot(qp, pc[j]) for j in range(PAGES_PER_CHUNK)], axis=1)
        s = _dot_nt(ql, kc) + s_pe
        m_new = jnp.maximum(m, jnp.max(s, axis=1, keepdims=True))
        alpha = jnp.exp(m - m_new)
        pr = jnp.exp(s - m_new)
        l = alpha * l + jnp.sum(pr, axis=1, keepdims=True)
        acc = alpha * acc + _dot(pr.astype(BF16), kc)
        return m_new, l, acc

    init = (jnp.full((rows, 1), -jnp.inf, F32), jnp.zeros((rows, 1), F32), jnp.zeros((rows, KV_LORA), F32))
    m, l, acc = lax.fori_loop(0, n_pages // PAGES_PER_CHUNK, step, init)

    qlf = ql.astype(F32)
    qpf = qp.astype(F32)
    cn = cn_ref[...]
    kn = kn_ref[...]
    trow = lax.broadcasted_iota(jnp.int32, (rows, 1), 0) % t_new
    s_new = []
    for j in range(t_new):
        sj = (jnp.sum(qlf * cn[j:j + 1, :], axis=1, keepdims=True)
              + jnp.sum(qpf * kn[j:j + 1, :], axis=1, keepdims=True))
        s_new.append(jnp.where(trow >= j, sj, -jnp.inf))
    m_new = m
    for sj in s_new:
        m_new = jnp.maximum(m_new, sj)
    alpha = jnp.exp(m - m_new)
    l = alpha * l
    acc = alpha * acc
    for j in range(t_new):
        pj = jnp.exp(s_new[j] - m_new)
        l = l + pj
        acc = acc + pj * cn[j:j + 1, :]
    o_ref[...] = acc / l


def _paged(page_table, ql, qp, ckv_new, kpe_new, cache_ckv, cache_kpe):
    n_seq, n_pages = page_table.shape
    rows = ql.shape[1]
    t_new = ckv_new.shape[1]
    grid_spec = pltpu.PrefetchScalarGridSpec(
        num_scalar_prefetch=1,
        grid=(n_seq,),
        in_specs=[pl.BlockSpec((None, rows, KV_LORA), lambda b, pt: (b, 0, 0)),
                  pl.BlockSpec((None, rows, ROPE_DIM), lambda b, pt: (b, 0, 0)),
                  pl.BlockSpec((None, t_new, KV_LORA), lambda b, pt: (b, 0, 0)),
                  pl.BlockSpec((None, t_new, ROPE_DIM), lambda b, pt: (b, 0, 0)),
                  pl.BlockSpec(memory_space=pl.ANY),
                  pl.BlockSpec(memory_space=pl.ANY)],
        out_specs=pl.BlockSpec((None, rows, KV_LORA), lambda b, pt: (b, 0, 0)),
        scratch_shapes=[pltpu.VMEM((2, n_pages, PAGE_SIZE, KV_LORA), F32),
                        pltpu.VMEM((2, n_pages, ROPE_DIM, PAGE_SIZE), F32),
                        pltpu.SemaphoreType.DMA((2, 2))],
    )
    return pl.pallas_call(
        functools.partial(_paged_body, n_pages=n_pages, n_seq=n_seq, t_new=t_new),
        grid_spec=grid_spec,
        out_shape=jax.ShapeDtypeStruct((n_seq, rows, KV_LORA), F32),
        compiler_params=_cparams(("arbitrary",)),
        name="paged_attn",
    )(page_table, ql, qp, ckv_new, kpe_new, cache_ckv, cache_kpe)


def _unabsorb_body(o_ref, w_ref, g_ref, out_ref):
    o = _dot(o_ref[...].astype(BF16), w_ref[...])
    r = lax.rsqrt(jnp.sum(o * o, axis=-1, keepdims=True) * (1.0 / V_DIM) + EPS)
    out_ref[...] = (o * r * g_ref[...]).astype(BF16)


def _unabsorb(o_lat_h, wuv_h, g_h):
    n = o_lat_h.shape[1]
    return pl.pallas_call(
        _unabsorb_body,
        grid=(N_HEADS,),
        in_specs=[pl.BlockSpec((None, n, KV_LORA), lambda h: (h, 0, 0)),
                  pl.BlockSpec((None, KV_LORA, HEAD_PAD), lambda h: (h, 0, 0)),
                  pl.BlockSpec((None, 1, HEAD_PAD), lambda h: (h, 0, 0))],
        out_specs=pl.BlockSpec((None, n, HEAD_PAD), lambda h: (h, 0, 0)),
        out_shape=jax.ShapeDtypeStruct((N_HEADS, n, HEAD_PAD), BF16),
        compiler_params=_cparams(("arbitrary",)),
        name="unabsorb",
    )(o_lat_h, wuv_h, g_h)


def _post_mix_body(x_ref, o_ref, z_ref, gt1_ref, sc_ref, sh_ref, gt2_ref, wo_ref, g2_ref,
                   wr_ref, eb_ref, wsgu_ref, wsd_ref, tri_ref, cin_ref,
                   xs_ref, h2_ref, idx_ref, gw_ref, rank_ref, cout_ref, cnt_ref, *, tm):
    i = pl.program_id(0)

    @pl.when(i == 0)
    def _():
        cnt_ref[...] = cin_ref[...]

    wo = wo_ref[...]
    mix = _dot(o_ref[...], wo[:ATTN_WIDTH, :]) + _dot(z_ref[...], wo[ATTN_WIDTH:, :])
    x1 = x_ref[...] + gt1_ref[...] * mix
    h2 = _rms(x1, g2_ref[...]) * (1.0 + sc_ref[...]) + sh_ref[...]
    h2_ref[...] = h2
    hb = h2.astype(BF16)

    gu = _dot(hb, wsgu_ref[...])
    gate, up = gu[:, :SHARED_FF], gu[:, SHARED_FF:]
    shared = _dot((gate * jax.nn.sigmoid(gate) * up).astype(BF16), wsd_ref[...])
    xs_ref[...] = x1 + gt2_ref[...] * shared

    s = jax.nn.sigmoid(_dot_nt(wr_ref[...], hb))
    biased = s + eb_ref[...]
    ninf = -jnp.inf
    gi = lax.broadcasted_iota(jnp.int32, (GROUP_SIZE, tm), 0).astype(F32)
    gs = []
    for g in range(N_GROUPS):
        blk = biased[g * GROUP_SIZE:(g + 1) * GROUP_SIZE, :]
        m1 = jnp.max(blk, axis=0, keepdims=True)
        i1 = jnp.min(jnp.where(blk == m1, gi, float(GROUP_SIZE)), axis=0, keepdims=True)
        m2 = jnp.max(jnp.where(gi == i1, ninf, blk), axis=0, keepdims=True)
        gs.append(m1 + m2)
    gscore = jnp.concatenate(gs, axis=0)
    giota = lax.broadcasted_iota(jnp.int32, (N_GROUPS, tm), 0).astype(F32)
    gsel = jnp.zeros((N_GROUPS, tm), F32)
    for _ in range(TOPK_GROUPS):
        gm = jnp.max(gscore, axis=0, keepdims=True)
        gidx = jnp.min(jnp.where(gscore == gm, giota, float(N_GROUPS)), axis=0, keepdims=True)
        hit = giota == gidx
        gsel = jnp.where(hit, 1.0, gsel)
        gscore = jnp.where(hit, ninf, gscore)
    masked = jnp.concatenate(
        [jnp.where(gsel[g:g + 1, :] > 0.0, biased[g * GROUP_SIZE:(g + 1) * GROUP_SIZE, :], ninf)
         for g in range(N_GROUPS)], axis=0)
    eiota = lax.broadcasted_iota(jnp.int32, (N_EXPERTS, tm), 0).astype(F32)
    idxs, ws = [], []
    selall = jnp.zeros((N_EXPERTS, tm), F32)
    for _ in range(TOP_K):
        mx = jnp.max(masked, axis=0, keepdims=True)
        ei = jnp.min(jnp.where(masked == mx, eiota, float(N_EXPERTS)), axis=0, keepdims=True)
        hit = eiota == ei
        ws.append(jnp.sum(jnp.where(hit, s, 0.0), axis=0, keepdims=True))
        idxs.append(ei)
        selall = jnp.where(hit, 1.0, selall)
        masked = jnp.where(hit, ninf, masked)
    wsum = ws[0]
    for w in ws[1:]:
        wsum = wsum + w
    gw_ref[...] = jnp.concatenate(ws, axis=0) / wsum * ROUTED_SCALE
    idx_ref[...] = jnp.concatenate(idxs, axis=0).astype(jnp.int32)

    before = _dot(selall.astype(BF16), tri_ref[...]) + cnt_ref[:, 0:1]
    ranks = [jnp.sum(jnp.where(eiota == ei, before, 0.0), axis=0, keepdims=True) for ei in idxs]
    rank_ref[...] = jnp.concatenate(ranks, axis=0).astype(jnp.int32)
    cnt_ref[...] = cnt_ref[...] + jnp.sum(selall, axis=1, keepdims=True)
    cout_ref[...] = cnt_ref[...]


def _post_mix(x2d, o, z, gt1, sc2, sh2, gt2, p, cnt_in, h2_buf, *, prompt, seq_len, row_off, n_total):
    n = x2d.shape[0]
    tm = TILE_PROMPT if prompt else TILE_SAMPLE
    nt = n // tm
    full = lambda a: pl.BlockSpec(a.shape, lambda i: (0,) * a.ndim)
    row_tile = lambda w: pl.BlockSpec((tm, w), lambda i: (i, 0))
    if prompt:
        tps = seq_len // tm
        mod_spec = pl.BlockSpec((None, 1, D_MODEL), lambda i: (i // tps, 0, 0))
    else:
        mod_spec = row_tile(D_MODEL)
    tri = p["tri_p"] if prompt else p["tri_s"]
    boff = row_off // tm
    ins = [x2d, o, z, gt1, sc2, sh2, gt2, p["w_out"], p["g2"], p["wr_t"], p["e_bias"], p["ws_gu"], p["ws_d"],
           tri, cnt_in]
    in_specs = [row_tile(D_MODEL), row_tile(ATTN_WIDTH), row_tile(CONV_DIM), mod_spec, mod_spec, mod_spec,
                mod_spec, full(p["w_out"]), full(p["g2"]), full(p["wr_t"]), full(p["e_bias"]), full(p["ws_gu"]),
                full(p["ws_d"]), full(tri), full(cnt_in)]
    out_shape = [jax.ShapeDtypeStruct((n, D_MODEL), F32), jax.ShapeDtypeStruct((n_total, D_MODEL), F32),
                 jax.ShapeDtypeStruct((TOP_K, n), jnp.int32), jax.ShapeDtypeStruct((TOP_K, n), F32),
                 jax.ShapeDtypeStruct((TOP_K, n), jnp.int32), jax.ShapeDtypeStruct((N_EXPERTS, LANES), F32)]
    col_tile = pl.BlockSpec((TOP_K, tm), lambda i: (0, i))
    out_specs = [row_tile(D_MODEL), pl.BlockSpec((tm, D_MODEL), lambda i: (i + boff, 0)),
                 col_tile, col_tile, col_tile, pl.BlockSpec((N_EXPERTS, LANES), lambda i: (0, 0))]
    aliases = {}
    if h2_buf is not None:
        ins.append(h2_buf)
        in_specs.append(pl.BlockSpec(memory_space=pl.ANY))
        aliases = {len(ins) - 1: 1}
        body = lambda *refs: _post_mix_body(*refs[:15], *refs[16:], tm=tm)
    else:
        body = functools.partial(_post_mix_body, tm=tm)
    return pl.pallas_call(
        body, grid=(nt,), in_specs=in_specs, out_specs=out_specs, out_shape=out_shape,
        scratch_shapes=[pltpu.VMEM((N_EXPERTS, LANES), F32)],
        input_output_aliases=aliases,
        compiler_params=_cparams(("arbitrary",)),
        name="post_mix_prompt" if prompt else "post_mix_sample",
    )(*ins)


def _dispatch_body(d_ref, h_ref, xs_ref, stage, sem, *, tm, nt):
    i = pl.program_id(0)
    slot = i % 2
    stage[slot] = h_ref[...]

    def body(r, _):
        for k in range(TOP_K):
            d = d_ref[0, 0, k * tm + r]
            pltpu.make_async_copy(stage.at[slot, pl.ds(r, 1)], xs_ref.at[pl.ds(d, 1)], sem.at[slot]).start()
        return 0
    lax.fori_loop(0, tm, body, 0)

    def drain(sl):
        for _ in range(TOP_K):
            pltpu.make_async_copy(stage.at[sl], xs_ref.at[pl.ds(0, tm)], sem.at[sl]).wait()

    @pl.when(i > 0)
    def _():
        drain(1 - slot)

    @pl.when(i == nt - 1)
    def _():
        drain(slot)


def _dispatch(dest_tiles, h2, n_rows):
    nt, _, w = dest_tiles.shape
    tm = w // TOP_K
    return pl.pallas_call(
        functools.partial(_dispatch_body, tm=tm, nt=nt),
        grid=(nt,),
        in_specs=[pl.BlockSpec((1, 1, w), lambda i: (i, 0, 0), memory_space=pltpu.SMEM),
                  pl.BlockSpec((tm, D_MODEL), lambda i: (i, 0))],
        out_specs=pl.BlockSpec(memory_space=pl.ANY),
        out_shape=jax.ShapeDtypeStruct((n_rows, D_MODEL), F32),
        scratch_shapes=[pltpu.VMEM((2, tm, D_MODEL), F32), pltpu.SemaphoreType.DMA((2,))],
        compiler_params=_cparams(("arbitrary",)),
        name="dispatch",
    )(dest_tiles, h2)


def _experts_body(be_ref, nv_ref, nu_ref, x_ref, wg_ref, wu_ref, wd_ref, y_ref, wgb, wub, wdb, *, ch):
    c = pl.program_id(0)
    prev_e = be_ref[jnp.maximum(c - 1, 0)]

    @pl.when(jnp.logical_or(c == 0, be_ref[c] != prev_e))
    def _():
        wgb[...] = wg_ref[...].astype(BF16)
        wub[...] = wu_ref[...].astype(BF16)
        wdb[...] = wd_ref[...].astype(BF16)

    @pl.when(c < nu_ref[0])
    def _():
        row = lax.broadcasted_iota(jnp.int32, (ch, 1), 0)
        xb = jnp.where(row < nv_ref[c], x_ref[...], 0.0).astype(BF16)
        gate = _dot(xb, wgb[...])
        up = _dot(xb, wub[...])
        act = (gate * jax.nn.sigmoid(gate) * up).astype(BF16)
        y_ref[...] = _dot(act, wdb[...])


def _experts(block_e, n_valid, n_used, x_sorted, w_e_gate, w_e_up, w_e_down):
    ch = ROUTE_CHUNK
    nch = x_sorted.shape[0] // ch
    last = lambda c, nu: jnp.minimum(c, jnp.maximum(nu[0] - 1, 0))
    grid_spec = pltpu.PrefetchScalarGridSpec(
        num_scalar_prefetch=3,
        grid=(nch,),
        in_specs=[pl.BlockSpec((ch, D_MODEL), lambda c, be, nv, nu: (last(c, nu), 0)),
                  pl.BlockSpec((None, D_MODEL, E_FF), lambda c, be, nv, nu: (be[c], 0, 0)),
                  pl.BlockSpec((None, D_MODEL, E_FF), lambda c, be, nv, nu: (be[c], 0, 0)),
                  pl.BlockSpec((None, E_FF, D_MODEL), lambda c, be, nv, nu: (be[c], 0, 0))],
        out_specs=pl.BlockSpec((ch, D_MODEL), lambda c, be, nv, nu: (last(c, nu), 0)),
        scratch_shapes=[pltpu.VMEM((D_MODEL, E_FF), BF16), pltpu.VMEM((D_MODEL, E_FF), BF16),
                        pltpu.VMEM((E_FF, D_MODEL), BF16)],
    )
    return pl.pallas_call(
        functools.partial(_experts_body, ch=ch),
        grid_spec=grid_spec,
        out_shape=jax.ShapeDtypeStruct((nch * ch, D_MODEL), F32),
        compiler_params=_cparams(("arbitrary",)),
        name="experts",
    )(block_e, n_valid, n_used, x_sorted, w_e_gate, w_e_up, w_e_down)


def _combine_body(dc_ref, dn_ref, y_ref, xs_ref, gw_ref, gt2_ref, gf_ref, fsc_ref, fsh_ref, out_ref,
                  gbuf, sem, *, tm, nt):
    i = pl.program_id(0)
    slot = i % 2

    def issue(d_ref, sl):
        def body(r, _):
            for k in range(TOP_K):
                d = d_ref[0, 0, k * tm + r]
                pltpu.make_async_copy(y_ref.at[pl.ds(d, 1)], gbuf.at[sl, k, pl.ds(r, 1)], sem.at[sl]).start()
            return 0
        lax.fori_loop(0, tm, body, 0)

    @pl.when(i == 0)
    def _():
        issue(dc_ref, 0)

    @pl.when(i + 1 < nt)
    def _():
        issue(dn_ref, 1 - slot)

    for k in range(TOP_K):
        pltpu.make_async_copy(y_ref.at[pl.ds(0, tm)], gbuf.at[slot, k], sem.at[slot]).wait()
    gw = gw_ref[...]
    f = gw[:, 0:1] * gbuf[slot, 0]
    for k in range(1, TOP_K):
        f = f + gw[:, k:k + 1] * gbuf[slot, k]
    x2 = xs_ref[...] + gt2_ref[...] * f
    out_ref[...] = _rms(x2, gf_ref[...]) * (1.0 + fsc_ref[...]) + fsh_ref[...]


def _combine(dest_tiles, y_sorted, xs, gw, gt2, g_final, fsc, fsh, *, prompt, seq_len):
    n = xs.shape[0]
    tm = COMBINE_TILE
    nt = n // tm
    row_tile = lambda w: pl.BlockSpec((tm, w), lambda i: (i, 0))
    if prompt:
        tps = seq_len // tm
        mod_spec = pl.BlockSpec((None, 1, D_MODEL), lambda i: (i // tps, 0, 0))
    else:
        mod_spec = row_tile(D_MODEL)
    return pl.pallas_call(
        functools.partial(_combine_body, tm=tm, nt=nt),
        grid=(nt,),
        in_specs=[pl.BlockSpec((1, 1, TOP_K * tm), lambda i: (i, 0, 0), memory_space=pltpu.SMEM),
                  pl.BlockSpec((1, 1, TOP_K * tm), lambda i: (jnp.minimum(i + 1, nt - 1), 0, 0),
                               memory_space=pltpu.SMEM),
                  pl.BlockSpec(memory_space=pl.ANY),
                  row_tile(D_MODEL), row_tile(TOP_K), mod_spec,
                  pl.BlockSpec((1, D_MODEL), lambda i: (0, 0)), mod_spec, mod_spec],
        out_specs=row_tile(D_MODEL),
        out_shape=jax.ShapeDtypeStruct((n, D_MODEL), F32),
        scratch_shapes=[pltpu.VMEM((2, TOP_K, tm, D_MODEL), F32), pltpu.SemaphoreType.DMA((2,))],
        compiler_params=_cparams(("arbitrary",)),
        name="combine_prompt" if prompt else "combine_sample",
    )(dest_tiles, dest_tiles, y_sorted, xs, gw, gt2, g_final, fsc, fsh)


def _rope_tables(pos):
    inv = 1.0 / (ROPE_THETA ** (jnp.arange(PE_HALF, dtype=F32) / PE_HALF))
    ang = pos.astype(F32)[:, None] * inv[None, :]
    cos, sin = jnp.cos(ang), jnp.sin(ang)
    n = pos.shape[0]
    ones = jnp.ones((n, PE_LO), F32)
    zeros = jnp.zeros((n, PE_LO), F32)
    zh = jnp.zeros((n, PE_HALF), F32)
    tail1 = jnp.ones((n, LANES - PE_LO - ROPE_DIM), F32)
    tail0 = jnp.zeros((n, LANES - PE_LO - ROPE_DIM), F32)
    cos_t = jnp.concatenate([ones, cos, cos, tail1], axis=1)
    sin_up = jnp.concatenate([zeros, zh, sin, tail0], axis=1)
    sin_dn = jnp.concatenate([zeros, -sin, zh, tail0], axis=1)
    return cos_t, sin_up, sin_dn


def _pad_heads(w, width):
    pad = [(0, 0)] * (w.ndim - 1) + [(0, HEAD_PAD - width)]
    w = jnp.pad(w, pad)
    return w.reshape(w.shape[:-2] + (N_HEADS * HEAD_PAD,))


def _block_diag_ones(n, blk):
    r = jnp.arange(n) // blk
    return (r[:, None] == r[None, :]).astype(BF16)


def _prep_weights(w_in, g_attn_norm, g_q, w_q_up, g_kv, w_uk, w_uv, w_conv, g_attn_out, g_conv_out, w_out,
                  g_ffn_norm, w_router, e_bias, w_s_gate, w_s_up, w_s_down):
    o0 = Q_LORA
    o1 = o0 + KV_LORA
    o2 = o1 + ROPE_DIM
    kpe_cols = jnp.pad(w_in[:, o1:o2], ((0, 0), (PE_LO, HEAD_PAD - PE_LO - ROPE_DIM)))
    w_in_p = jnp.concatenate([w_in[:, :o1], kpe_cols, w_in[:, o2:]], axis=1).astype(BF16)
    wq = _pad_heads(w_q_up.reshape(Q_LORA, N_HEADS, NOPE_DIM + ROPE_DIM), NOPE_DIM + ROPE_DIM).astype(BF16)
    wuk = _pad_heads(w_uk, NOPE_DIM).astype(BF16)
    wuv = w_uv.reshape(KV_LORA, ATTN_WIDTH).astype(BF16)
    wuk_t = jnp.pad(jnp.transpose(w_uk, (1, 2, 0)), ((0, 0), (0, HEAD_PAD - NOPE_DIM), (0, 0))).astype(BF16)
    wuv_h = jnp.pad(jnp.transpose(w_uv, (1, 0, 2)), ((0, 0), (0, 0), (0, HEAD_PAD - V_DIM))).astype(BF16)
    g_attn_h = jnp.pad(g_attn_out.reshape(N_HEADS, 1, V_DIM), ((0, 0), (0, 0), (0, HEAD_PAD - V_DIM)))
    tri = lambda t: (jnp.arange(t)[:, None] < jnp.arange(t)[None, :]).astype(BF16)
    return {
        "g1": g_attn_norm.reshape(1, -1), "w_in": w_in_p, "g_q": g_q.reshape(1, -1), "wq": wq,
        "g_kv": g_kv.reshape(1, -1), "wuk": wuk, "wuv": wuv, "wuk_t": wuk_t, "wuv_h": wuv_h,
        "w_conv": w_conv, "g_conv": g_conv_out.reshape(1, -1), "gmat": _block_diag_ones(CONV_DIM, CONV_GROUP_DIM),
        "g_attn": g_attn_out.reshape(1, -1), "g_attn_h": g_attn_h, "gm_head": _block_diag_ones(LANES, V_DIM),
        "w_out": w_out.astype(BF16), "g2": g_ffn_norm.reshape(1, -1), "wr_t": w_router.T.astype(BF16),
        "e_bias": e_bias.reshape(-1, 1), "ws_gu": jnp.concatenate([w_s_gate, w_s_up], axis=1).astype(BF16),
        "ws_d": w_s_down.astype(BF16), "tri_p": tri(TILE_PROMPT), "tri_s": tri(TILE_SAMPLE),
    }


def kernel(x_prompt, x_sample, c_prompt, c_sample, cache_ckv, cache_kpe, state_conv, page_table, w_ada, b_ada, g_attn_norm, w_in, g_q, w_q_up, g_kv, w_uk, w_uv, w_conv, g_attn_out, g_conv_out, w_out, g_ffn_norm, w_router, e_bias, w_e_gate, w_e_up, w_e_down, w_s_gate, w_s_up, w_s_down, w_ada_final, b_ada_final, g_final):
    assert w_ada.shape[0] == 1, "one layer"
    bsz, seq, d = x_prompt.shape
    nseq, t_new, _ = x_sample.shape
    n_p, n_s = bsz * seq, nseq * t_new
    n_tot = n_p + n_s
    past = page_table.shape[1] * PAGE_SIZE

    p = _prep_weights(w_in[0], g_attn_norm[0], g_q[0], w_q_up[0], g_kv[0], w_uk[0], w_uv[0], w_conv[0],
                      g_attn_out[0], g_conv_out[0], w_out[0], g_ffn_norm[0], w_router[0], e_bias[0],
                      w_s_gate[0], w_s_up[0], w_s_down[0])

    c_all = jnp.concatenate([c_prompt, c_sample], axis=0)
    mod = _ada(c_all, w_ada[0], b_ada[0])
    modf = _ada(c_all, w_ada_final, b_ada_final)
    sh1, sc1, gt1, sh2, sc2, gt2 = [mod[:, j * d:(j + 1) * d] for j in range(6)]
    fsh, fsc = modf[:, :d], modf[:, d:]
    per_batch = lambda a: a[:bsz].reshape(bsz, 1, d)
    per_token = lambda a: jnp.repeat(a[bsz:], t_new, axis=0)

    xp = x_prompt.reshape(n_p, d)
    xs_in = x_sample.reshape(n_s, d)
    tabs_p = _rope_tables(jnp.arange(seq, dtype=jnp.int32))
    tabs_s = tuple(jnp.tile(t, (nseq, 1)) for t in _rope_tables(past + jnp.arange(t_new, dtype=jnp.int32)))

    q_p, k_p, v_p, ckv_p, kpe_p, z_p, conv_p = _mix_in(
        xp, per_batch(sc1), per_batch(sh1), p, tabs_p, prompt=True, seq_len=seq)
    o_p = _flash(q_p, k_p, v_p, p["g_attn"], p["gm_head"], bsz, seq)

    prev = state_conv[0]
    prev_a = jnp.repeat(prev[:, 1, :], t_new, axis=0)
    prev_b = jnp.repeat(prev[:, 0, :], t_new, axis=0)
    q_s, ckv_s, kpe_s, z_s, u_s = _mix_in(
        xs_in, per_token(sc1), per_token(sh1), p, tabs_s, prompt=False, prev_a=prev_a, prev_b=prev_b,
        t_new=t_new)
    ql_h, qp_h = _absorb(q_s, p["wuk_t"])
    rows = N_HEADS * t_new
    to_seq = lambda a: jnp.transpose(a.reshape(N_HEADS, nseq, t_new, a.shape[-1]), (1, 0, 2, 3)).reshape(
        nseq, rows, a.shape[-1])
    o_lat = _paged(page_table, to_seq(ql_h), to_seq(qp_h), ckv_s.reshape(nseq, t_new, KV_LORA),
                   kpe_s.reshape(nseq, t_new, ROPE_DIM), cache_ckv, jnp.swapaxes(cache_kpe, 2, 3))
    o_lat_h = jnp.transpose(o_lat.reshape(nseq, N_HEADS, t_new, KV_LORA), (1, 0, 2, 3)).reshape(
        N_HEADS, n_s, KV_LORA)
    o_s_h = _unabsorb(o_lat_h, p["wuv_h"], p["g_attn_h"])
    o_s = jnp.transpose(o_s_h[:, :, :V_DIM], (1, 0, 2)).reshape(n_s, ATTN_WIDTH)

    cnt0 = jnp.zeros((N_EXPERTS, LANES), F32)
    xs_p, h2, idx_p, gw_p, rank_p, cnt_p = _post_mix(
        xp, o_p, z_p, per_batch(gt1), per_batch(sc2), per_batch(sh2), per_batch(gt2), p, cnt0, None,
        prompt=True, seq_len=seq, row_off=0, n_total=n_tot)
    xs_s, h2, idx_s, gw_s, rank_s, cnt_s = _post_mix(
        xs_in, o_s, z_s, per_token(gt1), per_token(sc2), per_token(sh2), per_token(gt2), p, cnt_p, h2,
        prompt=False, seq_len=None, row_off=n_p, n_total=n_tot)

    ch = ROUTE_CHUNK
    counts = cnt_s[:, 0].astype(jnp.int32)
    padded = (counts + ch - 1) // ch * ch
    pend = jnp.cumsum(padded)
    pstart = pend - padded
    idx_all = jnp.concatenate([idx_p, idx_s], axis=1)
    rank_all = jnp.concatenate([rank_p, rank_s], axis=1)
    eids = jnp.arange(N_EXPERTS, dtype=jnp.int32)
    lookup = lambda table, keys: jnp.sum(jnp.where(keys[..., None] == eids, table, 0), axis=-1)
    dest = lookup(pstart, idx_all) + rank_all
    nch = -(-(n_tot * TOP_K) // ch) + N_EXPERTS
    chunk_start = jnp.arange(nch, dtype=jnp.int32) * ch
    block_e = jnp.minimum(jnp.sum((pend[None, :] <= chunk_start[:, None]).astype(jnp.int32), axis=1),
                          N_EXPERTS - 1)
    n_valid = jnp.clip(lookup(pstart + counts, block_e) - chunk_start, 0, ch).astype(jnp.int32)
    n_used = (pend[-1] // ch).astype(jnp.int32).reshape(1)

    tmc = COMBINE_TILE

    def dest_tiles(dst):
        n = dst.shape[1]
        return jnp.transpose(dst.reshape(TOP_K, n // tmc, tmc), (1, 0, 2)).reshape(n // tmc, 1, TOP_K * tmc)

    x_sorted = _dispatch(dest_tiles(dest), h2, nch * ch)
    y_sorted = _experts(block_e, n_valid, n_used, x_sorted, w_e_gate[0], w_e_up[0], w_e_down[0])


    gfin = g_final.reshape(1, d)
    y_p = _combine(dest_tiles(dest[:, :n_p]), y_sorted, xs_p, gw_p.T, per_batch(gt2), gfin,
                   per_batch(fsc), per_batch(fsh), prompt=True, seq_len=seq)
    y_s = _combine(dest_tiles(dest[:, n_p:]), y_sorted, xs_s, gw_s.T, per_token(gt2), gfin,
                   per_token(fsc), per_token(fsh), prompt=False, seq_len=None)

    return (y_p.reshape(bsz, seq, d), y_s.reshape(nseq, t_new, d),
            ckv_p.reshape(1, bsz, seq, KV_LORA), kpe_p.reshape(1, bsz, seq, ROPE_DIM),
            conv_p.reshape(1, bsz, CONV_W - 1, CONV_DIM),
            ckv_s.reshape(1, nseq, t_new, KV_LORA), kpe_s.reshape(1, nseq, t_new, ROPE_DIM),
            u_s.reshape(nseq, t_new, CONV_DIM)[:, t_new - (CONV_W - 1):, :].reshape(1, nseq, CONV_W - 1, CONV_DIM))
```

```python
import functools
import math

import jax
import jax.numpy as jnp
from jax import lax
from jax.experimental import pallas as pl
from jax.experimental.pallas import tpu as pltpu

F32 = jnp.float32
BF16 = jnp.bfloat16

D_MODEL = 1024
N_HEADS = 8
NOPE_DIM = 64
ROPE_DIM = 32
V_DIM = 64
Q_LORA = 384
KV_LORA = 256
ATTN_WIDTH = N_HEADS * V_DIM
CONV_DIM = 512
CONV_GROUPS = 8
CONV_GROUP_DIM = CONV_DIM // CONV_GROUPS
CONV_W = 3
ROPE_THETA = 10000.0
PAGE_SIZE = 128
N_EXPERTS = 256
TOP_K = 8
N_GROUPS = 8
GROUP_SIZE = N_EXPERTS // N_GROUPS
TOPK_GROUPS = 4
E_FF = 256
SHARED_FF = 256
ROUTED_SCALE = 2.5
EPS = 1e-6

LANES = 128
HEAD_PAD = LANES
QK_WIDTH = N_HEADS * HEAD_PAD
PE_LO = NOPE_DIM
PE_HALF = ROPE_DIM // 2
IN_PAD_COLS = Q_LORA + KV_LORA + HEAD_PAD + 3 * CONV_DIM
VMEM_LIMIT = 48 * 1024 * 1024
Q_SCALE = (NOPE_DIM + ROPE_DIM) ** -0.5 * math.log2(math.e)
PACK_W = D_MODEL // 2

TILE_PROMPT = 256
TILE_SAMPLE = 128
TQ = 512
TK = 256
ROUTE_CHUNK = 256
COMBINE_TILE = 128


def _cparams(sem, vmem=VMEM_LIMIT):
    return pltpu.CompilerParams(dimension_semantics=sem, vmem_limit_bytes=vmem)


def _dot(a, b):
    return jnp.dot(a, b, preferred_element_type=F32)


def _dot_nt(a, b):
    return lax.dot_general(a, b, (((1,), (1,)), ((), ())), preferred_element_type=F32)


def _rms(x, g):
    r = lax.rsqrt(jnp.mean(x * x, axis=-1, keepdims=True) + EPS)
    return (x * r) * g


def _pack_pair(xb):
    lo = lax.bitcast_convert_type(xb[:, :PACK_W].astype(F32), jnp.uint32) >> 16
    hi = lax.bitcast_convert_type(xb[:, PACK_W:].astype(F32), jnp.uint32) & jnp.uint32(0xFFFF0000)
    return lo | hi


def _unpack_pair(w):
    lo = lax.bitcast_convert_type(w << 16, F32)
    hi = lax.bitcast_convert_type(w & jnp.uint32(0xFFFF0000), F32)
    return lo, hi


def _ada_body(c_ref, w_ref, b_ref, o_ref):
    c = c_ref[...]
    a = (c * jax.nn.sigmoid(c)).astype(BF16)
    o_ref[...] = _dot(a, w_ref[...].astype(BF16)) + b_ref[...]


def _ada(c_all, w, b):
    m, d = c_all.shape
    n = w.shape[1]
    tn = 1024
    return pl.pallas_call(
        _ada_body,
        grid=(n // tn,),
        in_specs=[pl.BlockSpec((m, d), lambda j: (0, 0)),
                  pl.BlockSpec((d, tn), lambda j: (0, j)),
                  pl.BlockSpec((1, tn), lambda j: (0, j))],
        out_specs=pl.BlockSpec((m, tn), lambda j: (0, j)),
        out_shape=jax.ShapeDtypeStruct((m, n), F32),
        compiler_params=_cparams(("arbitrary",)),
        name="ada",
    )(c_all, w, b.reshape(1, n))


def _rope_lanes(x, cos, sin_up, sin_dn):
    w = x.shape[1]
    up = pltpu.roll(x, PE_HALF, 1)
    dn = pltpu.roll(x, w - PE_HALF, 1)
    return x * cos + up * sin_up + dn * sin_dn


def _mix_in_body(*refs, prompt, tm, tiles_per_seq, t_new):
    if prompt:
        (x_ref, sc_ref, sh_ref, g1_ref, win_ref, gq_ref, wq_ref, gkv_ref, wuk_ref, wuv_ref,
         cos_ref, sup_ref, sdn_ref, wconv_ref, gconv_ref, gmat_ref,
         q_ref, k_ref, v_ref, ckv_ref, kpe_ref, z_ref, cst_ref, carry_ref) = refs
    else:
        (x_ref, sc_ref, sh_ref, g1_ref, win_ref, gq_ref, wq_ref, gkv_ref,
         cos_ref, sup_ref, sdn_ref, wconv_ref, gconv_ref, gmat_ref, pa_ref, pb_ref,
         q_ref, ckv_ref, kpe_ref, z_ref, u_ref) = refs

    x = x_ref[...]
    h = _rms(x, g1_ref[...]) * (1.0 + sc_ref[...]) + sh_ref[...]
    proj = _dot(h.astype(BF16), win_ref[...])
    o0 = Q_LORA
    o1 = o0 + KV_LORA
    o2 = o1 + HEAD_PAD
    o3 = o2 + CONV_DIM
    o4 = o3 + CONV_DIM
    q_a, kv_a, kpe_blk = proj[:, :o0], proj[:, o0:o1], proj[:, o1:o2]
    b_g, c_g, u_in = proj[:, o2:o3], proj[:, o3:o4], proj[:, o4:]

    cos, sup, sdn = cos_ref[...], sup_ref[...], sdn_ref[...]
    cos8 = jnp.concatenate([cos] * N_HEADS, axis=1)
    sup8 = jnp.concatenate([sup] * N_HEADS, axis=1)
    sdn8 = jnp.concatenate([sdn] * N_HEADS, axis=1)

    qn = _rms(q_a, gq_ref[...]).astype(BF16)
    q = _dot(qn, wq_ref[...]) * Q_SCALE
    q_ref[...] = _rope_lanes(q, cos8, sup8, sdn8).astype(BF16)

    ckv = _rms(kv_a, gkv_ref[...])
    ckv_ref[...] = ckv
    kpe = _rope_lanes(kpe_blk, cos, sup, sdn)
    kpe_ref[...] = kpe[:, PE_LO:PE_LO + ROPE_DIM]

    if prompt:
        ckvb = ckv.astype(BF16)
        k = _dot(ckvb, wuk_ref[...]) + jnp.concatenate([kpe] * N_HEADS, axis=1)
        k_ref[...] = k.astype(BF16)
        lane = lax.broadcasted_iota(jnp.int32, (1, QK_WIDTH), 1)
        ones_hi = jnp.where(lane % HEAD_PAD >= V_DIM, 1.0, 0.0)
        v_ref[...] = (_dot(ckvb, wuv_ref[...]) + ones_hi).astype(BF16)

    u = c_g * u_in
    row = lax.broadcasted_iota(jnp.int32, (tm, 1), 0)
    r1 = pltpu.roll(u, 1, 0)
    r2 = pltpu.roll(u, 2, 0)
    if prompt:
        @pl.when(pl.program_id(0) % tiles_per_seq == 0)
        def _():
            carry_ref[...] = jnp.zeros_like(carry_ref)
        c6 = carry_ref[6:7, :]
        c7 = carry_ref[7:8, :]
        um1 = jnp.where(row == 0, c7, r1)
        um2 = jnp.where(row == 0, c6, jnp.where(row == 1, c7, r2))
        carry_ref[...] = u[tm - 8:, :]
        cst_ref[...] = u[tm - (CONV_W - 1):, :]
    else:
        t = row % t_new
        um1 = jnp.where(t == 0, pa_ref[...], r1)
        um2 = jnp.where(t == 0, pb_ref[...], jnp.where(t == 1, pa_ref[...], r2))
        u_ref[...] = u
    wc = wconv_ref[...]
    y = um2 * wc[0:1, :] + um1 * wc[1:2, :] + u * wc[2:3, :]
    zz = b_g * y
    ms = _dot((zz * zz).astype(BF16), gmat_ref[...]) * (1.0 / CONV_GROUP_DIM)
    z_ref[...] = (zz * lax.rsqrt(ms + EPS) * gconv_ref[...]).astype(BF16)


def _mix_in(x2d, sc, sh, p, rope_tabs, *, prompt, seq_len=None, prev_a=None, prev_b=None, t_new=1):
    n = x2d.shape[0]
    tm = TILE_PROMPT if prompt else TILE_SAMPLE
    nt = n // tm
    cos_t, sup_t, sdn_t = rope_tabs
    full = lambda a: pl.BlockSpec(a.shape, lambda i: (0,) * a.ndim)
    row_tile = lambda w: pl.BlockSpec((tm, w), lambda i: (i, 0))
    if prompt:
        tps = seq_len // tm
        mod_spec = pl.BlockSpec((None, 1, D_MODEL), lambda i: (i // tps, 0, 0))
        tab_spec = pl.BlockSpec((tm, LANES), lambda i: (i % tps, 0))
        nb = n // seq_len
        ins = [x2d, sc, sh, p["g1"], p["w_in"], p["g_q"], p["wq"], p["g_kv"], p["wuk"], p["wuv"],
               cos_t, sup_t, sdn_t, p["w_conv"], p["g_conv"], p["gmat"]]
        in_specs = [row_tile(D_MODEL), mod_spec, mod_spec, full(p["g1"]), full(p["w_in"]), full(p["g_q"]),
                    full(p["wq"]), full(p["g_kv"]), full(p["wuk"]), full(p["wuv"]),
                    tab_spec, tab_spec, tab_spec, full(p["w_conv"]), full(p["g_conv"]), full(p["gmat"])]
        out_shape = [jax.ShapeDtypeStruct((n, QK_WIDTH), BF16), jax.ShapeDtypeStruct((n, QK_WIDTH), BF16),
                     jax.ShapeDtypeStruct((n, QK_WIDTH), BF16), jax.ShapeDtypeStruct((n, KV_LORA), F32),
                     jax.ShapeDtypeStruct((n, ROPE_DIM), F32), jax.ShapeDtypeStruct((n, CONV_DIM), BF16),
                     jax.ShapeDtypeStruct((nb, CONV_W - 1, CONV_DIM), F32)]
        out_specs = [row_tile(QK_WIDTH), row_tile(QK_WIDTH), row_tile(QK_WIDTH), row_tile(KV_LORA),
                     row_tile(ROPE_DIM), row_tile(CONV_DIM),
                     pl.BlockSpec((None, CONV_W - 1, CONV_DIM), lambda i: (i // tps, 0, 0))]
        scratch = [pltpu.VMEM((8, CONV_DIM), F32)]
    else:
        tps = 1
        ins = [x2d, sc, sh, p["g1"], p["w_in"], p["g_q"], p["wq"], p["g_kv"],
               cos_t, sup_t, sdn_t, p["w_conv"], p["g_conv"], p["gmat"], prev_a, prev_b]
        in_specs = [row_tile(D_MODEL), row_tile(D_MODEL), row_tile(D_MODEL), full(p["g1"]), full(p["w_in"]),
                    full(p["g_q"]), full(p["wq"]), full(p["g_kv"]),
                    row_tile(LANES), row_tile(LANES), row_tile(LANES), full(p["w_conv"]), full(p["g_conv"]),
                    full(p["gmat"]), row_tile(CONV_DIM), row_tile(CONV_DIM)]
        out_shape = [jax.ShapeDtypeStruct((n, QK_WIDTH), BF16), jax.ShapeDtypeStruct((n, KV_LORA), F32),
                     jax.ShapeDtypeStruct((n, ROPE_DIM), F32), jax.ShapeDtypeStruct((n, CONV_DIM), BF16),
                     jax.ShapeDtypeStruct((n, CONV_DIM), F32)]
        out_specs = [row_tile(QK_WIDTH), row_tile(KV_LORA), row_tile(ROPE_DIM), row_tile(CONV_DIM),
                     row_tile(CONV_DIM)]
        scratch = []
    return pl.pallas_call(
        functools.partial(_mix_in_body, prompt=prompt, tm=tm, tiles_per_seq=tps, t_new=t_new),
        grid=(nt,), in_specs=in_specs, out_specs=out_specs, out_shape=out_shape,
        scratch_shapes=scratch, compiler_params=_cparams(("arbitrary",)),
        name="mix_in_prompt" if prompt else "mix_in_sample",
    )(*ins)


def _flash_body(q_ref, k_ref, v_ref, g_ref, gm_ref, o_ref, m_sc, acc_sc, s_sc, *, tq, tk):
    assert tq == 2 * tk
    i = pl.program_id(2)
    m_sc[...] = jnp.full(m_sc.shape, -jnp.inf, F32)
    acc_sc[...] = jnp.zeros(acc_sc.shape, F32)
    row_g = i * tq + lax.broadcasted_iota(jnp.int32, (tq, tk), 0)
    col_l = lax.broadcasted_iota(jnp.int32, (tq, tk), 1)
    heads = [slice(hh * HEAD_PAD, (hh + 1) * HEAD_PAD) for hh in range(2)]

    def logits(t, slot):
        ks = pl.multiple_of(t * tk, tk)
        for hh in range(2):
            s_sc[slot, hh] = _dot_nt(q_ref[:, heads[hh]], k_ref[pl.ds(ks, tk), heads[hh]])

    def consume(t, slot, masked):
        ks = pl.multiple_of(t * tk, tk)
        for hh in range(2):
            s = s_sc[slot, hh]
            if masked:
                s = jnp.where(ks + col_l <= row_g, s, -jnp.inf)
            m_prev = m_sc[hh]
            m_new = jnp.maximum(m_prev, jnp.max(s, axis=1, keepdims=True))
            alpha = jnp.exp2(m_prev - m_new)
            pr = jnp.exp2(s - jnp.concatenate([m_new] * (tk // LANES), axis=1))
            acc_sc[hh] = alpha * acc_sc[hh] + _dot(pr.astype(BF16), v_ref[pl.ds(ks, tk), heads[hh]])
            m_sc[hh] = m_new

    logits(0, 0)

    def tile_pair(p, _):
        logits(2 * p + 1, 1)
        consume(2 * p, 0, False)
        logits(2 * p + 2, 0)
        consume(2 * p + 1, 1, False)
        return 0
    lax.fori_loop(0, i, tile_pair, 0)
    logits(2 * i + 1, 1)
    consume(2 * i, 0, True)
    consume(2 * i + 1, 1, True)

    outs = []
    for hh in range(2):
        acc = acc_sc[hh]
        outs.append(acc * pltpu.roll(1.0 / acc, V_DIM, 1))
    lane = lax.broadcasted_iota(jnp.int32, (tq, LANES), 1)
    o = jnp.where(lane < V_DIM, outs[0], pltpu.roll(outs[1], V_DIM, 1))
    ms = _dot((o * o).astype(BF16), gm_ref[...]) * (1.0 / V_DIM)
    o_ref[...] = (o * lax.rsqrt(ms + EPS) * g_ref[...]).astype(BF16)


def _flash(q, k, v, g_attn, gm_head, batch, seq_len):
    n = q.shape[0]
    nq = seq_len // TQ
    pair = pl.BlockSpec((seq_len, 2 * HEAD_PAD), lambda b, p, i: (b, p))
    return pl.pallas_call(
        functools.partial(_flash_body, tq=TQ, tk=TK),
        grid=(batch, N_HEADS // 2, nq),
        in_specs=[pl.BlockSpec((TQ, 2 * HEAD_PAD), lambda b, p, i: (b * nq + i, p)),
                  pair, pair,
                  pl.BlockSpec((1, 2 * V_DIM), lambda b, p, i: (0, p)),
                  pl.BlockSpec((LANES, LANES), lambda b, p, i: (0, 0))],
        out_specs=pl.BlockSpec((TQ, 2 * V_DIM), lambda b, p, i: (b * nq + i, p)),
        out_shape=jax.ShapeDtypeStruct((n, ATTN_WIDTH), BF16),
        scratch_shapes=[pltpu.VMEM((2, TQ, LANES), F32), pltpu.VMEM((2, TQ, LANES), F32),
                        pltpu.VMEM((2, 2, TQ, TK), F32)],
        compiler_params=_cparams(("arbitrary", "arbitrary", "arbitrary")),
        name="flash_prompt",
    )(q, k, v, g_attn, gm_head)


def _absorb_body(q_ref, w_ref, ql_ref, qp_ref):
    qb = q_ref[...]
    ql_ref[...] = _dot(qb, w_ref[...]).astype(BF16)
    qp_ref[...] = qb[:, PE_LO:PE_LO + ROPE_DIM]


def _absorb(q_s, wuk_t):
    n = q_s.shape[0]
    return pl.pallas_call(
        _absorb_body,
        grid=(N_HEADS,),
        in_specs=[pl.BlockSpec((n, HEAD_PAD), lambda h: (0, h)),
                  pl.BlockSpec((None, HEAD_PAD, KV_LORA), lambda h: (h, 0, 0))],
        out_specs=[pl.BlockSpec((None, n, KV_LORA), lambda h: (h, 0, 0)),
                   pl.BlockSpec((None, n, ROPE_DIM), lambda h: (h, 0, 0))],
        out_shape=[jax.ShapeDtypeStruct((N_HEADS, n, KV_LORA), BF16),
                   jax.ShapeDtypeStruct((N_HEADS, n, ROPE_DIM), BF16)],
        compiler_params=_cparams(("arbitrary",)),
        name="absorb",
    )(q_s, wuk_t)


PAGES_PER_CHUNK = 8


def _paged_body(pt_ref, ql_ref, qp_ref, cn_ref, kn_ref, cckv_ref, ckpe_ref, o_ref,
                ckv_buf, kpe_buf, s_all, kcb, sems, *, n_pages, n_seq, t_new):
    b = pl.program_id(0)
    slot = b % 2

    def fetch(seq, sl):
        def body(pg, _):
            page = pt_ref[seq, pg]
            pltpu.make_async_copy(cckv_ref.at[0, page], ckv_buf.at[sl, pg], sems.at[0, sl]).start()
            pltpu.make_async_copy(ckpe_ref.at[0, page], kpe_buf.at[sl, pg], sems.at[1, sl]).start()
            return 0
        lax.fori_loop(0, n_pages, body, 0)

    @pl.when(b == 0)
    def _():
        fetch(0, 0)

    @pl.when(b + 1 < n_seq)
    def _():
        fetch(b + 1, 1 - slot)

    pltpu.make_async_copy(cckv_ref.at[0, pl.ds(0, n_pages)], ckv_buf.at[slot], sems.at[0, slot]).wait()
    pltpu.make_async_copy(ckpe_ref.at[0, pl.ds(0, n_pages)], kpe_buf.at[slot], sems.at[1, slot]).wait()

    ql = ql_ref[...]
    qp = qp_ref[...]
    rows = ql.shape[0]
    ck = PAGES_PER_CHUNK * PAGE_SIZE

    n_chunks = n_pages // PAGES_PER_CHUNK

    m = jnp.full((rows, 1), -jnp.inf, F32)
    for c in range(n_chunks):
        pages = slice(c * PAGES_PER_CHUNK, (c + 1) * PAGES_PER_CHUNK)
        kc = ckv_buf[slot, pages].reshape(ck, KV_LORA).astype(BF16)
        kcb[c] = kc
        pc = kpe_buf[slot, pages].astype(BF16)
        s_pe = jnp.concatenate([_dot(qp, pc[j]) for j in range(PAGES_PER_CHUNK)], axis=1)
        s = _dot_nt(ql, kc) + s_pe
        s_all[c] = s
        m = jnp.maximum(m, jnp.max(s, axis=1, keepdims=True))

    qlf = ql.astype(F32)
    qpf = qp.astype(F32)
    cn = cn_ref[...]
    kn = kn_ref[...]
    trow = lax.broadcasted_iota(jnp.int32, (rows, 1), 0) % t_new
    s_new = []
    for j in range(t_new):
        sj = (jnp.sum(qlf * cn[j:j + 1, :], axis=1, keepdims=True)
              + jnp.sum(qpf * kn[j:j + 1, :], axis=1, keepdims=True))
        sj = jnp.where(trow >= j, sj, -jnp.inf)
        s_new.append(sj)
        m = jnp.maximum(m, sj)

    l = jnp.zeros((rows, 1), F32)
    acc = jnp.zeros((rows, KV_LORA), F32)
    for c in range(n_chunks):
        pr = jnp.exp2(s_all[c] - m)
        l = l + jnp.sum(pr, axis=1, keepdims=True)
        acc = acc + _dot(pr.astype(BF16), kcb[c])
    for j in range(t_new):
        pj = jnp.exp2(s_new[j] - m)
        l = l + pj
        acc = acc + pj * cn[j:j + 1, :]
    o_ref[...] = acc / l


def _paged(page_table, ql, qp, ckv_new, kpe_new, cache_ckv, cache_kpe):
    n_seq, n_pages = page_table.shape
    rows = ql.shape[1]
    t_new = ckv_new.shape[1]
    grid_spec = pltpu.PrefetchScalarGridSpec(
        num_scalar_prefetch=1,
        grid=(n_seq,),
        in_specs=[pl.BlockSpec((None, rows, KV_LORA), lambda b, pt: (b, 0, 0)),
                  pl.BlockSpec((None, rows, ROPE_DIM), lambda b, pt: (b, 0, 0)),
                  pl.BlockSpec((None, t_new, KV_LORA), lambda b, pt: (b, 0, 0)),
                  pl.BlockSpec((None, t_new, ROPE_DIM), lambda b, pt: (b, 0, 0)),
                  pl.BlockSpec(memory_space=pl.ANY),
                  pl.BlockSpec(memory_space=pl.ANY)],
        out_specs=pl.BlockSpec((None, rows, KV_LORA), lambda b, pt: (b, 0, 0)),
        scratch_shapes=[pltpu.VMEM((2, n_pages, PAGE_SIZE, KV_LORA), F32),
                        pltpu.VMEM((2, n_pages, ROPE_DIM, PAGE_SIZE), F32),
                        pltpu.VMEM((n_pages // PAGES_PER_CHUNK, rows, PAGES_PER_CHUNK * PAGE_SIZE), F32),
                        pltpu.VMEM((n_pages // PAGES_PER_CHUNK, PAGES_PER_CHUNK * PAGE_SIZE, KV_LORA), BF16),
                        pltpu.SemaphoreType.DMA((2, 2))],
    )
    return pl.pallas_call(
        functools.partial(_paged_body, n_pages=n_pages, n_seq=n_seq, t_new=t_new),
        grid_spec=grid_spec,
        out_shape=jax.ShapeDtypeStruct((n_seq, rows, KV_LORA), F32),
        compiler_params=_cparams(("arbitrary",)),
        name="paged_attn",
    )(page_table, ql, qp, ckv_new, kpe_new, cache_ckv, cache_kpe)


def _unabsorb_body(o_ref, w_ref, g_ref, out_ref):
    o = _dot(o_ref[...].astype(BF16), w_ref[...])
    r = lax.rsqrt(jnp.sum(o * o, axis=-1, keepdims=True) * (1.0 / V_DIM) + EPS)
    out_ref[...] = (o * r * g_ref[...]).astype(BF16)


def _unabsorb(o_lat_h, wuv_h, g_h):
    n = o_lat_h.shape[1]
    return pl.pallas_call(
        _unabsorb_body,
        grid=(N_HEADS,),
        in_specs=[pl.BlockSpec((None, n, KV_LORA), lambda h: (h, 0, 0)),
                  pl.BlockSpec((None, KV_LORA, HEAD_PAD), lambda h: (h, 0, 0)),
                  pl.BlockSpec((None, 1, HEAD_PAD), lambda h: (h, 0, 0))],
        out_specs=pl.BlockSpec((None, n, HEAD_PAD), lambda h: (h, 0, 0)),
        out_shape=jax.ShapeDtypeStruct((N_HEADS, n, HEAD_PAD), BF16),
        compiler_params=_cparams(("arbitrary",)),
        name="unabsorb",
    )(o_lat_h, wuv_h, g_h)


def _post_mix_body(x_ref, o_ref, z_ref, gt1_ref, sc_ref, sh_ref, gt2_ref, wo_ref, g2_ref,
                   wr_ref, eb_ref, wsgu_ref, wsd_ref, tri_ref, cin_ref,
                   xs_ref, h2_ref, idx_ref, gw_ref, rank_ref, cout_ref, cnt_ref, *, tm):
    i = pl.program_id(0)

    @pl.when(i == 0)
    def _():
        cnt_ref[...] = cin_ref[...]

    wo = wo_ref[...]
    mix = _dot(o_ref[...], wo[:ATTN_WIDTH, :]) + _dot(z_ref[...], wo[ATTN_WIDTH:, :])
    x1 = x_ref[...] + gt1_ref[...] * mix
    h2 = _rms(x1, g2_ref[...]) * (1.0 + sc_ref[...]) + sh_ref[...]
    hb = h2.astype(BF16)
    h2_ref[...] = _pack_pair(hb)

    gu = _dot(hb, wsgu_ref[...])
    gate, up = gu[:, :SHARED_FF], gu[:, SHARED_FF:]
    shared = _dot((gate * jax.nn.sigmoid(gate) * up).astype(BF16), wsd_ref[...])
    xs_ref[...] = x1 + gt2_ref[...] * shared

    s = jax.nn.sigmoid(_dot_nt(wr_ref[...], hb))
    biased = s + eb_ref[...]
    ninf = -jnp.inf
    gi = lax.broadcasted_iota(jnp.int32, (GROUP_SIZE, tm), 0).astype(F32)
    gs = []
    for g in range(N_GROUPS):
        blk = biased[g * GROUP_SIZE:(g + 1) * GROUP_SIZE, :]
        m1 = jnp.max(blk, axis=0, keepdims=True)
        i1 = jnp.min(jnp.where(blk == m1, gi, float(GROUP_SIZE)), axis=0, keepdims=True)
        m2 = jnp.max(jnp.where(gi == i1, ninf, blk), axis=0, keepdims=True)
        gs.append(m1 + m2)
    gscore = jnp.concatenate(gs, axis=0)
    giota = lax.broadcasted_iota(jnp.int32, (N_GROUPS, tm), 0).astype(F32)
    gsel = jnp.zeros((N_GROUPS, tm), F32)
    for _ in range(TOPK_GROUPS):
        gm = jnp.max(gscore, axis=0, keepdims=True)
        gidx = jnp.min(jnp.where(gscore == gm, giota, float(N_GROUPS)), axis=0, keepdims=True)
        hit = giota == gidx
        gsel = jnp.where(hit, 1.0, gsel)
        gscore = jnp.where(hit, ninf, gscore)
    masked = jnp.concatenate(
        [jnp.where(gsel[g:g + 1, :] > 0.0, biased[g * GROUP_SIZE:(g + 1) * GROUP_SIZE, :], ninf)
         for g in range(N_GROUPS)], axis=0)
    eiota = lax.broadcasted_iota(jnp.int32, (N_EXPERTS, tm), 0).astype(F32)
    idxs, ws = [], []
    selall = jnp.zeros((N_EXPERTS, tm), F32)
    for _ in range(TOP_K):
        mx = jnp.max(masked, axis=0, keepdims=True)
        ei = jnp.min(jnp.where(masked == mx, eiota, float(N_EXPERTS)), axis=0, keepdims=True)
        hit = eiota == ei
        ws.append(jnp.sum(jnp.where(hit, s, 0.0), axis=0, keepdims=True))
        idxs.append(ei)
        selall = jnp.where(hit, 1.0, selall)
        masked = jnp.where(hit, ninf, masked)
    wsum = ws[0]
    for w in ws[1:]:
        wsum = wsum + w
    gw_ref[...] = jnp.concatenate(ws, axis=0) / wsum * ROUTED_SCALE
    idx_ref[...] = jnp.concatenate(idxs, axis=0).astype(jnp.int32)

    before = _dot(selall.astype(BF16), tri_ref[...]) + cnt_ref[:, 0:1]
    ranks = [jnp.sum(jnp.where(eiota == ei, before, 0.0), axis=0, keepdims=True) for ei in idxs]
    rank_ref[...] = jnp.concatenate(ranks, axis=0).astype(jnp.int32)
    cnt_ref[...] = cnt_ref[...] + jnp.sum(selall, axis=1, keepdims=True)
    cout_ref[...] = cnt_ref[...]


def _post_mix(x2d, o, z, gt1, sc2, sh2, gt2, p, cnt_in, h2_buf, *, prompt, seq_len, row_off, n_total):
    n = x2d.shape[0]
    tm = TILE_PROMPT if prompt else TILE_SAMPLE
    nt = n // tm
    full = lambda a: pl.BlockSpec(a.shape, lambda i: (0,) * a.ndim)
    row_tile = lambda w: pl.BlockSpec((tm, w), lambda i: (i, 0))
    if prompt:
        tps = seq_len // tm
        mod_spec = pl.BlockSpec((None, 1, D_MODEL), lambda i: (i // tps, 0, 0))
    else:
        mod_spec = row_tile(D_MODEL)
    tri = p["tri_p"] if prompt else p["tri_s"]
    boff = row_off // tm
    ins = [x2d, o, z, gt1, sc2, sh2, gt2, p["w_out"], p["g2"], p["wr_t"], p["e_bias"], p["ws_gu"], p["ws_d"],
           tri, cnt_in]
    in_specs = [row_tile(D_MODEL), row_tile(ATTN_WIDTH), row_tile(CONV_DIM), mod_spec, mod_spec, mod_spec,
                mod_spec, full(p["w_out"]), full(p["g2"]), full(p["wr_t"]), full(p["e_bias"]), full(p["ws_gu"]),
                full(p["ws_d"]), full(tri), full(cnt_in)]
    out_shape = [jax.ShapeDtypeStruct((n, D_MODEL), F32), jax.ShapeDtypeStruct((n_total, PACK_W), jnp.uint32),
                 jax.ShapeDtypeStruct((TOP_K, n), jnp.int32), jax.ShapeDtypeStruct((TOP_K, n), F32),
                 jax.ShapeDtypeStruct((TOP_K, n), jnp.int32), jax.ShapeDtypeStruct((N_EXPERTS, LANES), F32)]
    col_tile = pl.BlockSpec((TOP_K, tm), lambda i: (0, i))
    out_specs = [row_tile(D_MODEL), pl.BlockSpec((tm, PACK_W), lambda i: (i + boff, 0)),
                 col_tile, col_tile, col_tile, pl.BlockSpec((N_EXPERTS, LANES), lambda i: (0, 0))]
    aliases = {}
    if h2_buf is not None:
        ins.append(h2_buf)
        in_specs.append(pl.BlockSpec(memory_space=pl.ANY))
        aliases = {len(ins) - 1: 1}
        body = lambda *refs: _post_mix_body(*refs[:15], *refs[16:], tm=tm)
    else:
        body = functools.partial(_post_mix_body, tm=tm)
    return pl.pallas_call(
        body, grid=(nt,), in_specs=in_specs, out_specs=out_specs, out_shape=out_shape,
        scratch_shapes=[pltpu.VMEM((N_EXPERTS, LANES), F32)],
        input_output_aliases=aliases,
        compiler_params=_cparams(("arbitrary",)),
        name="post_mix_prompt" if prompt else "post_mix_sample",
    )(*ins)


def _dispatch_body(d_ref, h_ref, xs_ref, stage, sem, *, tm, nt):
    i = pl.program_id(0)
    slot = i % 2
    stage[slot] = h_ref[...]

    def body(rb, _):
        r0 = pl.multiple_of(rb * 8, 8)
        for rr in range(8):
            for k in range(TOP_K):
                d = d_ref[0, 0, k * tm + r0 + rr]
                pltpu.make_async_copy(stage.at[slot, pl.ds(r0 + rr, 1)], xs_ref.at[pl.ds(d, 1)],
                                      sem.at[slot]).start()
        return 0
    lax.fori_loop(0, tm // 8, body, 0)

    def drain(sl):
        for _ in range(TOP_K):
            pltpu.make_async_copy(stage.at[sl], xs_ref.at[pl.ds(0, tm)], sem.at[sl]).wait()

    @pl.when(i > 0)
    def _():
        drain(1 - slot)

    @pl.when(i == nt - 1)
    def _():
        drain(slot)


def _dispatch(dest_tiles, h2, n_rows):
    nt, _, w = dest_tiles.shape
    tm = w // TOP_K
    return pl.pallas_call(
        functools.partial(_dispatch_body, tm=tm, nt=nt),
        grid=(nt,),
        in_specs=[pl.BlockSpec((1, 1, w), lambda i: (i, 0, 0), memory_space=pltpu.SMEM),
                  pl.BlockSpec((tm, h2.shape[1]), lambda i: (i, 0))],
        out_specs=pl.BlockSpec(memory_space=pl.ANY),
        out_shape=jax.ShapeDtypeStruct((n_rows, h2.shape[1]), h2.dtype),
        scratch_shapes=[pltpu.VMEM((2, tm, h2.shape[1]), h2.dtype), pltpu.SemaphoreType.DMA((2,))],
        compiler_params=_cparams(("arbitrary",)),
        name="dispatch",
    )(dest_tiles, h2)


def _experts_body(be_ref, nv_ref, nu_ref, x_ref, wg_ref, wu_ref, wd_ref, y_ref, wgb, wub, wdb, *, ch):
    c = pl.program_id(0)
    prev_e = be_ref[jnp.maximum(c - 1, 0)]

    @pl.when(jnp.logical_or(c == 0, be_ref[c] != prev_e))
    def _():
        wgb[...] = wg_ref[...].astype(BF16)
        wub[...] = wu_ref[...].astype(BF16)
        wdb[...] = wd_ref[...].astype(BF16)

    @pl.when(c < nu_ref[0])
    def _():
        row = lax.broadcasted_iota(jnp.int32, (ch, 1), 0)
        w = jnp.where(row < nv_ref[c], x_ref[...], jnp.uint32(0))
        lo, hi = _unpack_pair(w)
        lo, hi = lo.astype(BF16), hi.astype(BF16)
        gate = _dot(lo, wgb[:PACK_W, :]) + _dot(hi, wgb[PACK_W:, :])
        up = _dot(lo, wub[:PACK_W, :]) + _dot(hi, wub[PACK_W:, :])
        act = (gate * jax.nn.sigmoid(gate) * up).astype(BF16)
        y_ref[...] = _pack_pair(_dot(act, wdb[...]).astype(BF16))


def _experts(block_e, n_valid, n_used, x_sorted, w_e_gate, w_e_up, w_e_down):
    ch = ROUTE_CHUNK
    nch = x_sorted.shape[0] // ch
    last = lambda c, nu: jnp.minimum(c, jnp.maximum(nu[0] - 1, 0))
    grid_spec = pltpu.PrefetchScalarGridSpec(
        num_scalar_prefetch=3,
        grid=(nch,),
        in_specs=[pl.BlockSpec((ch, PACK_W), lambda c, be, nv, nu: (last(c, nu), 0)),
                  pl.BlockSpec((None, D_MODEL, E_FF), lambda c, be, nv, nu: (be[c], 0, 0)),
                  pl.BlockSpec((None, D_MODEL, E_FF), lambda c, be, nv, nu: (be[c], 0, 0)),
                  pl.BlockSpec((None, E_FF, D_MODEL), lambda c, be, nv, nu: (be[c], 0, 0))],
        out_specs=pl.BlockSpec((ch, PACK_W), lambda c, be, nv, nu: (last(c, nu), 0)),
        scratch_shapes=[pltpu.VMEM((D_MODEL, E_FF), BF16), pltpu.VMEM((D_MODEL, E_FF), BF16),
                        pltpu.VMEM((E_FF, D_MODEL), BF16)],
    )
    return pl.pallas_call(
        functools.partial(_experts_body, ch=ch),
        grid_spec=grid_spec,
        out_shape=jax.ShapeDtypeStruct((nch * ch, PACK_W), jnp.uint32),
        compiler_params=_cparams(("arbitrary",)),
        name="experts",
    )(block_e, n_valid, n_used, x_sorted, w_e_gate, w_e_up, w_e_down)


def _combine_body(dc_ref, dn_ref, y_ref, xs_ref, gw_ref, gt2_ref, gf_ref, fsc_ref, fsh_ref, out_ref,
                  gbuf, sem, *, tm, nt):
    i = pl.program_id(0)
    slot = i % 2

    def issue(d_ref, sl):
        def body(rb, _):
            r0 = pl.multiple_of(rb * 8, 8)
            for rr in range(8):
                for k in range(TOP_K):
                    d = d_ref[0, 0, k * tm + r0 + rr]
                    pltpu.make_async_copy(y_ref.at[pl.ds(d, 1)], gbuf.at[sl, k, pl.ds(r0 + rr, 1)],
                                          sem.at[sl]).start()
            return 0
        lax.fori_loop(0, tm // 8, body, 0)

    @pl.when(i == 0)
    def _():
        issue(dc_ref, 0)

    @pl.when(i + 1 < nt)
    def _():
        issue(dn_ref, 1 - slot)

    for k in range(TOP_K):
        pltpu.make_async_copy(y_ref.at[pl.ds(0, tm)], gbuf.at[slot, k], sem.at[slot]).wait()
    gw = gw_ref[...]
    f_lo = jnp.zeros((tm, PACK_W), F32)
    f_hi = jnp.zeros((tm, PACK_W), F32)
    for k in range(TOP_K):
        lo, hi = _unpack_pair(gbuf[slot, k])
        f_lo = f_lo + gw[:, k:k + 1] * lo
        f_hi = f_hi + gw[:, k:k + 1] * hi
    f = jnp.concatenate([f_lo, f_hi], axis=1)
    x2 = xs_ref[...] + gt2_ref[...] * f
    out_ref[...] = _rms(x2, gf_ref[...]) * (1.0 + fsc_ref[...]) + fsh_ref[...]


def _combine(dest_tiles, y_sorted, xs, gw, gt2, g_final, fsc, fsh, *, prompt, seq_len):
    n = xs.shape[0]
    tm = COMBINE_TILE
    nt = n // tm
    row_tile = lambda w: pl.BlockSpec((tm, w), lambda i: (i, 0))
    if prompt:
        tps = seq_len // tm
        mod_spec = pl.BlockSpec((None, 1, D_MODEL), lambda i: (i // tps, 0, 0))
    else:
        mod_spec = row_tile(D_MODEL)
    return pl.pallas_call(
        functools.partial(_combine_body, tm=tm, nt=nt),
        grid=(nt,),
        in_specs=[pl.BlockSpec((1, 1, TOP_K * tm), lambda i: (i, 0, 0), memory_space=pltpu.SMEM),
                  pl.BlockSpec((1, 1, TOP_K * tm), lambda i: (jnp.minimum(i + 1, nt - 1), 0, 0),
                               memory_space=pltpu.SMEM),
                  pl.BlockSpec(memory_space=pl.ANY),
                  row_tile(D_MODEL), row_tile(TOP_K), mod_spec,
                  pl.BlockSpec((1, D_MODEL), lambda i: (0, 0)), mod_spec, mod_spec],
        out_specs=row_tile(D_MODEL),
        out_shape=jax.ShapeDtypeStruct((n, D_MODEL), F32),
        scratch_shapes=[pltpu.VMEM((2, TOP_K, tm, PACK_W), jnp.uint32), pltpu.SemaphoreType.DMA((2,))],
        compiler_params=_cparams(("arbitrary",)),
        name="combine_prompt" if prompt else "combine_sample",
    )(dest_tiles, dest_tiles, y_sorted, xs, gw, gt2, g_final, fsc, fsh)


def _rope_tables(pos):
    inv = 1.0 / (ROPE_THETA ** (jnp.arange(PE_HALF, dtype=F32) / PE_HALF))
    ang = pos.astype(F32)[:, None] * inv[None, :]
    cos, sin = jnp.cos(ang), jnp.sin(ang)
    n = pos.shape[0]
    ones = jnp.ones((n, PE_LO), F32)
    zeros = jnp.zeros((n, PE_LO), F32)
    zh = jnp.zeros((n, PE_HALF), F32)
    tail1 = jnp.ones((n, LANES - PE_LO - ROPE_DIM), F32)
    tail0 = jnp.zeros((n, LANES - PE_LO - ROPE_DIM), F32)
    cos_t = jnp.concatenate([ones, cos, cos, tail1], axis=1)
    sin_up = jnp.concatenate([zeros, zh, sin, tail0], axis=1)
    sin_dn = jnp.concatenate([zeros, -sin, zh, tail0], axis=1)
    return cos_t, sin_up, sin_dn


def _pad_heads(w, width):
    pad = [(0, 0)] * (w.ndim - 1) + [(0, HEAD_PAD - width)]
    w = jnp.pad(w, pad)
    return w.reshape(w.shape[:-2] + (N_HEADS * HEAD_PAD,))


def _block_diag_ones(n, blk):
    r = jnp.arange(n) // blk
    return (r[:, None] == r[None, :]).astype(BF16)


def _prep_weights(w_in, g_attn_norm, g_q, w_q_up, g_kv, w_uk, w_uv, w_conv, g_attn_out, g_conv_out, w_out,
                  g_ffn_norm, w_router, e_bias, w_s_gate, w_s_up, w_s_down):
    o0 = Q_LORA
    o1 = o0 + KV_LORA
    o2 = o1 + ROPE_DIM
    kpe_cols = jnp.pad(w_in[:, o1:o2], ((0, 0), (PE_LO, HEAD_PAD - PE_LO - ROPE_DIM)))
    w_in_p = jnp.concatenate([w_in[:, :o1], kpe_cols, w_in[:, o2:]], axis=1).astype(BF16)
    wq = _pad_heads(w_q_up.reshape(Q_LORA, N_HEADS, NOPE_DIM + ROPE_DIM), NOPE_DIM + ROPE_DIM).astype(BF16)
    wuk = _pad_heads(w_uk, NOPE_DIM).astype(BF16)
    wuv = _pad_heads(w_uv, V_DIM).astype(BF16)
    wuk_t = jnp.pad(jnp.transpose(w_uk, (1, 2, 0)), ((0, 0), (0, HEAD_PAD - NOPE_DIM), (0, 0))).astype(BF16)
    wuv_h = jnp.pad(jnp.transpose(w_uv, (1, 0, 2)), ((0, 0), (0, 0), (0, HEAD_PAD - V_DIM))).astype(BF16)
    g_attn_h = jnp.pad(g_attn_out.reshape(N_HEADS, 1, V_DIM), ((0, 0), (0, 0), (0, HEAD_PAD - V_DIM)))
    tri = lambda t: (jnp.arange(t)[:, None] < jnp.arange(t)[None, :]).astype(BF16)
    return {
        "g1": g_attn_norm.reshape(1, -1), "w_in": w_in_p, "g_q": g_q.reshape(1, -1), "wq": wq,
        "g_kv": g_kv.reshape(1, -1), "wuk": wuk, "wuv": wuv, "wuk_t": wuk_t, "wuv_h": wuv_h,
        "w_conv": w_conv, "g_conv": g_conv_out.reshape(1, -1), "gmat": _block_diag_ones(CONV_DIM, CONV_GROUP_DIM),
        "g_attn": g_attn_out.reshape(1, -1), "g_attn_h": g_attn_h, "gm_head": _block_diag_ones(LANES, V_DIM),
        "w_out": w_out.astype(BF16), "g2": g_ffn_norm.reshape(1, -1), "wr_t": w_router.T.astype(BF16),
        "e_bias": e_bias.reshape(-1, 1), "ws_gu": jnp.concatenate([w_s_gate, w_s_up], axis=1).astype(BF16),
        "ws_d": w_s_down.astype(BF16), "tri_p": tri(TILE_PROMPT), "tri_s": tri(TILE_SAMPLE),
    }


def kernel(x_prompt, x_sample, c_prompt, c_sample, cache_ckv, cache_kpe, state_conv, page_table, w_ada, b_ada, g_attn_norm, w_in, g_q, w_q_up, g_kv, w_uk, w_uv, w_conv, g_attn_out, g_conv_out, w_out, g_ffn_norm, w_router, e_bias, w_e_gate, w_e_up, w_e_down, w_s_gate, w_s_up, w_s_down, w_ada_final, b_ada_final, g_final):
    assert w_ada.shape[0] == 1, "one layer"
    bsz, seq, d = x_prompt.shape
    nseq, t_new, _ = x_sample.shape
    n_p, n_s = bsz * seq, nseq * t_new
    n_tot = n_p + n_s
    past = page_table.shape[1] * PAGE_SIZE

    p = _prep_weights(w_in[0], g_attn_norm[0], g_q[0], w_q_up[0], g_kv[0], w_uk[0], w_uv[0], w_conv[0],
                      g_attn_out[0], g_conv_out[0], w_out[0], g_ffn_norm[0], w_router[0], e_bias[0],
                      w_s_gate[0], w_s_up[0], w_s_down[0])

    c_all = jnp.concatenate([c_prompt, c_sample], axis=0)
    mod = _ada(c_all, w_ada[0], b_ada[0])
    modf = _ada(c_all, w_ada_final, b_ada_final)
    sh1, sc1, gt1, sh2, sc2, gt2 = [mod[:, j * d:(j + 1) * d] for j in range(6)]
    fsh, fsc = modf[:, :d], modf[:, d:]
    per_batch = lambda a: a[:bsz].reshape(bsz, 1, d)
    per_token = lambda a: jnp.repeat(a[bsz:], t_new, axis=0)

    xp = x_prompt.reshape(n_p, d)
    xs_in = x_sample.reshape(n_s, d)
    tabs_p = _rope_tables(jnp.arange(seq, dtype=jnp.int32))
    tabs_s = tuple(jnp.tile(t, (nseq, 1)) for t in _rope_tables(past + jnp.arange(t_new, dtype=jnp.int32)))

    q_p, k_p, v_p, ckv_p, kpe_p, z_p, conv_p = _mix_in(
        xp, per_batch(sc1), per_batch(sh1), p, tabs_p, prompt=True, seq_len=seq)
    o_p = _flash(q_p, k_p, v_p, p["g_attn"], p["gm_head"], bsz, seq)

    prev = state_conv[0]
    prev_a = jnp.repeat(prev[:, 1, :], t_new, axis=0)
    prev_b = jnp.repeat(prev[:, 0, :], t_new, axis=0)
    q_s, ckv_s, kpe_s, z_s, u_s = _mix_in(
        xs_in, per_token(sc1), per_token(sh1), p, tabs_s, prompt=False, prev_a=prev_a, prev_b=prev_b,
        t_new=t_new)
    ql_h, qp_h = _absorb(q_s, p["wuk_t"])
    rows = N_HEADS * t_new
    to_seq = lambda a: jnp.transpose(a.reshape(N_HEADS, nseq, t_new, a.shape[-1]), (1, 0, 2, 3)).reshape(
        nseq, rows, a.shape[-1])
    o_lat = _paged(page_table, to_seq(ql_h), to_seq(qp_h), ckv_s.reshape(nseq, t_new, KV_LORA),
                   kpe_s.reshape(nseq, t_new, ROPE_DIM), cache_ckv, jnp.swapaxes(cache_kpe, 2, 3))
    o_lat_h = jnp.transpose(o_lat.reshape(nseq, N_HEADS, t_new, KV_LORA), (1, 0, 2, 3)).reshape(
        N_HEADS, n_s, KV_LORA)
    o_s_h = _unabsorb(o_lat_h, p["wuv_h"], p["g_attn_h"])
    o_s = jnp.transpose(o_s_h[:, :, :V_DIM], (1, 0, 2)).reshape(n_s, ATTN_WIDTH)

    cnt0 = jnp.zeros((N_EXPERTS, LANES), F32)
    xs_p, h2, idx_p, gw_p, rank_p, cnt_p = _post_mix(
        xp, o_p, z_p, per_batch(gt1), per_batch(sc2), per_batch(sh2), per_batch(gt2), p, cnt0, None,
        prompt=True, seq_len=seq, row_off=0, n_total=n_tot)
    xs_s, h2, idx_s, gw_s, rank_s, cnt_s = _post_mix(
        xs_in, o_s, z_s, per_token(gt1), per_token(sc2), per_token(sh2), per_token(gt2), p, cnt_p, h2,
        prompt=False, seq_len=None, row_off=n_p, n_total=n_tot)

    ch = ROUTE_CHUNK
    counts = cnt_s[:, 0].astype(jnp.int32)
    padded = (counts + ch - 1) // ch * ch
    pend = jnp.cumsum(padded)
    pstart = pend - padded
    idx_all = jnp.concatenate([idx_p, idx_s], axis=1)
    rank_all = jnp.concatenate([rank_p, rank_s], axis=1)
    eids = jnp.arange(N_EXPERTS, dtype=jnp.int32)
    lookup = lambda table, keys: jnp.sum(jnp.where(keys[..., None] == eids, table, 0), axis=-1)
    dest = lookup(pstart, idx_all) + rank_all
    nch = -(-(n_tot * TOP_K) // ch) + N_EXPERTS
    chunk_start = jnp.arange(nch, dtype=jnp.int32) * ch
    block_e = jnp.minimum(jnp.sum((pend[None, :] <= chunk_start[:, None]).astype(jnp.int32), axis=1),
                          N_EXPERTS - 1)
    n_valid = jnp.clip(lookup(pstart + counts, block_e) - chunk_start, 0, ch).astype(jnp.int32)
    n_used = (pend[-1] // ch).astype(jnp.int32).reshape(1)

    tmc = COMBINE_TILE

    def dest_tiles(dst):
        n = dst.shape[1]
        return jnp.transpose(dst.reshape(TOP_K, n // tmc, tmc), (1, 0, 2)).reshape(n // tmc, 1, TOP_K * tmc)

    x_sorted = _dispatch(dest_tiles(dest), h2, nch * ch)
    y_sorted = _experts(block_e, n_valid, n_used, x_sorted, w_e_gate[0], w_e_up[0], w_e_down[0])


    gfin = g_final.reshape(1, d)
    y_p = _combine(dest_tiles(dest[:, :n_p]), y_sorted, xs_p, gw_p.T, per_batch(gt2), gfin,
                   per_batch(fsc), per_batch(fsh), prompt=True, seq_len=seq)
    y_s = _combine(dest_tiles(dest[:, n_p:]), y_sorted, xs_s, gw_s.T, per_token(gt2), gfin,
                   per_token(fsc), per_token(fsh), prompt=False, seq_len=None)

    return (y_p.reshape(bsz, seq, d), y_s.reshape(nseq, t_new, d),
            ckv_p.reshape(1, bsz, seq, KV_LORA), kpe_p.reshape(1, bsz, seq, ROPE_DIM),
            conv_p.reshape(1, bsz, CONV_W - 1, CONV_DIM),
            ckv_s.reshape(1, nseq, t_new, KV_LORA), kpe_s.reshape(1, nseq, t_new, ROPE_DIM),
            u_s.reshape(nseq, t_new, CONV_DIM)[:, t_new - (CONV_W - 1):, :].reshape(1, nseq, CONV_W - 1, CONV_DIM))
```

```python
import functools
import math

import jax
import jax.numpy as jnp
from jax import lax
from jax.experimental import pallas as pl
from jax.experimental.pallas import tpu as pltpu

F32 = jnp.float32
BF16 = jnp.bfloat16

D_MODEL = 1024
N_HEADS = 8
NOPE_DIM = 64
ROPE_DIM = 32
V_DIM = 64
Q_LORA = 384
KV_LORA = 256
ATTN_WIDTH = N_HEADS * V_DIM
CONV_DIM = 512
CONV_GROUPS = 8
CONV_GROUP_DIM = CONV_DIM // CONV_GROUPS
CONV_W = 3
ROPE_THETA = 10000.0
PAGE_SIZE = 128
N_EXPERTS = 256
TOP_K = 8
N_GROUPS = 8
GROUP_SIZE = N_EXPERTS // N_GROUPS
TOPK_GROUPS = 4
E_FF = 256
SHARED_FF = 256
ROUTED_SCALE = 2.5
EPS = 1e-6

LANES = 128
HEAD_PAD = LANES
QK_WIDTH = N_HEADS * HEAD_PAD
PE_LO = NOPE_DIM
PE_HALF = ROPE_DIM // 2
IN_PAD_COLS = Q_LORA + KV_LORA + HEAD_PAD + 3 * CONV_DIM
VMEM_LIMIT = 48 * 1024 * 1024
Q_SCALE = (NOPE_DIM + ROPE_DIM) ** -0.5 * math.log2(math.e)
PACK_W = D_MODEL // 2

TILE_PROMPT = 256
TILE_SAMPLE = 128
TQ = 512
TK = 256
ROUTE_CHUNK = 256
COMBINE_TILE = 128


def _cparams(sem, vmem=VMEM_LIMIT):
    return pltpu.CompilerParams(dimension_semantics=sem, vmem_limit_bytes=vmem)


def _dot(a, b):
    return jnp.dot(a, b, preferred_element_type=F32)


def _dot_nt(a, b):
    return lax.dot_general(a, b, (((1,), (1,)), ((), ())), preferred_element_type=F32)


def _rms(x, g):
    r = lax.rsqrt(jnp.mean(x * x, axis=-1, keepdims=True) + EPS)
    return (x * r) * g


def _pack_pair(xb):
    lo = lax.bitcast_convert_type(xb[:, :PACK_W].astype(F32), jnp.uint32) >> 16
    hi = lax.bitcast_convert_type(xb[:, PACK_W:].astype(F32), jnp.uint32) & jnp.uint32(0xFFFF0000)
    return lo | hi


def _unpack_pair(w):
    lo = lax.bitcast_convert_type(w << 16, F32)
    hi = lax.bitcast_convert_type(w & jnp.uint32(0xFFFF0000), F32)
    return lo, hi


def _ada_body(c_ref, w_ref, b_ref, o_ref):
    c = c_ref[...]
    a = (c * jax.nn.sigmoid(c)).astype(BF16)
    o_ref[...] = _dot(a, w_ref[...].astype(BF16)) + b_ref[...]


def _ada(c_all, w, b):
    m, d = c_all.shape
    n = w.shape[1]
    tn = 1024
    return pl.pallas_call(
        _ada_body,
        grid=(n // tn,),
        in_specs=[pl.BlockSpec((m, d), lambda j: (0, 0)),
                  pl.BlockSpec((d, tn), lambda j: (0, j)),
                  pl.BlockSpec((1, tn), lambda j: (0, j))],
        out_specs=pl.BlockSpec((m, tn), lambda j: (0, j)),
        out_shape=jax.ShapeDtypeStruct((m, n), F32),
        compiler_params=_cparams(("arbitrary",)),
        name="ada",
    )(c_all, w, b.reshape(1, n))


def _rope_lanes(x, cos, sin_up, sin_dn):
    w = x.shape[1]
    up = pltpu.roll(x, PE_HALF, 1)
    dn = pltpu.roll(x, w - PE_HALF, 1)
    return x * cos + up * sin_up + dn * sin_dn


def _mix_in_body(*refs, prompt, tm, tiles_per_seq, t_new):
    if prompt:
        (x_ref, sc_ref, sh_ref, g1_ref, win_ref, gq_ref, wq_ref, gkv_ref, wuk_ref, wuv_ref,
         cos_ref, sup_ref, sdn_ref, wconv_ref, gconv_ref, gmat_ref,
         q_ref, k_ref, v_ref, ckv_ref, kpe_ref, z_ref, cst_ref, carry_ref) = refs
    else:
        (x_ref, sc_ref, sh_ref, g1_ref, win_ref, gq_ref, wq_ref, gkv_ref,
         cos_ref, sup_ref, sdn_ref, wconv_ref, gconv_ref, gmat_ref, pa_ref, pb_ref,
         q_ref, ckv_ref, kpe_ref, z_ref, u_ref) = refs

    x = x_ref[...]
    h = _rms(x, g1_ref[...]) * (1.0 + sc_ref[...]) + sh_ref[...]
    proj = _dot(h.astype(BF16), win_ref[...])
    o0 = Q_LORA
    o1 = o0 + KV_LORA
    o2 = o1 + HEAD_PAD
    o3 = o2 + CONV_DIM
    o4 = o3 + CONV_DIM
    q_a, kv_a, kpe_blk = proj[:, :o0], proj[:, o0:o1], proj[:, o1:o2]
    b_g, c_g, u_in = proj[:, o2:o3], proj[:, o3:o4], proj[:, o4:]

    cos, sup, sdn = cos_ref[...], sup_ref[...], sdn_ref[...]
    cos8 = jnp.concatenate([cos] * N_HEADS, axis=1)
    sup8 = jnp.concatenate([sup] * N_HEADS, axis=1)
    sdn8 = jnp.concatenate([sdn] * N_HEADS, axis=1)

    qn = _rms(q_a, gq_ref[...]).astype(BF16)
    q = _dot(qn, wq_ref[...]) * Q_SCALE
    q_ref[...] = _rope_lanes(q, cos8, sup8, sdn8).astype(BF16)

    ckv = _rms(kv_a, gkv_ref[...])
    ckv_ref[...] = ckv
    kpe = _rope_lanes(kpe_blk, cos, sup, sdn)
    kpe_ref[...] = kpe[:, PE_LO:PE_LO + ROPE_DIM]

    if prompt:
        ckvb = ckv.astype(BF16)
        k = _dot(ckvb, wuk_ref[...]) + jnp.concatenate([kpe] * N_HEADS, axis=1)
        k_ref[...] = k.astype(BF16)
        lane = lax.broadcasted_iota(jnp.int32, (1, QK_WIDTH), 1)
        ones_hi = jnp.where(lane % HEAD_PAD >= V_DIM, 1.0, 0.0)
        v_ref[...] = (_dot(ckvb, wuv_ref[...]) + ones_hi).astype(BF16)

    u = c_g * u_in
    row = lax.broadcasted_iota(jnp.int32, (tm, 1), 0)
    r1 = pltpu.roll(u, 1, 0)
    r2 = pltpu.roll(u, 2, 0)
    if prompt:
        @pl.when(pl.program_id(0) % tiles_per_seq == 0)
        def _():
            carry_ref[...] = jnp.zeros_like(carry_ref)
        c6 = carry_ref[6:7, :]
        c7 = carry_ref[7:8, :]
        um1 = jnp.where(row == 0, c7, r1)
        um2 = jnp.where(row == 0, c6, jnp.where(row == 1, c7, r2))
        carry_ref[...] = u[tm - 8:, :]
        cst_ref[...] = u[tm - (CONV_W - 1):, :]
    else:
        t = row % t_new
        um1 = jnp.where(t == 0, pa_ref[...], r1)
        um2 = jnp.where(t == 0, pb_ref[...], jnp.where(t == 1, pa_ref[...], r2))
        u_ref[...] = u
    wc = wconv_ref[...]
    y = um2 * wc[0:1, :] + um1 * wc[1:2, :] + u * wc[2:3, :]
    zz = b_g * y
    ms = _dot((zz * zz).astype(BF16), gmat_ref[...]) * (1.0 / CONV_GROUP_DIM)
    z_ref[...] = (zz * lax.rsqrt(ms + EPS) * gconv_ref[...]).astype(BF16)


def _mix_in(x2d, sc, sh, p, rope_tabs, *, prompt, seq_len=None, prev_a=None, prev_b=None, t_new=1):
    n = x2d.shape[0]
    tm = TILE_PROMPT if prompt else TILE_SAMPLE
    nt = n // tm
    cos_t, sup_t, sdn_t = rope_tabs
    full = lambda a: pl.BlockSpec(a.shape, lambda i: (0,) * a.ndim)
    row_tile = lambda w: pl.BlockSpec((tm, w), lambda i: (i, 0))
    if prompt:
        tps = seq_len // tm
        mod_spec = pl.BlockSpec((None, 1, D_MODEL), lambda i: (i // tps, 0, 0))
        tab_spec = pl.BlockSpec((tm, LANES), lambda i: (i % tps, 0))
        nb = n // seq_len
        ins = [x2d, sc, sh, p["g1"], p["w_in"], p["g_q"], p["wq"], p["g_kv"], p["wuk"], p["wuv"],
               cos_t, sup_t, sdn_t, p["w_conv"], p["g_conv"], p["gmat"]]
        in_specs = [row_tile(D_MODEL), mod_spec, mod_spec, full(p["g1"]), full(p["w_in"]), full(p["g_q"]),
                    full(p["wq"]), full(p["g_kv"]), full(p["wuk"]), full(p["wuv"]),
                    tab_spec, tab_spec, tab_spec, full(p["w_conv"]), full(p["g_conv"]), full(p["gmat"])]
        out_shape = [jax.ShapeDtypeStruct((n, QK_WIDTH), BF16), jax.ShapeDtypeStruct((n, QK_WIDTH), BF16),
                     jax.ShapeDtypeStruct((n, QK_WIDTH), BF16), jax.ShapeDtypeStruct((n, KV_LORA), F32),
                     jax.ShapeDtypeStruct((n, ROPE_DIM), F32), jax.ShapeDtypeStruct((n, CONV_DIM), BF16),
                     jax.ShapeDtypeStruct((nb, CONV_W - 1, CONV_DIM), F32)]
        out_specs = [row_tile(QK_WIDTH), row_tile(QK_WIDTH), row_tile(QK_WIDTH), row_tile(KV_LORA),
                     row_tile(ROPE_DIM), row_tile(CONV_DIM),
                     pl.BlockSpec((None, CONV_W - 1, CONV_DIM), lambda i: (i // tps, 0, 0))]
        scratch = [pltpu.VMEM((8, CONV_DIM), F32)]
    else:
        tps = 1
        ins = [x2d, sc, sh, p["g1"], p["w_in"], p["g_q"], p["wq"], p["g_kv"],
               cos_t, sup_t, sdn_t, p["w_conv"], p["g_conv"], p["gmat"], prev_a, prev_b]
        in_specs = [row_tile(D_MODEL), row_tile(D_MODEL), row_tile(D_MODEL), full(p["g1"]), full(p["w_in"]),
                    full(p["g_q"]), full(p["wq"]), full(p["g_kv"]),
                    row_tile(LANES), row_tile(LANES), row_tile(LANES), full(p["w_conv"]), full(p["g_conv"]),
                    full(p["gmat"]), row_tile(CONV_DIM), row_tile(CONV_DIM)]
        out_shape = [jax.ShapeDtypeStruct((n, QK_WIDTH), BF16), jax.ShapeDtypeStruct((n, KV_LORA), F32),
                     jax.ShapeDtypeStruct((n, ROPE_DIM), F32), jax.ShapeDtypeStruct((n, CONV_DIM), BF16),
                     jax.ShapeDtypeStruct((n, CONV_DIM), F32)]
        out_specs = [row_tile(QK_WIDTH), row_tile(KV_LORA), row_tile(ROPE_DIM), row_tile(CONV_DIM),
                     row_tile(CONV_DIM)]
        scratch = []
    return pl.pallas_call(
        functools.partial(_mix_in_body, prompt=prompt, tm=tm, tiles_per_seq=tps, t_new=t_new),
        grid=(nt,), in_specs=in_specs, out_specs=out_specs, out_shape=out_shape,
        scratch_shapes=scratch, compiler_params=_cparams(("arbitrary",)),
        name="mix_in_prompt" if prompt else "mix_in_sample",
    )(*ins)


def _flash_body(q_ref, k_ref, v_ref, g_ref, gm_ref, o_ref, m_sc, acc_sc, s_sc, *, tq, tk):
    assert tq == 2 * tk
    i = pl.program_id(2)
    m_sc[...] = jnp.full(m_sc.shape, -jnp.inf, F32)
    acc_sc[...] = jnp.zeros(acc_sc.shape, F32)
    row_g = i * tq + lax.broadcasted_iota(jnp.int32, (tq, tk), 0)
    col_l = lax.broadcasted_iota(jnp.int32, (tq, tk), 1)
    heads = [slice(hh * HEAD_PAD, (hh + 1) * HEAD_PAD) for hh in range(2)]

    def logits(t, slot):
        ks = pl.multiple_of(t * tk, tk)
        for hh in range(2):
            s_sc[slot, hh] = _dot_nt(q_ref[:, heads[hh]], k_ref[pl.ds(ks, tk), heads[hh]])

    def consume(t, slot, masked):
        ks = pl.multiple_of(t * tk, tk)
        for hh in range(2):
            s = s_sc[slot, hh]
            if masked:
                s = jnp.where(ks + col_l <= row_g, s, -jnp.inf)
            m_prev = m_sc[hh]
            m_new = jnp.maximum(m_prev, jnp.max(s, axis=1, keepdims=True))
            alpha = jnp.exp2(m_prev - m_new)
            pr = jnp.exp2(s - jnp.concatenate([m_new] * (tk // LANES), axis=1))
            acc_sc[hh] = alpha * acc_sc[hh] + _dot(pr.astype(BF16), v_ref[pl.ds(ks, tk), heads[hh]])
            m_sc[hh] = m_new

    logits(0, 0)

    def tile_pair(p, _):
        logits(2 * p + 1, 1)
        consume(2 * p, 0, False)
        logits(2 * p + 2, 0)
        consume(2 * p + 1, 1, False)
        return 0
    lax.fori_loop(0, i, tile_pair, 0)
    logits(2 * i + 1, 1)
    consume(2 * i, 0, True)
    consume(2 * i + 1, 1, True)

    outs = []
    for hh in range(2):
        acc = acc_sc[hh]
        outs.append(acc * pltpu.roll(1.0 / acc, V_DIM, 1))
    lane = lax.broadcasted_iota(jnp.int32, (tq, LANES), 1)
    o = jnp.where(lane < V_DIM, outs[0], pltpu.roll(outs[1], V_DIM, 1))
    ms = _dot((o * o).astype(BF16), gm_ref[...]) * (1.0 / V_DIM)
    o_ref[...] = (o * lax.rsqrt(ms + EPS) * g_ref[...]).astype(BF16)


def _flash(q, k, v, g_attn, gm_head, batch, seq_len):
    n = q.shape[0]
    nq = seq_len // TQ
    pair = pl.BlockSpec((seq_len, 2 * HEAD_PAD), lambda b, p, i: (b, p))
    return pl.pallas_call(
        functools.partial(_flash_body, tq=TQ, tk=TK),
        grid=(batch, N_HEADS // 2, nq),
        in_specs=[pl.BlockSpec((TQ, 2 * HEAD_PAD), lambda b, p, i: (b * nq + i, p)),
                  pair, pair,
                  pl.BlockSpec((1, 2 * V_DIM), lambda b, p, i: (0, p)),
                  pl.BlockSpec((LANES, LANES), lambda b, p, i: (0, 0))],
        out_specs=pl.BlockSpec((TQ, 2 * V_DIM), lambda b, p, i: (b * nq + i, p)),
        out_shape=jax.ShapeDtypeStruct((n, ATTN_WIDTH), BF16),
        scratch_shapes=[pltpu.VMEM((2, TQ, LANES), F32), pltpu.VMEM((2, TQ, LANES), F32),
                        pltpu.VMEM((2, 2, TQ, TK), F32)],
        compiler_params=_cparams(("arbitrary", "arbitrary", "arbitrary")),
        name="flash_prompt",
    )(q, k, v, g_attn, gm_head)


def _absorb_body(q_ref, w_ref, ql_ref, qp_ref):
    qb = q_ref[...]
    ql_ref[...] = _dot(qb, w_ref[...]).astype(BF16)
    qp_ref[...] = qb[:, PE_LO:PE_LO + ROPE_DIM]


def _absorb(q_s, wuk_t):
    n = q_s.shape[0]
    return pl.pallas_call(
        _absorb_body,
        grid=(N_HEADS,),
        in_specs=[pl.BlockSpec((n, HEAD_PAD), lambda h: (0, h)),
                  pl.BlockSpec((None, HEAD_PAD, KV_LORA), lambda h: (h, 0, 0))],
        out_specs=[pl.BlockSpec((None, n, KV_LORA), lambda h: (h, 0, 0)),
                   pl.BlockSpec((None, n, ROPE_DIM), lambda h: (h, 0, 0))],
        out_shape=[jax.ShapeDtypeStruct((N_HEADS, n, KV_LORA), BF16),
                   jax.ShapeDtypeStruct((N_HEADS, n, ROPE_DIM), BF16)],
        compiler_params=_cparams(("arbitrary",)),
        name="absorb",
    )(q_s, wuk_t)


PAGES_PER_CHUNK = 8


def _paged_body(pt_ref, ql_ref, qp_ref, cn_ref, kn_ref, cckv_ref, ckpe_ref, o_ref,
                ckv_buf, kpe_buf, s_all, kcb, sems, *, n_pages, n_seq, t_new):
    b = pl.program_id(0)
    slot = b % 2

    def fetch(seq, sl):
        def body(pg, _):
            page = pt_ref[seq, pg]
            pltpu.make_async_copy(cckv_ref.at[0, page], ckv_buf.at[sl, pg], sems.at[0, sl]).start()
            pltpu.make_async_copy(ckpe_ref.at[0, page], kpe_buf.at[sl, pg], sems.at[1, sl]).start()
            return 0
        lax.fori_loop(0, n_pages, body, 0)

    @pl.when(b == 0)
    def _():
        fetch(0, 0)

    @pl.when(b + 1 < n_seq)
    def _():
        fetch(b + 1, 1 - slot)

    pltpu.make_async_copy(cckv_ref.at[0, pl.ds(0, n_pages)], ckv_buf.at[slot], sems.at[0, slot]).wait()
    pltpu.make_async_copy(ckpe_ref.at[0, pl.ds(0, n_pages)], kpe_buf.at[slot], sems.at[1, slot]).wait()

    ql = ql_ref[...]
    qp = qp_ref[...]
    rows = ql.shape[0]
    ck = PAGES_PER_CHUNK * PAGE_SIZE

    n_chunks = n_pages // PAGES_PER_CHUNK

    m = jnp.full((rows, 1), -jnp.inf, F32)
    for c in range(n_chunks):
        pages = slice(c * PAGES_PER_CHUNK, (c + 1) * PAGES_PER_CHUNK)
        kc = ckv_buf[slot, pages].reshape(ck, KV_LORA).astype(BF16)
        kcb[c] = kc
        pc = kpe_buf[slot, pages].astype(BF16)
        s_pe = jnp.concatenate([_dot(qp, pc[j]) for j in range(PAGES_PER_CHUNK)], axis=1)
        s = _dot_nt(ql, kc) + s_pe
        s_all[c] = s
        m = jnp.maximum(m, jnp.max(s, axis=1, keepdims=True))

    qlf = ql.astype(F32)
    qpf = qp.astype(F32)
    cn = cn_ref[...]
    kn = kn_ref[...]
    trow = lax.broadcasted_iota(jnp.int32, (rows, 1), 0) % t_new
    s_new = []
    for j in range(t_new):
        sj = (jnp.sum(qlf * cn[j:j + 1, :], axis=1, keepdims=True)
              + jnp.sum(qpf * kn[j:j + 1, :], axis=1, keepdims=True))
        sj = jnp.where(trow >= j, sj, -jnp.inf)
        s_new.append(sj)
        m = jnp.maximum(m, sj)

    l = jnp.zeros((rows, 1), F32)
    acc = jnp.zeros((rows, KV_LORA), F32)
    for c in range(n_chunks):
        pr = jnp.exp2(s_all[c] - m)
        l = l + jnp.sum(pr, axis=1, keepdims=True)
        acc = acc + _dot(pr.astype(BF16), kcb[c])
    for j in range(t_new):
        pj = jnp.exp2(s_new[j] - m)
        l = l + pj
        acc = acc + pj * cn[j:j + 1, :]
    o_ref[...] = acc / l


def _paged(page_table, ql, qp, ckv_new, kpe_new, cache_ckv, cache_kpe):
    n_seq, n_pages = page_table.shape
    rows = ql.shape[1]
    t_new = ckv_new.shape[1]
    grid_spec = pltpu.PrefetchScalarGridSpec(
        num_scalar_prefetch=1,
        grid=(n_seq,),
        in_specs=[pl.BlockSpec((None, rows, KV_LORA), lambda b, pt: (b, 0, 0)),
                  pl.BlockSpec((None, rows, ROPE_DIM), lambda b, pt: (b, 0, 0)),
                  pl.BlockSpec((None, t_new, KV_LORA), lambda b, pt: (b, 0, 0)),
                  pl.BlockSpec((None, t_new, ROPE_DIM), lambda b, pt: (b, 0, 0)),
                  pl.BlockSpec(memory_space=pl.ANY),
                  pl.BlockSpec(memory_space=pl.ANY)],
        out_specs=pl.BlockSpec((None, rows, KV_LORA), lambda b, pt: (b, 0, 0)),
        scratch_shapes=[pltpu.VMEM((2, n_pages, PAGE_SIZE, KV_LORA), F32),
                        pltpu.VMEM((2, n_pages, ROPE_DIM, PAGE_SIZE), F32),
                        pltpu.VMEM((n_pages // PAGES_PER_CHUNK, rows, PAGES_PER_CHUNK * PAGE_SIZE), F32),
                        pltpu.VMEM((n_pages // PAGES_PER_CHUNK, PAGES_PER_CHUNK * PAGE_SIZE, KV_LORA), BF16),
                        pltpu.SemaphoreType.DMA((2, 2))],
    )
    return pl.pallas_call(
        functools.partial(_paged_body, n_pages=n_pages, n_seq=n_seq, t_new=t_new),
        grid_spec=grid_spec,
        out_shape=jax.ShapeDtypeStruct((n_seq, rows, KV_LORA), F32),
        compiler_params=_cparams(("arbitrary",)),
        name="paged_attn",
    )(page_table, ql, qp, ckv_new, kpe_new, cache_ckv, cache_kpe)


def _unabsorb_body(o_ref, w_ref, g_ref, out_ref):
    o = _dot(o_ref[...].astype(BF16), w_ref[...])
    r = lax.rsqrt(jnp.sum(o * o, axis=-1, keepdims=True) * (1.0 / V_DIM) + EPS)
    out_ref[...] = (o * r * g_ref[...]).astype(BF16)


def _unabsorb(o_lat_h, wuv_h, g_h):
    n = o_lat_h.shape[1]
    return pl.pallas_call(
        _unabsorb_body,
        grid=(N_HEADS,),
        in_specs=[pl.BlockSpec((None, n, KV_LORA), lambda h: (h, 0, 0)),
                  pl.BlockSpec((None, KV_LORA, HEAD_PAD), lambda h: (h, 0, 0)),
                  pl.BlockSpec((None, 1, HEAD_PAD), lambda h: (h, 0, 0))],
        out_specs=pl.BlockSpec((None, n, HEAD_PAD), lambda h: (h, 0, 0)),
        out_shape=jax.ShapeDtypeStruct((N_HEADS, n, HEAD_PAD), BF16),
        compiler_params=_cparams(("arbitrary",)),
        name="unabsorb",
    )(o_lat_h, wuv_h, g_h)


def _post_mix_body(x_ref, o_ref, z_ref, gt1_ref, sc_ref, sh_ref, gt2_ref, wo_ref, g2_ref,
                   wr_ref, eb_ref, wsgu_ref, wsd_ref, tri_ref, cin_ref,
                   xs_ref, h2_ref, idx_ref, gw_ref, rank_ref, cout_ref, cnt_ref, *, tm):
    i = pl.program_id(0)

    @pl.when(i == 0)
    def _():
        cnt_ref[...] = cin_ref[...]

    wo = wo_ref[...]
    mix = _dot(o_ref[...], wo[:ATTN_WIDTH, :]) + _dot(z_ref[...], wo[ATTN_WIDTH:, :])
    x1 = x_ref[...] + gt1_ref[...] * mix
    h2 = _rms(x1, g2_ref[...]) * (1.0 + sc_ref[...]) + sh_ref[...]
    hb = h2.astype(BF16)
    h2_ref[...] = _pack_pair(hb)

    gu = _dot(hb, wsgu_ref[...])
    gate, up = gu[:, :SHARED_FF], gu[:, SHARED_FF:]
    shared = _dot((gate * jax.nn.sigmoid(gate) * up).astype(BF16), wsd_ref[...])
    xs_ref[...] = x1 + gt2_ref[...] * shared

    s = jax.nn.sigmoid(_dot_nt(wr_ref[...], hb))
    biased = s + eb_ref[...]
    ninf = -jnp.inf
    gi = lax.broadcasted_iota(jnp.int32, (GROUP_SIZE, tm), 0).astype(F32)
    gs = []
    for g in range(N_GROUPS):
        blk = biased[g * GROUP_SIZE:(g + 1) * GROUP_SIZE, :]
        m1 = jnp.max(blk, axis=0, keepdims=True)
        i1 = jnp.min(jnp.where(blk == m1, gi, float(GROUP_SIZE)), axis=0, keepdims=True)
        m2 = jnp.max(jnp.where(gi == i1, ninf, blk), axis=0, keepdims=True)
        gs.append(m1 + m2)
    gscore = jnp.concatenate(gs, axis=0)
    giota = lax.broadcasted_iota(jnp.int32, (N_GROUPS, tm), 0).astype(F32)
    gsel = jnp.zeros((N_GROUPS, tm), F32)
    for _ in range(TOPK_GROUPS):
        gm = jnp.max(gscore, axis=0, keepdims=True)
        gidx = jnp.min(jnp.where(gscore == gm, giota, float(N_GROUPS)), axis=0, keepdims=True)
        hit = giota == gidx
        gsel = jnp.where(hit, 1.0, gsel)
        gscore = jnp.where(hit, ninf, gscore)
    masked = jnp.concatenate(
        [jnp.where(gsel[g:g + 1, :] > 0.0, biased[g * GROUP_SIZE:(g + 1) * GROUP_SIZE, :], ninf)
         for g in range(N_GROUPS)], axis=0)
    eiota = lax.broadcasted_iota(jnp.int32, (N_EXPERTS, tm), 0).astype(F32)
    idxs, ws = [], []
    selall = jnp.zeros((N_EXPERTS, tm), F32)
    for _ in range(TOP_K):
        mx = jnp.max(masked, axis=0, keepdims=True)
        ei = jnp.min(jnp.where(masked == mx, eiota, float(N_EXPERTS)), axis=0, keepdims=True)
        hit = eiota == ei
        ws.append(jnp.sum(jnp.where(hit, s, 0.0), axis=0, keepdims=True))
        idxs.append(ei)
        selall = jnp.where(hit, 1.0, selall)
        masked = jnp.where(hit, ninf, masked)
    wsum = ws[0]
    for w in ws[1:]:
        wsum = wsum + w
    gw_ref[...] = jnp.concatenate(ws, axis=0) / wsum * ROUTED_SCALE
    idx_ref[...] = jnp.concatenate(idxs, axis=0).astype(jnp.int32)

    before = _dot(selall.astype(BF16), tri_ref[...]) + cnt_ref[:, 0:1]
    ranks = [jnp.sum(jnp.where(eiota == ei, before, 0.0), axis=0, keepdims=True) for ei in idxs]
    rank_ref[...] = jnp.concatenate(ranks, axis=0).astype(jnp.int32)
    cnt_ref[...] = cnt_ref[...] + jnp.sum(selall, axis=1, keepdims=True)
    cout_ref[...] = cnt_ref[...]


def _post_mix(x2d, o, z, gt1, sc2, sh2, gt2, p, cnt_in, h2_buf, *, prompt, seq_len, row_off, n_total):
    n = x2d.shape[0]
    tm = TILE_PROMPT if prompt else TILE_SAMPLE
    nt = n // tm
    full = lambda a: pl.BlockSpec(a.shape, lambda i: (0,) * a.ndim)
    row_tile = lambda w: pl.BlockSpec((tm, w), lambda i: (i, 0))
    if prompt:
        tps = seq_len // tm
        mod_spec = pl.BlockSpec((None, 1, D_MODEL), lambda i: (i // tps, 0, 0))
    else:
        mod_spec = row_tile(D_MODEL)
    tri = p["tri_p"] if prompt else p["tri_s"]
    boff = row_off // tm
    ins = [x2d, o, z, gt1, sc2, sh2, gt2, p["w_out"], p["g2"], p["wr_t"], p["e_bias"], p["ws_gu"], p["ws_d"],
           tri, cnt_in]
    in_specs = [row_tile(D_MODEL), row_tile(ATTN_WIDTH), row_tile(CONV_DIM), mod_spec, mod_spec, mod_spec,
                mod_spec, full(p["w_out"]), full(p["g2"]), full(p["wr_t"]), full(p["e_bias"]), full(p["ws_gu"]),
                full(p["ws_d"]), full(tri), full(cnt_in)]
    out_shape = [jax.ShapeDtypeStruct((n, D_MODEL), F32), jax.ShapeDtypeStruct((n_total, PACK_W), jnp.uint32),
                 jax.ShapeDtypeStruct((TOP_K, n), jnp.int32), jax.ShapeDtypeStruct((TOP_K, n), F32),
                 jax.ShapeDtypeStruct((TOP_K, n), jnp.int32), jax.ShapeDtypeStruct((N_EXPERTS, LANES), F32)]
    col_tile = pl.BlockSpec((TOP_K, tm), lambda i: (0, i))
    out_specs = [row_tile(D_MODEL), pl.BlockSpec((tm, PACK_W), lambda i: (i + boff, 0)),
                 col_tile, col_tile, col_tile, pl.BlockSpec((N_EXPERTS, LANES), lambda i: (0, 0))]
    aliases = {}
    if h2_buf is not None:
        ins.append(h2_buf)
        in_specs.append(pl.BlockSpec(memory_space=pl.ANY))
        aliases = {len(ins) - 1: 1}
        body = lambda *refs: _post_mix_body(*refs[:15], *refs[16:], tm=tm)
    else:
        body = functools.partial(_post_mix_body, tm=tm)
    return pl.pallas_call(
        body, grid=(nt,), in_specs=in_specs, out_specs=out_specs, out_shape=out_shape,
        scratch_shapes=[pltpu.VMEM((N_EXPERTS, LANES), F32)],
        input_output_aliases=aliases,
        compiler_params=_cparams(("arbitrary",)),
        name="post_mix_prompt" if prompt else "post_mix_sample",
    )(*ins)


def _dispatch_body(d_ref, h_ref, xs_ref, stage, sem, *, tm, nt):
    i = pl.program_id(0)
    slot = i % 2
    stage[slot] = h_ref[...]

    def body(rb, _):
        r0 = pl.multiple_of(rb * 8, 8)
        for rr in range(8):
            for k in range(TOP_K):
                d = d_ref[0, 0, k * tm + r0 + rr]
                pltpu.make_async_copy(stage.at[slot, pl.ds(r0 + rr, 1)], xs_ref.at[pl.ds(d, 1)],
                                      sem.at[slot]).start(priority=k % 2)
        return 0
    lax.fori_loop(0, tm // 8, body, 0)

    def drain(sl):
        for _ in range(TOP_K):
            pltpu.make_async_copy(stage.at[sl], xs_ref.at[pl.ds(0, tm)], sem.at[sl]).wait()

    @pl.when(i > 0)
    def _():
        drain(1 - slot)

    @pl.when(i == nt - 1)
    def _():
        drain(slot)


def _dispatch(dest_tiles, h2, n_rows):
    nt, _, w = dest_tiles.shape
    tm = w // TOP_K
    return pl.pallas_call(
        functools.partial(_dispatch_body, tm=tm, nt=nt),
        grid=(nt,),
        in_specs=[pl.BlockSpec((1, 1, w), lambda i: (i, 0, 0), memory_space=pltpu.SMEM),
                  pl.BlockSpec((tm, h2.shape[1]), lambda i: (i, 0))],
        out_specs=pl.BlockSpec(memory_space=pl.ANY),
        out_shape=jax.ShapeDtypeStruct((n_rows, h2.shape[1]), h2.dtype),
        scratch_shapes=[pltpu.VMEM((2, tm, h2.shape[1]), h2.dtype), pltpu.SemaphoreType.DMA((2,))],
        compiler_params=_cparams(("arbitrary",)),
        name="dispatch",
    )(dest_tiles, h2)


def _experts_body(be_ref, nv_ref, nu_ref, nx_ref, x_ref, wg_ref, wu_ref, wd_ref, y_ref,
                  sg, su, sd, wgb, wub, wdb, sem, *, ch):
    c = pl.program_id(0)

    def copies(e):
        return (pltpu.make_async_copy(wg_ref.at[e], sg, sem.at[0]),
                pltpu.make_async_copy(wu_ref.at[e], su, sem.at[1]),
                pltpu.make_async_copy(wd_ref.at[e], sd, sem.at[2]))

    @pl.when(c == 0)
    def _():
        for cp in copies(be_ref[0]):
            cp.start()

    nxt = nx_ref[c]

    @pl.when(nxt >= -1)
    def _():
        for cp in copies(0):
            cp.wait()
        wgb[...] = sg[...].astype(BF16)
        wub[...] = su[...].astype(BF16)
        wdb[...] = sd[...].astype(BF16)

        @pl.when(nxt >= 0)
        def _():
            for cp in copies(nxt):
                cp.start()

    @pl.when(c < nu_ref[0])
    def _():
        row = lax.broadcasted_iota(jnp.int32, (ch, 1), 0)
        w = jnp.where(row < nv_ref[c], x_ref[...], jnp.uint32(0))
        lo, hi = _unpack_pair(w)
        lo, hi = lo.astype(BF16), hi.astype(BF16)
        gate = _dot(lo, wgb[:PACK_W, :]) + _dot(hi, wgb[PACK_W:, :])
        up = _dot(lo, wub[:PACK_W, :]) + _dot(hi, wub[PACK_W:, :])
        act = (gate * jax.nn.sigmoid(gate) * up).astype(BF16)
        y_ref[...] = _pack_pair(_dot(act, wdb[...]).astype(BF16))


def _experts(block_e, n_valid, n_used, next_e, x_sorted, w_e_gate, w_e_up, w_e_down):
    ch = ROUTE_CHUNK
    nch = x_sorted.shape[0] // ch
    last = lambda c, nu: jnp.minimum(c, jnp.maximum(nu[0] - 1, 0))
    grid_spec = pltpu.PrefetchScalarGridSpec(
        num_scalar_prefetch=4,
        grid=(nch,),
        in_specs=[pl.BlockSpec((ch, PACK_W), lambda c, be, nv, nu, nx: (last(c, nu), 0)),
                  pl.BlockSpec(memory_space=pl.ANY), pl.BlockSpec(memory_space=pl.ANY),
                  pl.BlockSpec(memory_space=pl.ANY)],
        out_specs=pl.BlockSpec((ch, PACK_W), lambda c, be, nv, nu, nx: (last(c, nu), 0)),
        scratch_shapes=[pltpu.VMEM((D_MODEL, E_FF), F32), pltpu.VMEM((D_MODEL, E_FF), F32),
                        pltpu.VMEM((E_FF, D_MODEL), F32),
                        pltpu.VMEM((D_MODEL, E_FF), BF16), pltpu.VMEM((D_MODEL, E_FF), BF16),
                        pltpu.VMEM((E_FF, D_MODEL), BF16), pltpu.SemaphoreType.DMA((3,))],
    )
    return pl.pallas_call(
        functools.partial(_experts_body, ch=ch),
        grid_spec=grid_spec,
        out_shape=jax.ShapeDtypeStruct((nch * ch, PACK_W), jnp.uint32),
        compiler_params=_cparams(("arbitrary",)),
        name="experts",
    )(block_e, n_valid, n_used, next_e, x_sorted, w_e_gate, w_e_up, w_e_down)


def _combine_body(dc_ref, dn_ref, y_ref, xs_ref, gw_ref, gt2_ref, gf_ref, fsc_ref, fsh_ref, out_ref,
                  gbuf, sem, *, tm, nt):
    i = pl.program_id(0)
    slot = i % 2

    def issue(d_ref, sl):
        def body(rb, _):
            r0 = pl.multiple_of(rb * 8, 8)
            for rr in range(8):
                for k in range(TOP_K):
                    d = d_ref[0, 0, k * tm + r0 + rr]
                    pltpu.make_async_copy(y_ref.at[pl.ds(d, 1)], gbuf.at[sl, k, pl.ds(r0 + rr, 1)],
                                          sem.at[sl]).start(priority=k % 2)
            return 0
        lax.fori_loop(0, tm // 8, body, 0)

    @pl.when(i == 0)
    def _():
        issue(dc_ref, 0)

    @pl.when(i + 1 < nt)
    def _():
        issue(dn_ref, 1 - slot)

    for k in range(TOP_K):
        pltpu.make_async_copy(y_ref.at[pl.ds(0, tm)], gbuf.at[slot, k], sem.at[slot]).wait()
    gw = gw_ref[...]
    f_lo = jnp.zeros((tm, PACK_W), F32)
    f_hi = jnp.zeros((tm, PACK_W), F32)
    for k in range(TOP_K):
        lo, hi = _unpack_pair(gbuf[slot, k])
        f_lo = f_lo + gw[:, k:k + 1] * lo
        f_hi = f_hi + gw[:, k:k + 1] * hi
    f = jnp.concatenate([f_lo, f_hi], axis=1)
    x2 = xs_ref[...] + gt2_ref[...] * f
    out_ref[...] = _rms(x2, gf_ref[...]) * (1.0 + fsc_ref[...]) + fsh_ref[...]


def _combine(dest_tiles, y_sorted, xs, gw, gt2, g_final, fsc, fsh, *, prompt, seq_len):
    n = xs.shape[0]
    tm = COMBINE_TILE
    nt = n // tm
    row_tile = lambda w: pl.BlockSpec((tm, w), lambda i: (i, 0))
    if prompt:
        tps = seq_len // tm
        mod_spec = pl.BlockSpec((None, 1, D_MODEL), lambda i: (i // tps, 0, 0))
    else:
        mod_spec = row_tile(D_MODEL)
    return pl.pallas_call(
        functools.partial(_combine_body, tm=tm, nt=nt),
        grid=(nt,),
        in_specs=[pl.BlockSpec((1, 1, TOP_K * tm), lambda i: (i, 0, 0), memory_space=pltpu.SMEM),
                  pl.BlockSpec((1, 1, TOP_K * tm), lambda i: (jnp.minimum(i + 1, nt - 1), 0, 0),
                               memory_space=pltpu.SMEM),
                  pl.BlockSpec(memory_space=pl.ANY),
                  row_tile(D_MODEL), row_tile(TOP_K), mod_spec,
                  pl.BlockSpec((1, D_MODEL), lambda i: (0, 0)), mod_spec, mod_spec],
        out_specs=row_tile(D_MODEL),
        out_shape=jax.ShapeDtypeStruct((n, D_MODEL), F32),
        scratch_shapes=[pltpu.VMEM((2, TOP_K, tm, PACK_W), jnp.uint32), pltpu.SemaphoreType.DMA((2,))],
        compiler_params=_cparams(("arbitrary",)),
        name="combine_prompt" if prompt else "combine_sample",
    )(dest_tiles, dest_tiles, y_sorted, xs, gw, gt2, g_final, fsc, fsh)


def _rope_tables(pos):
    inv = 1.0 / (ROPE_THETA ** (jnp.arange(PE_HALF, dtype=F32) / PE_HALF))
    ang = pos.astype(F32)[:, None] * inv[None, :]
    cos, sin = jnp.cos(ang), jnp.sin(ang)
    n = pos.shape[0]
    ones = jnp.ones((n, PE_LO), F32)
    zeros = jnp.zeros((n, PE_LO), F32)
    zh = jnp.zeros((n, PE_HALF), F32)
    tail1 = jnp.ones((n, LANES - PE_LO - ROPE_DIM), F32)
    tail0 = jnp.zeros((n, LANES - PE_LO - ROPE_DIM), F32)
    cos_t = jnp.concatenate([ones, cos, cos, tail1], axis=1)
    sin_up = jnp.concatenate([zeros, zh, sin, tail0], axis=1)
    sin_dn = jnp.concatenate([zeros, -sin, zh, tail0], axis=1)
    return cos_t, sin_up, sin_dn


def _pad_heads(w, width):
    pad = [(0, 0)] * (w.ndim - 1) + [(0, HEAD_PAD - width)]
    w = jnp.pad(w, pad)
    return w.reshape(w.shape[:-2] + (N_HEADS * HEAD_PAD,))


def _block_diag_ones(n, blk):
    r = jnp.arange(n) // blk
    return (r[:, None] == r[None, :]).astype(BF16)


def _prep_weights(w_in, g_attn_norm, g_q, w_q_up, g_kv, w_uk, w_uv, w_conv, g_attn_out, g_conv_out, w_out,
                  g_ffn_norm, w_router, e_bias, w_s_gate, w_s_up, w_s_down):
    o0 = Q_LORA
    o1 = o0 + KV_LORA
    o2 = o1 + ROPE_DIM
    kpe_cols = jnp.pad(w_in[:, o1:o2], ((0, 0), (PE_LO, HEAD_PAD - PE_LO - ROPE_DIM)))
    w_in_p = jnp.concatenate([w_in[:, :o1], kpe_cols, w_in[:, o2:]], axis=1).astype(BF16)
    wq = _pad_heads(w_q_up.reshape(Q_LORA, N_HEADS, NOPE_DIM + ROPE_DIM), NOPE_DIM + ROPE_DIM).astype(BF16)
    wuk = _pad_heads(w_uk, NOPE_DIM).astype(BF16)
    wuv = _pad_heads(w_uv, V_DIM).astype(BF16)
    wuk_t = jnp.pad(jnp.transpose(w_uk, (1, 2, 0)), ((0, 0), (0, HEAD_PAD - NOPE_DIM), (0, 0))).astype(BF16)
    wuv_h = jnp.pad(jnp.transpose(w_uv, (1, 0, 2)), ((0, 0), (0, 0), (0, HEAD_PAD - V_DIM))).astype(BF16)
    g_attn_h = jnp.pad(g_attn_out.reshape(N_HEADS, 1, V_DIM), ((0, 0), (0, 0), (0, HEAD_PAD - V_DIM)))
    tri = lambda t: (jnp.arange(t)[:, None] < jnp.arange(t)[None, :]).astype(BF16)
    return {
        "g1": g_attn_norm.reshape(1, -1), "w_in": w_in_p, "g_q": g_q.reshape(1, -1), "wq": wq,
        "g_kv": g_kv.reshape(1, -1), "wuk": wuk, "wuv": wuv, "wuk_t": wuk_t, "wuv_h": wuv_h,
        "w_conv": w_conv, "g_conv": g_conv_out.reshape(1, -1), "gmat": _block_diag_ones(CONV_DIM, CONV_GROUP_DIM),
        "g_attn": g_attn_out.reshape(1, -1), "g_attn_h": g_attn_h, "gm_head": _block_diag_ones(LANES, V_DIM),
        "w_out": w_out.astype(BF16), "g2": g_ffn_norm.reshape(1, -1), "wr_t": w_router.T.astype(BF16),
        "e_bias": e_bias.reshape(-1, 1), "ws_gu": jnp.concatenate([w_s_gate, w_s_up], axis=1).astype(BF16),
        "ws_d": w_s_down.astype(BF16), "tri_p": tri(TILE_PROMPT), "tri_s": tri(TILE_SAMPLE),
    }


def kernel(x_prompt, x_sample, c_prompt, c_sample, cache_ckv, cache_kpe, state_conv, page_table, w_ada, b_ada, g_attn_norm, w_in, g_q, w_q_up, g_kv, w_uk, w_uv, w_conv, g_attn_out, g_conv_out, w_out, g_ffn_norm, w_router, e_bias, w_e_gate, w_e_up, w_e_down, w_s_gate, w_s_up, w_s_down, w_ada_final, b_ada_final, g_final):
    assert w_ada.shape[0] == 1, "one layer"
    bsz, seq, d = x_prompt.shape
    nseq, t_new, _ = x_sample.shape
    n_p, n_s = bsz * seq, nseq * t_new
    n_tot = n_p + n_s
    past = page_table.shape[1] * PAGE_SIZE

    p = _prep_weights(w_in[0], g_attn_norm[0], g_q[0], w_q_up[0], g_kv[0], w_uk[0], w_uv[0], w_conv[0],
                      g_attn_out[0], g_conv_out[0], w_out[0], g_ffn_norm[0], w_router[0], e_bias[0],
                      w_s_gate[0], w_s_up[0], w_s_down[0])

    c_all = jnp.concatenate([c_prompt, c_sample], axis=0)
    mod = _ada(c_all, w_ada[0], b_ada[0])
    modf = _ada(c_all, w_ada_final, b_ada_final)
    sh1, sc1, gt1, sh2, sc2, gt2 = [mod[:, j * d:(j + 1) * d] for j in range(6)]
    fsh, fsc = modf[:, :d], modf[:, d:]
    per_batch = lambda a: a[:bsz].reshape(bsz, 1, d)
    per_token = lambda a: jnp.repeat(a[bsz:], t_new, axis=0)

    xp = x_prompt.reshape(n_p, d)
    xs_in = x_sample.reshape(n_s, d)
    tabs_p = _rope_tables(jnp.arange(seq, dtype=jnp.int32))
    tabs_s = tuple(jnp.tile(t, (nseq, 1)) for t in _rope_tables(past + jnp.arange(t_new, dtype=jnp.int32)))

    q_p, k_p, v_p, ckv_p, kpe_p, z_p, conv_p = _mix_in(
        xp, per_batch(sc1), per_batch(sh1), p, tabs_p, prompt=True, seq_len=seq)
    o_p = _flash(q_p, k_p, v_p, p["g_attn"], p["gm_head"], bsz, seq)

    prev = state_conv[0]
    prev_a = jnp.repeat(prev[:, 1, :], t_new, axis=0)
    prev_b = jnp.repeat(prev[:, 0, :], t_new, axis=0)
    q_s, ckv_s, kpe_s, z_s, u_s = _mix_in(
        xs_in, per_token(sc1), per_token(sh1), p, tabs_s, prompt=False, prev_a=prev_a, prev_b=prev_b,
        t_new=t_new)
    ql_h, qp_h = _absorb(q_s, p["wuk_t"])
    rows = N_HEADS * t_new
    to_seq = lambda a: jnp.transpose(a.reshape(N_HEADS, nseq, t_new, a.shape[-1]), (1, 0, 2, 3)).reshape(
        nseq, rows, a.shape[-1])
    o_lat = _paged(page_table, to_seq(ql_h), to_seq(qp_h), ckv_s.reshape(nseq, t_new, KV_LORA),
                   kpe_s.reshape(nseq, t_new, ROPE_DIM), cache_ckv, jnp.swapaxes(cache_kpe, 2, 3))
    o_lat_h = jnp.transpose(o_lat.reshape(nseq, N_HEADS, t_new, KV_LORA), (1, 0, 2, 3)).reshape(
        N_HEADS, n_s, KV_LORA)
    o_s_h = _unabsorb(o_lat_h, p["wuv_h"], p["g_attn_h"])
    o_s = jnp.transpose(o_s_h[:, :, :V_DIM], (1, 0, 2)).reshape(n_s, ATTN_WIDTH)

    cnt0 = jnp.zeros((N_EXPERTS, LANES), F32)
    xs_p, h2, idx_p, gw_p, rank_p, cnt_p = _post_mix(
        xp, o_p, z_p, per_batch(gt1), per_batch(sc2), per_batch(sh2), per_batch(gt2), p, cnt0, None,
        prompt=True, seq_len=seq, row_off=0, n_total=n_tot)
    xs_s, h2, idx_s, gw_s, rank_s, cnt_s = _post_mix(
        xs_in, o_s, z_s, per_token(gt1), per_token(sc2), per_token(sh2), per_token(gt2), p, cnt_p, h2,
        prompt=False, seq_len=None, row_off=n_p, n_total=n_tot)

    ch = ROUTE_CHUNK
    counts = cnt_s[:, 0].astype(jnp.int32)
    padded = (counts + ch - 1) // ch * ch
    pend = jnp.cumsum(padded)
    pstart = pend - padded
    idx_all = jnp.concatenate([idx_p, idx_s], axis=1)
    rank_all = jnp.concatenate([rank_p, rank_s], axis=1)
    eids = jnp.arange(N_EXPERTS, dtype=jnp.int32)
    lookup = lambda table, keys: jnp.sum(jnp.where(keys[..., None] == eids, table, 0), axis=-1)
    dest = lookup(pstart, idx_all) + rank_all
    nch = -(-(n_tot * TOP_K) // ch) + N_EXPERTS
    chunk_start = jnp.arange(nch, dtype=jnp.int32) * ch
    block_e = jnp.minimum(jnp.sum((pend[None, :] <= chunk_start[:, None]).astype(jnp.int32), axis=1),
                          N_EXPERTS - 1)
    n_valid = jnp.clip(lookup(pstart + counts, block_e) - chunk_start, 0, ch).astype(jnp.int32)
    n_used_s = (pend[-1] // ch).astype(jnp.int32)
    n_used = n_used_s.reshape(1)
    cidx = jnp.arange(nch, dtype=jnp.int32)
    run_end = lookup(pend, block_e) // ch
    after = jnp.sum(jnp.where(run_end[:, None] == cidx[None, :], block_e[None, :], 0), axis=1)
    is_first = (cidx == lookup(pstart, block_e) // ch) & (cidx < n_used_s)
    next_e = jnp.where(is_first, jnp.where(run_end < n_used_s, after, -1), -2).astype(jnp.int32)

    tmc = COMBINE_TILE

    def dest_tiles(dst):
        n = dst.shape[1]
        return jnp.transpose(dst.reshape(TOP_K, n // tmc, tmc), (1, 0, 2)).reshape(n // tmc, 1, TOP_K * tmc)

    x_sorted = _dispatch(dest_tiles(dest), h2, nch * ch)
    y_sorted = _experts(block_e, n_valid, n_used, next_e, x_sorted, w_e_gate[0], w_e_up[0], w_e_down[0])


    gfin = g_final.reshape(1, d)
    y_p = _combine(dest_tiles(dest[:, :n_p]), y_sorted, xs_p, gw_p.T, per_batch(gt2), gfin,
                   per_batch(fsc), per_batch(fsh), prompt=True, seq_len=seq)
    y_s = _combine(dest_tiles(dest[:, n_p:]), y_sorted, xs_s, gw_s.T, per_token(gt2), gfin,
                   per_token(fsc), per_token(fsh), prompt=False, seq_len=None)

    return (y_p.reshape(bsz, seq, d), y_s.reshape(nseq, t_new, d),
            ckv_p.reshape(1, bsz, seq, KV_LORA), kpe_p.reshape(1, bsz, seq, ROPE_DIM),
            conv_p.reshape(1, bsz, CONV_W - 1, CONV_DIM),
            ckv_s.reshape(1, nseq, t_new, KV_LORA), kpe_s.reshape(1, nseq, t_new, ROPE_DIM),
            u_s.reshape(nseq, t_new, CONV_DIM)[:, t_new - (CONV_W - 1):, :].reshape(1, nseq, CONV_W - 1, CONV_DIM))
```

```python
import functools
import math

import jax
import jax.numpy as jnp
from jax import lax
from jax.experimental import pallas as pl
from jax.experimental.pallas import tpu as pltpu
from jax.experimental.pallas import tpu_sc as plsc

F32 = jnp.float32
BF16 = jnp.bfloat16

D_MODEL = 1024
N_HEADS = 8
NOPE_DIM = 64
ROPE_DIM = 32
V_DIM = 64
Q_LORA = 384
KV_LORA = 256
ATTN_WIDTH = N_HEADS * V_DIM
CONV_DIM = 512
CONV_GROUPS = 8
CONV_GROUP_DIM = CONV_DIM // CONV_GROUPS
CONV_W = 3
ROPE_THETA = 10000.0
PAGE_SIZE = 128
N_EXPERTS = 256
TOP_K = 8
N_GROUPS = 8
GROUP_SIZE = N_EXPERTS // N_GROUPS
TOPK_GROUPS = 4
E_FF = 256
SHARED_FF = 256
ROUTED_SCALE = 2.5
EPS = 1e-6

LANES = 128
HEAD_PAD = LANES
QK_WIDTH = N_HEADS * HEAD_PAD
PE_LO = NOPE_DIM
PE_HALF = ROPE_DIM // 2
IN_PAD_COLS = Q_LORA + KV_LORA + HEAD_PAD + 3 * CONV_DIM
VMEM_LIMIT = 48 * 1024 * 1024
Q_SCALE = (NOPE_DIM + ROPE_DIM) ** -0.5 * math.log2(math.e)
PACK_W = D_MODEL // 2

TILE_PROMPT = 256
TILE_SAMPLE = 128
TQ = 512
TK = 256
ROUTE_CHUNK = 256
COMBINE_TILE = 128


def _cparams(sem, vmem=VMEM_LIMIT):
    return pltpu.CompilerParams(dimension_semantics=sem, vmem_limit_bytes=vmem)


def _dot(a, b):
    return jnp.dot(a, b, preferred_element_type=F32)


def _dot_nt(a, b):
    return lax.dot_general(a, b, (((1,), (1,)), ((), ())), preferred_element_type=F32)


def _rms(x, g):
    r = lax.rsqrt(jnp.mean(x * x, axis=-1, keepdims=True) + EPS)
    return (x * r) * g


def _pack_pair(xb):
    lo = lax.bitcast_convert_type(xb[:, :PACK_W].astype(F32), jnp.uint32) >> 16
    hi = lax.bitcast_convert_type(xb[:, PACK_W:].astype(F32), jnp.uint32) & jnp.uint32(0xFFFF0000)
    return lo | hi


def _unpack_pair(w):
    lo = lax.bitcast_convert_type(w << 16, F32)
    hi = lax.bitcast_convert_type(w & jnp.uint32(0xFFFF0000), F32)
    return lo, hi


def _ada_body(c_ref, w_ref, b_ref, o_ref):
    c = c_ref[...]
    a = (c * jax.nn.sigmoid(c)).astype(BF16)
    o_ref[...] = _dot(a, w_ref[...].astype(BF16)) + b_ref[...]


def _ada(c_all, w, b):
    m, d = c_all.shape
    n = w.shape[1]
    tn = 1024
    return pl.pallas_call(
        _ada_body,
        grid=(n // tn,),
        in_specs=[pl.BlockSpec((m, d), lambda j: (0, 0)),
                  pl.BlockSpec((d, tn), lambda j: (0, j)),
                  pl.BlockSpec((1, tn), lambda j: (0, j))],
        out_specs=pl.BlockSpec((m, tn), lambda j: (0, j)),
        out_shape=jax.ShapeDtypeStruct((m, n), F32),
        compiler_params=_cparams(("arbitrary",)),
        name="ada",
    )(c_all, w, b.reshape(1, n))


def _rope_lanes(x, cos, sin_up, sin_dn):
    w = x.shape[1]
    up = pltpu.roll(x, PE_HALF, 1)
    dn = pltpu.roll(x, w - PE_HALF, 1)
    return x * cos + up * sin_up + dn * sin_dn


def _mix_in_body(*refs, prompt, tm, tiles_per_seq, t_new):
    if prompt:
        (x_ref, sc_ref, sh_ref, g1_ref, win_ref, gq_ref, wq_ref, gkv_ref, wuk_ref, wuv_ref,
         cos_ref, sup_ref, sdn_ref, wconv_ref, gconv_ref, gmat_ref,
         q_ref, k_ref, v_ref, ckv_ref, kpe_ref, z_ref, cst_ref, carry_ref) = refs
    else:
        (x_ref, sc_ref, sh_ref, g1_ref, win_ref, gq_ref, wq_ref, gkv_ref,
         cos_ref, sup_ref, sdn_ref, wconv_ref, gconv_ref, gmat_ref, pa_ref, pb_ref,
         q_ref, ckv_ref, kpe_ref, z_ref, u_ref) = refs

    x = x_ref[...]
    h = _rms(x, g1_ref[...]) * (1.0 + sc_ref[...]) + sh_ref[...]
    proj = _dot(h.astype(BF16), win_ref[...])
    o0 = Q_LORA
    o1 = o0 + KV_LORA
    o2 = o1 + HEAD_PAD
    o3 = o2 + CONV_DIM
    o4 = o3 + CONV_DIM
    q_a, kv_a, kpe_blk = proj[:, :o0], proj[:, o0:o1], proj[:, o1:o2]
    b_g, c_g, u_in = proj[:, o2:o3], proj[:, o3:o4], proj[:, o4:]

    cos, sup, sdn = cos_ref[...], sup_ref[...], sdn_ref[...]
    cos8 = jnp.concatenate([cos] * N_HEADS, axis=1)
    sup8 = jnp.concatenate([sup] * N_HEADS, axis=1)
    sdn8 = jnp.concatenate([sdn] * N_HEADS, axis=1)

    qn = _rms(q_a, gq_ref[...]).astype(BF16)
    q = _dot(qn, wq_ref[...]) * Q_SCALE
    q_ref[...] = _rope_lanes(q, cos8, sup8, sdn8).astype(BF16)

    ckv = _rms(kv_a, gkv_ref[...])
    ckv_ref[...] = ckv
    kpe = _rope_lanes(kpe_blk, cos, sup, sdn)
    kpe_ref[...] = kpe[:, PE_LO:PE_LO + ROPE_DIM]

    if prompt:
        ckvb = ckv.astype(BF16)
        k = _dot(ckvb, wuk_ref[...]) + jnp.concatenate([kpe] * N_HEADS, axis=1)
        k_ref[...] = k.astype(BF16)
        lane = lax.broadcasted_iota(jnp.int32, (1, QK_WIDTH), 1)
        ones_hi = jnp.where(lane % HEAD_PAD >= V_DIM, 1.0, 0.0)
        v_ref[...] = (_dot(ckvb, wuv_ref[...]) + ones_hi).astype(BF16)

    u = c_g * u_in
    row = lax.broadcasted_iota(jnp.int32, (tm, 1), 0)
    r1 = pltpu.roll(u, 1, 0)
    r2 = pltpu.roll(u, 2, 0)
    if prompt:
        @pl.when(pl.program_id(0) % tiles_per_seq == 0)
        def _():
            carry_ref[...] = jnp.zeros_like(carry_ref)
        c6 = carry_ref[6:7, :]
        c7 = carry_ref[7:8, :]
        um1 = jnp.where(row == 0, c7, r1)
        um2 = jnp.where(row == 0, c6, jnp.where(row == 1, c7, r2))
        carry_ref[...] = u[tm - 8:, :]
        cst_ref[...] = u[tm - (CONV_W - 1):, :]
    else:
        t = row % t_new
        um1 = jnp.where(t == 0, pa_ref[...], r1)
        um2 = jnp.where(t == 0, pb_ref[...], jnp.where(t == 1, pa_ref[...], r2))
        u_ref[...] = u
    wc = wconv_ref[...]
    y = um2 * wc[0:1, :] + um1 * wc[1:2, :] + u * wc[2:3, :]
    zz = b_g * y
    ms = _dot((zz * zz).astype(BF16), gmat_ref[...]) * (1.0 / CONV_GROUP_DIM)
    z_ref[...] = (zz * lax.rsqrt(ms + EPS) * gconv_ref[...]).astype(BF16)


def _mix_in(x2d, sc, sh, p, rope_tabs, *, prompt, seq_len=None, prev_a=None, prev_b=None, t_new=1):
    n = x2d.shape[0]
    tm = TILE_PROMPT if prompt else TILE_SAMPLE
    nt = n // tm
    cos_t, sup_t, sdn_t = rope_tabs
    full = lambda a: pl.BlockSpec(a.shape, lambda i: (0,) * a.ndim)
    row_tile = lambda w: pl.BlockSpec((tm, w), lambda i: (i, 0))
    if prompt:
        tps = seq_len // tm
        mod_spec = pl.BlockSpec((None, 1, D_MODEL), lambda i: (i // tps, 0, 0))
        tab_spec = pl.BlockSpec((tm, LANES), lambda i: (i % tps, 0))
        nb = n // seq_len
        ins = [x2d, sc, sh, p["g1"], p["w_in"], p["g_q"], p["wq"], p["g_kv"], p["wuk"], p["wuv"],
               cos_t, sup_t, sdn_t, p["w_conv"], p["g_conv"], p["gmat"]]
        in_specs = [row_tile(D_MODEL), mod_spec, mod_spec, full(p["g1"]), full(p["w_in"]), full(p["g_q"]),
                    full(p["wq"]), full(p["g_kv"]), full(p["wuk"]), full(p["wuv"]),
                    tab_spec, tab_spec, tab_spec, full(p["w_conv"]), full(p["g_conv"]), full(p["gmat"])]
        out_shape = [jax.ShapeDtypeStruct((n, QK_WIDTH), BF16), jax.ShapeDtypeStruct((n, QK_WIDTH), BF16),
                     jax.ShapeDtypeStruct((n, QK_WIDTH), BF16), jax.ShapeDtypeStruct((n, KV_LORA), F32),
                     jax.ShapeDtypeStruct((n, ROPE_DIM), F32), jax.ShapeDtypeStruct((n, CONV_DIM), BF16),
                     jax.ShapeDtypeStruct((nb, CONV_W - 1, CONV_DIM), F32)]
        out_specs = [row_tile(QK_WIDTH), row_tile(QK_WIDTH), row_tile(QK_WIDTH), row_tile(KV_LORA),
                     row_tile(ROPE_DIM), row_tile(CONV_DIM),
                     pl.BlockSpec((None, CONV_W - 1, CONV_DIM), lambda i: (i // tps, 0, 0))]
        scratch = [pltpu.VMEM((8, CONV_DIM), F32)]
    else:
        tps = 1
        ins = [x2d, sc, sh, p["g1"], p["w_in"], p["g_q"], p["wq"], p["g_kv"],
               cos_t, sup_t, sdn_t, p["w_conv"], p["g_conv"], p["gmat"], prev_a, prev_b]
        in_specs = [row_tile(D_MODEL), row_tile(D_MODEL), row_tile(D_MODEL), full(p["g1"]), full(p["w_in"]),
                    full(p["g_q"]), full(p["wq"]), full(p["g_kv"]),
                    row_tile(LANES), row_tile(LANES), row_tile(LANES), full(p["w_conv"]), full(p["g_conv"]),
                    full(p["gmat"]), row_tile(CONV_DIM), row_tile(CONV_DIM)]
        out_shape = [jax.ShapeDtypeStruct((n, QK_WIDTH), BF16), jax.ShapeDtypeStruct((n, KV_LORA), F32),
                     jax.ShapeDtypeStruct((n, ROPE_DIM), F32), jax.ShapeDtypeStruct((n, CONV_DIM), BF16),
                     jax.ShapeDtypeStruct((n, CONV_DIM), F32)]
        out_specs = [row_tile(QK_WIDTH), row_tile(KV_LORA), row_tile(ROPE_DIM), row_tile(CONV_DIM),
                     row_tile(CONV_DIM)]
        scratch = []
    return pl.pallas_call(
        functools.partial(_mix_in_body, prompt=prompt, tm=tm, tiles_per_seq=tps, t_new=t_new),
        grid=(nt,), in_specs=in_specs, out_specs=out_specs, out_shape=out_shape,
        scratch_shapes=scratch, compiler_params=_cparams(("arbitrary",)),
        name="mix_in_prompt" if prompt else "mix_in_sample",
    )(*ins)


def _flash_body(q_ref, k_ref, v_ref, g_ref, gm_ref, o_ref, m_sc, acc_sc, s_sc, *, tq, tk):
    assert tq == 2 * tk
    i = pl.program_id(2)
    m_sc[...] = jnp.full(m_sc.shape, -jnp.inf, F32)
    acc_sc[...] = jnp.zeros(acc_sc.shape, F32)
    row_g = i * tq + lax.broadcasted_iota(jnp.int32, (tq, tk), 0)
    col_l = lax.broadcasted_iota(jnp.int32, (tq, tk), 1)
    heads = [slice(hh * HEAD_PAD, (hh + 1) * HEAD_PAD) for hh in range(2)]

    def logits(t, slot):
        ks = pl.multiple_of(t * tk, tk)
        for hh in range(2):
            s_sc[slot, hh] = _dot_nt(q_ref[:, heads[hh]], k_ref[pl.ds(ks, tk), heads[hh]])

    def consume(t, slot, masked):
        ks = pl.multiple_of(t * tk, tk)
        for hh in range(2):
            s = s_sc[slot, hh]
            if masked:
                s = jnp.where(ks + col_l <= row_g, s, -jnp.inf)
            m_prev = m_sc[hh]
            m_new = jnp.maximum(m_prev, jnp.max(s, axis=1, keepdims=True))
            alpha = jnp.exp2(m_prev - m_new)
            pr = jnp.exp2(s - jnp.concatenate([m_new] * (tk // LANES), axis=1))
            acc_sc[hh] = alpha * acc_sc[hh] + _dot(pr.astype(BF16), v_ref[pl.ds(ks, tk), heads[hh]])
            m_sc[hh] = m_new

    logits(0, 0)

    def tile_pair(p, _):
        logits(2 * p + 1, 1)
        consume(2 * p, 0, False)
        logits(2 * p + 2, 0)
        consume(2 * p + 1, 1, False)
        return 0
    lax.fori_loop(0, i, tile_pair, 0)
    logits(2 * i + 1, 1)
    consume(2 * i, 0, True)
    consume(2 * i + 1, 1, True)

    outs = []
    for hh in range(2):
        acc = acc_sc[hh]
        outs.append(acc * pltpu.roll(1.0 / acc, V_DIM, 1))
    lane = lax.broadcasted_iota(jnp.int32, (tq, LANES), 1)
    o = jnp.where(lane < V_DIM, outs[0], pltpu.roll(outs[1], V_DIM, 1))
    ms = _dot((o * o).astype(BF16), gm_ref[...]) * (1.0 / V_DIM)
    o_ref[...] = (o * lax.rsqrt(ms + EPS) * g_ref[...]).astype(BF16)


def _flash(q, k, v, g_attn, gm_head, batch, seq_len):
    n = q.shape[0]
    nq = seq_len // TQ
    pair = pl.BlockSpec((seq_len, 2 * HEAD_PAD), lambda b, p, i: (b, p))
    return pl.pallas_call(
        functools.partial(_flash_body, tq=TQ, tk=TK),
        grid=(batch, N_HEADS // 2, nq),
        in_specs=[pl.BlockSpec((TQ, 2 * HEAD_PAD), lambda b, p, i: (b * nq + i, p)),
                  pair, pair,
                  pl.BlockSpec((1, 2 * V_DIM), lambda b, p, i: (0, p)),
                  pl.BlockSpec((LANES, LANES), lambda b, p, i: (0, 0))],
        out_specs=pl.BlockSpec((TQ, 2 * V_DIM), lambda b, p, i: (b * nq + i, p)),
        out_shape=jax.ShapeDtypeStruct((n, ATTN_WIDTH), BF16),
        scratch_shapes=[pltpu.VMEM((2, TQ, LANES), F32), pltpu.VMEM((2, TQ, LANES), F32),
                        pltpu.VMEM((2, 2, TQ, TK), F32)],
        compiler_params=_cparams(("arbitrary", "arbitrary", "arbitrary")),
        name="flash_prompt",
    )(q, k, v, g_attn, gm_head)


def _absorb_body(q_ref, w_ref, ql_ref, qp_ref):
    qb = q_ref[...]
    ql_ref[...] = _dot(qb, w_ref[...]).astype(BF16)
    qp_ref[...] = qb[:, PE_LO:PE_LO + ROPE_DIM]


def _absorb(q_s, wuk_t):
    n = q_s.shape[0]
    return pl.pallas_call(
        _absorb_body,
        grid=(N_HEADS,),
        in_specs=[pl.BlockSpec((n, HEAD_PAD), lambda h: (0, h)),
                  pl.BlockSpec((None, HEAD_PAD, KV_LORA), lambda h: (h, 0, 0))],
        out_specs=[pl.BlockSpec((None, n, KV_LORA), lambda h: (h, 0, 0)),
                   pl.BlockSpec((None, n, ROPE_DIM), lambda h: (h, 0, 0))],
        out_shape=[jax.ShapeDtypeStruct((N_HEADS, n, KV_LORA), BF16),
                   jax.ShapeDtypeStruct((N_HEADS, n, ROPE_DIM), BF16)],
        compiler_params=_cparams(("arbitrary",)),
        name="absorb",
    )(q_s, wuk_t)


PAGES_PER_CHUNK = 8


def _paged_body(pt_ref, ql_ref, qp_ref, cn_ref, kn_ref, cckv_ref, ckpe_ref, o_ref,
                ckv_buf, kpe_buf, s_all, kcb, sems, *, n_pages, n_seq, t_new):
    b = pl.program_id(0)
    slot = b % 2

    def fetch(seq, sl):
        def body(pg, _):
            page = pt_ref[seq, pg]
            pltpu.make_async_copy(cckv_ref.at[0, page], ckv_buf.at[sl, pg], sems.at[0, sl]).start()
            pltpu.make_async_copy(ckpe_ref.at[0, page], kpe_buf.at[sl, pg], sems.at[1, sl]).start()
            return 0
        lax.fori_loop(0, n_pages, body, 0)

    @pl.when(b == 0)
    def _():
        fetch(0, 0)

    @pl.when(b + 1 < n_seq)
    def _():
        fetch(b + 1, 1 - slot)

    pltpu.make_async_copy(cckv_ref.at[0, pl.ds(0, n_pages)], ckv_buf.at[slot], sems.at[0, slot]).wait()
    pltpu.make_async_copy(ckpe_ref.at[0, pl.ds(0, n_pages)], kpe_buf.at[slot], sems.at[1, slot]).wait()

    ql = ql_ref[...]
    qp = qp_ref[...]
    rows = ql.shape[0]
    ck = PAGES_PER_CHUNK * PAGE_SIZE

    n_chunks = n_pages // PAGES_PER_CHUNK

    m = jnp.full((rows, 1), -jnp.inf, F32)
    for c in range(n_chunks):
        pages = slice(c * PAGES_PER_CHUNK, (c + 1) * PAGES_PER_CHUNK)
        kc = ckv_buf[slot, pages].reshape(ck, KV_LORA).astype(BF16)
        kcb[c] = kc
        pc = kpe_buf[slot, pages].astype(BF16)
        s_pe = jnp.concatenate([_dot(qp, pc[j]) for j in range(PAGES_PER_CHUNK)], axis=1)
        s = _dot_nt(ql, kc) + s_pe
        s_all[c] = s
        m = jnp.maximum(m, jnp.max(s, axis=1, keepdims=True))

    qlf = ql.astype(F32)
    qpf = qp.astype(F32)
    cn = cn_ref[...]
    kn = kn_ref[...]
    trow = lax.broadcasted_iota(jnp.int32, (rows, 1), 0) % t_new
    s_new = []
    for j in range(t_new):
        sj = (jnp.sum(qlf * cn[j:j + 1, :], axis=1, keepdims=True)
              + jnp.sum(qpf * kn[j:j + 1, :], axis=1, keepdims=True))
        sj = jnp.where(trow >= j, sj, -jnp.inf)
        s_new.append(sj)
        m = jnp.maximum(m, sj)

    l = jnp.zeros((rows, 1), F32)
    acc = jnp.zeros((rows, KV_LORA), F32)
    for c in range(n_chunks):
        pr = jnp.exp2(s_all[c] - m)
        l = l + jnp.sum(pr, axis=1, keepdims=True)
        acc = acc + _dot(pr.astype(BF16), kcb[c])
    for j in range(t_new):
        pj = jnp.exp2(s_new[j] - m)
        l = l + pj
        acc = acc + pj * cn[j:j + 1, :]
    o_ref[...] = acc / l


def _paged(page_table, ql, qp, ckv_new, kpe_new, cache_ckv, cache_kpe):
    n_seq, n_pages = page_table.shape
    rows = ql.shape[1]
    t_new = ckv_new.shape[1]
    grid_spec = pltpu.PrefetchScalarGridSpec(
        num_scalar_prefetch=1,
        grid=(n_seq,),
        in_specs=[pl.BlockSpec((None, rows, KV_LORA), lambda b, pt: (b, 0, 0)),
                  pl.BlockSpec((None, rows, ROPE_DIM), lambda b, pt: (b, 0, 0)),
                  pl.BlockSpec((None, t_new, KV_LORA), lambda b, pt: (b, 0, 0)),
                  pl.BlockSpec((None, t_new, ROPE_DIM), lambda b, pt: (b, 0, 0)),
                  pl.BlockSpec(memory_space=pl.ANY),
                  pl.BlockSpec(memory_space=pl.ANY)],
        out_specs=pl.BlockSpec((None, rows, KV_LORA), lambda b, pt: (b, 0, 0)),
        scratch_shapes=[pltpu.VMEM((2, n_pages, PAGE_SIZE, KV_LORA), F32),
                        pltpu.VMEM((2, n_pages, ROPE_DIM, PAGE_SIZE), F32),
                        pltpu.VMEM((n_pages // PAGES_PER_CHUNK, rows, PAGES_PER_CHUNK * PAGE_SIZE), F32),
                        pltpu.VMEM((n_pages // PAGES_PER_CHUNK, PAGES_PER_CHUNK * PAGE_SIZE, KV_LORA), BF16),
                        pltpu.SemaphoreType.DMA((2, 2))],
    )
    return pl.pallas_call(
        functools.partial(_paged_body, n_pages=n_pages, n_seq=n_seq, t_new=t_new),
        grid_spec=grid_spec,
        out_shape=jax.ShapeDtypeStruct((n_seq, rows, KV_LORA), F32),
        compiler_params=_cparams(("arbitrary",)),
        name="paged_attn",
    )(page_table, ql, qp, ckv_new, kpe_new, cache_ckv, cache_kpe)


def _unabsorb_body(o_ref, w_ref, g_ref, out_ref):
    o = _dot(o_ref[...].astype(BF16), w_ref[...])
    r = lax.rsqrt(jnp.sum(o * o, axis=-1, keepdims=True) * (1.0 / V_DIM) + EPS)
    out_ref[...] = (o * r * g_ref[...]).astype(BF16)


def _unabsorb(o_lat_h, wuv_h, g_h):
    n = o_lat_h.shape[1]
    return pl.pallas_call(
        _unabsorb_body,
        grid=(N_HEADS,),
        in_specs=[pl.BlockSpec((None, n, KV_LORA), lambda h: (h, 0, 0)),
                  pl.BlockSpec((None, KV_LORA, HEAD_PAD), lambda h: (h, 0, 0)),
                  pl.BlockSpec((None, 1, HEAD_PAD), lambda h: (h, 0, 0))],
        out_specs=pl.BlockSpec((None, n, HEAD_PAD), lambda h: (h, 0, 0)),
        out_shape=jax.ShapeDtypeStruct((N_HEADS, n, HEAD_PAD), BF16),
        compiler_params=_cparams(("arbitrary",)),
        name="unabsorb",
    )(o_lat_h, wuv_h, g_h)


def _post_mix_body(x_ref, o_ref, z_ref, gt1_ref, sc_ref, sh_ref, gt2_ref, wo_ref, g2_ref,
                   wr_ref, eb_ref, wsgu_ref, wsd_ref, tri_ref, cin_ref,
                   xs_ref, h2_ref, idx_ref, gw_ref, rank_ref, cout_ref, cnt_ref, *, tm):
    i = pl.program_id(0)

    @pl.when(i == 0)
    def _():
        cnt_ref[...] = cin_ref[...]

    wo = wo_ref[...]
    mix = _dot(o_ref[...], wo[:ATTN_WIDTH, :]) + _dot(z_ref[...], wo[ATTN_WIDTH:, :])
    x1 = x_ref[...] + gt1_ref[...] * mix
    h2 = _rms(x1, g2_ref[...]) * (1.0 + sc_ref[...]) + sh_ref[...]
    hb = h2.astype(BF16)
    h2_ref[...] = _pack_pair(hb)

    gu = _dot(hb, wsgu_ref[...])
    gate, up = gu[:, :SHARED_FF], gu[:, SHARED_FF:]
    shared = _dot((gate * jax.nn.sigmoid(gate) * up).astype(BF16), wsd_ref[...])
    xs_ref[...] = x1 + gt2_ref[...] * shared

    s = jax.nn.sigmoid(_dot_nt(wr_ref[...], hb))
    biased = s + eb_ref[...]
    ninf = -jnp.inf
    gi = lax.broadcasted_iota(jnp.int32, (GROUP_SIZE, tm), 0).astype(F32)
    gs = []
    for g in range(N_GROUPS):
        blk = biased[g * GROUP_SIZE:(g + 1) * GROUP_SIZE, :]
        m1 = jnp.max(blk, axis=0, keepdims=True)
        i1 = jnp.min(jnp.where(blk == m1, gi, float(GROUP_SIZE)), axis=0, keepdims=True)
        m2 = jnp.max(jnp.where(gi == i1, ninf, blk), axis=0, keepdims=True)
        gs.append(m1 + m2)
    gscore = jnp.concatenate(gs, axis=0)
    giota = lax.broadcasted_iota(jnp.int32, (N_GROUPS, tm), 0).astype(F32)
    gsel = jnp.zeros((N_GROUPS, tm), F32)
    for _ in range(TOPK_GROUPS):
        gm = jnp.max(gscore, axis=0, keepdims=True)
        gidx = jnp.min(jnp.where(gscore == gm, giota, float(N_GROUPS)), axis=0, keepdims=True)
        hit = giota == gidx
        gsel = jnp.where(hit, 1.0, gsel)
        gscore = jnp.where(hit, ninf, gscore)
    masked = jnp.concatenate(
        [jnp.where(gsel[g:g + 1, :] > 0.0, biased[g * GROUP_SIZE:(g + 1) * GROUP_SIZE, :], ninf)
         for g in range(N_GROUPS)], axis=0)
    eiota = lax.broadcasted_iota(jnp.int32, (N_EXPERTS, tm), 0).astype(F32)
    idxs, ws = [], []
    selall = jnp.zeros((N_EXPERTS, tm), F32)
    for _ in range(TOP_K):
        mx = jnp.max(masked, axis=0, keepdims=True)
        ei = jnp.min(jnp.where(masked == mx, eiota, float(N_EXPERTS)), axis=0, keepdims=True)
        hit = eiota == ei
        ws.append(jnp.sum(jnp.where(hit, s, 0.0), axis=0, keepdims=True))
        idxs.append(ei)
        selall = jnp.where(hit, 1.0, selall)
        masked = jnp.where(hit, ninf, masked)
    wsum = ws[0]
    for w in ws[1:]:
        wsum = wsum + w
    gw_ref[...] = jnp.concatenate(ws, axis=0) / wsum * ROUTED_SCALE
    idx_ref[...] = jnp.concatenate(idxs, axis=0).astype(jnp.int32)

    before = _dot(selall.astype(BF16), tri_ref[...]) + cnt_ref[:, 0:1]
    ranks = [jnp.sum(jnp.where(eiota == ei, before, 0.0), axis=0, keepdims=True) for ei in idxs]
    rank_ref[...] = jnp.concatenate(ranks, axis=0).astype(jnp.int32)
    cnt_ref[...] = cnt_ref[...] + jnp.sum(selall, axis=1, keepdims=True)
    cout_ref[...] = cnt_ref[...]


def _post_mix(x2d, o, z, gt1, sc2, sh2, gt2, p, cnt_in, h2_buf, *, prompt, seq_len, row_off, n_total):
    n = x2d.shape[0]
    tm = TILE_PROMPT if prompt else TILE_SAMPLE
    nt = n // tm
    full = lambda a: pl.BlockSpec(a.shape, lambda i: (0,) * a.ndim)
    row_tile = lambda w: pl.BlockSpec((tm, w), lambda i: (i, 0))
    if prompt:
        tps = seq_len // tm
        mod_spec = pl.BlockSpec((None, 1, D_MODEL), lambda i: (i // tps, 0, 0))
    else:
        mod_spec = row_tile(D_MODEL)
    tri = p["tri_p"] if prompt else p["tri_s"]
    boff = row_off // tm
    ins = [x2d, o, z, gt1, sc2, sh2, gt2, p["w_out"], p["g2"], p["wr_t"], p["e_bias"], p["ws_gu"], p["ws_d"],
           tri, cnt_in]
    in_specs = [row_tile(D_MODEL), row_tile(ATTN_WIDTH), row_tile(CONV_DIM), mod_spec, mod_spec, mod_spec,
                mod_spec, full(p["w_out"]), full(p["g2"]), full(p["wr_t"]), full(p["e_bias"]), full(p["ws_gu"]),
                full(p["ws_d"]), full(tri), full(cnt_in)]
    out_shape = [jax.ShapeDtypeStruct((n, D_MODEL), F32), jax.ShapeDtypeStruct((n_total, PACK_W), jnp.uint32),
                 jax.ShapeDtypeStruct((TOP_K, n), jnp.int32), jax.ShapeDtypeStruct((TOP_K, n), F32),
                 jax.ShapeDtypeStruct((TOP_K, n), jnp.int32), jax.ShapeDtypeStruct((N_EXPERTS, LANES), F32)]
    col_tile = pl.BlockSpec((TOP_K, tm), lambda i: (0, i))
    out_specs = [row_tile(D_MODEL), pl.BlockSpec((tm, PACK_W), lambda i: (i + boff, 0)),
                 col_tile, col_tile, col_tile, pl.BlockSpec((N_EXPERTS, LANES), lambda i: (0, 0))]
    aliases = {}
    if h2_buf is not None:
        ins.append(h2_buf)
        in_specs.append(pl.BlockSpec(memory_space=pl.ANY))
        aliases = {len(ins) - 1: 1}
        body = lambda *refs: _post_mix_body(*refs[:15], *refs[16:], tm=tm)
    else:
        body = functools.partial(_post_mix_body, tm=tm)
    return pl.pallas_call(
        body, grid=(nt,), in_specs=in_specs, out_specs=out_specs, out_shape=out_shape,
        scratch_shapes=[pltpu.VMEM((N_EXPERTS, LANES), F32)],
        input_output_aliases=aliases,
        compiler_params=_cparams(("arbitrary",)),
        name="post_mix_prompt" if prompt else "post_mix_sample",
    )(*ins)


def _dispatch_body(d_ref, h_ref, xs_ref, stage, sem, *, tm, nt):
    i = pl.program_id(0)
    slot = i % 2
    stage[slot] = h_ref[...]

    def body(rb, _):
        r0 = pl.multiple_of(rb * 8, 8)
        for rr in range(8):
            for k in range(TOP_K):
                d = d_ref[0, 0, k * tm + r0 + rr]
                pltpu.make_async_copy(stage.at[slot, pl.ds(r0 + rr, 1)], xs_ref.at[pl.ds(d, 1)],
                                      sem.at[slot]).start(priority=k % 2)
        return 0
    lax.fori_loop(0, tm // 8, body, 0)

    def drain(sl):
        for _ in range(TOP_K):
            pltpu.make_async_copy(stage.at[sl], xs_ref.at[pl.ds(0, tm)], sem.at[sl]).wait()

    @pl.when(i > 0)
    def _():
        drain(1 - slot)

    @pl.when(i == nt - 1)
    def _():
        drain(slot)


def _dispatch(dest_tiles, h2, n_rows):
    nt, _, w = dest_tiles.shape
    tm = w // TOP_K
    return pl.pallas_call(
        functools.partial(_dispatch_body, tm=tm, nt=nt),
        grid=(nt,),
        in_specs=[pl.BlockSpec((1, 1, w), lambda i: (i, 0, 0), memory_space=pltpu.SMEM),
                  pl.BlockSpec((tm, h2.shape[1]), lambda i: (i, 0))],
        out_specs=pl.BlockSpec(memory_space=pl.ANY),
        out_shape=jax.ShapeDtypeStruct((n_rows, h2.shape[1]), h2.dtype),
        scratch_shapes=[pltpu.VMEM((2, tm, h2.shape[1]), h2.dtype), pltpu.SemaphoreType.DMA((2,))],
        compiler_params=_cparams(("arbitrary",)),
        name="dispatch",
    )(dest_tiles, h2)


def _experts_body(be_ref, nv_ref, nu_ref, nx_ref, x_ref, wg_ref, wu_ref, wd_ref, y_ref,
                  sg, su, sd, wgb, wub, wdb, sem, *, ch):
    c = pl.program_id(0)

    def copies(e):
        return (pltpu.make_async_copy(wg_ref.at[e], sg, sem.at[0]),
                pltpu.make_async_copy(wu_ref.at[e], su, sem.at[1]),
                pltpu.make_async_copy(wd_ref.at[e], sd, sem.at[2]))

    @pl.when(c == 0)
    def _():
        for cp in copies(be_ref[0]):
            cp.start()

    nxt = nx_ref[c]

    @pl.when(nxt >= -1)
    def _():
        for cp in copies(0):
            cp.wait()
        wgb[...] = sg[...].astype(BF16)
        wub[...] = su[...].astype(BF16)
        wdb[...] = sd[...].astype(BF16)

        @pl.when(nxt >= 0)
        def _():
            for cp in copies(nxt):
                cp.start()

    @pl.when(c < nu_ref[0])
    def _():
        row = lax.broadcasted_iota(jnp.int32, (ch, 1), 0)
        w = jnp.where(row < nv_ref[c], x_ref[...], jnp.uint32(0))
        lo, hi = _unpack_pair(w)
        lo, hi = lo.astype(BF16), hi.astype(BF16)
        gate = _dot(lo, wgb[:PACK_W, :]) + _dot(hi, wgb[PACK_W:, :])
        up = _dot(lo, wub[:PACK_W, :]) + _dot(hi, wub[PACK_W:, :])
        act = (gate * jax.nn.sigmoid(gate) * up).astype(BF16)
        y_ref[...] = _pack_pair(_dot(act, wdb[...]).astype(BF16))


def _experts(block_e, n_valid, n_used, next_e, x_sorted, w_e_gate, w_e_up, w_e_down):
    ch = ROUTE_CHUNK
    nch = x_sorted.shape[0] // ch
    last = lambda c, nu: jnp.minimum(c, jnp.maximum(nu[0] - 1, 0))
    grid_spec = pltpu.PrefetchScalarGridSpec(
        num_scalar_prefetch=4,
        grid=(nch,),
        in_specs=[pl.BlockSpec((ch, PACK_W), lambda c, be, nv, nu, nx: (last(c, nu), 0)),
                  pl.BlockSpec(memory_space=pl.ANY), pl.BlockSpec(memory_space=pl.ANY),
                  pl.BlockSpec(memory_space=pl.ANY)],
        out_specs=pl.BlockSpec((ch, PACK_W), lambda c, be, nv, nu, nx: (last(c, nu), 0)),
        scratch_shapes=[pltpu.VMEM((D_MODEL, E_FF), F32), pltpu.VMEM((D_MODEL, E_FF), F32),
                        pltpu.VMEM((E_FF, D_MODEL), F32),
                        pltpu.VMEM((D_MODEL, E_FF), BF16), pltpu.VMEM((D_MODEL, E_FF), BF16),
                        pltpu.VMEM((E_FF, D_MODEL), BF16), pltpu.SemaphoreType.DMA((3,))],
    )
    return pl.pallas_call(
        functools.partial(_experts_body, ch=ch),
        grid_spec=grid_spec,
        out_shape=jax.ShapeDtypeStruct((nch * ch, PACK_W), jnp.uint32),
        compiler_params=_cparams(("arbitrary",)),
        name="experts",
    )(block_e, n_valid, n_used, next_e, x_sorted, w_e_gate, w_e_up, w_e_down)


def _combine_body(dc_ref, dn_ref, y_ref, xs_ref, gw_ref, gt2_ref, gf_ref, fsc_ref, fsh_ref, out_ref,
                  gbuf, sem, *, tm, nt):
    i = pl.program_id(0)
    slot = i % 2

    def issue(d_ref, sl):
        def body(rb, _):
            r0 = pl.multiple_of(rb * 8, 8)
            for rr in range(8):
                for k in range(TOP_K):
                    d = d_ref[0, 0, k * tm + r0 + rr]
                    pltpu.make_async_copy(y_ref.at[pl.ds(d, 1)], gbuf.at[sl, k, pl.ds(r0 + rr, 1)],
                                          sem.at[sl]).start(priority=k % 2)
            return 0
        lax.fori_loop(0, tm // 8, body, 0)

    @pl.when(i == 0)
    def _():
        issue(dc_ref, 0)

    @pl.when(i + 1 < nt)
    def _():
        issue(dn_ref, 1 - slot)

    for k in range(TOP_K):
        pltpu.make_async_copy(y_ref.at[pl.ds(0, tm)], gbuf.at[slot, k], sem.at[slot]).wait()
    gw = gw_ref[...]
    f_lo = jnp.zeros((tm, PACK_W), F32)
    f_hi = jnp.zeros((tm, PACK_W), F32)
    for k in range(TOP_K):
        lo, hi = _unpack_pair(gbuf[slot, k])
        f_lo = f_lo + gw[:, k:k + 1] * lo
        f_hi = f_hi + gw[:, k:k + 1] * hi
    f = jnp.concatenate([f_lo, f_hi], axis=1)
    x2 = xs_ref[...] + gt2_ref[...] * f
    out_ref[...] = _rms(x2, gf_ref[...]) * (1.0 + fsc_ref[...]) + fsh_ref[...]


def _combine(dest_tiles, y_sorted, xs, gw, gt2, g_final, fsc, fsh, *, prompt, seq_len):
    n = xs.shape[0]
    tm = COMBINE_TILE
    nt = n // tm
    row_tile = lambda w: pl.BlockSpec((tm, w), lambda i: (i, 0))
    if prompt:
        tps = seq_len // tm
        mod_spec = pl.BlockSpec((None, 1, D_MODEL), lambda i: (i // tps, 0, 0))
    else:
        mod_spec = row_tile(D_MODEL)
    return pl.pallas_call(
        functools.partial(_combine_body, tm=tm, nt=nt),
        grid=(nt,),
        in_specs=[pl.BlockSpec((1, 1, TOP_K * tm), lambda i: (i, 0, 0), memory_space=pltpu.SMEM),
                  pl.BlockSpec((1, 1, TOP_K * tm), lambda i: (jnp.minimum(i + 1, nt - 1), 0, 0),
                               memory_space=pltpu.SMEM),
                  pl.BlockSpec(memory_space=pl.ANY),
                  row_tile(D_MODEL), row_tile(TOP_K), mod_spec,
                  pl.BlockSpec((1, D_MODEL), lambda i: (0, 0)), mod_spec, mod_spec],
        out_specs=row_tile(D_MODEL),
        out_shape=jax.ShapeDtypeStruct((n, D_MODEL), F32),
        scratch_shapes=[pltpu.VMEM((2, TOP_K, tm, PACK_W), jnp.uint32), pltpu.SemaphoreType.DMA((2,))],
        compiler_params=_cparams(("arbitrary",)),
        name="combine_prompt" if prompt else "combine_sample",
    )(dest_tiles, dest_tiles, y_sorted, xs, gw, gt2, g_final, fsc, fsh)


SC_CORES = 2
SC_SUBCORES = 16
SC_WORKERS = SC_CORES * SC_SUBCORES
SCATTER_WIN = 48
GATHER_WIN = 64


def _sc_mesh():
    return plsc.VectorSubcoreMesh(core_axis_name="c", subcore_axis_name="s")


def _sc_worker():
    return lax.axis_index("s") * SC_CORES + lax.axis_index("c")


def _sc_scatter_rows(rows, idx, n_out):
    n, width = rows.shape
    nw, nwin_k, win = idx.shape
    n_win = nwin_k // TOP_K
    per_w = n // nw

    def body(rows_hbm, idx_hbm, out_hbm, idx_v, buf):
        wid = _sc_worker()
        base = wid * per_w
        pltpu.sync_copy(idx_hbm.at[wid], idx_v)

        @pl.loop(0, n_win)
        def _(j):
            pltpu.sync_copy(rows_hbm.at[pl.ds(base + j * win, win)], buf)
            for k in range(TOP_K):
                pltpu.sync_copy(buf, out_hbm.at[idx_v.at[j * TOP_K + k]])

    return pl.kernel(
        body, out_type=jax.ShapeDtypeStruct((n_out, width), rows.dtype), mesh=_sc_mesh(),
        scratch_types=[pltpu.VMEM((nwin_k, win), jnp.int32), pltpu.VMEM((win, width), rows.dtype)],
        name="sc_scatter_rows",
    )(rows, idx)


def _sc_gather_rows(table, idx):
    nw, n_win, win = idx.shape
    width = table.shape[1]
    per_w = n_win * win

    def body(table_hbm, idx_hbm, out_hbm, idx_v, buf, sem):
        wid = _sc_worker()
        base = wid * per_w
        pltpu.sync_copy(idx_hbm.at[wid], idx_v)

        @pl.loop(0, n_win)
        def _(j):
            pltpu.async_copy(table_hbm.at[idx_v.at[j]], buf, sem).wait()
            pltpu.sync_copy(buf, out_hbm.at[pl.ds(base + j * win, win)])

    return pl.kernel(
        body, out_type=jax.ShapeDtypeStruct((nw * per_w, width), table.dtype), mesh=_sc_mesh(),
        scratch_types=[pltpu.VMEM((n_win, win), jnp.int32), pltpu.VMEM((win, width), table.dtype),
                       pltpu.SemaphoreType.DMA],
        name="sc_gather_rows",
    )(table, idx)


def _finish_body(y_ref, xs_ref, gw_ref, gt2_ref, gf_ref, fsc_ref, fsh_ref, out_ref, *, tm):
    gw = gw_ref[...]
    f_lo = jnp.zeros((tm, PACK_W), F32)
    f_hi = jnp.zeros((tm, PACK_W), F32)
    for k in range(TOP_K):
        lo, hi = _unpack_pair(y_ref[:, k * PACK_W:(k + 1) * PACK_W])
        f_lo = f_lo + gw[:, k:k + 1] * lo
        f_hi = f_hi + gw[:, k:k + 1] * hi
    f = jnp.concatenate([f_lo, f_hi], axis=1)
    x2 = xs_ref[...] + gt2_ref[...] * f
    out_ref[...] = _rms(x2, gf_ref[...]) * (1.0 + fsc_ref[...]) + fsh_ref[...]


def _finish(y_tok, xs, gw, gt2, g_final, fsc, fsh, *, prompt, seq_len, row_off):
    n = xs.shape[0]
    tm = COMBINE_TILE
    nt = n // tm
    boff = row_off // tm
    row_tile = lambda w: pl.BlockSpec((tm, w), lambda i: (i, 0))
    if prompt:
        tps = seq_len // tm
        mod_spec = pl.BlockSpec((None, 1, D_MODEL), lambda i: (i // tps, 0, 0))
    else:
        mod_spec = row_tile(D_MODEL)
    return pl.pallas_call(
        functools.partial(_finish_body, tm=tm),
        grid=(nt,),
        in_specs=[pl.BlockSpec((tm, TOP_K * PACK_W), lambda i: (i + boff, 0)),
                  row_tile(D_MODEL), row_tile(TOP_K), mod_spec,
                  pl.BlockSpec((1, D_MODEL), lambda i: (0, 0)), mod_spec, mod_spec],
        out_specs=row_tile(D_MODEL),
        out_shape=jax.ShapeDtypeStruct((n, D_MODEL), F32),
        compiler_params=_cparams(("arbitrary",)),
        name="finish_prompt" if prompt else "finish_sample",
    )(y_tok, xs, gw, gt2, g_final, fsc, fsh)


def _rope_tables(pos):
    inv = 1.0 / (ROPE_THETA ** (jnp.arange(PE_HALF, dtype=F32) / PE_HALF))
    ang = pos.astype(F32)[:, None] * inv[None, :]
    cos, sin = jnp.cos(ang), jnp.sin(ang)
    n = pos.shape[0]
    ones = jnp.ones((n, PE_LO), F32)
    zeros = jnp.zeros((n, PE_LO), F32)
    zh = jnp.zeros((n, PE_HALF), F32)
    tail1 = jnp.ones((n, LANES - PE_LO - ROPE_DIM), F32)
    tail0 = jnp.zeros((n, LANES - PE_LO - ROPE_DIM), F32)
    cos_t = jnp.concatenate([ones, cos, cos, tail1], axis=1)
    sin_up = jnp.concatenate([zeros, zh, sin, tail0], axis=1)
    sin_dn = jnp.concatenate([zeros, -sin, zh, tail0], axis=1)
    return cos_t, sin_up, sin_dn


def _pad_heads(w, width):
    pad = [(0, 0)] * (w.ndim - 1) + [(0, HEAD_PAD - width)]
    w = jnp.pad(w, pad)
    return w.reshape(w.shape[:-2] + (N_HEADS * HEAD_PAD,))


def _block_diag_ones(n, blk):
    r = jnp.arange(n) // blk
    return (r[:, None] == r[None, :]).astype(BF16)


def _prep_weights(w_in, g_attn_norm, g_q, w_q_up, g_kv, w_uk, w_uv, w_conv, g_attn_out, g_conv_out, w_out,
                  g_ffn_norm, w_router, e_bias, w_s_gate, w_s_up, w_s_down):
    o0 = Q_LORA
    o1 = o0 + KV_LORA
    o2 = o1 + ROPE_DIM
    kpe_cols = jnp.pad(w_in[:, o1:o2], ((0, 0), (PE_LO, HEAD_PAD - PE_LO - ROPE_DIM)))
    w_in_p = jnp.concatenate([w_in[:, :o1], kpe_cols, w_in[:, o2:]], axis=1).astype(BF16)
    wq = _pad_heads(w_q_up.reshape(Q_LORA, N_HEADS, NOPE_DIM + ROPE_DIM), NOPE_DIM + ROPE_DIM).astype(BF16)
    wuk = _pad_heads(w_uk, NOPE_DIM).astype(BF16)
    wuv = _pad_heads(w_uv, V_DIM).astype(BF16)
    wuk_t = jnp.pad(jnp.transpose(w_uk, (1, 2, 0)), ((0, 0), (0, HEAD_PAD - NOPE_DIM), (0, 0))).astype(BF16)
    wuv_h = jnp.pad(jnp.transpose(w_uv, (1, 0, 2)), ((0, 0), (0, 0), (0, HEAD_PAD - V_DIM))).astype(BF16)
    g_attn_h = jnp.pad(g_attn_out.reshape(N_HEADS, 1, V_DIM), ((0, 0), (0, 0), (0, HEAD_PAD - V_DIM)))
    tri = lambda t: (jnp.arange(t)[:, None] < jnp.arange(t)[None, :]).astype(BF16)
    return {
        "g1": g_attn_norm.reshape(1, -1), "w_in": w_in_p, "g_q": g_q.reshape(1, -1), "wq": wq,
        "g_kv": g_kv.reshape(1, -1), "wuk": wuk, "wuv": wuv, "wuk_t": wuk_t, "wuv_h": wuv_h,
        "w_conv": w_conv, "g_conv": g_conv_out.reshape(1, -1), "gmat": _block_diag_ones(CONV_DIM, CONV_GROUP_DIM),
        "g_attn": g_attn_out.reshape(1, -1), "g_attn_h": g_attn_h, "gm_head": _block_diag_ones(LANES, V_DIM),
        "w_out": w_out.astype(BF16), "g2": g_ffn_norm.reshape(1, -1), "wr_t": w_router.T.astype(BF16),
        "e_bias": e_bias.reshape(-1, 1), "ws_gu": jnp.concatenate([w_s_gate, w_s_up], axis=1).astype(BF16),
        "ws_d": w_s_down.astype(BF16), "tri_p": tri(TILE_PROMPT), "tri_s": tri(TILE_SAMPLE),
    }


def kernel(x_prompt, x_sample, c_prompt, c_sample, cache_ckv, cache_kpe, state_conv, page_table, w_ada, b_ada, g_attn_norm, w_in, g_q, w_q_up, g_kv, w_uk, w_uv, w_conv, g_attn_out, g_conv_out, w_out, g_ffn_norm, w_router, e_bias, w_e_gate, w_e_up, w_e_down, w_s_gate, w_s_up, w_s_down, w_ada_final, b_ada_final, g_final):
    assert w_ada.shape[0] == 1, "one layer"
    bsz, seq, d = x_prompt.shape
    nseq, t_new, _ = x_sample.shape
    n_p, n_s = bsz * seq, nseq * t_new
    n_tot = n_p + n_s
    past = page_table.shape[1] * PAGE_SIZE

    p = _prep_weights(w_in[0], g_attn_norm[0], g_q[0], w_q_up[0], g_kv[0], w_uk[0], w_uv[0], w_conv[0],
                      g_attn_out[0], g_conv_out[0], w_out[0], g_ffn_norm[0], w_router[0], e_bias[0],
                      w_s_gate[0], w_s_up[0], w_s_down[0])

    c_all = jnp.concatenate([c_prompt, c_sample], axis=0)
    mod = _ada(c_all, w_ada[0], b_ada[0])
    modf = _ada(c_all, w_ada_final, b_ada_final)
    sh1, sc1, gt1, sh2, sc2, gt2 = [mod[:, j * d:(j + 1) * d] for j in range(6)]
    fsh, fsc = modf[:, :d], modf[:, d:]
    per_batch = lambda a: a[:bsz].reshape(bsz, 1, d)
    per_token = lambda a: jnp.repeat(a[bsz:], t_new, axis=0)

    xp = x_prompt.reshape(n_p, d)
    xs_in = x_sample.reshape(n_s, d)
    tabs_p = _rope_tables(jnp.arange(seq, dtype=jnp.int32))
    tabs_s = tuple(jnp.tile(t, (nseq, 1)) for t in _rope_tables(past + jnp.arange(t_new, dtype=jnp.int32)))

    q_p, k_p, v_p, ckv_p, kpe_p, z_p, conv_p = _mix_in(
        xp, per_batch(sc1), per_batch(sh1), p, tabs_p, prompt=True, seq_len=seq)
    o_p = _flash(q_p, k_p, v_p, p["g_attn"], p["gm_head"], bsz, seq)

    prev = state_conv[0]
    prev_a = jnp.repeat(prev[:, 1, :], t_new, axis=0)
    prev_b = jnp.repeat(prev[:, 0, :], t_new, axis=0)
    q_s, ckv_s, kpe_s, z_s, u_s = _mix_in(
        xs_in, per_token(sc1), per_token(sh1), p, tabs_s, prompt=False, prev_a=prev_a, prev_b=prev_b,
        t_new=t_new)
    ql_h, qp_h = _absorb(q_s, p["wuk_t"])
    rows = N_HEADS * t_new
    to_seq = lambda a: jnp.transpose(a.reshape(N_HEADS, nseq, t_new, a.shape[-1]), (1, 0, 2, 3)).reshape(
        nseq, rows, a.shape[-1])
    o_lat = _paged(page_table, to_seq(ql_h), to_seq(qp_h), ckv_s.reshape(nseq, t_new, KV_LORA),
                   kpe_s.reshape(nseq, t_new, ROPE_DIM), cache_ckv, jnp.swapaxes(cache_kpe, 2, 3))
    o_lat_h = jnp.transpose(o_lat.reshape(nseq, N_HEADS, t_new, KV_LORA), (1, 0, 2, 3)).reshape(
        N_HEADS, n_s, KV_LORA)
    o_s_h = _unabsorb(o_lat_h, p["wuv_h"], p["g_attn_h"])
    o_s = jnp.transpose(o_s_h[:, :, :V_DIM], (1, 0, 2)).reshape(n_s, ATTN_WIDTH)

    cnt0 = jnp.zeros((N_EXPERTS, LANES), F32)
    xs_p, h2, idx_p, gw_p, rank_p, cnt_p = _post_mix(
        xp, o_p, z_p, per_batch(gt1), per_batch(sc2), per_batch(sh2), per_batch(gt2), p, cnt0, None,
        prompt=True, seq_len=seq, row_off=0, n_total=n_tot)
    xs_s, h2, idx_s, gw_s, rank_s, cnt_s = _post_mix(
        xs_in, o_s, z_s, per_token(gt1), per_token(sc2), per_token(sh2), per_token(gt2), p, cnt_p, h2,
        prompt=False, seq_len=None, row_off=n_p, n_total=n_tot)

    ch = ROUTE_CHUNK
    counts = cnt_s[:, 0].astype(jnp.int32)
    padded = (counts + ch - 1) // ch * ch
    pend = jnp.cumsum(padded)
    pstart = pend - padded
    idx_all = jnp.concatenate([idx_p, idx_s], axis=1)
    rank_all = jnp.concatenate([rank_p, rank_s], axis=1)
    eids = jnp.arange(N_EXPERTS, dtype=jnp.int32)
    lookup = lambda table, keys: jnp.sum(jnp.where(keys[..., None] == eids, table, 0), axis=-1)
    dest = lookup(pstart, idx_all) + rank_all
    nch = -(-(n_tot * TOP_K) // ch) + N_EXPERTS
    chunk_start = jnp.arange(nch, dtype=jnp.int32) * ch
    block_e = jnp.minimum(jnp.sum((pend[None, :] <= chunk_start[:, None]).astype(jnp.int32), axis=1),
                          N_EXPERTS - 1)
    n_valid = jnp.clip(lookup(pstart + counts, block_e) - chunk_start, 0, ch).astype(jnp.int32)
    n_used_s = (pend[-1] // ch).astype(jnp.int32)
    n_used = n_used_s.reshape(1)
    cidx = jnp.arange(nch, dtype=jnp.int32)
    run_end = lookup(pend, block_e) // ch
    after = jnp.sum(jnp.where(run_end[:, None] == cidx[None, :], block_e[None, :], 0), axis=1)
    is_first = (cidx == lookup(pstart, block_e) // ch) & (cidx < n_used_s)
    next_e = jnp.where(is_first, jnp.where(run_end < n_used_s, after, -1), -2).astype(jnp.int32)

    tmc = COMBINE_TILE

    def dest_tiles(dst):
        n = dst.shape[1]
        return jnp.transpose(dst.reshape(TOP_K, n // tmc, tmc), (1, 0, 2)).reshape(n // tmc, 1, TOP_K * tmc)

    per_w = n_tot // SC_WORKERS
    idx_sc = jnp.transpose(dest.reshape(TOP_K, SC_WORKERS, per_w // SCATTER_WIN, SCATTER_WIN),
                           (1, 2, 0, 3)).reshape(SC_WORKERS, (per_w // SCATTER_WIN) * TOP_K, SCATTER_WIN)
    idx_ga = dest.T.reshape(SC_WORKERS, (per_w * TOP_K) // GATHER_WIN, GATHER_WIN)
    x_sorted = _sc_scatter_rows(h2, idx_sc, nch * ch)
    y_sorted = _experts(block_e, n_valid, n_used, next_e, x_sorted, w_e_gate[0], w_e_up[0], w_e_down[0])
    y_tok = _sc_gather_rows(y_sorted, idx_ga).reshape(n_tot, TOP_K * PACK_W)

    gfin = g_final.reshape(1, d)
    y_p = _finish(y_tok, xs_p, gw_p.T, per_batch(gt2), gfin, per_batch(fsc), per_batch(fsh),
                  prompt=True, seq_len=seq, row_off=0)
    y_s = _finish(y_tok, xs_s, gw_s.T, per_token(gt2), gfin, per_token(fsc), per_token(fsh),
                  prompt=False, seq_len=None, row_off=n_p)

    return (y_p.reshape(bsz, seq, d), y_s.reshape(nseq, t_new, d),
            ckv_p.reshape(1, bsz, seq, KV_LORA), kpe_p.reshape(1, bsz, seq, ROPE_DIM),
            conv_p.reshape(1, bsz, CONV_W - 1, CONV_DIM),
            ckv_s.reshape(1, nseq, t_new, KV_LORA), kpe_s.reshape(1, nseq, t_new, ROPE_DIM),
            u_s.reshape(nseq, t_new, CONV_DIM)[:, t_new - (CONV_W - 1):, :].reshape(1, nseq, CONV_W - 1, CONV_DIM))
```

```python
import functools
import math

import jax
import jax.numpy as jnp
from jax import lax
from jax.experimental import pallas as pl
from jax.experimental.pallas import tpu as pltpu
from jax.experimental.pallas import tpu_sc as plsc

F32 = jnp.float32
BF16 = jnp.bfloat16

D_MODEL = 1024
N_HEADS = 8
NOPE_DIM = 64
ROPE_DIM = 32
V_DIM = 64
Q_LORA = 384
KV_LORA = 256
ATTN_WIDTH = N_HEADS * V_DIM
CONV_DIM = 512
CONV_GROUPS = 8
CONV_GROUP_DIM = CONV_DIM // CONV_GROUPS
CONV_W = 3
ROPE_THETA = 10000.0
PAGE_SIZE = 128
N_EXPERTS = 256
TOP_K = 8
N_GROUPS = 8
GROUP_SIZE = N_EXPERTS // N_GROUPS
TOPK_GROUPS = 4
E_FF = 256
SHARED_FF = 256
ROUTED_SCALE = 2.5
EPS = 1e-6

LANES = 128
HEAD_PAD = LANES
QK_WIDTH = N_HEADS * HEAD_PAD
PE_LO = NOPE_DIM
PE_HALF = ROPE_DIM // 2
IN_PAD_COLS = Q_LORA + KV_LORA + HEAD_PAD + 3 * CONV_DIM
VMEM_LIMIT = 48 * 1024 * 1024
Q_SCALE = (NOPE_DIM + ROPE_DIM) ** -0.5 * math.log2(math.e)
PACK_W = D_MODEL // 2

TILE_PROMPT = 256
TILE_SAMPLE = 128
TQ = 512
TK = 256
ROUTE_CHUNK = 256
COMBINE_TILE = 128


def _cparams(sem, vmem=VMEM_LIMIT):
    return pltpu.CompilerParams(dimension_semantics=sem, vmem_limit_bytes=vmem)


def _dot(a, b):
    return jnp.dot(a, b, preferred_element_type=F32)


def _dot_nt(a, b):
    return lax.dot_general(a, b, (((1,), (1,)), ((), ())), preferred_element_type=F32)


def _rms(x, g):
    r = lax.rsqrt(jnp.mean(x * x, axis=-1, keepdims=True) + EPS)
    return (x * r) * g


def _pack_pair(xb):
    lo = lax.bitcast_convert_type(xb[:, :PACK_W].astype(F32), jnp.uint32) >> 16
    hi = lax.bitcast_convert_type(xb[:, PACK_W:].astype(F32), jnp.uint32) & jnp.uint32(0xFFFF0000)
    return lo | hi


def _unpack_pair(w):
    lo = lax.bitcast_convert_type(w << 16, F32)
    hi = lax.bitcast_convert_type(w & jnp.uint32(0xFFFF0000), F32)
    return lo, hi


def _ada_body(c_ref, w_ref, b_ref, o_ref):
    c = c_ref[...]
    a = (c * jax.nn.sigmoid(c)).astype(BF16)
    o_ref[...] = _dot(a, w_ref[...].astype(BF16)) + b_ref[...]


def _ada(c_all, w, b):
    m, d = c_all.shape
    n = w.shape[1]
    tn = 1024
    return pl.pallas_call(
        _ada_body,
        grid=(n // tn,),
        in_specs=[pl.BlockSpec((m, d), lambda j: (0, 0)),
                  pl.BlockSpec((d, tn), lambda j: (0, j)),
                  pl.BlockSpec((1, tn), lambda j: (0, j))],
        out_specs=pl.BlockSpec((m, tn), lambda j: (0, j)),
        out_shape=jax.ShapeDtypeStruct((m, n), F32),
        compiler_params=_cparams(("arbitrary",)),
        name="ada",
    )(c_all, w, b.reshape(1, n))


def _rope_lanes(x, cos, sin_up, sin_dn):
    w = x.shape[1]
    up = pltpu.roll(x, PE_HALF, 1)
    dn = pltpu.roll(x, w - PE_HALF, 1)
    return x * cos + up * sin_up + dn * sin_dn


def _mix_in_body(*refs, prompt, tm, tiles_per_seq, t_new):
    if prompt:
        (x_ref, sc_ref, sh_ref, g1_ref, win_ref, gq_ref, wq_ref, gkv_ref, wuk_ref, wuv_ref,
         cos_ref, sup_ref, sdn_ref, wconv_ref, gconv_ref, gmat_ref,
         q_ref, k_ref, v_ref, ckv_ref, kpe_ref, z_ref, cst_ref, carry_ref) = refs
    else:
        (x_ref, sc_ref, sh_ref, g1_ref, win_ref, gq_ref, wq_ref, gkv_ref,
         cos_ref, sup_ref, sdn_ref, wconv_ref, gconv_ref, gmat_ref, pa_ref, pb_ref,
         q_ref, ckv_ref, kpe_ref, z_ref, u_ref) = refs

    x = x_ref[...]
    h = _rms(x, g1_ref[...]) * (1.0 + sc_ref[...]) + sh_ref[...]
    proj = _dot(h.astype(BF16), win_ref[...])
    o0 = Q_LORA
    o1 = o0 + KV_LORA
    o2 = o1 + HEAD_PAD
    o3 = o2 + CONV_DIM
    o4 = o3 + CONV_DIM
    q_a, kv_a, kpe_blk = proj[:, :o0], proj[:, o0:o1], proj[:, o1:o2]
    b_g, c_g, u_in = proj[:, o2:o3], proj[:, o3:o4], proj[:, o4:]

    cos, sup, sdn = cos_ref[...], sup_ref[...], sdn_ref[...]
    cos8 = jnp.concatenate([cos] * N_HEADS, axis=1)
    sup8 = jnp.concatenate([sup] * N_HEADS, axis=1)
    sdn8 = jnp.concatenate([sdn] * N_HEADS, axis=1)

    qn = _rms(q_a, gq_ref[...]).astype(BF16)
    q = _dot(qn, wq_ref[...]) * Q_SCALE
    q_ref[...] = _rope_lanes(q, cos8, sup8, sdn8).astype(BF16)

    ckv = _rms(kv_a, gkv_ref[...])
    ckv_ref[...] = ckv
    kpe = _rope_lanes(kpe_blk, cos, sup, sdn)
    kpe_ref[...] = kpe[:, PE_LO:PE_LO + ROPE_DIM]

    if prompt:
        ckvb = ckv.astype(BF16)
        k = _dot(ckvb, wuk_ref[...]) + jnp.concatenate([kpe] * N_HEADS, axis=1)
        k_ref[...] = k.astype(BF16)
        lane = lax.broadcasted_iota(jnp.int32, (1, QK_WIDTH), 1)
        ones_hi = jnp.where(lane % HEAD_PAD >= V_DIM, 1.0, 0.0)
        v_ref[...] = (_dot(ckvb, wuv_ref[...]) + ones_hi).astype(BF16)

    u = c_g * u_in
    row = lax.broadcasted_iota(jnp.int32, (tm, 1), 0)
    r1 = pltpu.roll(u, 1, 0)
    r2 = pltpu.roll(u, 2, 0)
    if prompt:
        @pl.when(pl.program_id(0) % tiles_per_seq == 0)
        def _():
            carry_ref[...] = jnp.zeros_like(carry_ref)
        c6 = carry_ref[6:7, :]
        c7 = carry_ref[7:8, :]
        um1 = jnp.where(row == 0, c7, r1)
        um2 = jnp.where(row == 0, c6, jnp.where(row == 1, c7, r2))
        carry_ref[...] = u[tm - 8:, :]
        cst_ref[...] = u[tm - (CONV_W - 1):, :]
    else:
        t = row % t_new
        um1 = jnp.where(t == 0, pa_ref[...], r1)
        um2 = jnp.where(t == 0, pb_ref[...], jnp.where(t == 1, pa_ref[...], r2))
        u_ref[...] = u
    wc = wconv_ref[...]
    y = um2 * wc[0:1, :] + um1 * wc[1:2, :] + u * wc[2:3, :]
    zz = b_g * y
    ms = _dot((zz * zz).astype(BF16), gmat_ref[...]) * (1.0 / CONV_GROUP_DIM)
    z_ref[...] = (zz * lax.rsqrt(ms + EPS) * gconv_ref[...]).astype(BF16)


def _mix_in(x2d, sc, sh, p, rope_tabs, *, prompt, seq_len=None, prev_a=None, prev_b=None, t_new=1):
    n = x2d.shape[0]
    tm = TILE_PROMPT if prompt else TILE_SAMPLE
    nt = n // tm
    cos_t, sup_t, sdn_t = rope_tabs
    full = lambda a: pl.BlockSpec(a.shape, lambda i: (0,) * a.ndim)
    row_tile = lambda w: pl.BlockSpec((tm, w), lambda i: (i, 0))
    if prompt:
        tps = seq_len // tm
        mod_spec = pl.BlockSpec((None, 1, D_MODEL), lambda i: (i // tps, 0, 0))
        tab_spec = pl.BlockSpec((tm, LANES), lambda i: (i % tps, 0))
        nb = n // seq_len
        ins = [x2d, sc, sh, p["g1"], p["w_in"], p["g_q"], p["wq"], p["g_kv"], p["wuk"], p["wuv"],
               cos_t, sup_t, sdn_t, p["w_conv"], p["g_conv"], p["gmat"]]
        in_specs = [row_tile(D_MODEL), mod_spec, mod_spec, full(p["g1"]), full(p["w_in"]), full(p["g_q"]),
                    full(p["wq"]), full(p["g_kv"]), full(p["wuk"]), full(p["wuv"]),
                    tab_spec, tab_spec, tab_spec, full(p["w_conv"]), full(p["g_conv"]), full(p["gmat"])]
        out_shape = [jax.ShapeDtypeStruct((n, QK_WIDTH), BF16), jax.ShapeDtypeStruct((n, QK_WIDTH), BF16),
                     jax.ShapeDtypeStruct((n, QK_WIDTH), BF16), jax.ShapeDtypeStruct((n, KV_LORA), F32),
                     jax.ShapeDtypeStruct((n, ROPE_DIM), F32), jax.ShapeDtypeStruct((n, CONV_DIM), BF16),
                     jax.ShapeDtypeStruct((nb, CONV_W - 1, CONV_DIM), F32)]
        out_specs = [row_tile(QK_WIDTH), row_tile(QK_WIDTH), row_tile(QK_WIDTH), row_tile(KV_LORA),
                     row_tile(ROPE_DIM), row_tile(CONV_DIM),
                     pl.BlockSpec((None, CONV_W - 1, CONV_DIM), lambda i: (i // tps, 0, 0))]
        scratch = [pltpu.VMEM((8, CONV_DIM), F32)]
    else:
        tps = 1
        ins = [x2d, sc, sh, p["g1"], p["w_in"], p["g_q"], p["wq"], p["g_kv"],
               cos_t, sup_t, sdn_t, p["w_conv"], p["g_conv"], p["gmat"], prev_a, prev_b]
        in_specs = [row_tile(D_MODEL), row_tile(D_MODEL), row_tile(D_MODEL), full(p["g1"]), full(p["w_in"]),
                    full(p["g_q"]), full(p["wq"]), full(p["g_kv"]),
                    row_tile(LANES), row_tile(LANES), row_tile(LANES), full(p["w_conv"]), full(p["g_conv"]),
                    full(p["gmat"]), row_tile(CONV_DIM), row_tile(CONV_DIM)]
        out_shape = [jax.ShapeDtypeStruct((n, QK_WIDTH), BF16), jax.ShapeDtypeStruct((n, KV_LORA), F32),
                     jax.ShapeDtypeStruct((n, ROPE_DIM), F32), jax.ShapeDtypeStruct((n, CONV_DIM), BF16),
                     jax.ShapeDtypeStruct((n, CONV_DIM), F32)]
        out_specs = [row_tile(QK_WIDTH), row_tile(KV_LORA), row_tile(ROPE_DIM), row_tile(CONV_DIM),
                     row_tile(CONV_DIM)]
        scratch = []
    return pl.pallas_call(
        functools.partial(_mix_in_body, prompt=prompt, tm=tm, tiles_per_seq=tps, t_new=t_new),
        grid=(nt,), in_specs=in_specs, out_specs=out_specs, out_shape=out_shape,
        scratch_shapes=scratch, compiler_params=_cparams(("arbitrary",)),
        name="mix_in_prompt" if prompt else "mix_in_sample",
    )(*ins)


def _flash_body(q_ref, k_ref, v_ref, g_ref, gm_ref, o_ref, m_sc, acc_sc, s_sc, *, tq, tk):
    assert tq == 2 * tk
    i = pl.program_id(2)
    m_sc[...] = jnp.full(m_sc.shape, -jnp.inf, F32)
    acc_sc[...] = jnp.zeros(acc_sc.shape, F32)
    row_g = i * tq + lax.broadcasted_iota(jnp.int32, (tq, tk), 0)
    col_l = lax.broadcasted_iota(jnp.int32, (tq, tk), 1)
    heads = [slice(hh * HEAD_PAD, (hh + 1) * HEAD_PAD) for hh in range(2)]

    def logits(t, slot):
        ks = pl.multiple_of(t * tk, tk)
        for hh in range(2):
            s_sc[slot, hh] = _dot_nt(q_ref[:, heads[hh]], k_ref[pl.ds(ks, tk), heads[hh]])

    def consume(t, slot, masked):
        ks = pl.multiple_of(t * tk, tk)
        for hh in range(2):
            s = s_sc[slot, hh]
            if masked:
                s = jnp.where(ks + col_l <= row_g, s, -jnp.inf)
            m_prev = m_sc[hh]
            m_new = jnp.maximum(m_prev, jnp.max(s, axis=1, keepdims=True))
            alpha = jnp.exp2(m_prev - m_new)
            pr = jnp.exp2(s - jnp.concatenate([m_new] * (tk // LANES), axis=1))
            acc_sc[hh] = alpha * acc_sc[hh] + _dot(pr.astype(BF16), v_ref[pl.ds(ks, tk), heads[hh]])
            m_sc[hh] = m_new

    logits(0, 0)

    def tile_pair(p, _):
        logits(2 * p + 1, 1)
        consume(2 * p, 0, False)
        logits(2 * p + 2, 0)
        consume(2 * p + 1, 1, False)
        return 0
    lax.fori_loop(0, i, tile_pair, 0)
    logits(2 * i + 1, 1)
    consume(2 * i, 0, True)
    consume(2 * i + 1, 1, True)

    outs = []
    for hh in range(2):
        acc = acc_sc[hh]
        outs.append(acc * pltpu.roll(1.0 / acc, V_DIM, 1))
    lane = lax.broadcasted_iota(jnp.int32, (tq, LANES), 1)
    o = jnp.where(lane < V_DIM, outs[0], pltpu.roll(outs[1], V_DIM, 1))
    ms = _dot((o * o).astype(BF16), gm_ref[...]) * (1.0 / V_DIM)
    o_ref[...] = (o * lax.rsqrt(ms + EPS) * g_ref[...]).astype(BF16)


def _flash(q, k, v, g_attn, gm_head, batch, seq_len):
    n = q.shape[0]
    nq = seq_len // TQ
    pair = pl.BlockSpec((seq_len, 2 * HEAD_PAD), lambda b, p, i: (b, p))
    return pl.pallas_call(
        functools.partial(_flash_body, tq=TQ, tk=TK),
        grid=(batch, N_HEADS // 2, nq),
        in_specs=[pl.BlockSpec((TQ, 2 * HEAD_PAD), lambda b, p, i: (b * nq + i, p)),
                  pair, pair,
                  pl.BlockSpec((1, 2 * V_DIM), lambda b, p, i: (0, p)),
                  pl.BlockSpec((LANES, LANES), lambda b, p, i: (0, 0))],
        out_specs=pl.BlockSpec((TQ, 2 * V_DIM), lambda b, p, i: (b * nq + i, p)),
        out_shape=jax.ShapeDtypeStruct((n, ATTN_WIDTH), BF16),
        scratch_shapes=[pltpu.VMEM((2, TQ, LANES), F32), pltpu.VMEM((2, TQ, LANES), F32),
                        pltpu.VMEM((2, 2, TQ, TK), F32)],
        compiler_params=_cparams(("arbitrary", "arbitrary", "arbitrary")),
        name="flash_prompt",
    )(q, k, v, g_attn, gm_head)


def _absorb_body(q_ref, w_ref, ql_ref, qp_ref):
    qb = q_ref[...]
    ql_ref[...] = _dot(qb, w_ref[...]).astype(BF16)
    qp_ref[...] = qb[:, PE_LO:PE_LO + ROPE_DIM]


def _absorb(q_s, wuk_t):
    n = q_s.shape[0]
    return pl.pallas_call(
        _absorb_body,
        grid=(N_HEADS,),
        in_specs=[pl.BlockSpec((n, HEAD_PAD), lambda h: (0, h)),
                  pl.BlockSpec((None, HEAD_PAD, KV_LORA), lambda h: (h, 0, 0))],
        out_specs=[pl.BlockSpec((None, n, KV_LORA), lambda h: (h, 0, 0)),
                   pl.BlockSpec((None, n, ROPE_DIM), lambda h: (h, 0, 0))],
        out_shape=[jax.ShapeDtypeStruct((N_HEADS, n, KV_LORA), BF16),
                   jax.ShapeDtypeStruct((N_HEADS, n, ROPE_DIM), BF16)],
        compiler_params=_cparams(("arbitrary",)),
        name="absorb",
    )(q_s, wuk_t)


PAGES_PER_CHUNK = 8


def _paged_body(pt_ref, ql_ref, qp_ref, cn_ref, kn_ref, cckv_ref, ckpe_ref, o_ref,
                ckv_buf, kpe_buf, s_all, kcb, sems, *, n_pages, n_seq, t_new):
    b = pl.program_id(0)
    slot = b % 2

    def fetch(seq, sl):
        def body(pg, _):
            page = pt_ref[seq, pg]
            pltpu.make_async_copy(cckv_ref.at[0, page], ckv_buf.at[sl, pg], sems.at[0, sl]).start()
            pltpu.make_async_copy(ckpe_ref.at[0, page], kpe_buf.at[sl, pg], sems.at[1, sl]).start()
            return 0
        lax.fori_loop(0, n_pages, body, 0)

    @pl.when(b == 0)
    def _():
        fetch(0, 0)

    @pl.when(b + 1 < n_seq)
    def _():
        fetch(b + 1, 1 - slot)

    pltpu.make_async_copy(cckv_ref.at[0, pl.ds(0, n_pages)], ckv_buf.at[slot], sems.at[0, slot]).wait()
    pltpu.make_async_copy(ckpe_ref.at[0, pl.ds(0, n_pages)], kpe_buf.at[slot], sems.at[1, slot]).wait()

    ql = ql_ref[...]
    qp = qp_ref[...]
    rows = ql.shape[0]
    ck = PAGES_PER_CHUNK * PAGE_SIZE

    n_chunks = n_pages // PAGES_PER_CHUNK

    m = jnp.full((rows, 1), -jnp.inf, F32)
    for c in range(n_chunks):
        pages = slice(c * PAGES_PER_CHUNK, (c + 1) * PAGES_PER_CHUNK)
        kc = ckv_buf[slot, pages].reshape(ck, KV_LORA).astype(BF16)
        kcb[c] = kc
        pc = kpe_buf[slot, pages].astype(BF16)
        s_pe = jnp.concatenate([_dot(qp, pc[j]) for j in range(PAGES_PER_CHUNK)], axis=1)
        s = _dot_nt(ql, kc) + s_pe
        s_all[c] = s
        m = jnp.maximum(m, jnp.max(s, axis=1, keepdims=True))

    qlf = ql.astype(F32)
    qpf = qp.astype(F32)
    cn = cn_ref[...]
    kn = kn_ref[...]
    trow = lax.broadcasted_iota(jnp.int32, (rows, 1), 0) % t_new
    s_new = []
    for j in range(t_new):
        sj = (jnp.sum(qlf * cn[j:j + 1, :], axis=1, keepdims=True)
              + jnp.sum(qpf * kn[j:j + 1, :], axis=1, keepdims=True))
        sj = jnp.where(trow >= j, sj, -jnp.inf)
        s_new.append(sj)
        m = jnp.maximum(m, sj)

    l = jnp.zeros((rows, 1), F32)
    acc = jnp.zeros((rows, KV_LORA), F32)
    for c in range(n_chunks):
        pr = jnp.exp2(s_all[c] - m)
        l = l + jnp.sum(pr, axis=1, keepdims=True)
        acc = acc + _dot(pr.astype(BF16), kcb[c])
    for j in range(t_new):
        pj = jnp.exp2(s_new[j] - m)
        l = l + pj
        acc = acc + pj * cn[j:j + 1, :]
    o_ref[...] = acc / l


def _paged(page_table, ql, qp, ckv_new, kpe_new, cache_ckv, cache_kpe):
    n_seq, n_pages = page_table.shape
    rows = ql.shape[1]
    t_new = ckv_new.shape[1]
    grid_spec = pltpu.PrefetchScalarGridSpec(
        num_scalar_prefetch=1,
        grid=(n_seq,),
        in_specs=[pl.BlockSpec((None, rows, KV_LORA), lambda b, pt: (b, 0, 0)),
                  pl.BlockSpec((None, rows, ROPE_DIM), lambda b, pt: (b, 0, 0)),
                  pl.BlockSpec((None, t_new, KV_LORA), lambda b, pt: (b, 0, 0)),
                  pl.BlockSpec((None, t_new, ROPE_DIM), lambda b, pt: (b, 0, 0)),
                  pl.BlockSpec(memory_space=pl.ANY),
                  pl.BlockSpec(memory_space=pl.ANY)],
        out_specs=pl.BlockSpec((None, rows, KV_LORA), lambda b, pt: (b, 0, 0)),
        scratch_shapes=[pltpu.VMEM((2, n_pages, PAGE_SIZE, KV_LORA), F32),
                        pltpu.VMEM((2, n_pages, ROPE_DIM, PAGE_SIZE), F32),
                        pltpu.VMEM((n_pages // PAGES_PER_CHUNK, rows, PAGES_PER_CHUNK * PAGE_SIZE), F32),
                        pltpu.VMEM((n_pages // PAGES_PER_CHUNK, PAGES_PER_CHUNK * PAGE_SIZE, KV_LORA), BF16),
                        pltpu.SemaphoreType.DMA((2, 2))],
    )
    return pl.pallas_call(
        functools.partial(_paged_body, n_pages=n_pages, n_seq=n_seq, t_new=t_new),
        grid_spec=grid_spec,
        out_shape=jax.ShapeDtypeStruct((n_seq, rows, KV_LORA), F32),
        compiler_params=_cparams(("arbitrary",)),
        name="paged_attn",
    )(page_table, ql, qp, ckv_new, kpe_new, cache_ckv, cache_kpe)


def _unabsorb_body(o_ref, w_ref, g_ref, out_ref):
    o = _dot(o_ref[...].astype(BF16), w_ref[...])
    r = lax.rsqrt(jnp.sum(o * o, axis=-1, keepdims=True) * (1.0 / V_DIM) + EPS)
    out_ref[...] = (o * r * g_ref[...]).astype(BF16)


def _unabsorb(o_lat_h, wuv_h, g_h):
    n = o_lat_h.shape[1]
    return pl.pallas_call(
        _unabsorb_body,
        grid=(N_HEADS,),
        in_specs=[pl.BlockSpec((None, n, KV_LORA), lambda h: (h, 0, 0)),
                  pl.BlockSpec((None, KV_LORA, HEAD_PAD), lambda h: (h, 0, 0)),
                  pl.BlockSpec((None, 1, HEAD_PAD), lambda h: (h, 0, 0))],
        out_specs=pl.BlockSpec((None, n, HEAD_PAD), lambda h: (h, 0, 0)),
        out_shape=jax.ShapeDtypeStruct((N_HEADS, n, HEAD_PAD), BF16),
        compiler_params=_cparams(("arbitrary",)),
        name="unabsorb",
    )(o_lat_h, wuv_h, g_h)


def _post_mix_body(x_ref, o_ref, z_ref, gt1_ref, sc_ref, sh_ref, gt2_ref, wo_ref, g2_ref,
                   wr_ref, eb_ref, wsgu_ref, wsd_ref, tri_ref, cin_ref,
                   xs_ref, h2_ref, idx_ref, gw_ref, rank_ref, cout_ref, cnt_ref, *, tm):
    i = pl.program_id(0)

    @pl.when(i == 0)
    def _():
        cnt_ref[...] = cin_ref[...]

    wo = wo_ref[...]
    mix = _dot(o_ref[...], wo[:ATTN_WIDTH, :]) + _dot(z_ref[...], wo[ATTN_WIDTH:, :])
    x1 = x_ref[...] + gt1_ref[...] * mix
    h2 = _rms(x1, g2_ref[...]) * (1.0 + sc_ref[...]) + sh_ref[...]
    hb = h2.astype(BF16)
    h2_ref[...] = _pack_pair(hb)

    gu = _dot(hb, wsgu_ref[...])
    gate, up = gu[:, :SHARED_FF], gu[:, SHARED_FF:]
    shared = _dot((gate * jax.nn.sigmoid(gate) * up).astype(BF16), wsd_ref[...])
    xs_ref[...] = x1 + gt2_ref[...] * shared

    s = jax.nn.sigmoid(_dot_nt(wr_ref[...], hb))
    biased = s + eb_ref[...]
    ninf = -jnp.inf
    gi = lax.broadcasted_iota(jnp.int32, (GROUP_SIZE, tm), 0).astype(F32)
    gs = []
    for g in range(N_GROUPS):
        blk = biased[g * GROUP_SIZE:(g + 1) * GROUP_SIZE, :]
        m1 = jnp.max(blk, axis=0, keepdims=True)
        i1 = jnp.min(jnp.where(blk == m1, gi, float(GROUP_SIZE)), axis=0, keepdims=True)
        m2 = jnp.max(jnp.where(gi == i1, ninf, blk), axis=0, keepdims=True)
        gs.append(m1 + m2)
    gscore = jnp.concatenate(gs, axis=0)
    giota = lax.broadcasted_iota(jnp.int32, (N_GROUPS, tm), 0).astype(F32)
    gsel = jnp.zeros((N_GROUPS, tm), F32)
    for _ in range(TOPK_GROUPS):
        gm = jnp.max(gscore, axis=0, keepdims=True)
        gidx = jnp.min(jnp.where(gscore == gm, giota, float(N_GROUPS)), axis=0, keepdims=True)
        hit = giota == gidx
        gsel = jnp.where(hit, 1.0, gsel)
        gscore = jnp.where(hit, ninf, gscore)
    masked = jnp.concatenate(
        [jnp.where(gsel[g:g + 1, :] > 0.0, biased[g * GROUP_SIZE:(g + 1) * GROUP_SIZE, :], ninf)
         for g in range(N_GROUPS)], axis=0)
    eiota = lax.broadcasted_iota(jnp.int32, (N_EXPERTS, tm), 0).astype(F32)
    idxs, ws = [], []
    selall = jnp.zeros((N_EXPERTS, tm), F32)
    for _ in range(TOP_K):
        mx = jnp.max(masked, axis=0, keepdims=True)
        ei = jnp.min(jnp.where(masked == mx, eiota, float(N_EXPERTS)), axis=0, keepdims=True)
        hit = eiota == ei
        ws.append(jnp.sum(jnp.where(hit, s, 0.0), axis=0, keepdims=True))
        idxs.append(ei)
        selall = jnp.where(hit, 1.0, selall)
        masked = jnp.where(hit, ninf, masked)
    wsum = ws[0]
    for w in ws[1:]:
        wsum = wsum + w
    gw_ref[...] = jnp.concatenate(ws, axis=0) / wsum * ROUTED_SCALE
    idx_ref[...] = jnp.concatenate(idxs, axis=0).astype(jnp.int32)

    before = _dot(selall.astype(BF16), tri_ref[...]) + cnt_ref[:, 0:1]
    ranks = [jnp.sum(jnp.where(eiota == ei, before, 0.0), axis=0, keepdims=True) for ei in idxs]
    rank_ref[...] = jnp.concatenate(ranks, axis=0).astype(jnp.int32)
    cnt_ref[...] = cnt_ref[...] + jnp.sum(selall, axis=1, keepdims=True)
    cout_ref[...] = cnt_ref[...]


def _post_mix(x2d, o, z, gt1, sc2, sh2, gt2, p, cnt_in, h2_buf, *, prompt, seq_len, row_off, n_total):
    n = x2d.shape[0]
    tm = TILE_PROMPT if prompt else TILE_SAMPLE
    nt = n // tm
    full = lambda a: pl.BlockSpec(a.shape, lambda i: (0,) * a.ndim)
    row_tile = lambda w: pl.BlockSpec((tm, w), lambda i: (i, 0))
    if prompt:
        tps = seq_len // tm
        mod_spec = pl.BlockSpec((None, 1, D_MODEL), lambda i: (i // tps, 0, 0))
    else:
        mod_spec = row_tile(D_MODEL)
    tri = p["tri_p"] if prompt else p["tri_s"]
    boff = row_off // tm
    ins = [x2d, o, z, gt1, sc2, sh2, gt2, p["w_out"], p["g2"], p["wr_t"], p["e_bias"], p["ws_gu"], p["ws_d"],
           tri, cnt_in]
    in_specs = [row_tile(D_MODEL), row_tile(ATTN_WIDTH), row_tile(CONV_DIM), mod_spec, mod_spec, mod_spec,
                mod_spec, full(p["w_out"]), full(p["g2"]), full(p["wr_t"]), full(p["e_bias"]), full(p["ws_gu"]),
                full(p["ws_d"]), full(tri), full(cnt_in)]
    out_shape = [jax.ShapeDtypeStruct((n, D_MODEL), F32), jax.ShapeDtypeStruct((n_total, PACK_W), jnp.uint32),
                 jax.ShapeDtypeStruct((TOP_K, n), jnp.int32), jax.ShapeDtypeStruct((TOP_K, n), F32),
                 jax.ShapeDtypeStruct((TOP_K, n), jnp.int32), jax.ShapeDtypeStruct((N_EXPERTS, LANES), F32)]
    col_tile = pl.BlockSpec((TOP_K, tm), lambda i: (0, i))
    out_specs = [row_tile(D_MODEL), pl.BlockSpec((tm, PACK_W), lambda i: (i + boff, 0)),
                 col_tile, col_tile, col_tile, pl.BlockSpec((N_EXPERTS, LANES), lambda i: (0, 0))]
    aliases = {}
    if h2_buf is not None:
        ins.append(h2_buf)
        in_specs.append(pl.BlockSpec(memory_space=pl.ANY))
        aliases = {len(ins) - 1: 1}
        body = lambda *refs: _post_mix_body(*refs[:15], *refs[16:], tm=tm)
    else:
        body = functools.partial(_post_mix_body, tm=tm)
    return pl.pallas_call(
        body, grid=(nt,), in_specs=in_specs, out_specs=out_specs, out_shape=out_shape,
        scratch_shapes=[pltpu.VMEM((N_EXPERTS, LANES), F32)],
        input_output_aliases=aliases,
        compiler_params=_cparams(("arbitrary",)),
        name="post_mix_prompt" if prompt else "post_mix_sample",
    )(*ins)


def _experts_body(be_ref, nv_ref, nu_ref, nx_ref, x_ref, wg_ref, wu_ref, wd_ref, y_ref,
                  sg, su, sd, wgb, wub, wdb, sem, *, ch):
    c = pl.program_id(0)

    def copies(e):
        return (pltpu.make_async_copy(wg_ref.at[e], sg, sem.at[0]),
                pltpu.make_async_copy(wu_ref.at[e], su, sem.at[1]),
                pltpu.make_async_copy(wd_ref.at[e], sd, sem.at[2]))

    @pl.when(c == 0)
    def _():
        for cp in copies(be_ref[0]):
            cp.start()

    nxt = nx_ref[c]

    @pl.when(nxt >= -1)
    def _():
        for cp in copies(0):
            cp.wait()
        wgb[...] = sg[...].astype(BF16)
        wub[...] = su[...].astype(BF16)
        wdb[...] = sd[...].astype(BF16)

        @pl.when(nxt >= 0)
        def _():
            for cp in copies(nxt):
                cp.start()

    @pl.when(c < nu_ref[0])
    def _():
        row = lax.broadcasted_iota(jnp.int32, (ch, 1), 0)
        w = jnp.where(row < nv_ref[c], x_ref[...], jnp.uint32(0))
        lo, hi = _unpack_pair(w)
        lo, hi = lo.astype(BF16), hi.astype(BF16)
        gate = _dot(lo, wgb[:PACK_W, :]) + _dot(hi, wgb[PACK_W:, :])
        up = _dot(lo, wub[:PACK_W, :]) + _dot(hi, wub[PACK_W:, :])
        act = (gate * jax.nn.sigmoid(gate) * up).astype(BF16)
        y_ref[...] = _pack_pair(_dot(act, wdb[...]).astype(BF16))


def _experts(block_e, n_valid, n_used, next_e, x_sorted, w_e_gate, w_e_up, w_e_down):
    ch = ROUTE_CHUNK
    nch = x_sorted.shape[0] // ch
    last = lambda c, nu: jnp.minimum(c, jnp.maximum(nu[0] - 1, 0))
    grid_spec = pltpu.PrefetchScalarGridSpec(
        num_scalar_prefetch=4,
        grid=(nch,),
        in_specs=[pl.BlockSpec((ch, PACK_W), lambda c, be, nv, nu, nx: (last(c, nu), 0)),
                  pl.BlockSpec(memory_space=pl.ANY), pl.BlockSpec(memory_space=pl.ANY),
                  pl.BlockSpec(memory_space=pl.ANY)],
        out_specs=pl.BlockSpec((ch, PACK_W), lambda c, be, nv, nu, nx: (last(c, nu), 0)),
        scratch_shapes=[pltpu.VMEM((D_MODEL, E_FF), F32), pltpu.VMEM((D_MODEL, E_FF), F32),
                        pltpu.VMEM((E_FF, D_MODEL), F32),
                        pltpu.VMEM((D_MODEL, E_FF), BF16), pltpu.VMEM((D_MODEL, E_FF), BF16),
                        pltpu.VMEM((E_FF, D_MODEL), BF16), pltpu.SemaphoreType.DMA((3,))],
    )
    return pl.pallas_call(
        functools.partial(_experts_body, ch=ch),
        grid_spec=grid_spec,
        out_shape=jax.ShapeDtypeStruct((nch * ch, PACK_W), jnp.uint32),
        compiler_params=_cparams(("arbitrary",)),
        name="experts",
    )(block_e, n_valid, n_used, next_e, x_sorted, w_e_gate, w_e_up, w_e_down)


SC_CORES = 2
SC_SUBCORES = 16
SC_WORKERS = SC_CORES * SC_SUBCORES
SCATTER_WIN = 48
GATHER_WIN = 64


def _sc_mesh():
    return plsc.VectorSubcoreMesh(core_axis_name="c", subcore_axis_name="s")


def _sc_worker():
    return lax.axis_index("s") * SC_CORES + lax.axis_index("c")


def _sc_scatter_rows(rows, idx, n_out):
    n, width = rows.shape
    nw, nwin_k, win = idx.shape
    n_win = nwin_k // TOP_K
    per_w = n // nw

    def body(rows_hbm, idx_hbm, out_hbm, idx_v, buf):
        wid = _sc_worker()
        base = wid * per_w
        pltpu.sync_copy(idx_hbm.at[wid], idx_v)

        @pl.loop(0, n_win)
        def _(j):
            pltpu.sync_copy(rows_hbm.at[pl.ds(base + j * win, win)], buf)
            for k in range(TOP_K):
                pltpu.sync_copy(buf, out_hbm.at[idx_v.at[j * TOP_K + k]])

    return pl.kernel(
        body, out_type=jax.ShapeDtypeStruct((n_out, width), rows.dtype), mesh=_sc_mesh(),
        scratch_types=[pltpu.VMEM((nwin_k, win), jnp.int32), pltpu.VMEM((win, width), rows.dtype)],
        name="sc_scatter_rows",
    )(rows, idx)


def _sc_gather_rows(table, idx):
    nw, n_win, win = idx.shape
    width = table.shape[1]
    per_w = n_win * win

    assert n_win % 2 == 0

    def body(table_hbm, idx_hbm, out_hbm, idx_v, buf, sem):
        wid = _sc_worker()
        base = wid * per_w
        pltpu.sync_copy(idx_hbm.at[wid], idx_v)

        def gather(j, b):
            return pltpu.make_async_copy(table_hbm.at[idx_v.at[j]], buf.at[b], sem.at[b])

        gather(0, 0).start()

        @pl.loop(0, n_win, step=2)
        def _(j):
            for b in range(2):
                @pl.when(j + b + 1 < n_win)
                def _():
                    gather(j + b + 1, 1 - b).start()
                gather(j + b, b).wait()
                pltpu.sync_copy(buf.at[b], out_hbm.at[pl.ds(base + (j + b) * win, win)])

    return pl.kernel(
        body, out_type=jax.ShapeDtypeStruct((nw * per_w, width), table.dtype), mesh=_sc_mesh(),
        scratch_types=[pltpu.VMEM((n_win, win), jnp.int32), pltpu.VMEM((2, win, width), table.dtype),
                       pltpu.SemaphoreType.DMA((2,))],
        name="sc_gather_rows",
    )(table, idx)


def _finish_body(y_ref, xs_ref, gw_ref, gt2_ref, gf_ref, fsc_ref, fsh_ref, out_ref, *, tm):
    gw = gw_ref[...]
    f_lo = jnp.zeros((tm, PACK_W), F32)
    f_hi = jnp.zeros((tm, PACK_W), F32)
    for k in range(TOP_K):
        lo, hi = _unpack_pair(y_ref[k * tm:(k + 1) * tm, :])
        f_lo = f_lo + gw[:, k:k + 1] * lo
        f_hi = f_hi + gw[:, k:k + 1] * hi
    f = jnp.concatenate([f_lo, f_hi], axis=1)
    x2 = xs_ref[...] + gt2_ref[...] * f
    out_ref[...] = _rms(x2, gf_ref[...]) * (1.0 + fsc_ref[...]) + fsh_ref[...]


def _finish(y_tok, xs, gw, gt2, g_final, fsc, fsh, *, prompt, seq_len, row_off):
    n = xs.shape[0]
    tm = COMBINE_TILE
    nt = n // tm
    boff = row_off // tm
    row_tile = lambda w: pl.BlockSpec((tm, w), lambda i: (i, 0))
    if prompt:
        tps = seq_len // tm
        mod_spec = pl.BlockSpec((None, 1, D_MODEL), lambda i: (i // tps, 0, 0))
    else:
        mod_spec = row_tile(D_MODEL)
    return pl.pallas_call(
        functools.partial(_finish_body, tm=tm),
        grid=(nt,),
        in_specs=[pl.BlockSpec((TOP_K * tm, PACK_W), lambda i: (i + boff, 0)),
                  row_tile(D_MODEL), row_tile(TOP_K), mod_spec,
                  pl.BlockSpec((1, D_MODEL), lambda i: (0, 0)), mod_spec, mod_spec],
        out_specs=row_tile(D_MODEL),
        out_shape=jax.ShapeDtypeStruct((n, D_MODEL), F32),
        compiler_params=_cparams(("arbitrary",)),
        name="finish_prompt" if prompt else "finish_sample",
    )(y_tok, xs, gw, gt2, g_final, fsc, fsh)


def _rope_tables(pos):
    inv = 1.0 / (ROPE_THETA ** (jnp.arange(PE_HALF, dtype=F32) / PE_HALF))
    ang = pos.astype(F32)[:, None] * inv[None, :]
    cos, sin = jnp.cos(ang), jnp.sin(ang)
    n = pos.shape[0]
    ones = jnp.ones((n, PE_LO), F32)
    zeros = jnp.zeros((n, PE_LO), F32)
    zh = jnp.zeros((n, PE_HALF), F32)
    tail1 = jnp.ones((n, LANES - PE_LO - ROPE_DIM), F32)
    tail0 = jnp.zeros((n, LANES - PE_LO - ROPE_DIM), F32)
    cos_t = jnp.concatenate([ones, cos, cos, tail1], axis=1)
    sin_up = jnp.concatenate([zeros, zh, sin, tail0], axis=1)
    sin_dn = jnp.concatenate([zeros, -sin, zh, tail0], axis=1)
    return cos_t, sin_up, sin_dn


def _pad_heads(w, width):
    pad = [(0, 0)] * (w.ndim - 1) + [(0, HEAD_PAD - width)]
    w = jnp.pad(w, pad)
    return w.reshape(w.shape[:-2] + (N_HEADS * HEAD_PAD,))


def _block_diag_ones(n, blk):
    r = jnp.arange(n) // blk
    return (r[:, None] == r[None, :]).astype(BF16)


def _prep_weights(w_in, g_attn_norm, g_q, w_q_up, g_kv, w_uk, w_uv, w_conv, g_attn_out, g_conv_out, w_out,
                  g_ffn_norm, w_router, e_bias, w_s_gate, w_s_up, w_s_down):
    o0 = Q_LORA
    o1 = o0 + KV_LORA
    o2 = o1 + ROPE_DIM
    kpe_cols = jnp.pad(w_in[:, o1:o2], ((0, 0), (PE_LO, HEAD_PAD - PE_LO - ROPE_DIM)))
    w_in_p = jnp.concatenate([w_in[:, :o1], kpe_cols, w_in[:, o2:]], axis=1).astype(BF16)
    wq = _pad_heads(w_q_up.reshape(Q_LORA, N_HEADS, NOPE_DIM + ROPE_DIM), NOPE_DIM + ROPE_DIM).astype(BF16)
    wuk = _pad_heads(w_uk, NOPE_DIM).astype(BF16)
    wuv = _pad_heads(w_uv, V_DIM).astype(BF16)
    wuk_t = jnp.pad(jnp.transpose(w_uk, (1, 2, 0)), ((0, 0), (0, HEAD_PAD - NOPE_DIM), (0, 0))).astype(BF16)
    wuv_h = jnp.pad(jnp.transpose(w_uv, (1, 0, 2)), ((0, 0), (0, 0), (0, HEAD_PAD - V_DIM))).astype(BF16)
    g_attn_h = jnp.pad(g_attn_out.reshape(N_HEADS, 1, V_DIM), ((0, 0), (0, 0), (0, HEAD_PAD - V_DIM)))
    tri = lambda t: (jnp.arange(t)[:, None] < jnp.arange(t)[None, :]).astype(BF16)
    return {
        "g1": g_attn_norm.reshape(1, -1), "w_in": w_in_p, "g_q": g_q.reshape(1, -1), "wq": wq,
        "g_kv": g_kv.reshape(1, -1), "wuk": wuk, "wuv": wuv, "wuk_t": wuk_t, "wuv_h": wuv_h,
        "w_conv": w_conv, "g_conv": g_conv_out.reshape(1, -1), "gmat": _block_diag_ones(CONV_DIM, CONV_GROUP_DIM),
        "g_attn": g_attn_out.reshape(1, -1), "g_attn_h": g_attn_h, "gm_head": _block_diag_ones(LANES, V_DIM),
        "w_out": w_out.astype(BF16), "g2": g_ffn_norm.reshape(1, -1), "wr_t": w_router.T.astype(BF16),
        "e_bias": e_bias.reshape(-1, 1), "ws_gu": jnp.concatenate([w_s_gate, w_s_up], axis=1).astype(BF16),
        "ws_d": w_s_down.astype(BF16), "tri_p": tri(TILE_PROMPT), "tri_s": tri(TILE_SAMPLE),
    }


def kernel(x_prompt, x_sample, c_prompt, c_sample, cache_ckv, cache_kpe, state_conv, page_table, w_ada, b_ada, g_attn_norm, w_in, g_q, w_q_up, g_kv, w_uk, w_uv, w_conv, g_attn_out, g_conv_out, w_out, g_ffn_norm, w_router, e_bias, w_e_gate, w_e_up, w_e_down, w_s_gate, w_s_up, w_s_down, w_ada_final, b_ada_final, g_final):
    assert w_ada.shape[0] == 1, "one layer"
    bsz, seq, d = x_prompt.shape
    nseq, t_new, _ = x_sample.shape
    n_p, n_s = bsz * seq, nseq * t_new
    n_tot = n_p + n_s
    past = page_table.shape[1] * PAGE_SIZE

    p = _prep_weights(w_in[0], g_attn_norm[0], g_q[0], w_q_up[0], g_kv[0], w_uk[0], w_uv[0], w_conv[0],
                      g_attn_out[0], g_conv_out[0], w_out[0], g_ffn_norm[0], w_router[0], e_bias[0],
                      w_s_gate[0], w_s_up[0], w_s_down[0])

    c_all = jnp.concatenate([c_prompt, c_sample], axis=0)
    mod = _ada(c_all, w_ada[0], b_ada[0])
    modf = _ada(c_all, w_ada_final, b_ada_final)
    sh1, sc1, gt1, sh2, sc2, gt2 = [mod[:, j * d:(j + 1) * d] for j in range(6)]
    fsh, fsc = modf[:, :d], modf[:, d:]
    per_batch = lambda a: a[:bsz].reshape(bsz, 1, d)
    per_token = lambda a: jnp.repeat(a[bsz:], t_new, axis=0)

    xp = x_prompt.reshape(n_p, d)
    xs_in = x_sample.reshape(n_s, d)
    tabs_p = _rope_tables(jnp.arange(seq, dtype=jnp.int32))
    tabs_s = tuple(jnp.tile(t, (nseq, 1)) for t in _rope_tables(past + jnp.arange(t_new, dtype=jnp.int32)))

    q_p, k_p, v_p, ckv_p, kpe_p, z_p, conv_p = _mix_in(
        xp, per_batch(sc1), per_batch(sh1), p, tabs_p, prompt=True, seq_len=seq)
    o_p = _flash(q_p, k_p, v_p, p["g_attn"], p["gm_head"], bsz, seq)

    prev = state_conv[0]
    prev_a = jnp.repeat(prev[:, 1, :], t_new, axis=0)
    prev_b = jnp.repeat(prev[:, 0, :], t_new, axis=0)
    q_s, ckv_s, kpe_s, z_s, u_s = _mix_in(
        xs_in, per_token(sc1), per_token(sh1), p, tabs_s, prompt=False, prev_a=prev_a, prev_b=prev_b,
        t_new=t_new)
    ql_h, qp_h = _absorb(q_s, p["wuk_t"])
    rows = N_HEADS * t_new
    to_seq = lambda a: jnp.transpose(a.reshape(N_HEADS, nseq, t_new, a.shape[-1]), (1, 0, 2, 3)).reshape(
        nseq, rows, a.shape[-1])
    o_lat = _paged(page_table, to_seq(ql_h), to_seq(qp_h), ckv_s.reshape(nseq, t_new, KV_LORA),
                   kpe_s.reshape(nseq, t_new, ROPE_DIM), cache_ckv, jnp.swapaxes(cache_kpe, 2, 3))
    o_lat_h = jnp.transpose(o_lat.reshape(nseq, N_HEADS, t_new, KV_LORA), (1, 0, 2, 3)).reshape(
        N_HEADS, n_s, KV_LORA)
    o_s_h = _unabsorb(o_lat_h, p["wuv_h"], p["g_attn_h"])
    o_s = jnp.transpose(o_s_h[:, :, :V_DIM], (1, 0, 2)).reshape(n_s, ATTN_WIDTH)

    cnt0 = jnp.zeros((N_EXPERTS, LANES), F32)
    xs_p, h2, idx_p, gw_p, rank_p, cnt_p = _post_mix(
        xp, o_p, z_p, per_batch(gt1), per_batch(sc2), per_batch(sh2), per_batch(gt2), p, cnt0, None,
        prompt=True, seq_len=seq, row_off=0, n_total=n_tot)
    xs_s, h2, idx_s, gw_s, rank_s, cnt_s = _post_mix(
        xs_in, o_s, z_s, per_token(gt1), per_token(sc2), per_token(sh2), per_token(gt2), p, cnt_p, h2,
        prompt=False, seq_len=None, row_off=n_p, n_total=n_tot)

    ch = ROUTE_CHUNK
    counts = cnt_s[:, 0].astype(jnp.int32)
    padded = (counts + ch - 1) // ch * ch
    pend = jnp.cumsum(padded)
    pstart = pend - padded
    idx_all = jnp.concatenate([idx_p, idx_s], axis=1)
    rank_all = jnp.concatenate([rank_p, rank_s], axis=1)
    eids = jnp.arange(N_EXPERTS, dtype=jnp.int32)
    lookup = lambda table, keys: jnp.sum(jnp.where(keys[..., None] == eids, table, 0), axis=-1)
    dest = lookup(pstart, idx_all) + rank_all
    nch = -(-(n_tot * TOP_K) // ch) + N_EXPERTS
    chunk_start = jnp.arange(nch, dtype=jnp.int32) * ch
    block_e = jnp.minimum(jnp.sum((pend[None, :] <= chunk_start[:, None]).astype(jnp.int32), axis=1),
                          N_EXPERTS - 1)
    n_valid = jnp.clip(lookup(pstart + counts, block_e) - chunk_start, 0, ch).astype(jnp.int32)
    n_used_s = (pend[-1] // ch).astype(jnp.int32)
    n_used = n_used_s.reshape(1)
    cidx = jnp.arange(nch, dtype=jnp.int32)
    run_end = lookup(pend, block_e) // ch
    after = jnp.sum(jnp.where(run_end[:, None] == cidx[None, :], block_e[None, :], 0), axis=1)
    is_first = (cidx == lookup(pstart, block_e) // ch) & (cidx < n_used_s)
    next_e = jnp.where(is_first, jnp.where(run_end < n_used_s, after, -1), -2).astype(jnp.int32)

    tmc = COMBINE_TILE

    def dest_tiles(dst):
        n = dst.shape[1]
        return jnp.transpose(dst.reshape(TOP_K, n // tmc, tmc), (1, 0, 2)).reshape(n // tmc, 1, TOP_K * tmc)

    per_w = n_tot // SC_WORKERS
    idx_sc = jnp.transpose(dest.reshape(TOP_K, SC_WORKERS, per_w // SCATTER_WIN, SCATTER_WIN),
                           (1, 2, 0, 3)).reshape(SC_WORKERS, (per_w // SCATTER_WIN) * TOP_K, SCATTER_WIN)
    idx_ga = dest_tiles(dest).reshape(SC_WORKERS, (per_w * TOP_K) // GATHER_WIN, GATHER_WIN)
    x_sorted = _sc_scatter_rows(h2, idx_sc, nch * ch)
    y_sorted = _experts(block_e, n_valid, n_used, next_e, x_sorted, w_e_gate[0], w_e_up[0], w_e_down[0])
    y_tok = _sc_gather_rows(y_sorted, idx_ga)

    gfin = g_final.reshape(1, d)
    y_p = _finish(y_tok, xs_p, gw_p.T, per_batch(gt2), gfin, per_batch(fsc), per_batch(fsh),
                  prompt=True, seq_len=seq, row_off=0)
    y_s = _finish(y_tok, xs_s, gw_s.T, per_token(gt2), gfin, per_token(fsc), per_token(fsh),
                  prompt=False, seq_len=None, row_off=n_p)

    return (y_p.reshape(bsz, seq, d), y_s.reshape(nseq, t_new, d),
            ckv_p.reshape(1, bsz, seq, KV_LORA), kpe_p.reshape(1, bsz, seq, ROPE_DIM),
            conv_p.reshape(1, bsz, CONV_W - 1, CONV_DIM),
            ckv_s.reshape(1, nseq, t_new, KV_LORA), kpe_s.reshape(1, nseq, t_new, ROPE_DIM),
            u_s.reshape(nseq, t_new, CONV_DIM)[:, t_new - (CONV_W - 1):, :].reshape(1, nseq, CONV_W - 1, CONV_DIM))
```

```python
import functools
import math

import jax
import jax.numpy as jnp
from jax import lax
from jax.experimental import pallas as pl
from jax.experimental.pallas import tpu as pltpu
from jax.experimental.pallas import tpu_sc as plsc

F32 = jnp.float32
BF16 = jnp.bfloat16

D_MODEL = 1024
N_HEADS = 8
NOPE_DIM = 64
ROPE_DIM = 32
V_DIM = 64
Q_LORA = 384
KV_LORA = 256
ATTN_WIDTH = N_HEADS * V_DIM
CONV_DIM = 512
CONV_GROUPS = 8
CONV_GROUP_DIM = CONV_DIM // CONV_GROUPS
CONV_W = 3
ROPE_THETA = 10000.0
PAGE_SIZE = 128
N_EXPERTS = 256
TOP_K = 8
N_GROUPS = 8
GROUP_SIZE = N_EXPERTS // N_GROUPS
TOPK_GROUPS = 4
E_FF = 256
SHARED_FF = 256
ROUTED_SCALE = 2.5
EPS = 1e-6

LANES = 128
HEAD_PAD = LANES
QK_WIDTH = N_HEADS * HEAD_PAD
PE_LO = NOPE_DIM
PE_HALF = ROPE_DIM // 2
IN_PAD_COLS = Q_LORA + KV_LORA + HEAD_PAD + 3 * CONV_DIM
VMEM_LIMIT = 48 * 1024 * 1024
Q_SCALE = (NOPE_DIM + ROPE_DIM) ** -0.5 * math.log2(math.e)
PACK_W = D_MODEL // 2

TILE_PROMPT = 256
TILE_SAMPLE = 128
TQ = 512
TK = 256
ROUTE_CHUNK = 256
COMBINE_TILE = 128


def _cparams(sem, vmem=VMEM_LIMIT):
    return pltpu.CompilerParams(dimension_semantics=sem, vmem_limit_bytes=vmem)


def _dot(a, b):
    return jnp.dot(a, b, preferred_element_type=F32)


def _dot_nt(a, b):
    return lax.dot_general(a, b, (((1,), (1,)), ((), ())), preferred_element_type=F32)


def _rms(x, g):
    r = lax.rsqrt(jnp.mean(x * x, axis=-1, keepdims=True) + EPS)
    return (x * r) * g


def _pack_pair(xb):
    lo = lax.bitcast_convert_type(xb[:, :PACK_W].astype(F32), jnp.uint32) >> 16
    hi = lax.bitcast_convert_type(xb[:, PACK_W:].astype(F32), jnp.uint32) & jnp.uint32(0xFFFF0000)
    return lo | hi


def _unpack_pair(w):
    lo = lax.bitcast_convert_type(w << 16, F32)
    hi = lax.bitcast_convert_type(w & jnp.uint32(0xFFFF0000), F32)
    return lo, hi


def _ada_body(c_ref, w_ref, b_ref, o_ref):
    c = c_ref[...]
    a = (c * jax.nn.sigmoid(c)).astype(BF16)
    o_ref[...] = _dot(a, w_ref[...].astype(BF16)) + b_ref[...]


def _ada(c_all, w, b):
    m, d = c_all.shape
    n = w.shape[1]
    tn = 1024
    return pl.pallas_call(
        _ada_body,
        grid=(n // tn,),
        in_specs=[pl.BlockSpec((m, d), lambda j: (0, 0)),
                  pl.BlockSpec((d, tn), lambda j: (0, j)),
                  pl.BlockSpec((1, tn), lambda j: (0, j))],
        out_specs=pl.BlockSpec((m, tn), lambda j: (0, j)),
        out_shape=jax.ShapeDtypeStruct((m, n), F32),
        compiler_params=_cparams(("arbitrary",)),
        name="ada",
    )(c_all, w, b.reshape(1, n))


def _rope_lanes(x, cos, sin_up, sin_dn):
    w = x.shape[1]
    up = pltpu.roll(x, PE_HALF, 1)
    dn = pltpu.roll(x, w - PE_HALF, 1)
    return x * cos + up * sin_up + dn * sin_dn


def _mix_in_body(*refs, prompt, tm, tiles_per_seq, t_new):
    if prompt:
        (x_ref, sc_ref, sh_ref, g1_ref, win_ref, gq_ref, wq_ref, gkv_ref, wuk_ref, wuv_ref,
         cos_ref, sup_ref, sdn_ref, wconv_ref, gconv_ref, gmat_ref,
         q_ref, k_ref, v_ref, ckv_ref, kpe_ref, z_ref, cst_ref, carry_ref) = refs
    else:
        (x_ref, sc_ref, sh_ref, g1_ref, win_ref, gq_ref, wq_ref, gkv_ref,
         cos_ref, sup_ref, sdn_ref, wconv_ref, gconv_ref, gmat_ref, pa_ref, pb_ref,
         q_ref, ckv_ref, kpe_ref, z_ref, u_ref) = refs

    x = x_ref[...]
    h = _rms(x, g1_ref[...]) * (1.0 + sc_ref[...]) + sh_ref[...]
    proj = _dot(h.astype(BF16), win_ref[...])
    o0 = Q_LORA
    o1 = o0 + KV_LORA
    o2 = o1 + HEAD_PAD
    o3 = o2 + CONV_DIM
    o4 = o3 + CONV_DIM
    q_a, kv_a, kpe_blk = proj[:, :o0], proj[:, o0:o1], proj[:, o1:o2]
    b_g, c_g, u_in = proj[:, o2:o3], proj[:, o3:o4], proj[:, o4:]

    cos, sup, sdn = cos_ref[...], sup_ref[...], sdn_ref[...]
    cos8 = jnp.concatenate([cos] * N_HEADS, axis=1)
    sup8 = jnp.concatenate([sup] * N_HEADS, axis=1)
    sdn8 = jnp.concatenate([sdn] * N_HEADS, axis=1)

    qn = _rms(q_a, gq_ref[...]).astype(BF16)
    q = _dot(qn, wq_ref[...]) * Q_SCALE
    q_ref[...] = _rope_lanes(q, cos8, sup8, sdn8).astype(BF16)

    ckv = _rms(kv_a, gkv_ref[...])
    ckv_ref[...] = ckv
    kpe = _rope_lanes(kpe_blk, cos, sup, sdn)
    kpe_ref[...] = kpe[:, PE_LO:PE_LO + ROPE_DIM]

    if prompt:
        ckvb = ckv.astype(BF16)
        k = _dot(ckvb, wuk_ref[...]) + jnp.concatenate([kpe] * N_HEADS, axis=1)
        k_ref[...] = k.astype(BF16)
        lane = lax.broadcasted_iota(jnp.int32, (1, QK_WIDTH), 1)
        ones_hi = jnp.where(lane % HEAD_PAD >= V_DIM, 1.0, 0.0)
        v_ref[...] = (_dot(ckvb, wuv_ref[...]) + ones_hi).astype(BF16)

    u = c_g * u_in
    row = lax.broadcasted_iota(jnp.int32, (tm, 1), 0)
    r1 = pltpu.roll(u, 1, 0)
    r2 = pltpu.roll(u, 2, 0)
    if prompt:
        @pl.when(pl.program_id(0) % tiles_per_seq == 0)
        def _():
            carry_ref[...] = jnp.zeros_like(carry_ref)
        c6 = carry_ref[6:7, :]
        c7 = carry_ref[7:8, :]
        um1 = jnp.where(row == 0, c7, r1)
        um2 = jnp.where(row == 0, c6, jnp.where(row == 1, c7, r2))
        carry_ref[...] = u[tm - 8:, :]
        cst_ref[...] = u[tm - (CONV_W - 1):, :]
    else:
        t = row % t_new
        um1 = jnp.where(t == 0, pa_ref[...], r1)
        um2 = jnp.where(t == 0, pb_ref[...], jnp.where(t == 1, pa_ref[...], r2))
        u_ref[...] = u
    wc = wconv_ref[...]
    y = um2 * wc[0:1, :] + um1 * wc[1:2, :] + u * wc[2:3, :]
    zz = b_g * y
    ms = _dot((zz * zz).astype(BF16), gmat_ref[...]) * (1.0 / CONV_GROUP_DIM)
    z_ref[...] = (zz * lax.rsqrt(ms + EPS) * gconv_ref[...]).astype(BF16)


def _mix_in(x2d, sc, sh, p, rope_tabs, *, prompt, seq_len=None, prev_a=None, prev_b=None, t_new=1):
    n = x2d.shape[0]
    tm = TILE_PROMPT if prompt else TILE_SAMPLE
    nt = n // tm
    cos_t, sup_t, sdn_t = rope_tabs
    full = lambda a: pl.BlockSpec(a.shape, lambda i: (0,) * a.ndim)
    row_tile = lambda w: pl.BlockSpec((tm, w), lambda i: (i, 0))
    if prompt:
        tps = seq_len // tm
        mod_spec = pl.BlockSpec((None, 1, D_MODEL), lambda i: (i // tps, 0, 0))
        tab_spec = pl.BlockSpec((tm, LANES), lambda i: (i % tps, 0))
        nb = n // seq_len
        ins = [x2d, sc, sh, p["g1"], p["w_in"], p["g_q"], p["wq"], p["g_kv"], p["wuk"], p["wuv"],
               cos_t, sup_t, sdn_t, p["w_conv"], p["g_conv"], p["gmat"]]
        in_specs = [row_tile(D_MODEL), mod_spec, mod_spec, full(p["g1"]), full(p["w_in"]), full(p["g_q"]),
                    full(p["wq"]), full(p["g_kv"]), full(p["wuk"]), full(p["wuv"]),
                    tab_spec, tab_spec, tab_spec, full(p["w_conv"]), full(p["g_conv"]), full(p["gmat"])]
        out_shape = [jax.ShapeDtypeStruct((n, QK_WIDTH), BF16), jax.ShapeDtypeStruct((n, QK_WIDTH), BF16),
                     jax.ShapeDtypeStruct((n, QK_WIDTH), BF16), jax.ShapeDtypeStruct((n, KV_LORA), F32),
                     jax.ShapeDtypeStruct((n, ROPE_DIM), F32), jax.ShapeDtypeStruct((n, CONV_DIM), BF16),
                     jax.ShapeDtypeStruct((nb, CONV_W - 1, CONV_DIM), F32)]
        out_specs = [row_tile(QK_WIDTH), row_tile(QK_WIDTH), row_tile(QK_WIDTH), row_tile(KV_LORA),
                     row_tile(ROPE_DIM), row_tile(CONV_DIM),
                     pl.BlockSpec((None, CONV_W - 1, CONV_DIM), lambda i: (i // tps, 0, 0))]
        scratch = [pltpu.VMEM((8, CONV_DIM), F32)]
    else:
        tps = 1
        ins = [x2d, sc, sh, p["g1"], p["w_in"], p["g_q"], p["wq"], p["g_kv"],
               cos_t, sup_t, sdn_t, p["w_conv"], p["g_conv"], p["gmat"], prev_a, prev_b]
        in_specs = [row_tile(D_MODEL), row_tile(D_MODEL), row_tile(D_MODEL), full(p["g1"]), full(p["w_in"]),
                    full(p["g_q"]), full(p["wq"]), full(p["g_kv"]),
                    row_tile(LANES), row_tile(LANES), row_tile(LANES), full(p["w_conv"]), full(p["g_conv"]),
                    full(p["gmat"]), row_tile(CONV_DIM), row_tile(CONV_DIM)]
        out_shape = [jax.ShapeDtypeStruct((n, QK_WIDTH), BF16), jax.ShapeDtypeStruct((n, KV_LORA), F32),
                     jax.ShapeDtypeStruct((n, ROPE_DIM), F32), jax.ShapeDtypeStruct((n, CONV_DIM), BF16),
                     jax.ShapeDtypeStruct((n, CONV_DIM), F32)]
        out_specs = [row_tile(QK_WIDTH), row_tile(KV_LORA), row_tile(ROPE_DIM), row_tile(CONV_DIM),
                     row_tile(CONV_DIM)]
        scratch = []
    return pl.pallas_call(
        functools.partial(_mix_in_body, prompt=prompt, tm=tm, tiles_per_seq=tps, t_new=t_new),
        grid=(nt,), in_specs=in_specs, out_specs=out_specs, out_shape=out_shape,
        scratch_shapes=scratch, compiler_params=_cparams(("arbitrary",)),
        name="mix_in_prompt" if prompt else "mix_in_sample",
    )(*ins)


def _flash_body(q_ref, k_ref, v_ref, g_ref, gm_ref, o_ref, m_sc, acc_sc, s_sc, *, tq, tk):
    assert tq == 2 * tk
    i = pl.program_id(2)
    m_sc[...] = jnp.full(m_sc.shape, -jnp.inf, F32)
    acc_sc[...] = jnp.zeros(acc_sc.shape, F32)
    row_g = i * tq + lax.broadcasted_iota(jnp.int32, (tq, tk), 0)
    col_l = lax.broadcasted_iota(jnp.int32, (tq, tk), 1)
    heads = [slice(hh * HEAD_PAD, (hh + 1) * HEAD_PAD) for hh in range(2)]

    def logits(t, slot):
        ks = pl.multiple_of(t * tk, tk)
        for hh in range(2):
            s_sc[slot, hh] = _dot_nt(q_ref[:, heads[hh]], k_ref[pl.ds(ks, tk), heads[hh]])

    def consume(t, slot, masked):
        ks = pl.multiple_of(t * tk, tk)
        for hh in range(2):
            s = s_sc[slot, hh]
            if masked:
                s = jnp.where(ks + col_l <= row_g, s, -jnp.inf)
            m_prev = m_sc[hh]
            m_new = jnp.maximum(m_prev, jnp.max(s, axis=1, keepdims=True))
            alpha = jnp.exp2(m_prev - m_new)
            pr = jnp.exp2(s - jnp.concatenate([m_new] * (tk // LANES), axis=1))
            acc_sc[hh] = alpha * acc_sc[hh] + _dot(pr.astype(BF16), v_ref[pl.ds(ks, tk), heads[hh]])
            m_sc[hh] = m_new

    logits(0, 0)

    def tile_pair(p, _):
        logits(2 * p + 1, 1)
        consume(2 * p, 0, False)
        logits(2 * p + 2, 0)
        consume(2 * p + 1, 1, False)
        return 0
    lax.fori_loop(0, i, tile_pair, 0)
    logits(2 * i + 1, 1)
    consume(2 * i, 0, True)
    consume(2 * i + 1, 1, True)

    outs = []
    for hh in range(2):
        acc = acc_sc[hh]
        outs.append(acc * pltpu.roll(1.0 / acc, V_DIM, 1))
    lane = lax.broadcasted_iota(jnp.int32, (tq, LANES), 1)
    o = jnp.where(lane < V_DIM, outs[0], pltpu.roll(outs[1], V_DIM, 1))
    ms = _dot((o * o).astype(BF16), gm_ref[...]) * (1.0 / V_DIM)
    o_ref[...] = (o * lax.rsqrt(ms + EPS) * g_ref[...]).astype(BF16)


def _flash(q, k, v, g_attn, gm_head, batch, seq_len):
    n = q.shape[0]
    nq = seq_len // TQ
    pair = pl.BlockSpec((seq_len, 2 * HEAD_PAD), lambda b, p, i: (b, p))
    return pl.pallas_call(
        functools.partial(_flash_body, tq=TQ, tk=TK),
        grid=(batch, N_HEADS // 2, nq),
        in_specs=[pl.BlockSpec((TQ, 2 * HEAD_PAD), lambda b, p, i: (b * nq + i, p)),
                  pair, pair,
                  pl.BlockSpec((1, 2 * V_DIM), lambda b, p, i: (0, p)),
                  pl.BlockSpec((LANES, LANES), lambda b, p, i: (0, 0))],
        out_specs=pl.BlockSpec((TQ, 2 * V_DIM), lambda b, p, i: (b * nq + i, p)),
        out_shape=jax.ShapeDtypeStruct((n, ATTN_WIDTH), BF16),
        scratch_shapes=[pltpu.VMEM((2, TQ, LANES), F32), pltpu.VMEM((2, TQ, LANES), F32),
                        pltpu.VMEM((2, 2, TQ, TK), F32)],
        compiler_params=_cparams(("arbitrary", "arbitrary", "arbitrary")),
        name="flash_prompt",
    )(q, k, v, g_attn, gm_head)


def _absorb_body(q_ref, w_ref, ql_ref, qp_ref):
    qb = q_ref[...]
    ql_ref[...] = _dot(qb, w_ref[...]).astype(BF16)
    qp_ref[...] = qb[:, PE_LO:PE_LO + ROPE_DIM]


def _absorb(q_s, wuk_t):
    n = q_s.shape[0]
    return pl.pallas_call(
        _absorb_body,
        grid=(N_HEADS,),
        in_specs=[pl.BlockSpec((n, HEAD_PAD), lambda h: (0, h)),
                  pl.BlockSpec((None, HEAD_PAD, KV_LORA), lambda h: (h, 0, 0))],
        out_specs=[pl.BlockSpec((None, n, KV_LORA), lambda h: (h, 0, 0)),
                   pl.BlockSpec((None, n, ROPE_DIM), lambda h: (h, 0, 0))],
        out_shape=[jax.ShapeDtypeStruct((N_HEADS, n, KV_LORA), BF16),
                   jax.ShapeDtypeStruct((N_HEADS, n, ROPE_DIM), BF16)],
        compiler_params=_cparams(("arbitrary",)),
        name="absorb",
    )(q_s, wuk_t)


PAGES_PER_CHUNK = 8


def _paged_body(pt_ref, ql_ref, qp_ref, cn_ref, kn_ref, cckv_ref, ckpe_ref, o_ref,
                ckv_buf, kpe_buf, s_all, kcb, sems, *, n_pages, n_seq, t_new):
    b = pl.program_id(0)
    slot = b % 2

    def fetch(seq, sl):
        def body(g, _):
            for u in range(PAGES_PER_CHUNK):
                pg = g * PAGES_PER_CHUNK + u
                page = pt_ref[seq, pg]
                pltpu.make_async_copy(cckv_ref.at[0, page], ckv_buf.at[sl, pg], sems.at[0, sl]).start()
                pltpu.make_async_copy(ckpe_ref.at[0, page], kpe_buf.at[sl, pg],
                                      sems.at[1, sl]).start(priority=1)
            return 0
        lax.fori_loop(0, n_pages // PAGES_PER_CHUNK, body, 0)

    @pl.when(b == 0)
    def _():
        fetch(0, 0)

    @pl.when(b + 1 < n_seq)
    def _():
        fetch(b + 1, 1 - slot)

    pltpu.make_async_copy(cckv_ref.at[0, pl.ds(0, n_pages)], ckv_buf.at[slot], sems.at[0, slot]).wait()
    pltpu.make_async_copy(ckpe_ref.at[0, pl.ds(0, n_pages)], kpe_buf.at[slot], sems.at[1, slot]).wait()

    ql = ql_ref[...]
    qp = qp_ref[...]
    rows = ql.shape[0]
    ck = PAGES_PER_CHUNK * PAGE_SIZE

    n_chunks = n_pages // PAGES_PER_CHUNK

    m = jnp.full((rows, 1), -jnp.inf, F32)
    for c in range(n_chunks):
        pages = slice(c * PAGES_PER_CHUNK, (c + 1) * PAGES_PER_CHUNK)
        kc = ckv_buf[slot, pages].reshape(ck, KV_LORA).astype(BF16)
        kcb[c] = kc
        pc = kpe_buf[slot, pages].astype(BF16)
        s_pe = jnp.concatenate([_dot(qp, pc[j]) for j in range(PAGES_PER_CHUNK)], axis=1)
        s = _dot_nt(ql, kc) + s_pe
        s_all[c] = s
        m = jnp.maximum(m, jnp.max(s, axis=1, keepdims=True))

    qlf = ql.astype(F32)
    qpf = qp.astype(F32)
    cn = cn_ref[...]
    kn = kn_ref[...]
    trow = lax.broadcasted_iota(jnp.int32, (rows, 1), 0) % t_new
    s_new = []
    for j in range(t_new):
        sj = (jnp.sum(qlf * cn[j:j + 1, :], axis=1, keepdims=True)
              + jnp.sum(qpf * kn[j:j + 1, :], axis=1, keepdims=True))
        sj = jnp.where(trow >= j, sj, -jnp.inf)
        s_new.append(sj)
        m = jnp.maximum(m, sj)

    l = jnp.zeros((rows, 1), F32)
    acc = jnp.zeros((rows, KV_LORA), F32)
    for c in range(n_chunks):
        pr = jnp.exp2(s_all[c] - m)
        l = l + jnp.sum(pr, axis=1, keepdims=True)
        acc = acc + _dot(pr.astype(BF16), kcb[c])
    for j in range(t_new):
        pj = jnp.exp2(s_new[j] - m)
        l = l + pj
        acc = acc + pj * cn[j:j + 1, :]
    o_ref[...] = acc / l


def _paged(page_table, ql, qp, ckv_new, kpe_new, cache_ckv, cache_kpe):
    n_seq, n_pages = page_table.shape
    rows = ql.shape[1]
    t_new = ckv_new.shape[1]
    grid_spec = pltpu.PrefetchScalarGridSpec(
        num_scalar_prefetch=1,
        grid=(n_seq,),
        in_specs=[pl.BlockSpec((None, rows, KV_LORA), lambda b, pt: (b, 0, 0)),
                  pl.BlockSpec((None, rows, ROPE_DIM), lambda b, pt: (b, 0, 0)),
                  pl.BlockSpec((None, t_new, KV_LORA), lambda b, pt: (b, 0, 0)),
                  pl.BlockSpec((None, t_new, ROPE_DIM), lambda b, pt: (b, 0, 0)),
                  pl.BlockSpec(memory_space=pl.ANY),
                  pl.BlockSpec(memory_space=pl.ANY)],
        out_specs=pl.BlockSpec((None, rows, KV_LORA), lambda b, pt: (b, 0, 0)),
        scratch_shapes=[pltpu.VMEM((2, n_pages, PAGE_SIZE, KV_LORA), F32),
                        pltpu.VMEM((2, n_pages, ROPE_DIM, PAGE_SIZE), F32),
                        pltpu.VMEM((n_pages // PAGES_PER_CHUNK, rows, PAGES_PER_CHUNK * PAGE_SIZE), F32),
                        pltpu.VMEM((n_pages // PAGES_PER_CHUNK, PAGES_PER_CHUNK * PAGE_SIZE, KV_LORA), BF16),
                        pltpu.SemaphoreType.DMA((2, 2))],
    )
    return pl.pallas_call(
        functools.partial(_paged_body, n_pages=n_pages, n_seq=n_seq, t_new=t_new),
        grid_spec=grid_spec,
        out_shape=jax.ShapeDtypeStruct((n_seq, rows, KV_LORA), F32),
        compiler_params=_cparams(("arbitrary",)),
        name="paged_attn",
    )(page_table, ql, qp, ckv_new, kpe_new, cache_ckv, cache_kpe)


def _unabsorb_body(o_ref, w_ref, g_ref, out_ref):
    o = _dot(o_ref[...].astype(BF16), w_ref[...])
    r = lax.rsqrt(jnp.sum(o * o, axis=-1, keepdims=True) * (1.0 / V_DIM) + EPS)
    out_ref[...] = (o * r * g_ref[...]).astype(BF16)


def _unabsorb(o_lat_h, wuv_h, g_h):
    n = o_lat_h.shape[1]
    return pl.pallas_call(
        _unabsorb_body,
        grid=(N_HEADS,),
        in_specs=[pl.BlockSpec((None, n, KV_LORA), lambda h: (h, 0, 0)),
                  pl.BlockSpec((None, KV_LORA, HEAD_PAD), lambda h: (h, 0, 0)),
                  pl.BlockSpec((None, 1, HEAD_PAD), lambda h: (h, 0, 0))],
        out_specs=pl.BlockSpec((None, n, HEAD_PAD), lambda h: (h, 0, 0)),
        out_shape=jax.ShapeDtypeStruct((N_HEADS, n, HEAD_PAD), BF16),
        compiler_params=_cparams(("arbitrary",)),
        name="unabsorb",
    )(o_lat_h, wuv_h, g_h)


def _post_mix_body(x_ref, o_ref, z_ref, gt1_ref, sc_ref, sh_ref, gt2_ref, wo_ref, g2_ref,
                   wr_ref, eb_ref, wsgu_ref, wsd_ref, tri_ref, cin_ref,
                   xs_ref, h2_ref, idx_ref, gw_ref, rank_ref, cout_ref, cnt_ref, *, tm):
    i = pl.program_id(0)

    @pl.when(i == 0)
    def _():
        cnt_ref[...] = cin_ref[...]

    wo = wo_ref[...]
    mix = _dot(o_ref[...], wo[:ATTN_WIDTH, :]) + _dot(z_ref[...], wo[ATTN_WIDTH:, :])
    x1 = x_ref[...] + gt1_ref[...] * mix
    h2 = _rms(x1, g2_ref[...]) * (1.0 + sc_ref[...]) + sh_ref[...]
    hb = h2.astype(BF16)
    h2_ref[...] = _pack_pair(hb)

    gu = _dot(hb, wsgu_ref[...])
    gate, up = gu[:, :SHARED_FF], gu[:, SHARED_FF:]
    shared = _dot((gate * jax.nn.sigmoid(gate) * up).astype(BF16), wsd_ref[...])
    xs_ref[...] = x1 + gt2_ref[...] * shared

    s = jax.nn.sigmoid(_dot_nt(wr_ref[...], hb))
    biased = s + eb_ref[...]
    ninf = -jnp.inf
    gi = lax.broadcasted_iota(jnp.int32, (GROUP_SIZE, tm), 0).astype(F32)
    gs = []
    for g in range(N_GROUPS):
        blk = biased[g * GROUP_SIZE:(g + 1) * GROUP_SIZE, :]
        m1 = jnp.max(blk, axis=0, keepdims=True)
        i1 = jnp.min(jnp.where(blk == m1, gi, float(GROUP_SIZE)), axis=0, keepdims=True)
        m2 = jnp.max(jnp.where(gi == i1, ninf, blk), axis=0, keepdims=True)
        gs.append(m1 + m2)
    gscore = jnp.concatenate(gs, axis=0)
    giota = lax.broadcasted_iota(jnp.int32, (N_GROUPS, tm), 0).astype(F32)
    gsel = jnp.zeros((N_GROUPS, tm), F32)
    for _ in range(TOPK_GROUPS):
        gm = jnp.max(gscore, axis=0, keepdims=True)
        gidx = jnp.min(jnp.where(gscore == gm, giota, float(N_GROUPS)), axis=0, keepdims=True)
        hit = giota == gidx
        gsel = jnp.where(hit, 1.0, gsel)
        gscore = jnp.where(hit, ninf, gscore)
    masked = jnp.concatenate(
        [jnp.where(gsel[g:g + 1, :] > 0.0, biased[g * GROUP_SIZE:(g + 1) * GROUP_SIZE, :], ninf)
         for g in range(N_GROUPS)], axis=0)
    eiota = lax.broadcasted_iota(jnp.int32, (N_EXPERTS, tm), 0).astype(F32)
    idxs, ws = [], []
    selall = jnp.zeros((N_EXPERTS, tm), F32)
    for _ in range(TOP_K):
        mx = jnp.max(masked, axis=0, keepdims=True)
        ei = jnp.min(jnp.where(masked == mx, eiota, float(N_EXPERTS)), axis=0, keepdims=True)
        hit = eiota == ei
        ws.append(jnp.sum(jnp.where(hit, s, 0.0), axis=0, keepdims=True))
        idxs.append(ei)
        selall = jnp.where(hit, 1.0, selall)
        masked = jnp.where(hit, ninf, masked)
    wsum = ws[0]
    for w in ws[1:]:
        wsum = wsum + w
    gw_ref[...] = jnp.concatenate(ws, axis=0) / wsum * ROUTED_SCALE
    idx_ref[...] = jnp.concatenate(idxs, axis=0).astype(jnp.int32)

    before = _dot(selall.astype(BF16), tri_ref[...]) + cnt_ref[:, 0:1]
    ranks = [jnp.sum(jnp.where(eiota == ei, before, 0.0), axis=0, keepdims=True) for ei in idxs]
    rank_ref[...] = jnp.concatenate(ranks, axis=0).astype(jnp.int32)
    cnt_ref[...] = cnt_ref[...] + jnp.sum(selall, axis=1, keepdims=True)
    cout_ref[...] = cnt_ref[...]


def _post_mix(x2d, o, z, gt1, sc2, sh2, gt2, p, cnt_in, h2_buf, *, prompt, seq_len, row_off, n_total):
    n = x2d.shape[0]
    tm = TILE_PROMPT if prompt else TILE_SAMPLE
    nt = n // tm
    full = lambda a: pl.BlockSpec(a.shape, lambda i: (0,) * a.ndim)
    row_tile = lambda w: pl.BlockSpec((tm, w), lambda i: (i, 0))
    if prompt:
        tps = seq_len // tm
        mod_spec = pl.BlockSpec((None, 1, D_MODEL), lambda i: (i // tps, 0, 0))
    else:
        mod_spec = row_tile(D_MODEL)
    tri = p["tri_p"] if prompt else p["tri_s"]
    boff = row_off // tm
    ins = [x2d, o, z, gt1, sc2, sh2, gt2, p["w_out"], p["g2"], p["wr_t"], p["e_bias"], p["ws_gu"], p["ws_d"],
           tri, cnt_in]
    in_specs = [row_tile(D_MODEL), row_tile(ATTN_WIDTH), row_tile(CONV_DIM), mod_spec, mod_spec, mod_spec,
                mod_spec, full(p["w_out"]), full(p["g2"]), full(p["wr_t"]), full(p["e_bias"]), full(p["ws_gu"]),
                full(p["ws_d"]), full(tri), full(cnt_in)]
    out_shape = [jax.ShapeDtypeStruct((n, D_MODEL), F32), jax.ShapeDtypeStruct((n_total, PACK_W), jnp.uint32),
                 jax.ShapeDtypeStruct((TOP_K, n), jnp.int32), jax.ShapeDtypeStruct((TOP_K, n), F32),
                 jax.ShapeDtypeStruct((TOP_K, n), jnp.int32), jax.ShapeDtypeStruct((N_EXPERTS, LANES), F32)]
    col_tile = pl.BlockSpec((TOP_K, tm), lambda i: (0, i))
    out_specs = [row_tile(D_MODEL), pl.BlockSpec((tm, PACK_W), lambda i: (i + boff, 0)),
                 col_tile, col_tile, col_tile, pl.BlockSpec((N_EXPERTS, LANES), lambda i: (0, 0))]
    aliases = {}
    if h2_buf is not None:
        ins.append(h2_buf)
        in_specs.append(pl.BlockSpec(memory_space=pl.ANY))
        aliases = {len(ins) - 1: 1}
        body = lambda *refs: _post_mix_body(*refs[:15], *refs[16:], tm=tm)
    else:
        body = functools.partial(_post_mix_body, tm=tm)
    return pl.pallas_call(
        body, grid=(nt,), in_specs=in_specs, out_specs=out_specs, out_shape=out_shape,
        scratch_shapes=[pltpu.VMEM((N_EXPERTS, LANES), F32)],
        input_output_aliases=aliases,
        compiler_params=_cparams(("arbitrary",)),
        name="post_mix_prompt" if prompt else "post_mix_sample",
    )(*ins)


def _experts_body(be_ref, nv_ref, nu_ref, nx_ref, x_ref, wg_ref, wu_ref, wd_ref, y_ref,
                  sg, su, sd, wgub, wdb, sem, *, ch):
    c = pl.program_id(0)

    def copies(e):
        return (pltpu.make_async_copy(wg_ref.at[e], sg, sem.at[0]),
                pltpu.make_async_copy(wu_ref.at[e], su, sem.at[1]),
                pltpu.make_async_copy(wd_ref.at[e], sd, sem.at[2]))

    @pl.when(c == 0)
    def _():
        for cp in copies(be_ref[0]):
            cp.start()

    nxt = nx_ref[c]

    @pl.when(nxt >= -1)
    def _():
        for cp in copies(0):
            cp.wait()
        wgub[:, :E_FF] = sg[...].astype(BF16)
        wgub[:, E_FF:] = su[...].astype(BF16)
        wdb[...] = sd[...].astype(BF16)

        @pl.when(nxt >= 0)
        def _():
            for cp in copies(nxt):
                cp.start()

    @pl.when(c < nu_ref[0])
    def _():
        row = lax.broadcasted_iota(jnp.int32, (ch, 1), 0)
        w = jnp.where(row < nv_ref[c], x_ref[...], jnp.uint32(0))
        lo, hi = _unpack_pair(w)
        lo, hi = lo.astype(BF16), hi.astype(BF16)
        gu = _dot(lo, wgub[:PACK_W, :]) + _dot(hi, wgub[PACK_W:, :])
        gate, up = gu[:, :E_FF], gu[:, E_FF:]
        act = (gate * jax.nn.sigmoid(gate) * up).astype(BF16)
        y_ref[...] = _pack_pair(_dot(act, wdb[...]).astype(BF16))


def _experts(block_e, n_valid, n_used, next_e, x_sorted, w_e_gate, w_e_up, w_e_down):
    ch = ROUTE_CHUNK
    nch = x_sorted.shape[0] // ch
    last = lambda c, nu: jnp.minimum(c, jnp.maximum(nu[0] - 1, 0))
    grid_spec = pltpu.PrefetchScalarGridSpec(
        num_scalar_prefetch=4,
        grid=(nch,),
        in_specs=[pl.BlockSpec((ch, PACK_W), lambda c, be, nv, nu, nx: (last(c, nu), 0)),
                  pl.BlockSpec(memory_space=pl.ANY), pl.BlockSpec(memory_space=pl.ANY),
                  pl.BlockSpec(memory_space=pl.ANY)],
        out_specs=pl.BlockSpec((ch, PACK_W), lambda c, be, nv, nu, nx: (last(c, nu), 0)),
        scratch_shapes=[pltpu.VMEM((D_MODEL, E_FF), F32), pltpu.VMEM((D_MODEL, E_FF), F32),
                        pltpu.VMEM((E_FF, D_MODEL), F32),
                        pltpu.VMEM((D_MODEL, 2 * E_FF), BF16),
                        pltpu.VMEM((E_FF, D_MODEL), BF16), pltpu.SemaphoreType.DMA((3,))],
    )
    return pl.pallas_call(
        functools.partial(_experts_body, ch=ch),
        grid_spec=grid_spec,
        out_shape=jax.ShapeDtypeStruct((nch * ch, PACK_W), jnp.uint32),
        compiler_params=_cparams(("arbitrary",)),
        name="experts",
    )(block_e, n_valid, n_used, next_e, x_sorted, w_e_gate, w_e_up, w_e_down)


SC_CORES = 2
SC_SUBCORES = 16
SC_WORKERS = SC_CORES * SC_SUBCORES
SCATTER_WIN = 48
GATHER_WIN = 64


def _sc_mesh():
    return plsc.VectorSubcoreMesh(core_axis_name="c", subcore_axis_name="s")


def _sc_worker():
    return lax.axis_index("s") * SC_CORES + lax.axis_index("c")


def _sc_scatter_rows(rows, idx, n_out):
    n, width = rows.shape
    nw, nwin_k, win = idx.shape
    n_win = nwin_k // TOP_K
    per_w = n // nw

    def body(rows_hbm, idx_hbm, out_hbm, idx_v, buf):
        wid = _sc_worker()
        base = wid * per_w
        pltpu.sync_copy(idx_hbm.at[wid], idx_v)

        @pl.loop(0, n_win)
        def _(j):
            pltpu.sync_copy(rows_hbm.at[pl.ds(base + j * win, win)], buf)
            for k in range(TOP_K):
                pltpu.sync_copy(buf, out_hbm.at[idx_v.at[j * TOP_K + k]])

    return pl.kernel(
        body, out_type=jax.ShapeDtypeStruct((n_out, width), rows.dtype), mesh=_sc_mesh(),
        scratch_types=[pltpu.VMEM((nwin_k, win), jnp.int32), pltpu.VMEM((win, width), rows.dtype)],
        name="sc_scatter_rows",
    )(rows, idx)


def _sc_gather_rows(table, idx):
    nw, n_win, win = idx.shape
    width = table.shape[1]
    per_w = n_win * win

    assert n_win % 2 == 0

    def body(table_hbm, idx_hbm, out_hbm, idx_v, buf, sem):
        wid = _sc_worker()
        base = wid * per_w
        pltpu.sync_copy(idx_hbm.at[wid], idx_v)

        def gather(j, b):
            return pltpu.make_async_copy(table_hbm.at[idx_v.at[j]], buf.at[b], sem.at[b])

        gather(0, 0).start()

        @pl.loop(0, n_win, step=2)
        def _(j):
            for b in range(2):
                @pl.when(j + b + 1 < n_win)
                def _():
                    gather(j + b + 1, 1 - b).start()
                gather(j + b, b).wait()
                pltpu.sync_copy(buf.at[b], out_hbm.at[pl.ds(base + (j + b) * win, win)])

    return pl.kernel(
        body, out_type=jax.ShapeDtypeStruct((nw * per_w, width), table.dtype), mesh=_sc_mesh(),
        scratch_types=[pltpu.VMEM((n_win, win), jnp.int32), pltpu.VMEM((2, win, width), table.dtype),
                       pltpu.SemaphoreType.DMA((2,))],
        name="sc_gather_rows",
    )(table, idx)


def _finish_body(y_ref, xs_ref, gw_ref, gt2_ref, gf_ref, fsc_ref, fsh_ref, out_ref, *, tm):
    gw = gw_ref[...]
    f_lo = jnp.zeros((tm, PACK_W), F32)
    f_hi = jnp.zeros((tm, PACK_W), F32)
    for k in range(TOP_K):
        lo, hi = _unpack_pair(y_ref[k * tm:(k + 1) * tm, :])
        f_lo = f_lo + gw[:, k:k + 1] * lo
        f_hi = f_hi + gw[:, k:k + 1] * hi
    f = jnp.concatenate([f_lo, f_hi], axis=1)
    x2 = xs_ref[...] + gt2_ref[...] * f
    out_ref[...] = _rms(x2, gf_ref[...]) * (1.0 + fsc_ref[...]) + fsh_ref[...]


def _finish(y_part, xs, gw, gt2, g_final, fsc, fsh, *, prompt, seq_len, tile_lo, n_tiles, y_tile_off, out_buf=None):
    n = xs.shape[0]
    tm = COMBINE_TILE
    row_tile = lambda w: pl.BlockSpec((tm, w), lambda i: (i + tile_lo, 0))
    if prompt:
        tps = seq_len // tm
        mod_spec = pl.BlockSpec((None, 1, D_MODEL), lambda i: ((i + tile_lo) // tps, 0, 0))
    else:
        mod_spec = row_tile(D_MODEL)
    ins = [y_part, xs, gw, gt2, g_final, fsc, fsh]
    in_specs = [pl.BlockSpec((TOP_K * tm, PACK_W), lambda i: (i + y_tile_off, 0)),
                row_tile(D_MODEL), row_tile(TOP_K), mod_spec,
                pl.BlockSpec((1, D_MODEL), lambda i: (0, 0)), mod_spec, mod_spec]
    aliases = {}
    body = functools.partial(_finish_body, tm=tm)
    if out_buf is not None:
        ins.append(out_buf)
        in_specs.append(pl.BlockSpec(memory_space=pl.ANY))
        aliases = {len(ins) - 1: 0}
        body = lambda *refs: _finish_body(*refs[:7], *refs[8:], tm=tm)
    return pl.pallas_call(
        body,
        grid=(n_tiles,),
        in_specs=in_specs,
        out_specs=row_tile(D_MODEL),
        out_shape=jax.ShapeDtypeStruct((n, D_MODEL), F32),
        input_output_aliases=aliases,
        compiler_params=_cparams(("arbitrary",)),
        name="finish_prompt" if prompt else "finish_sample",
    )(*ins)


def _rope_tables(pos):
    inv = 1.0 / (ROPE_THETA ** (jnp.arange(PE_HALF, dtype=F32) / PE_HALF))
    ang = pos.astype(F32)[:, None] * inv[None, :]
    cos, sin = jnp.cos(ang), jnp.sin(ang)
    n = pos.shape[0]
    ones = jnp.ones((n, PE_LO), F32)
    zeros = jnp.zeros((n, PE_LO), F32)
    zh = jnp.zeros((n, PE_HALF), F32)
    tail1 = jnp.ones((n, LANES - PE_LO - ROPE_DIM), F32)
    tail0 = jnp.zeros((n, LANES - PE_LO - ROPE_DIM), F32)
    cos_t = jnp.concatenate([ones, cos, cos, tail1], axis=1)
    sin_up = jnp.concatenate([zeros, zh, sin, tail0], axis=1)
    sin_dn = jnp.concatenate([zeros, -sin, zh, tail0], axis=1)
    return cos_t, sin_up, sin_dn


def _pad_heads(w, width):
    pad = [(0, 0)] * (w.ndim - 1) + [(0, HEAD_PAD - width)]
    w = jnp.pad(w, pad)
    return w.reshape(w.shape[:-2] + (N_HEADS * HEAD_PAD,))


def _block_diag_ones(n, blk):
    r = jnp.arange(n) // blk
    return (r[:, None] == r[None, :]).astype(BF16)


def _prep_weights(w_in, g_attn_norm, g_q, w_q_up, g_kv, w_uk, w_uv, w_conv, g_attn_out, g_conv_out, w_out,
                  g_ffn_norm, w_router, e_bias, w_s_gate, w_s_up, w_s_down):
    o0 = Q_LORA
    o1 = o0 + KV_LORA
    o2 = o1 + ROPE_DIM
    kpe_cols = jnp.pad(w_in[:, o1:o2], ((0, 0), (PE_LO, HEAD_PAD - PE_LO - ROPE_DIM)))
    w_in_p = jnp.concatenate([w_in[:, :o1], kpe_cols, w_in[:, o2:]], axis=1).astype(BF16)
    wq = _pad_heads(w_q_up.reshape(Q_LORA, N_HEADS, NOPE_DIM + ROPE_DIM), NOPE_DIM + ROPE_DIM).astype(BF16)
    wuk = _pad_heads(w_uk, NOPE_DIM).astype(BF16)
    wuv = _pad_heads(w_uv, V_DIM).astype(BF16)
    wuk_t = jnp.pad(jnp.transpose(w_uk, (1, 2, 0)), ((0, 0), (0, HEAD_PAD - NOPE_DIM), (0, 0))).astype(BF16)
    wuv_h = jnp.pad(jnp.transpose(w_uv, (1, 0, 2)), ((0, 0), (0, 0), (0, HEAD_PAD - V_DIM))).astype(BF16)
    g_attn_h = jnp.pad(g_attn_out.reshape(N_HEADS, 1, V_DIM), ((0, 0), (0, 0), (0, HEAD_PAD - V_DIM)))
    tri = lambda t: (jnp.arange(t)[:, None] < jnp.arange(t)[None, :]).astype(BF16)
    return {
        "g1": g_attn_norm.reshape(1, -1), "w_in": w_in_p, "g_q": g_q.reshape(1, -1), "wq": wq,
        "g_kv": g_kv.reshape(1, -1), "wuk": wuk, "wuv": wuv, "wuk_t": wuk_t, "wuv_h": wuv_h,
        "w_conv": w_conv, "g_conv": g_conv_out.reshape(1, -1), "gmat": _block_diag_ones(CONV_DIM, CONV_GROUP_DIM),
        "g_attn": g_attn_out.reshape(1, -1), "g_attn_h": g_attn_h, "gm_head": _block_diag_ones(LANES, V_DIM),
        "w_out": w_out.astype(BF16), "g2": g_ffn_norm.reshape(1, -1), "wr_t": w_router.T.astype(BF16),
        "e_bias": e_bias.reshape(-1, 1), "ws_gu": jnp.concatenate([w_s_gate, w_s_up], axis=1).astype(BF16),
        "ws_d": w_s_down.astype(BF16), "tri_p": tri(TILE_PROMPT), "tri_s": tri(TILE_SAMPLE),
    }


def kernel(x_prompt, x_sample, c_prompt, c_sample, cache_ckv, cache_kpe, state_conv, page_table, w_ada, b_ada, g_attn_norm, w_in, g_q, w_q_up, g_kv, w_uk, w_uv, w_conv, g_attn_out, g_conv_out, w_out, g_ffn_norm, w_router, e_bias, w_e_gate, w_e_up, w_e_down, w_s_gate, w_s_up, w_s_down, w_ada_final, b_ada_final, g_final):
    assert w_ada.shape[0] == 1, "one layer"
    bsz, seq, d = x_prompt.shape
    nseq, t_new, _ = x_sample.shape
    n_p, n_s = bsz * seq, nseq * t_new
    n_tot = n_p + n_s
    past = page_table.shape[1] * PAGE_SIZE

    p = _prep_weights(w_in[0], g_attn_norm[0], g_q[0], w_q_up[0], g_kv[0], w_uk[0], w_uv[0], w_conv[0],
                      g_attn_out[0], g_conv_out[0], w_out[0], g_ffn_norm[0], w_router[0], e_bias[0],
                      w_s_gate[0], w_s_up[0], w_s_down[0])

    c_all = jnp.concatenate([c_prompt, c_sample], axis=0)
    mod = _ada(c_all, w_ada[0], b_ada[0])
    modf = _ada(c_all, w_ada_final, b_ada_final)
    sh1, sc1, gt1, sh2, sc2, gt2 = [mod[:, j * d:(j + 1) * d] for j in range(6)]
    fsh, fsc = modf[:, :d], modf[:, d:]
    per_batch = lambda a: a[:bsz].reshape(bsz, 1, d)
    per_token = lambda a: jnp.repeat(a[bsz:], t_new, axis=0)

    xp = x_prompt.reshape(n_p, d)
    xs_in = x_sample.reshape(n_s, d)
    tabs_p = _rope_tables(jnp.arange(seq, dtype=jnp.int32))
    tabs_s = tuple(jnp.tile(t, (nseq, 1)) for t in _rope_tables(past + jnp.arange(t_new, dtype=jnp.int32)))

    q_p, k_p, v_p, ckv_p, kpe_p, z_p, conv_p = _mix_in(
        xp, per_batch(sc1), per_batch(sh1), p, tabs_p, prompt=True, seq_len=seq)
    o_p = _flash(q_p, k_p, v_p, p["g_attn"], p["gm_head"], bsz, seq)

    prev = state_conv[0]
    prev_a = jnp.repeat(prev[:, 1, :], t_new, axis=0)
    prev_b = jnp.repeat(prev[:, 0, :], t_new, axis=0)
    q_s, ckv_s, kpe_s, z_s, u_s = _mix_in(
        xs_in, per_token(sc1), per_token(sh1), p, tabs_s, prompt=False, prev_a=prev_a, prev_b=prev_b,
        t_new=t_new)
    ql_h, qp_h = _absorb(q_s, p["wuk_t"])
    rows = N_HEADS * t_new
    to_seq = lambda a: jnp.transpose(a.reshape(N_HEADS, nseq, t_new, a.shape[-1]), (1, 0, 2, 3)).reshape(
        nseq, rows, a.shape[-1])
    o_lat = _paged(page_table, to_seq(ql_h), to_seq(qp_h), ckv_s.reshape(nseq, t_new, KV_LORA),
                   kpe_s.reshape(nseq, t_new, ROPE_DIM), cache_ckv, jnp.swapaxes(cache_kpe, 2, 3))
    o_lat_h = jnp.transpose(o_lat.reshape(nseq, N_HEADS, t_new, KV_LORA), (1, 0, 2, 3)).reshape(
        N_HEADS, n_s, KV_LORA)
    o_s_h = _unabsorb(o_lat_h, p["wuv_h"], p["g_attn_h"])
    o_s = jnp.transpose(o_s_h[:, :, :V_DIM], (1, 0, 2)).reshape(n_s, ATTN_WIDTH)

    cnt0 = jnp.zeros((N_EXPERTS, LANES), F32)
    xs_p, h2, idx_p, gw_p, rank_p, cnt_p = _post_mix(
        xp, o_p, z_p, per_batch(gt1), per_batch(sc2), per_batch(sh2), per_batch(gt2), p, cnt0, None,
        prompt=True, seq_len=seq, row_off=0, n_total=n_tot)
    xs_s, h2, idx_s, gw_s, rank_s, cnt_s = _post_mix(
        xs_in, o_s, z_s, per_token(gt1), per_token(sc2), per_token(sh2), per_token(gt2), p, cnt_p, h2,
        prompt=False, seq_len=None, row_off=n_p, n_total=n_tot)

    ch = ROUTE_CHUNK
    counts = cnt_s[:, 0].astype(jnp.int32)
    padded = (counts + ch - 1) // ch * ch
    pend = jnp.cumsum(padded)
    pstart = pend - padded
    idx_all = jnp.concatenate([idx_p, idx_s], axis=1)
    rank_all = jnp.concatenate([rank_p, rank_s], axis=1)
    eids = jnp.arange(N_EXPERTS, dtype=jnp.int32)
    lookup = lambda table, keys: jnp.sum(jnp.where(keys[..., None] == eids, table, 0), axis=-1)
    dest = lookup(pstart, idx_all) + rank_all
    nch = -(-(n_tot * TOP_K) // ch) + N_EXPERTS
    chunk_start = jnp.arange(nch, dtype=jnp.int32) * ch
    block_e = jnp.minimum(jnp.sum((pend[None, :] <= chunk_start[:, None]).astype(jnp.int32), axis=1),
                          N_EXPERTS - 1)
    n_valid = jnp.clip(lookup(pstart + counts, block_e) - chunk_start, 0, ch).astype(jnp.int32)
    n_used_s = (pend[-1] // ch).astype(jnp.int32)
    n_used = n_used_s.reshape(1)
    cidx = jnp.arange(nch, dtype=jnp.int32)
    run_end = lookup(pend, block_e) // ch
    after = jnp.sum(jnp.where(run_end[:, None] == cidx[None, :], block_e[None, :], 0), axis=1)
    is_first = (cidx == lookup(pstart, block_e) // ch) & (cidx < n_used_s)
    next_e = jnp.where(is_first, jnp.where(run_end < n_used_s, after, -1), -2).astype(jnp.int32)

    tmc = COMBINE_TILE

    def dest_tiles(dst):
        n = dst.shape[1]
        return jnp.transpose(dst.reshape(TOP_K, n // tmc, tmc), (1, 0, 2)).reshape(n // tmc, 1, TOP_K * tmc)

    per_w = n_tot // SC_WORKERS
    idx_sc = jnp.transpose(dest.reshape(TOP_K, SC_WORKERS, per_w // SCATTER_WIN, SCATTER_WIN),
                           (1, 2, 0, 3)).reshape(SC_WORKERS, (per_w // SCATTER_WIN) * TOP_K, SCATTER_WIN)
    x_sorted = _sc_scatter_rows(h2, idx_sc, nch * ch)
    y_sorted = _experts(block_e, n_valid, n_used, next_e, x_sorted, w_e_gate[0], w_e_up[0], w_e_down[0])

    idx_tiles = dest_tiles(dest)
    tiles_p = n_p // tmc
    tiles_a = tiles_p // 2
    sc_idx = lambda part: part.reshape(SC_WORKERS, -1, GATHER_WIN)
    y_a = _sc_gather_rows(y_sorted, sc_idx(idx_tiles[:tiles_a]))
    y_b = _sc_gather_rows(y_sorted, sc_idx(idx_tiles[tiles_a:]))

    gfin = g_final.reshape(1, d)
    fin_p = functools.partial(_finish, xs=xs_p, gw=gw_p.T, gt2=per_batch(gt2), g_final=gfin, fsc=per_batch(fsc),
                              fsh=per_batch(fsh), prompt=True, seq_len=seq)
    y_p = fin_p(y_a, tile_lo=0, n_tiles=tiles_a, y_tile_off=0)
    y_p = fin_p(y_b, tile_lo=tiles_a, n_tiles=tiles_p - tiles_a, y_tile_off=0, out_buf=y_p)
    y_s = _finish(y_b, xs_s, gw_s.T, per_token(gt2), gfin, per_token(fsc), per_token(fsh), prompt=False,
                  seq_len=None, tile_lo=0, n_tiles=n_s // tmc, y_tile_off=tiles_p - tiles_a)

    return (y_p.reshape(bsz, seq, d), y_s.reshape(nseq, t_new, d),
            ckv_p.reshape(1, bsz, seq, KV_LORA), kpe_p.reshape(1, bsz, seq, ROPE_DIM),
            conv_p.reshape(1, bsz, CONV_W - 1, CONV_DIM),
            ckv_s.reshape(1, nseq, t_new, KV_LORA), kpe_s.reshape(1, nseq, t_new, ROPE_DIM),
            u_s.reshape(nseq, t_new, CONV_DIM)[:, t_new - (CONV_W - 1):, :].reshape(1, nseq, CONV_W - 1, CONV_DIM))
```

```python
import functools
import math

import jax
import jax.numpy as jnp
from jax import lax
from jax.experimental import pallas as pl
from jax.experimental.pallas import tpu as pltpu
from jax.experimental.pallas import tpu_sc as plsc

F32 = jnp.float32
BF16 = jnp.bfloat16

D_MODEL = 1024
N_HEADS = 8
NOPE_DIM = 64
ROPE_DIM = 32
V_DIM = 64
Q_LORA = 384
KV_LORA = 256
ATTN_WIDTH = N_HEADS * V_DIM
CONV_DIM = 512
CONV_GROUPS = 8
CONV_GROUP_DIM = CONV_DIM // CONV_GROUPS
CONV_W = 3
ROPE_THETA = 10000.0
PAGE_SIZE = 128
N_EXPERTS = 256
TOP_K = 8
N_GROUPS = 8
GROUP_SIZE = N_EXPERTS // N_GROUPS
TOPK_GROUPS = 4
E_FF = 256
SHARED_FF = 256
ROUTED_SCALE = 2.5
EPS = 1e-6

LANES = 128
HEAD_PAD = LANES
QK_WIDTH = N_HEADS * HEAD_PAD
PE_LO = NOPE_DIM
PE_HALF = ROPE_DIM // 2
IN_PAD_COLS = Q_LORA + KV_LORA + HEAD_PAD + 3 * CONV_DIM
VMEM_LIMIT = 48 * 1024 * 1024
Q_SCALE = (NOPE_DIM + ROPE_DIM) ** -0.5 * math.log2(math.e)
PACK_W = D_MODEL // 2

TILE_PROMPT = 256
TILE_SAMPLE = 128
TQ = 512
TK = 256
ROUTE_CHUNK = 512
COMBINE_TILE = 256


def _cparams(sem, vmem=VMEM_LIMIT):
    return pltpu.CompilerParams(dimension_semantics=sem, vmem_limit_bytes=vmem)


def _dot(a, b):
    return jnp.dot(a, b, preferred_element_type=F32)


def _dot_nt(a, b):
    return lax.dot_general(a, b, (((1,), (1,)), ((), ())), preferred_element_type=F32)


def _rms(x, g):
    r = lax.rsqrt(jnp.mean(x * x, axis=-1, keepdims=True) + EPS)
    return (x * r) * g


def _pack_pair(xb):
    lo = lax.bitcast_convert_type(xb[:, :PACK_W].astype(F32), jnp.uint32) >> 16
    hi = lax.bitcast_convert_type(xb[:, PACK_W:].astype(F32), jnp.uint32) & jnp.uint32(0xFFFF0000)
    return lo | hi


def _unpack_pair(w):
    lo = lax.bitcast_convert_type(w << 16, F32)
    hi = lax.bitcast_convert_type(w & jnp.uint32(0xFFFF0000), F32)
    return lo, hi


def _ada_body(c_ref, w_ref, b_ref, o_ref):
    c = c_ref[...]
    a = (c * jax.nn.sigmoid(c)).astype(BF16)
    o_ref[...] = _dot(a, w_ref[...].astype(BF16)) + b_ref[...]


def _ada(c_all, w, b):
    m, d = c_all.shape
    n = w.shape[1]
    tn = 1024
    return pl.pallas_call(
        _ada_body,
        grid=(n // tn,),
        in_specs=[pl.BlockSpec((m, d), lambda j: (0, 0)),
                  pl.BlockSpec((d, tn), lambda j: (0, j)),
                  pl.BlockSpec((1, tn), lambda j: (0, j))],
        out_specs=pl.BlockSpec((m, tn), lambda j: (0, j)),
        out_shape=jax.ShapeDtypeStruct((m, n), F32),
        compiler_params=_cparams(("arbitrary",)),
        name="ada",
    )(c_all, w, b.reshape(1, n))


def _rope_lanes(x, cos, sin_up, sin_dn):
    w = x.shape[1]
    up = pltpu.roll(x, PE_HALF, 1)
    dn = pltpu.roll(x, w - PE_HALF, 1)
    return x * cos + up * sin_up + dn * sin_dn


def _mix_in_body(*refs, prompt, tm, tiles_per_seq, t_new):
    if prompt:
        (x_ref, sc_ref, sh_ref, g1_ref, win_ref, gq_ref, wq_ref, gkv_ref, wuk_ref, wuv_ref,
         cos_ref, sup_ref, sdn_ref, wconv_ref, gconv_ref, gmat_ref,
         q_ref, k_ref, v_ref, ckv_ref, kpe_ref, z_ref, cst_ref, carry_ref) = refs
    else:
        (x_ref, sc_ref, sh_ref, g1_ref, win_ref, gq_ref, wq_ref, gkv_ref,
         cos_ref, sup_ref, sdn_ref, wconv_ref, gconv_ref, gmat_ref, pa_ref, pb_ref,
         q_ref, ckv_ref, kpe_ref, z_ref, u_ref) = refs

    x = x_ref[...]
    h = _rms(x, g1_ref[...]) * (1.0 + sc_ref[...]) + sh_ref[...]
    proj = _dot(h.astype(BF16), win_ref[...])
    o0 = Q_LORA
    o1 = o0 + KV_LORA
    o2 = o1 + HEAD_PAD
    o3 = o2 + CONV_DIM
    o4 = o3 + CONV_DIM
    q_a, kv_a, kpe_blk = proj[:, :o0], proj[:, o0:o1], proj[:, o1:o2]
    b_g, c_g, u_in = proj[:, o2:o3], proj[:, o3:o4], proj[:, o4:]

    cos, sup, sdn = cos_ref[...], sup_ref[...], sdn_ref[...]
    cos8 = jnp.concatenate([cos] * N_HEADS, axis=1)
    sup8 = jnp.concatenate([sup] * N_HEADS, axis=1)
    sdn8 = jnp.concatenate([sdn] * N_HEADS, axis=1)

    qn = _rms(q_a, gq_ref[...]).astype(BF16)
    q = _dot(qn, wq_ref[...]) * Q_SCALE
    q_ref[...] = _rope_lanes(q, cos8, sup8, sdn8).astype(BF16)

    ckv = _rms(kv_a, gkv_ref[...])
    ckv_ref[...] = ckv
    kpe = _rope_lanes(kpe_blk, cos, sup, sdn)
    kpe_ref[...] = kpe[:, PE_LO:PE_LO + ROPE_DIM]

    if prompt:
        ckvb = ckv.astype(BF16)
        k = _dot(ckvb, wuk_ref[...]) + jnp.concatenate([kpe] * N_HEADS, axis=1)
        k_ref[...] = k.astype(BF16)
        lane = lax.broadcasted_iota(jnp.int32, (1, QK_WIDTH), 1)
        ones_hi = jnp.where(lane % HEAD_PAD >= V_DIM, 1.0, 0.0)
        v_ref[...] = (_dot(ckvb, wuv_ref[...]) + ones_hi).astype(BF16)

    u = c_g * u_in
    row = lax.broadcasted_iota(jnp.int32, (tm, 1), 0)
    r1 = pltpu.roll(u, 1, 0)
    r2 = pltpu.roll(u, 2, 0)
    if prompt:
        @pl.when(pl.program_id(0) % tiles_per_seq == 0)
        def _():
            carry_ref[...] = jnp.zeros_like(carry_ref)
        c6 = carry_ref[6:7, :]
        c7 = carry_ref[7:8, :]
        um1 = jnp.where(row == 0, c7, r1)
        um2 = jnp.where(row == 0, c6, jnp.where(row == 1, c7, r2))
        carry_ref[...] = u[tm - 8:, :]
        cst_ref[...] = u[tm - (CONV_W - 1):, :]
    else:
        t = row % t_new
        um1 = jnp.where(t == 0, pa_ref[...], r1)
        um2 = jnp.where(t == 0, pb_ref[...], jnp.where(t == 1, pa_ref[...], r2))
        u_ref[...] = u
    wc = wconv_ref[...]
    y = um2 * wc[0:1, :] + um1 * wc[1:2, :] + u * wc[2:3, :]
    zz = b_g * y
    ms = _dot((zz * zz).astype(BF16), gmat_ref[...]) * (1.0 / CONV_GROUP_DIM)
    z_ref[...] = (zz * lax.rsqrt(ms + EPS) * gconv_ref[...]).astype(BF16)


def _mix_in(x2d, sc, sh, p, rope_tabs, *, prompt, seq_len=None, prev_a=None, prev_b=None, t_new=1):
    n = x2d.shape[0]
    tm = TILE_PROMPT if prompt else TILE_SAMPLE
    nt = n // tm
    cos_t, sup_t, sdn_t = rope_tabs
    full = lambda a: pl.BlockSpec(a.shape, lambda i: (0,) * a.ndim)
    row_tile = lambda w: pl.BlockSpec((tm, w), lambda i: (i, 0))
    if prompt:
        tps = seq_len // tm
        mod_spec = pl.BlockSpec((None, 1, D_MODEL), lambda i: (i // tps, 0, 0))
        tab_spec = pl.BlockSpec((tm, LANES), lambda i: (i % tps, 0))
        nb = n // seq_len
        ins = [x2d, sc, sh, p["g1"], p["w_in"], p["g_q"], p["wq"], p["g_kv"], p["wuk"], p["wuv"],
               cos_t, sup_t, sdn_t, p["w_conv"], p["g_conv"], p["gmat"]]
        in_specs = [row_tile(D_MODEL), mod_spec, mod_spec, full(p["g1"]), full(p["w_in"]), full(p["g_q"]),
                    full(p["wq"]), full(p["g_kv"]), full(p["wuk"]), full(p["wuv"]),
                    tab_spec, tab_spec, tab_spec, full(p["w_conv"]), full(p["g_conv"]), full(p["gmat"])]
        out_shape = [jax.ShapeDtypeStruct((n, QK_WIDTH), BF16), jax.ShapeDtypeStruct((n, QK_WIDTH), BF16),
                     jax.ShapeDtypeStruct((n, QK_WIDTH), BF16), jax.ShapeDtypeStruct((n, KV_LORA), F32),
                     jax.ShapeDtypeStruct((n, ROPE_DIM), F32), jax.ShapeDtypeStruct((n, CONV_DIM), BF16),
                     jax.ShapeDtypeStruct((nb, CONV_W - 1, CONV_DIM), F32)]
        out_specs = [row_tile(QK_WIDTH), row_tile(QK_WIDTH), row_tile(QK_WIDTH), row_tile(KV_LORA),
                     row_tile(ROPE_DIM), row_tile(CONV_DIM),
                     pl.BlockSpec((None, CONV_W - 1, CONV_DIM), lambda i: (i // tps, 0, 0))]
        scratch = [pltpu.VMEM((8, CONV_DIM), F32)]
    else:
        tps = 1
        ins = [x2d, sc, sh, p["g1"], p["w_in"], p["g_q"], p["wq"], p["g_kv"],
               cos_t, sup_t, sdn_t, p["w_conv"], p["g_conv"], p["gmat"], prev_a, prev_b]
        in_specs = [row_tile(D_MODEL), row_tile(D_MODEL), row_tile(D_MODEL), full(p["g1"]), full(p["w_in"]),
                    full(p["g_q"]), full(p["wq"]), full(p["g_kv"]),
                    row_tile(LANES), row_tile(LANES), row_tile(LANES), full(p["w_conv"]), full(p["g_conv"]),
                    full(p["gmat"]), row_tile(CONV_DIM), row_tile(CONV_DIM)]
        out_shape = [jax.ShapeDtypeStruct((n, QK_WIDTH), BF16), jax.ShapeDtypeStruct((n, KV_LORA), F32),
                     jax.ShapeDtypeStruct((n, ROPE_DIM), F32), jax.ShapeDtypeStruct((n, CONV_DIM), BF16),
                     jax.ShapeDtypeStruct((n, CONV_DIM), F32)]
        out_specs = [row_tile(QK_WIDTH), row_tile(KV_LORA), row_tile(ROPE_DIM), row_tile(CONV_DIM),
                     row_tile(CONV_DIM)]
        scratch = []
    return pl.pallas_call(
        functools.partial(_mix_in_body, prompt=prompt, tm=tm, tiles_per_seq=tps, t_new=t_new),
        grid=(nt,), in_specs=in_specs, out_specs=out_specs, out_shape=out_shape,
        scratch_shapes=scratch, compiler_params=_cparams(("arbitrary",)),
        name="mix_in_prompt" if prompt else "mix_in_sample",
    )(*ins)


def _flash_body(q_ref, k_ref, v_ref, g_ref, gm_ref, o_ref, m_sc, acc_sc, s_sc, *, tq, tk):
    assert tq % tk == 0
    i = pl.program_id(2)
    m_sc[...] = jnp.full(m_sc.shape, -jnp.inf, F32)
    acc_sc[...] = jnp.zeros(acc_sc.shape, F32)
    row_g = i * tq + lax.broadcasted_iota(jnp.int32, (tq, tk), 0)
    col_l = lax.broadcasted_iota(jnp.int32, (tq, tk), 1)
    heads = [slice(hh * HEAD_PAD, (hh + 1) * HEAD_PAD) for hh in range(2)]

    def logits(t, slot):
        ks = pl.multiple_of(t * tk, tk)
        for hh in range(2):
            s_sc[slot, hh] = _dot_nt(q_ref[:, heads[hh]], k_ref[pl.ds(ks, tk), heads[hh]])

    def consume(t, slot, masked):
        ks = pl.multiple_of(t * tk, tk)
        for hh in range(2):
            s = s_sc[slot, hh]
            if masked:
                s = jnp.where(ks + col_l <= row_g, s, -jnp.inf)
            m_prev = m_sc[hh]
            m_new = jnp.maximum(m_prev, jnp.max(s, axis=1, keepdims=True))
            alpha = jnp.exp2(m_prev - m_new)
            pr = jnp.exp2(s - jnp.concatenate([m_new] * (tk // LANES), axis=1))
            acc_sc[hh] = alpha * acc_sc[hh] + _dot(pr.astype(BF16), v_ref[pl.ds(ks, tk), heads[hh]])
            m_sc[hh] = m_new

    n_diag = tq // tk
    n_full = i * n_diag
    logits(0, 0)
    if n_diag % 2 == 0:
        def tile_pair(p, _):
            logits(2 * p + 1, 1)
            consume(2 * p, 0, False)
            logits(2 * p + 2, 0)
            consume(2 * p + 1, 1, False)
            return 0
        lax.fori_loop(0, n_full // 2, tile_pair, 0)
    else:
        def tile(t, _):
            logits(t + 1, (t + 1) % 2)
            consume(t, t % 2, False)
            return 0
        lax.fori_loop(0, n_full, tile, 0)
    for jj in range(n_diag):
        t = n_full + jj
        slot = jj % 2 if n_diag % 2 == 0 else t % 2
        if jj + 1 < n_diag:
            logits(t + 1, 1 - slot)
        consume(t, slot, True)

    outs = []
    for hh in range(2):
        acc = acc_sc[hh]
        outs.append(acc * pltpu.roll(1.0 / acc, V_DIM, 1))
    lane = lax.broadcasted_iota(jnp.int32, (tq, LANES), 1)
    o = jnp.where(lane < V_DIM, outs[0], pltpu.roll(outs[1], V_DIM, 1))
    ms = _dot((o * o).astype(BF16), gm_ref[...]) * (1.0 / V_DIM)
    o_ref[...] = (o * lax.rsqrt(ms + EPS) * g_ref[...]).astype(BF16)


def _flash(q, k, v, g_attn, gm_head, batch, seq_len):
    n = q.shape[0]
    nq = seq_len // TQ
    pair = pl.BlockSpec((seq_len, 2 * HEAD_PAD), lambda b, p, i: (b, p))
    return pl.pallas_call(
        functools.partial(_flash_body, tq=TQ, tk=TK),
        grid=(batch, N_HEADS // 2, nq),
        in_specs=[pl.BlockSpec((TQ, 2 * HEAD_PAD), lambda b, p, i: (b * nq + i, p)),
                  pair, pair,
                  pl.BlockSpec((1, 2 * V_DIM), lambda b, p, i: (0, p)),
                  pl.BlockSpec((LANES, LANES), lambda b, p, i: (0, 0))],
        out_specs=pl.BlockSpec((TQ, 2 * V_DIM), lambda b, p, i: (b * nq + i, p)),
        out_shape=jax.ShapeDtypeStruct((n, ATTN_WIDTH), BF16),
        scratch_shapes=[pltpu.VMEM((2, TQ, LANES), F32), pltpu.VMEM((2, TQ, LANES), F32),
                        pltpu.VMEM((2, 2, TQ, TK), F32)],
        compiler_params=_cparams(("arbitrary", "arbitrary", "arbitrary")),
        name="flash_prompt",
    )(q, k, v, g_attn, gm_head)


def _absorb_body(q_ref, w_ref, ql_ref, qp_ref):
    qb = q_ref[...]
    ql_ref[...] = _dot(qb, w_ref[...]).astype(BF16)
    qp_ref[...] = qb[:, PE_LO:PE_LO + ROPE_DIM]


def _absorb(q_s, wuk_t):
    n = q_s.shape[0]
    return pl.pallas_call(
        _absorb_body,
        grid=(N_HEADS,),
        in_specs=[pl.BlockSpec((n, HEAD_PAD), lambda h: (0, h)),
                  pl.BlockSpec((None, HEAD_PAD, KV_LORA), lambda h: (h, 0, 0))],
        out_specs=[pl.BlockSpec((None, n, KV_LORA), lambda h: (h, 0, 0)),
                   pl.BlockSpec((None, n, ROPE_DIM), lambda h: (h, 0, 0))],
        out_shape=[jax.ShapeDtypeStruct((N_HEADS, n, KV_LORA), BF16),
                   jax.ShapeDtypeStruct((N_HEADS, n, ROPE_DIM), BF16)],
        compiler_params=_cparams(("arbitrary",)),
        name="absorb",
    )(q_s, wuk_t)


PAGES_PER_CHUNK = 8


def _paged_body(pt_ref, ql_ref, qp_ref, cn_ref, kn_ref, cckv_ref, ckpe_ref, o_ref,
                ckv_buf, kpe_buf, s_all, kcb, sems, *, n_pages, n_seq, t_new):
    b = pl.program_id(0)
    slot = b % 2

    def fetch(seq, sl):
        def body(g, _):
            for u in range(PAGES_PER_CHUNK):
                pg = g * PAGES_PER_CHUNK + u
                page = pt_ref[seq, pg]
                pltpu.make_async_copy(cckv_ref.at[0, page], ckv_buf.at[sl, pg], sems.at[0, sl]).start()
                pltpu.make_async_copy(ckpe_ref.at[0, page], kpe_buf.at[sl, pg],
                                      sems.at[1, sl]).start(priority=1)
            return 0
        lax.fori_loop(0, n_pages // PAGES_PER_CHUNK, body, 0)

    @pl.when(b == 0)
    def _():
        fetch(0, 0)

    @pl.when(b + 1 < n_seq)
    def _():
        fetch(b + 1, 1 - slot)

    pltpu.make_async_copy(cckv_ref.at[0, pl.ds(0, n_pages)], ckv_buf.at[slot], sems.at[0, slot]).wait()
    pltpu.make_async_copy(ckpe_ref.at[0, pl.ds(0, n_pages)], kpe_buf.at[slot], sems.at[1, slot]).wait()

    ql = ql_ref[...]
    qp = qp_ref[...]
    rows = ql.shape[0]
    ck = PAGES_PER_CHUNK * PAGE_SIZE

    n_chunks = n_pages // PAGES_PER_CHUNK

    m = jnp.full((rows, 1), -jnp.inf, F32)
    for c in range(n_chunks):
        pages = slice(c * PAGES_PER_CHUNK, (c + 1) * PAGES_PER_CHUNK)
        kc = ckv_buf[slot, pages].reshape(ck, KV_LORA).astype(BF16)
        kcb[c] = kc
        pc = kpe_buf[slot, pages].astype(BF16)
        s_pe = jnp.concatenate([_dot(qp, pc[j]) for j in range(PAGES_PER_CHUNK)], axis=1)
        s = _dot_nt(ql, kc) + s_pe
        s_all[c] = s
        m = jnp.maximum(m, jnp.max(s, axis=1, keepdims=True))

    qlf = ql.astype(F32)
    qpf = qp.astype(F32)
    cn = cn_ref[...]
    kn = kn_ref[...]
    trow = lax.broadcasted_iota(jnp.int32, (rows, 1), 0) % t_new
    s_new = []
    for j in range(t_new):
        sj = (jnp.sum(qlf * cn[j:j + 1, :], axis=1, keepdims=True)
              + jnp.sum(qpf * kn[j:j + 1, :], axis=1, keepdims=True))
        sj = jnp.where(trow >= j, sj, -jnp.inf)
        s_new.append(sj)
        m = jnp.maximum(m, sj)

    l = jnp.zeros((rows, 1), F32)
    acc = jnp.zeros((rows, KV_LORA), F32)
    for c in range(n_chunks):
        pr = jnp.exp2(s_all[c] - m)
        l = l + jnp.sum(pr, axis=1, keepdims=True)
        acc = acc + _dot(pr.astype(BF16), kcb[c])
    for j in range(t_new):
        pj = jnp.exp2(s_new[j] - m)
        l = l + pj
        acc = acc + pj * cn[j:j + 1, :]
    o_ref[...] = acc / l


def _paged(page_table, ql, qp, ckv_new, kpe_new, cache_ckv, cache_kpe):
    n_seq, n_pages = page_table.shape
    rows = ql.shape[1]
    t_new = ckv_new.shape[1]
    grid_spec = pltpu.PrefetchScalarGridSpec(
        num_scalar_prefetch=1,
        grid=(n_seq,),
        in_specs=[pl.BlockSpec((None, rows, KV_LORA), lambda b, pt: (b, 0, 0)),
                  pl.BlockSpec((None, rows, ROPE_DIM), lambda b, pt: (b, 0, 0)),
                  pl.BlockSpec((None, t_new, KV_LORA), lambda b, pt: (b, 0, 0)),
                  pl.BlockSpec((None, t_new, ROPE_DIM), lambda b, pt: (b, 0, 0)),
                  pl.BlockSpec(memory_space=pl.ANY),
                  pl.BlockSpec(memory_space=pl.ANY)],
        out_specs=pl.BlockSpec((None, rows, KV_LORA), lambda b, pt: (b, 0, 0)),
        scratch_shapes=[pltpu.VMEM((2, n_pages, PAGE_SIZE, KV_LORA), F32),
                        pltpu.VMEM((2, n_pages, ROPE_DIM, PAGE_SIZE), F32),
                        pltpu.VMEM((n_pages // PAGES_PER_CHUNK, rows, PAGES_PER_CHUNK * PAGE_SIZE), F32),
                        pltpu.VMEM((n_pages // PAGES_PER_CHUNK, PAGES_PER_CHUNK * PAGE_SIZE, KV_LORA), BF16),
                        pltpu.SemaphoreType.DMA((2, 2))],
    )
    return pl.pallas_call(
        functools.partial(_paged_body, n_pages=n_pages, n_seq=n_seq, t_new=t_new),
        grid_spec=grid_spec,
        out_shape=jax.ShapeDtypeStruct((n_seq, rows, KV_LORA), F32),
        compiler_params=_cparams(("arbitrary",)),
        name="paged_attn",
    )(page_table, ql, qp, ckv_new, kpe_new, cache_ckv, cache_kpe)


def _unabsorb_body(o_ref, w_ref, g_ref, out_ref):
    o = _dot(o_ref[...].astype(BF16), w_ref[...])
    r = lax.rsqrt(jnp.sum(o * o, axis=-1, keepdims=True) * (1.0 / V_DIM) + EPS)
    out_ref[...] = (o * r * g_ref[...]).astype(BF16)


def _unabsorb(o_lat_h, wuv_h, g_h):
    n = o_lat_h.shape[1]
    return pl.pallas_call(
        _unabsorb_body,
        grid=(N_HEADS,),
        in_specs=[pl.BlockSpec((None, n, KV_LORA), lambda h: (h, 0, 0)),
                  pl.BlockSpec((None, KV_LORA, HEAD_PAD), lambda h: (h, 0, 0)),
                  pl.BlockSpec((None, 1, HEAD_PAD), lambda h: (h, 0, 0))],
        out_specs=pl.BlockSpec((None, n, HEAD_PAD), lambda h: (h, 0, 0)),
        out_shape=jax.ShapeDtypeStruct((N_HEADS, n, HEAD_PAD), BF16),
        compiler_params=_cparams(("arbitrary",)),
        name="unabsorb",
    )(o_lat_h, wuv_h, g_h)


def _post_mix_body(x_ref, o_ref, z_ref, gt1_ref, sc_ref, sh_ref, gt2_ref, wo_ref, g2_ref,
                   wr_ref, eb_ref, wsgu_ref, wsd_ref, tri_ref, cin_ref,
                   xs_ref, h2_ref, idx_ref, gw_ref, rank_ref, cout_ref, cnt_ref, *, tm):
    i = pl.program_id(0)

    @pl.when(i == 0)
    def _():
        cnt_ref[...] = cin_ref[...]

    wo = wo_ref[...]
    mix = _dot(o_ref[...], wo[:ATTN_WIDTH, :]) + _dot(z_ref[...], wo[ATTN_WIDTH:, :])
    x1 = x_ref[...] + gt1_ref[...] * mix
    h2 = _rms(x1, g2_ref[...]) * (1.0 + sc_ref[...]) + sh_ref[...]
    hb = h2.astype(BF16)
    h2_ref[...] = _pack_pair(hb)

    gu = _dot(hb, wsgu_ref[...])
    gate, up = gu[:, :SHARED_FF], gu[:, SHARED_FF:]
    shared = _dot((gate * jax.nn.sigmoid(gate) * up).astype(BF16), wsd_ref[...])
    xs_ref[...] = x1 + gt2_ref[...] * shared

    s = jax.nn.sigmoid(_dot_nt(wr_ref[...], hb))
    biased = s + eb_ref[...]
    ninf = -jnp.inf
    gi = lax.broadcasted_iota(jnp.int32, (GROUP_SIZE, tm), 0).astype(F32)
    gs = []
    for g in range(N_GROUPS):
        blk = biased[g * GROUP_SIZE:(g + 1) * GROUP_SIZE, :]
        m1 = jnp.max(blk, axis=0, keepdims=True)
        i1 = jnp.min(jnp.where(blk == m1, gi, float(GROUP_SIZE)), axis=0, keepdims=True)
        m2 = jnp.max(jnp.where(gi == i1, ninf, blk), axis=0, keepdims=True)
        gs.append(m1 + m2)
    gscore = jnp.concatenate(gs, axis=0)
    giota = lax.broadcasted_iota(jnp.int32, (N_GROUPS, tm), 0).astype(F32)
    gsel = jnp.zeros((N_GROUPS, tm), F32)
    for _ in range(TOPK_GROUPS):
        gm = jnp.max(gscore, axis=0, keepdims=True)
        gidx = jnp.min(jnp.where(gscore == gm, giota, float(N_GROUPS)), axis=0, keepdims=True)
        hit = giota == gidx
        gsel = jnp.where(hit, 1.0, gsel)
        gscore = jnp.where(hit, ninf, gscore)
    masked = jnp.concatenate(
        [jnp.where(gsel[g:g + 1, :] > 0.0, biased[g * GROUP_SIZE:(g + 1) * GROUP_SIZE, :], ninf)
         for g in range(N_GROUPS)], axis=0)
    eiota = lax.broadcasted_iota(jnp.int32, (N_EXPERTS, tm), 0).astype(F32)
    idxs, ws = [], []
    selall = jnp.zeros((N_EXPERTS, tm), F32)
    for _ in range(TOP_K):
        mx = jnp.max(masked, axis=0, keepdims=True)
        ei = jnp.min(jnp.where(masked == mx, eiota, float(N_EXPERTS)), axis=0, keepdims=True)
        hit = eiota == ei
        ws.append(jnp.sum(jnp.where(hit, s, 0.0), axis=0, keepdims=True))
        idxs.append(ei)
        selall = jnp.where(hit, 1.0, selall)
        masked = jnp.where(hit, ninf, masked)
    wsum = ws[0]
    for w in ws[1:]:
        wsum = wsum + w
    gw_ref[...] = jnp.concatenate(ws, axis=0) / wsum * ROUTED_SCALE
    idx_ref[...] = jnp.concatenate(idxs, axis=0).astype(jnp.int32)

    before = _dot(selall.astype(BF16), tri_ref[...]) + cnt_ref[:, 0:1]
    ranks = [jnp.sum(jnp.where(eiota == ei, before, 0.0), axis=0, keepdims=True) for ei in idxs]
    rank_ref[...] = jnp.concatenate(ranks, axis=0).astype(jnp.int32)
    cnt_ref[...] = cnt_ref[...] + jnp.sum(selall, axis=1, keepdims=True)
    cout_ref[...] = cnt_ref[...]


def _post_mix(x2d, o, z, gt1, sc2, sh2, gt2, p, cnt_in, h2_buf, *, prompt, seq_len, row_off, n_total):
    n = x2d.shape[0]
    tm = TILE_PROMPT if prompt else TILE_SAMPLE
    nt = n // tm
    full = lambda a: pl.BlockSpec(a.shape, lambda i: (0,) * a.ndim)
    row_tile = lambda w: pl.BlockSpec((tm, w), lambda i: (i, 0))
    if prompt:
        tps = seq_len // tm
        mod_spec = pl.BlockSpec((None, 1, D_MODEL), lambda i: (i // tps, 0, 0))
    else:
        mod_spec = row_tile(D_MODEL)
    tri = p["tri_p"] if prompt else p["tri_s"]
    boff = row_off // tm
    ins = [x2d, o, z, gt1, sc2, sh2, gt2, p["w_out"], p["g2"], p["wr_t"], p["e_bias"], p["ws_gu"], p["ws_d"],
           tri, cnt_in]
    in_specs = [row_tile(D_MODEL), row_tile(ATTN_WIDTH), row_tile(CONV_DIM), mod_spec, mod_spec, mod_spec,
                mod_spec, full(p["w_out"]), full(p["g2"]), full(p["wr_t"]), full(p["e_bias"]), full(p["ws_gu"]),
                full(p["ws_d"]), full(tri), full(cnt_in)]
    out_shape = [jax.ShapeDtypeStruct((n, D_MODEL), F32), jax.ShapeDtypeStruct((n_total, PACK_W), jnp.uint32),
                 jax.ShapeDtypeStruct((TOP_K, n), jnp.int32), jax.ShapeDtypeStruct((TOP_K, n), F32),
                 jax.ShapeDtypeStruct((TOP_K, n), jnp.int32), jax.ShapeDtypeStruct((N_EXPERTS, LANES), F32)]
    col_tile = pl.BlockSpec((TOP_K, tm), lambda i: (0, i))
    out_specs = [row_tile(D_MODEL), pl.BlockSpec((tm, PACK_W), lambda i: (i + boff, 0)),
                 col_tile, col_tile, col_tile, pl.BlockSpec((N_EXPERTS, LANES), lambda i: (0, 0))]
    aliases = {}
    if h2_buf is not None:
        ins.append(h2_buf)
        in_specs.append(pl.BlockSpec(memory_space=pl.ANY))
        aliases = {len(ins) - 1: 1}
        body = lambda *refs: _post_mix_body(*refs[:15], *refs[16:], tm=tm)
    else:
        body = functools.partial(_post_mix_body, tm=tm)
    return pl.pallas_call(
        body, grid=(nt,), in_specs=in_specs, out_specs=out_specs, out_shape=out_shape,
        scratch_shapes=[pltpu.VMEM((N_EXPERTS, LANES), F32)],
        input_output_aliases=aliases,
        compiler_params=_cparams(("arbitrary",)),
        name="post_mix_prompt" if prompt else "post_mix_sample",
    )(*ins)


def _experts_body(be_ref, nv_ref, nu_ref, nx_ref, x_ref, wg_ref, wu_ref, wd_ref, y_ref,
                  sg, su, sd, wgub, wdb, sem, *, ch):
    c = pl.program_id(0)

    def copies(e):
        return (pltpu.make_async_copy(wg_ref.at[e], sg, sem.at[0]),
                pltpu.make_async_copy(wu_ref.at[e], su, sem.at[1]),
                pltpu.make_async_copy(wd_ref.at[e], sd, sem.at[2]))

    @pl.when(c == 0)
    def _():
        for cp in copies(be_ref[0]):
            cp.start()

    nxt = nx_ref[c]

    @pl.when(nxt >= -1)
    def _():
        for cp in copies(0):
            cp.wait()
        wgub[:, :E_FF] = sg[...].astype(BF16)
        wgub[:, E_FF:] = su[...].astype(BF16)
        wdb[...] = sd[...].astype(BF16)

        @pl.when(nxt >= 0)
        def _():
            for cp in copies(nxt):
                cp.start()

    nv = jnp.where(c < nu_ref[0], nv_ref[c], 0)

    def run(rows):
        row = lax.broadcasted_iota(jnp.int32, (rows, 1), 0)
        w = jnp.where(row < nv, x_ref[:rows, :], jnp.uint32(0))
        lo, hi = _unpack_pair(w)
        lo, hi = lo.astype(BF16), hi.astype(BF16)
        gu = _dot(lo, wgub[:PACK_W, :]) + _dot(hi, wgub[PACK_W:, :])
        gate, up = gu[:, :E_FF], gu[:, E_FF:]
        act = (gate * jax.nn.sigmoid(gate) * up).astype(BF16)
        y_ref[:rows, :] = _pack_pair(_dot(act, wdb[...]).astype(BF16))

    sizes = [ch // 4, ch // 2, ch]
    for lo_n, hi_n in zip([0] + sizes[:-1], sizes):
        pl.when(jnp.logical_and(nv > lo_n, nv <= hi_n))(functools.partial(run, hi_n))


def _experts(block_e, n_valid, n_used, next_e, x_sorted, w_e_gate, w_e_up, w_e_down):
    ch = ROUTE_CHUNK
    nch = x_sorted.shape[0] // ch
    last = lambda c, nu: jnp.minimum(c, jnp.maximum(nu[0] - 1, 0))
    grid_spec = pltpu.PrefetchScalarGridSpec(
        num_scalar_prefetch=4,
        grid=(nch,),
        in_specs=[pl.BlockSpec((ch, PACK_W), lambda c, be, nv, nu, nx: (last(c, nu), 0)),
                  pl.BlockSpec(memory_space=pl.ANY), pl.BlockSpec(memory_space=pl.ANY),
                  pl.BlockSpec(memory_space=pl.ANY)],
        out_specs=pl.BlockSpec((ch, PACK_W), lambda c, be, nv, nu, nx: (last(c, nu), 0)),
        scratch_shapes=[pltpu.VMEM((D_MODEL, E_FF), F32), pltpu.VMEM((D_MODEL, E_FF), F32),
                        pltpu.VMEM((E_FF, D_MODEL), F32),
                        pltpu.VMEM((D_MODEL, 2 * E_FF), BF16),
                        pltpu.VMEM((E_FF, D_MODEL), BF16), pltpu.SemaphoreType.DMA((3,))],
    )
    return pl.pallas_call(
        functools.partial(_experts_body, ch=ch),
        grid_spec=grid_spec,
        out_shape=jax.ShapeDtypeStruct((nch * ch, PACK_W), jnp.uint32),
        compiler_params=_cparams(("arbitrary",)),
        name="experts",
    )(block_e, n_valid, n_used, next_e, x_sorted, w_e_gate, w_e_up, w_e_down)


SC_CORES = 2
SC_SUBCORES = 16
SC_WORKERS = SC_CORES * SC_SUBCORES
SCATTER_WIN = 24
GATHER_WIN = 64


def _sc_mesh():
    return plsc.VectorSubcoreMesh(core_axis_name="c", subcore_axis_name="s")


def _sc_worker():
    return lax.axis_index("s") * SC_CORES + lax.axis_index("c")


def _sc_scatter_rows(rows, idx, n_out):
    n, width = rows.shape
    nw, nwin_k, win = idx.shape
    n_win = nwin_k // TOP_K
    per_w = n // nw

    assert n_win % 2 == 0

    def body(rows_hbm, idx_hbm, out_hbm, idx_v, buf, sem_in, sem_out):
        wid = _sc_worker()
        base = wid * per_w
        pltpu.sync_copy(idx_hbm.at[wid], idx_v)

        def load(j, b):
            return pltpu.make_async_copy(rows_hbm.at[pl.ds(base + j * win, win)], buf.at[b], sem_in.at[b])

        def scatters(j, b):
            return [pltpu.make_async_copy(buf.at[b], out_hbm.at[idx_v.at[j * TOP_K + k]], sem_out.at[b])
                    for k in range(TOP_K)]

        load(0, 0).start()

        @pl.loop(0, n_win, step=2)
        def _(j):
            for b in range(2):
                jj = j + b
                load(jj, b).wait()
                for cp in scatters(jj, b):
                    cp.start()

                @pl.when(jj >= 1)
                def _():
                    for cp in scatters(jj - 1, 1 - b):
                        cp.wait()

                @pl.when(jj + 1 < n_win)
                def _():
                    load(jj + 1, 1 - b).start()

        for cp in scatters(n_win - 1, 1):
            cp.wait()

    return pl.kernel(
        body, out_type=jax.ShapeDtypeStruct((n_out, width), rows.dtype), mesh=_sc_mesh(),
        scratch_types=[pltpu.VMEM((nwin_k, win), jnp.int32), pltpu.VMEM((2, win, width), rows.dtype),
                       pltpu.SemaphoreType.DMA((2,)), pltpu.SemaphoreType.DMA((2,))],
        name="sc_scatter_rows",
    )(rows, idx)


def _sc_gather_rows(table, idx):
    nw, n_win, win = idx.shape
    width = table.shape[1]
    per_w = n_win * win

    assert n_win % 2 == 0

    def body(table_hbm, idx_hbm, out_hbm, idx_v, buf, sem):
        wid = _sc_worker()
        base = wid * per_w
        pltpu.sync_copy(idx_hbm.at[wid], idx_v)

        def gather(j, b):
            return pltpu.make_async_copy(table_hbm.at[idx_v.at[j]], buf.at[b], sem.at[b])

        gather(0, 0).start()

        @pl.loop(0, n_win, step=2)
        def _(j):
            for b in range(2):
                @pl.when(j + b + 1 < n_win)
                def _():
                    gather(j + b + 1, 1 - b).start()
                gather(j + b, b).wait()
                pltpu.sync_copy(buf.at[b], out_hbm.at[pl.ds(base + (j + b) * win, win)])

    return pl.kernel(
        body, out_type=jax.ShapeDtypeStruct((nw * per_w, width), table.dtype), mesh=_sc_mesh(),
        scratch_types=[pltpu.VMEM((n_win, win), jnp.int32), pltpu.VMEM((2, win, width), table.dtype),
                       pltpu.SemaphoreType.DMA((2,))],
        name="sc_gather_rows",
    )(table, idx)


def _finish_body(y_ref, xs_ref, gw_ref, gt2_ref, gf_ref, fsc_ref, fsh_ref, out_ref, *, tm):
    gw = gw_ref[...]
    f_lo = jnp.zeros((tm, PACK_W), F32)
    f_hi = jnp.zeros((tm, PACK_W), F32)
    for k in range(TOP_K):
        lo, hi = _unpack_pair(y_ref[k * tm:(k + 1) * tm, :])
        f_lo = f_lo + gw[:, k:k + 1] * lo
        f_hi = f_hi + gw[:, k:k + 1] * hi
    f = jnp.concatenate([f_lo, f_hi], axis=1)
    x2 = xs_ref[...] + gt2_ref[...] * f
    out_ref[...] = _rms(x2, gf_ref[...]) * (1.0 + fsc_ref[...]) + fsh_ref[...]


def _finish(y_part, xs, gw, gt2, g_final, fsc, fsh, *, prompt, seq_len, tile_lo, n_tiles, y_tile_off, out_buf=None):
    n = xs.shape[0]
    tm = COMBINE_TILE
    row_tile = lambda w: pl.BlockSpec((tm, w), lambda i: (i + tile_lo, 0))
    if prompt:
        tps = seq_len // tm
        mod_spec = pl.BlockSpec((None, 1, D_MODEL), lambda i: ((i + tile_lo) // tps, 0, 0))
    else:
        mod_spec = row_tile(D_MODEL)
    ins = [y_part, xs, gw, gt2, g_final, fsc, fsh]
    in_specs = [pl.BlockSpec((TOP_K * tm, PACK_W), lambda i: (i + y_tile_off, 0)),
                row_tile(D_MODEL), row_tile(TOP_K), mod_spec,
                pl.BlockSpec((1, D_MODEL), lambda i: (0, 0)), mod_spec, mod_spec]
    aliases = {}
    body = functools.partial(_finish_body, tm=tm)
    if out_buf is not None:
        ins.append(out_buf)
        in_specs.append(pl.BlockSpec(memory_space=pl.ANY))
        aliases = {len(ins) - 1: 0}
        body = lambda *refs: _finish_body(*refs[:7], *refs[8:], tm=tm)
    return pl.pallas_call(
        body,
        grid=(n_tiles,),
        in_specs=in_specs,
        out_specs=row_tile(D_MODEL),
        out_shape=jax.ShapeDtypeStruct((n, D_MODEL), F32),
        input_output_aliases=aliases,
        compiler_params=_cparams(("arbitrary",)),
        name="finish_prompt" if prompt else "finish_sample",
    )(*ins)


def _rope_tables(pos):
    inv = 1.0 / (ROPE_THETA ** (jnp.arange(PE_HALF, dtype=F32) / PE_HALF))
    ang = pos.astype(F32)[:, None] * inv[None, :]
    cos, sin = jnp.cos(ang), jnp.sin(ang)
    n = pos.shape[0]
    ones = jnp.ones((n, PE_LO), F32)
    zeros = jnp.zeros((n, PE_LO), F32)
    zh = jnp.zeros((n, PE_HALF), F32)
    tail1 = jnp.ones((n, LANES - PE_LO - ROPE_DIM), F32)
    tail0 = jnp.zeros((n, LANES - PE_LO - ROPE_DIM), F32)
    cos_t = jnp.concatenate([ones, cos, cos, tail1], axis=1)
    sin_up = jnp.concatenate([zeros, zh, sin, tail0], axis=1)
    sin_dn = jnp.concatenate([zeros, -sin, zh, tail0], axis=1)
    return cos_t, sin_up, sin_dn


def _pad_heads(w, width):
    pad = [(0, 0)] * (w.ndim - 1) + [(0, HEAD_PAD - width)]
    w = jnp.pad(w, pad)
    return w.reshape(w.shape[:-2] + (N_HEADS * HEAD_PAD,))


def _block_diag_ones(n, blk):
    r = jnp.arange(n) // blk
    return (r[:, None] == r[None, :]).astype(BF16)


def _prep_weights(w_in, g_attn_norm, g_q, w_q_up, g_kv, w_uk, w_uv, w_conv, g_attn_out, g_conv_out, w_out,
                  g_ffn_norm, w_router, e_bias, w_s_gate, w_s_up, w_s_down):
    o0 = Q_LORA
    o1 = o0 + KV_LORA
    o2 = o1 + ROPE_DIM
    kpe_cols = jnp.pad(w_in[:, o1:o2], ((0, 0), (PE_LO, HEAD_PAD - PE_LO - ROPE_DIM)))
    w_in_p = jnp.concatenate([w_in[:, :o1], kpe_cols, w_in[:, o2:]], axis=1).astype(BF16)
    wq = _pad_heads(w_q_up.reshape(Q_LORA, N_HEADS, NOPE_DIM + ROPE_DIM), NOPE_DIM + ROPE_DIM).astype(BF16)
    wuk = _pad_heads(w_uk, NOPE_DIM).astype(BF16)
    wuv = _pad_heads(w_uv, V_DIM).astype(BF16)
    wuk_t = jnp.pad(jnp.transpose(w_uk, (1, 2, 0)), ((0, 0), (0, HEAD_PAD - NOPE_DIM), (0, 0))).astype(BF16)
    wuv_h = jnp.pad(jnp.transpose(w_uv, (1, 0, 2)), ((0, 0), (0, 0), (0, HEAD_PAD - V_DIM))).astype(BF16)
    g_attn_h = jnp.pad(g_attn_out.reshape(N_HEADS, 1, V_DIM), ((0, 0), (0, 0), (0, HEAD_PAD - V_DIM)))
    tri = lambda t: (jnp.arange(t)[:, None] < jnp.arange(t)[None, :]).astype(BF16)
    return {
        "g1": g_attn_norm.reshape(1, -1), "w_in": w_in_p, "g_q": g_q.reshape(1, -1), "wq": wq,
        "g_kv": g_kv.reshape(1, -1), "wuk": wuk, "wuv": wuv, "wuk_t": wuk_t, "wuv_h": wuv_h,
        "w_conv": w_conv, "g_conv": g_conv_out.reshape(1, -1), "gmat": _block_diag_ones(CONV_DIM, CONV_GROUP_DIM),
        "g_attn": g_attn_out.reshape(1, -1), "g_attn_h": g_attn_h, "gm_head": _block_diag_ones(LANES, V_DIM),
        "w_out": w_out.astype(BF16), "g2": g_ffn_norm.reshape(1, -1), "wr_t": w_router.T.astype(BF16),
        "e_bias": e_bias.reshape(-1, 1), "ws_gu": jnp.concatenate([w_s_gate, w_s_up], axis=1).astype(BF16),
        "ws_d": w_s_down.astype(BF16), "tri_p": tri(TILE_PROMPT), "tri_s": tri(TILE_SAMPLE),
    }


def kernel(x_prompt, x_sample, c_prompt, c_sample, cache_ckv, cache_kpe, state_conv, page_table, w_ada, b_ada, g_attn_norm, w_in, g_q, w_q_up, g_kv, w_uk, w_uv, w_conv, g_attn_out, g_conv_out, w_out, g_ffn_norm, w_router, e_bias, w_e_gate, w_e_up, w_e_down, w_s_gate, w_s_up, w_s_down, w_ada_final, b_ada_final, g_final):
    assert w_ada.shape[0] == 1, "one layer"
    bsz, seq, d = x_prompt.shape
    nseq, t_new, _ = x_sample.shape
    n_p, n_s = bsz * seq, nseq * t_new
    n_tot = n_p + n_s
    past = page_table.shape[1] * PAGE_SIZE

    p = _prep_weights(w_in[0], g_attn_norm[0], g_q[0], w_q_up[0], g_kv[0], w_uk[0], w_uv[0], w_conv[0],
                      g_attn_out[0], g_conv_out[0], w_out[0], g_ffn_norm[0], w_router[0], e_bias[0],
                      w_s_gate[0], w_s_up[0], w_s_down[0])

    c_all = jnp.concatenate([c_prompt, c_sample], axis=0)
    mod = _ada(c_all, w_ada[0], b_ada[0])
    modf = _ada(c_all, w_ada_final, b_ada_final)
    sh1, sc1, gt1, sh2, sc2, gt2 = [mod[:, j * d:(j + 1) * d] for j in range(6)]
    fsh, fsc = modf[:, :d], modf[:, d:]
    per_batch = lambda a: a[:bsz].reshape(bsz, 1, d)
    per_token = lambda a: jnp.repeat(a[bsz:], t_new, axis=0)

    xp = x_prompt.reshape(n_p, d)
    xs_in = x_sample.reshape(n_s, d)
    tabs_p = _rope_tables(jnp.arange(seq, dtype=jnp.int32))
    tabs_s = tuple(jnp.tile(t, (nseq, 1)) for t in _rope_tables(past + jnp.arange(t_new, dtype=jnp.int32)))

    q_p, k_p, v_p, ckv_p, kpe_p, z_p, conv_p = _mix_in(
        xp, per_batch(sc1), per_batch(sh1), p, tabs_p, prompt=True, seq_len=seq)
    o_p = _flash(q_p, k_p, v_p, p["g_attn"], p["gm_head"], bsz, seq)

    prev = state_conv[0]
    prev_a = jnp.repeat(prev[:, 1, :], t_new, axis=0)
    prev_b = jnp.repeat(prev[:, 0, :], t_new, axis=0)
    q_s, ckv_s, kpe_s, z_s, u_s = _mix_in(
        xs_in, per_token(sc1), per_token(sh1), p, tabs_s, prompt=False, prev_a=prev_a, prev_b=prev_b,
        t_new=t_new)
    ql_h, qp_h = _absorb(q_s, p["wuk_t"])
    rows = N_HEADS * t_new
    to_seq = lambda a: jnp.transpose(a.reshape(N_HEADS, nseq, t_new, a.shape[-1]), (1, 0, 2, 3)).reshape(
        nseq, rows, a.shape[-1])
    o_lat = _paged(page_table, to_seq(ql_h), to_seq(qp_h), ckv_s.reshape(nseq, t_new, KV_LORA),
                   kpe_s.reshape(nseq, t_new, ROPE_DIM), cache_ckv, jnp.swapaxes(cache_kpe, 2, 3))
    o_lat_h = jnp.transpose(o_lat.reshape(nseq, N_HEADS, t_new, KV_LORA), (1, 0, 2, 3)).reshape(
        N_HEADS, n_s, KV_LORA)
    o_s_h = _unabsorb(o_lat_h, p["wuv_h"], p["g_attn_h"])
    o_s = jnp.transpose(o_s_h[:, :, :V_DIM], (1, 0, 2)).reshape(n_s, ATTN_WIDTH)

    cnt0 = jnp.zeros((N_EXPERTS, LANES), F32)
    xs_p, h2, idx_p, gw_p, rank_p, cnt_p = _post_mix(
        xp, o_p, z_p, per_batch(gt1), per_batch(sc2), per_batch(sh2), per_batch(gt2), p, cnt0, None,
        prompt=True, seq_len=seq, row_off=0, n_total=n_tot)
    xs_s, h2, idx_s, gw_s, rank_s, cnt_s = _post_mix(
        xs_in, o_s, z_s, per_token(gt1), per_token(sc2), per_token(sh2), per_token(gt2), p, cnt_p, h2,
        prompt=False, seq_len=None, row_off=n_p, n_total=n_tot)

    ch = ROUTE_CHUNK
    counts = cnt_s[:, 0].astype(jnp.int32)
    padded = (counts + ch - 1) // ch * ch
    pend = jnp.cumsum(padded)
    pstart = pend - padded
    idx_all = jnp.concatenate([idx_p, idx_s], axis=1)
    rank_all = jnp.concatenate([rank_p, rank_s], axis=1)
    eids = jnp.arange(N_EXPERTS, dtype=jnp.int32)
    lookup = lambda table, keys: jnp.sum(jnp.where(keys[..., None] == eids, table, 0), axis=-1)
    dest = lookup(pstart, idx_all) + rank_all
    nch = -(-(n_tot * TOP_K) // ch) + N_EXPERTS
    chunk_start = jnp.arange(nch, dtype=jnp.int32) * ch
    block_e = jnp.minimum(jnp.sum((pend[None, :] <= chunk_start[:, None]).astype(jnp.int32), axis=1),
                          N_EXPERTS - 1)
    n_valid = jnp.clip(lookup(pstart + counts, block_e) - chunk_start, 0, ch).astype(jnp.int32)
    n_used_s = (pend[-1] // ch).astype(jnp.int32)
    n_used = n_used_s.reshape(1)
    cidx = jnp.arange(nch, dtype=jnp.int32)
    run_end = lookup(pend, block_e) // ch
    after = jnp.sum(jnp.where(run_end[:, None] == cidx[None, :], block_e[None, :], 0), axis=1)
    is_first = (cidx == lookup(pstart, block_e) // ch) & (cidx < n_used_s)
    next_e = jnp.where(is_first, jnp.where(run_end < n_used_s, after, -1), -2).astype(jnp.int32)

    tmc = COMBINE_TILE

    def dest_tiles(dst):
        n = dst.shape[1]
        return jnp.transpose(dst.reshape(TOP_K, n // tmc, tmc), (1, 0, 2)).reshape(n // tmc, 1, TOP_K * tmc)

    per_w = n_tot // SC_WORKERS
    idx_sc = jnp.transpose(dest.reshape(TOP_K, SC_WORKERS, per_w // SCATTER_WIN, SCATTER_WIN),
                           (1, 2, 0, 3)).reshape(SC_WORKERS, (per_w // SCATTER_WIN) * TOP_K, SCATTER_WIN)
    x_sorted = _sc_scatter_rows(h2, idx_sc, nch * ch)
    y_sorted = _experts(block_e, n_valid, n_used, next_e, x_sorted, w_e_gate[0], w_e_up[0], w_e_down[0])

    idx_tiles = dest_tiles(dest)
    tiles_p = n_p // tmc
    tiles_a = tiles_p // 2
    sc_idx = lambda part: part.reshape(SC_WORKERS, -1, GATHER_WIN)
    y_a = _sc_gather_rows(y_sorted, sc_idx(idx_tiles[:tiles_a]))
    y_b = _sc_gather_rows(y_sorted, sc_idx(idx_tiles[tiles_a:]))

    gfin = g_final.reshape(1, d)
    fin_p = functools.partial(_finish, xs=xs_p, gw=gw_p.T, gt2=per_batch(gt2), g_final=gfin, fsc=per_batch(fsc),
                              fsh=per_batch(fsh), prompt=True, seq_len=seq)
    y_p = fin_p(y_a, tile_lo=0, n_tiles=tiles_a, y_tile_off=0)
    y_p = fin_p(y_b, tile_lo=tiles_a, n_tiles=tiles_p - tiles_a, y_tile_off=0, out_buf=y_p)
    y_s = _finish(y_b, xs_s, gw_s.T, per_token(gt2), gfin, per_token(fsc), per_token(fsh), prompt=False,
                  seq_len=None, tile_lo=0, n_tiles=n_s // tmc, y_tile_off=tiles_p - tiles_a)

    return (y_p.reshape(bsz, seq, d), y_s.reshape(nseq, t_new, d),
            ckv_p.reshape(1, bsz, seq, KV_LORA), kpe_p.reshape(1, bsz, seq, ROPE_DIM),
            conv_p.reshape(1, bsz, CONV_W - 1, CONV_DIM),
            ckv_s.reshape(1, nseq, t_new, KV_LORA), kpe_s.reshape(1, nseq, t_new, ROPE_DIM),
            u_s.reshape(nseq, t_new, CONV_DIM)[:, t_new - (CONV_W - 1):, :].reshape(1, nseq, CONV_W - 1, CONV_DIM))
```

```python
import functools
import math

import jax
import jax.numpy as jnp
from jax import lax
from jax.experimental import pallas as pl
from jax.experimental.pallas import tpu as pltpu
from jax.experimental.pallas import tpu_sc as plsc

F32 = jnp.float32
BF16 = jnp.bfloat16

D_MODEL = 1024
N_HEADS = 8
NOPE_DIM = 64
ROPE_DIM = 32
V_DIM = 64
Q_LORA = 384
KV_LORA = 256
ATTN_WIDTH = N_HEADS * V_DIM
CONV_DIM = 512
CONV_GROUPS = 8
CONV_GROUP_DIM = CONV_DIM // CONV_GROUPS
CONV_W = 3
ROPE_THETA = 10000.0
PAGE_SIZE = 128
N_EXPERTS = 256
TOP_K = 8
N_GROUPS = 8
GROUP_SIZE = N_EXPERTS // N_GROUPS
TOPK_GROUPS = 4
E_FF = 256
SHARED_FF = 256
ROUTED_SCALE = 2.5
EPS = 1e-6

LANES = 128
HEAD_PAD = LANES
QK_WIDTH = N_HEADS * HEAD_PAD
PE_LO = NOPE_DIM
PE_HALF = ROPE_DIM // 2
IN_PAD_COLS = Q_LORA + KV_LORA + HEAD_PAD + 3 * CONV_DIM
VMEM_LIMIT = 48 * 1024 * 1024
Q_SCALE = (NOPE_DIM + ROPE_DIM) ** -0.5 * math.log2(math.e)
PACK_W = D_MODEL // 2

TILE_PROMPT = 256
TILE_SAMPLE = 128
TQ = 512
TK = 256
STRIP = 128
ROUTE_CHUNK = 512
COMBINE_TILE = 256


def _cparams(sem, vmem=VMEM_LIMIT):
    return pltpu.CompilerParams(dimension_semantics=sem, vmem_limit_bytes=vmem)


def _dot(a, b):
    return jnp.dot(a, b, preferred_element_type=F32)


def _dot_nt(a, b):
    return lax.dot_general(a, b, (((1,), (1,)), ((), ())), preferred_element_type=F32)


def _rms(x, g):
    r = lax.rsqrt(jnp.mean(x * x, axis=-1, keepdims=True) + EPS)
    return (x * r) * g


def _pack_pair(xb):
    lo = lax.bitcast_convert_type(xb[:, :PACK_W].astype(F32), jnp.uint32) >> 16
    hi = lax.bitcast_convert_type(xb[:, PACK_W:].astype(F32), jnp.uint32) & jnp.uint32(0xFFFF0000)
    return lo | hi


def _unpack_pair(w):
    lo = lax.bitcast_convert_type(w << 16, F32)
    hi = lax.bitcast_convert_type(w & jnp.uint32(0xFFFF0000), F32)
    return lo, hi


def _ada_body(c_ref, w_ref, b_ref, o_ref):
    c = c_ref[...]
    a = (c * jax.nn.sigmoid(c)).astype(BF16)
    o_ref[...] = _dot(a, w_ref[...].astype(BF16)) + b_ref[...]


def _ada(c_all, w, b):
    m, d = c_all.shape
    n = w.shape[1]
    tn = 1024
    return pl.pallas_call(
        _ada_body,
        grid=(n // tn,),
        in_specs=[pl.BlockSpec((m, d), lambda j: (0, 0)),
                  pl.BlockSpec((d, tn), lambda j: (0, j)),
                  pl.BlockSpec((1, tn), lambda j: (0, j))],
        out_specs=pl.BlockSpec((m, tn), lambda j: (0, j)),
        out_shape=jax.ShapeDtypeStruct((m, n), F32),
        compiler_params=_cparams(("arbitrary",)),
        name="ada",
    )(c_all, w, b.reshape(1, n))


def _rope_lanes(x, cos, sin_up, sin_dn):
    w = x.shape[1]
    up = pltpu.roll(x, PE_HALF, 1)
    dn = pltpu.roll(x, w - PE_HALF, 1)
    return x * cos + up * sin_up + dn * sin_dn


def _mix_in_body(*refs, prompt, tm, tiles_per_seq, t_new):
    if prompt:
        (x_ref, sc_ref, sh_ref, g1_ref, win_ref, gq_ref, wq_ref, gkv_ref, wuk_ref, wuv_ref,
         cos_ref, sup_ref, sdn_ref, wconv_ref, gconv_ref, gmat_ref,
         q_ref, k_ref, v_ref, ckv_ref, kpe_ref, z_ref, cst_ref, carry_ref) = refs
    else:
        (x_ref, sc_ref, sh_ref, g1_ref, win_ref, gq_ref, wq_ref, gkv_ref,
         cos_ref, sup_ref, sdn_ref, wconv_ref, gconv_ref, gmat_ref, pa_ref, pb_ref,
         q_ref, ckv_ref, kpe_ref, z_ref, u_ref) = refs

    x = x_ref[...]
    h = _rms(x, g1_ref[...]) * (1.0 + sc_ref[...]) + sh_ref[...]
    proj = _dot(h.astype(BF16), win_ref[...])
    o0 = Q_LORA
    o1 = o0 + KV_LORA
    o2 = o1 + HEAD_PAD
    o3 = o2 + CONV_DIM
    o4 = o3 + CONV_DIM
    q_a, kv_a, kpe_blk = proj[:, :o0], proj[:, o0:o1], proj[:, o1:o2]
    b_g, c_g, u_in = proj[:, o2:o3], proj[:, o3:o4], proj[:, o4:]

    cos, sup, sdn = cos_ref[...], sup_ref[...], sdn_ref[...]
    cos8 = jnp.concatenate([cos] * N_HEADS, axis=1)
    sup8 = jnp.concatenate([sup] * N_HEADS, axis=1)
    sdn8 = jnp.concatenate([sdn] * N_HEADS, axis=1)

    qn = _rms(q_a, gq_ref[...]).astype(BF16)
    q = _dot(qn, wq_ref[...]) * Q_SCALE
    q_ref[...] = _rope_lanes(q, cos8, sup8, sdn8).astype(BF16)

    ckv = _rms(kv_a, gkv_ref[...])
    ckv_ref[...] = ckv
    kpe = _rope_lanes(kpe_blk, cos, sup, sdn)
    if prompt:
        kpe_ref[...] = kpe.T[PE_LO:PE_LO + ROPE_DIM, :]
    else:
        kpe_ref[...] = kpe[:, PE_LO:PE_LO + ROPE_DIM]

    if prompt:
        ckvb = ckv.astype(BF16)
        k = _dot(ckvb, wuk_ref[...]) + jnp.concatenate([kpe] * N_HEADS, axis=1)
        k_ref[...] = k.astype(BF16)
        lane = lax.broadcasted_iota(jnp.int32, (1, QK_WIDTH), 1)
        ones_hi = jnp.where(lane % HEAD_PAD >= V_DIM, 1.0, 0.0)
        v_ref[...] = (_dot(ckvb, wuv_ref[...]) + ones_hi).astype(BF16)

    u = c_g * u_in
    row = lax.broadcasted_iota(jnp.int32, (tm, 1), 0)
    r1 = pltpu.roll(u, 1, 0)
    r2 = pltpu.roll(u, 2, 0)
    if prompt:
        @pl.when(pl.program_id(0) % tiles_per_seq == 0)
        def _():
            carry_ref[...] = jnp.zeros_like(carry_ref)
        c6 = carry_ref[6:7, :]
        c7 = carry_ref[7:8, :]
        um1 = jnp.where(row == 0, c7, r1)
        um2 = jnp.where(row == 0, c6, jnp.where(row == 1, c7, r2))
        carry_ref[...] = u[tm - 8:, :]
        cst_ref[...] = u[tm - (CONV_W - 1):, :]
    else:
        t = row % t_new
        um1 = jnp.where(t == 0, pa_ref[...], r1)
        um2 = jnp.where(t == 0, pb_ref[...], jnp.where(t == 1, pa_ref[...], r2))
        u_ref[...] = u
    wc = wconv_ref[...]
    y = um2 * wc[0:1, :] + um1 * wc[1:2, :] + u * wc[2:3, :]
    zz = b_g * y
    ms = _dot((zz * zz).astype(BF16), gmat_ref[...]) * (1.0 / CONV_GROUP_DIM)
    z_ref[...] = (zz * lax.rsqrt(ms + EPS) * gconv_ref[...]).astype(BF16)


def _mix_in(x2d, sc, sh, p, rope_tabs, *, prompt, seq_len=None, prev_a=None, prev_b=None, t_new=1):
    n = x2d.shape[0]
    tm = TILE_PROMPT if prompt else TILE_SAMPLE
    nt = n // tm
    cos_t, sup_t, sdn_t = rope_tabs
    full = lambda a: pl.BlockSpec(a.shape, lambda i: (0,) * a.ndim)
    row_tile = lambda w: pl.BlockSpec((tm, w), lambda i: (i, 0))
    if prompt:
        tps = seq_len // tm
        mod_spec = pl.BlockSpec((None, 1, D_MODEL), lambda i: (i // tps, 0, 0))
        tab_spec = pl.BlockSpec((tm, LANES), lambda i: (i % tps, 0))
        nb = n // seq_len
        ins = [x2d, sc, sh, p["g1"], p["w_in"], p["g_q"], p["wq"], p["g_kv"], p["wuk"], p["wuv"],
               cos_t, sup_t, sdn_t, p["w_conv"], p["g_conv"], p["gmat"]]
        in_specs = [row_tile(D_MODEL), mod_spec, mod_spec, full(p["g1"]), full(p["w_in"]), full(p["g_q"]),
                    full(p["wq"]), full(p["g_kv"]), full(p["wuk"]), full(p["wuv"]),
                    tab_spec, tab_spec, tab_spec, full(p["w_conv"]), full(p["g_conv"]), full(p["gmat"])]
        out_shape = [jax.ShapeDtypeStruct((n, QK_WIDTH), BF16), jax.ShapeDtypeStruct((n, QK_WIDTH), BF16),
                     jax.ShapeDtypeStruct((n, QK_WIDTH), BF16), jax.ShapeDtypeStruct((n, KV_LORA), F32),
                     jax.ShapeDtypeStruct((nb, ROPE_DIM, seq_len), F32), jax.ShapeDtypeStruct((n, CONV_DIM), BF16),
                     jax.ShapeDtypeStruct((nb, CONV_W - 1, CONV_DIM), F32)]
        out_specs = [row_tile(QK_WIDTH), row_tile(QK_WIDTH), row_tile(QK_WIDTH), row_tile(KV_LORA),
                     pl.BlockSpec((None, ROPE_DIM, tm), lambda i: (i // tps, 0, i % tps)), row_tile(CONV_DIM),
                     pl.BlockSpec((None, CONV_W - 1, CONV_DIM), lambda i: (i // tps, 0, 0))]
        scratch = [pltpu.VMEM((8, CONV_DIM), F32)]
    else:
        tps = 1
        ins = [x2d, sc, sh, p["g1"], p["w_in"], p["g_q"], p["wq"], p["g_kv"],
               cos_t, sup_t, sdn_t, p["w_conv"], p["g_conv"], p["gmat"], prev_a, prev_b]
        in_specs = [row_tile(D_MODEL), row_tile(D_MODEL), row_tile(D_MODEL), full(p["g1"]), full(p["w_in"]),
                    full(p["g_q"]), full(p["wq"]), full(p["g_kv"]),
                    row_tile(LANES), row_tile(LANES), row_tile(LANES), full(p["w_conv"]), full(p["g_conv"]),
                    full(p["gmat"]), row_tile(CONV_DIM), row_tile(CONV_DIM)]
        out_shape = [jax.ShapeDtypeStruct((n, QK_WIDTH), BF16), jax.ShapeDtypeStruct((n, KV_LORA), F32),
                     jax.ShapeDtypeStruct((n, ROPE_DIM), F32), jax.ShapeDtypeStruct((n, CONV_DIM), BF16),
                     jax.ShapeDtypeStruct((n, CONV_DIM), F32)]
        out_specs = [row_tile(QK_WIDTH), row_tile(KV_LORA), row_tile(ROPE_DIM), row_tile(CONV_DIM),
                     row_tile(CONV_DIM)]
        scratch = []
    return pl.pallas_call(
        functools.partial(_mix_in_body, prompt=prompt, tm=tm, tiles_per_seq=tps, t_new=t_new),
        grid=(nt,), in_specs=in_specs, out_specs=out_specs, out_shape=out_shape,
        scratch_shapes=scratch, compiler_params=_cparams(("arbitrary",)),
        name="mix_in_prompt" if prompt else "mix_in_sample",
    )(*ins)


def _flash_body(q_ref, k_ref, v_ref, g_ref, gm_ref, o_ref, m_sc, acc_sc, s_sc, p_sc, *, tq, tk):
    assert tq % tk == 0
    i = pl.program_id(2)
    m_sc[...] = jnp.full(m_sc.shape, -jnp.inf, F32)
    acc_sc[...] = jnp.zeros(acc_sc.shape, F32)
    row_g = i * tq + lax.broadcasted_iota(jnp.int32, (STRIP, tk), 0)
    col_l = lax.broadcasted_iota(jnp.int32, (STRIP, tk), 1)
    heads = [slice(hh * HEAD_PAD, (hh + 1) * HEAD_PAD) for hh in range(2)]

    def logits(t, slot, row_lo=0):
        ks = pl.multiple_of(t * tk, tk)
        for hh in range(2):
            s_sc[slot, hh, row_lo:, :] = _dot_nt(q_ref[row_lo:, heads[hh]],
                                                 k_ref[pl.ds(ks, tk), heads[hh]])

    def consume(t, slot, masked, row_lo=0):
        ks = pl.multiple_of(t * tk, tk)
        for hh in range(2):
            for r0 in range(row_lo, tq, STRIP):
                rows = slice(r0, r0 + STRIP)
                s = s_sc[slot, hh, rows, :]
                if masked:
                    s = jnp.where(ks + col_l <= row_g + r0, s, -jnp.inf)
                m_prev = m_sc[hh, rows, :]
                m_new = jnp.maximum(m_prev, jnp.max(s, axis=1, keepdims=True))
                alpha = jnp.exp2(m_prev - m_new)
                pr = jnp.exp2(s - jnp.concatenate([m_new] * (tk // LANES), axis=1))
                p_sc[hh, rows, :] = pr.astype(BF16)
                acc_sc[hh, rows, :] = alpha * acc_sc[hh, rows, :]
                m_sc[hh, rows, :] = m_new
            acc_sc[hh, row_lo:, :] = acc_sc[hh, row_lo:, :] + _dot(p_sc[hh, row_lo:, :],
                                                                   v_ref[pl.ds(ks, tk), heads[hh]])

    n_diag = tq // tk
    n_full = i * n_diag
    logits(0, 0)
    if n_diag % 2 == 0:
        def tile_pair(p, _):
            logits(2 * p + 1, 1)
            consume(2 * p, 0, False)
            logits(2 * p + 2, 0)
            consume(2 * p + 1, 1, False)
            return 0
        lax.fori_loop(0, n_full // 2, tile_pair, 0)
    else:
        def tile(t, _):
            logits(t + 1, (t + 1) % 2)
            consume(t, t % 2, False)
            return 0
        lax.fori_loop(0, n_full, tile, 0)
    for jj in range(n_diag):
        t = n_full + jj
        slot = jj % 2 if n_diag % 2 == 0 else t % 2
        if jj + 1 < n_diag:
            logits(t + 1, 1 - slot, row_lo=(jj + 1) * tk)
        consume(t, slot, True, row_lo=jj * tk)

    outs = []
    for hh in range(2):
        acc = acc_sc[hh]
        outs.append(acc * pltpu.roll(1.0 / acc, V_DIM, 1))
    lane = lax.broadcasted_iota(jnp.int32, (tq, LANES), 1)
    o = jnp.where(lane < V_DIM, outs[0], pltpu.roll(outs[1], V_DIM, 1))
    ms = _dot((o * o).astype(BF16), gm_ref[...]) * (1.0 / V_DIM)
    o_ref[...] = (o * lax.rsqrt(ms + EPS) * g_ref[...]).astype(BF16)


def _flash(q, k, v, g_attn, gm_head, batch, seq_len):
    n = q.shape[0]
    nq = seq_len // TQ
    pair = pl.BlockSpec((seq_len, 2 * HEAD_PAD), lambda b, p, i: (b, p))
    return pl.pallas_call(
        functools.partial(_flash_body, tq=TQ, tk=TK),
        grid=(batch, N_HEADS // 2, nq),
        in_specs=[pl.BlockSpec((TQ, 2 * HEAD_PAD), lambda b, p, i: (b * nq + i, p)),
                  pair, pair,
                  pl.BlockSpec((1, 2 * V_DIM), lambda b, p, i: (0, p)),
                  pl.BlockSpec((LANES, LANES), lambda b, p, i: (0, 0))],
        out_specs=pl.BlockSpec((TQ, 2 * V_DIM), lambda b, p, i: (b * nq + i, p)),
        out_shape=jax.ShapeDtypeStruct((n, ATTN_WIDTH), BF16),
        scratch_shapes=[pltpu.VMEM((2, TQ, LANES), F32), pltpu.VMEM((2, TQ, LANES), F32),
                        pltpu.VMEM((2, 2, TQ, TK), F32), pltpu.VMEM((2, TQ, TK), BF16)],
        compiler_params=_cparams(("arbitrary", "arbitrary", "arbitrary")),
        name="flash_prompt",
    )(q, k, v, g_attn, gm_head)


def _absorb_body(q_ref, w_ref, ql_ref, qp_ref):
    qb = q_ref[...]
    ql_ref[...] = _dot(qb, w_ref[...]).astype(BF16)
    qp_ref[...] = qb[:, PE_LO:PE_LO + ROPE_DIM]


def _absorb(q_s, wuk_t):
    n = q_s.shape[0]
    return pl.pallas_call(
        _absorb_body,
        grid=(N_HEADS,),
        in_specs=[pl.BlockSpec((n, HEAD_PAD), lambda h: (0, h)),
                  pl.BlockSpec((None, HEAD_PAD, KV_LORA), lambda h: (h, 0, 0))],
        out_specs=[pl.BlockSpec((None, n, KV_LORA), lambda h: (h, 0, 0)),
                   pl.BlockSpec((None, n, ROPE_DIM), lambda h: (h, 0, 0))],
        out_shape=[jax.ShapeDtypeStruct((N_HEADS, n, KV_LORA), BF16),
                   jax.ShapeDtypeStruct((N_HEADS, n, ROPE_DIM), BF16)],
        compiler_params=_cparams(("arbitrary",)),
        name="absorb",
    )(q_s, wuk_t)


PAGES_PER_CHUNK = 8


def _paged_body(pt_ref, ql_ref, qp_ref, cn_ref, kn_ref, cckv_ref, ckpe_ref, o_ref,
                ckv_buf, kpe_buf, s_all, kcb, sems, *, n_pages, n_seq, t_new):
    b = pl.program_id(0)
    slot = b % 2

    def fetch(seq, sl):
        def body(g, _):
            for u in range(PAGES_PER_CHUNK):
                pg = g * PAGES_PER_CHUNK + u
                page = pt_ref[seq, pg]
                pltpu.make_async_copy(cckv_ref.at[0, page], ckv_buf.at[sl, pg], sems.at[0, sl]).start()
                pltpu.make_async_copy(ckpe_ref.at[0, page], kpe_buf.at[sl, pg],
                                      sems.at[1, sl]).start(priority=1)
            return 0
        lax.fori_loop(0, n_pages // PAGES_PER_CHUNK, body, 0)

    @pl.when(b == 0)
    def _():
        fetch(0, 0)

    @pl.when(b + 1 < n_seq)
    def _():
        fetch(b + 1, 1 - slot)

    pltpu.make_async_copy(cckv_ref.at[0, pl.ds(0, n_pages)], ckv_buf.at[slot], sems.at[0, slot]).wait()
    pltpu.make_async_copy(ckpe_ref.at[0, pl.ds(0, n_pages)], kpe_buf.at[slot], sems.at[1, slot]).wait()

    ql = ql_ref[...]
    qp = qp_ref[...]
    rows = ql.shape[0]
    ck = PAGES_PER_CHUNK * PAGE_SIZE

    n_chunks = n_pages // PAGES_PER_CHUNK

    m = jnp.full((rows, 1), -jnp.inf, F32)
    for c in range(n_chunks):
        pages = slice(c * PAGES_PER_CHUNK, (c + 1) * PAGES_PER_CHUNK)
        kc = ckv_buf[slot, pages].reshape(ck, KV_LORA).astype(BF16)
        kcb[c] = kc
        pc = kpe_buf[slot, pages].astype(BF16)
        s_pe = jnp.concatenate([_dot(qp, pc[j]) for j in range(PAGES_PER_CHUNK)], axis=1)
        s = _dot_nt(ql, kc) + s_pe
        s_all[c] = s
        m = jnp.maximum(m, jnp.max(s, axis=1, keepdims=True))

    qlf = ql.astype(F32)
    qpf = qp.astype(F32)
    cn = cn_ref[...]
    kn = kn_ref[...]
    trow = lax.broadcasted_iota(jnp.int32, (rows, 1), 0) % t_new
    s_new = []
    for j in range(t_new):
        sj = (jnp.sum(qlf * cn[j:j + 1, :], axis=1, keepdims=True)
              + jnp.sum(qpf * kn[j:j + 1, :], axis=1, keepdims=True))
        sj = jnp.where(trow >= j, sj, -jnp.inf)
        s_new.append(sj)
        m = jnp.maximum(m, sj)

    l = jnp.zeros((rows, 1), F32)
    acc = jnp.zeros((rows, KV_LORA), F32)
    for c in range(n_chunks):
        pr = jnp.exp2(s_all[c] - m)
        l = l + jnp.sum(pr, axis=1, keepdims=True)
        acc = acc + _dot(pr.astype(BF16), kcb[c])
    for j in range(t_new):
        pj = jnp.exp2(s_new[j] - m)
        l = l + pj
        acc = acc + pj * cn[j:j + 1, :]
    o_ref[...] = acc / l


def _paged(page_table, ql, qp, ckv_new, kpe_new, cache_ckv, cache_kpe):
    n_seq, n_pages = page_table.shape
    rows = ql.shape[1]
    t_new = ckv_new.shape[1]
    grid_spec = pltpu.PrefetchScalarGridSpec(
        num_scalar_prefetch=1,
        grid=(n_seq,),
        in_specs=[pl.BlockSpec((None, rows, KV_LORA), lambda b, pt: (b, 0, 0)),
                  pl.BlockSpec((None, rows, ROPE_DIM), lambda b, pt: (b, 0, 0)),
                  pl.BlockSpec((None, t_new, KV_LORA), lambda b, pt: (b, 0, 0)),
                  pl.BlockSpec((None, t_new, ROPE_DIM), lambda b, pt: (b, 0, 0)),
                  pl.BlockSpec(memory_space=pl.ANY),
                  pl.BlockSpec(memory_space=pl.ANY)],
        out_specs=pl.BlockSpec((None, rows, KV_LORA), lambda b, pt: (b, 0, 0)),
        scratch_shapes=[pltpu.VMEM((2, n_pages, PAGE_SIZE, KV_LORA), F32),
                        pltpu.VMEM((2, n_pages, ROPE_DIM, PAGE_SIZE), F32),
                        pltpu.VMEM((n_pages // PAGES_PER_CHUNK, rows, PAGES_PER_CHUNK * PAGE_SIZE), F32),
                        pltpu.VMEM((n_pages // PAGES_PER_CHUNK, PAGES_PER_CHUNK * PAGE_SIZE, KV_LORA), BF16),
                        pltpu.SemaphoreType.DMA((2, 2))],
    )
    return pl.pallas_call(
        functools.partial(_paged_body, n_pages=n_pages, n_seq=n_seq, t_new=t_new),
        grid_spec=grid_spec,
        out_shape=jax.ShapeDtypeStruct((n_seq, rows, KV_LORA), F32),
        compiler_params=_cparams(("arbitrary",)),
        name="paged_attn",
    )(page_table, ql, qp, ckv_new, kpe_new, cache_ckv, cache_kpe)


def _unabsorb_body(o_ref, w_ref, g_ref, out_ref):
    o = _dot(o_ref[...].astype(BF16), w_ref[...])
    r = lax.rsqrt(jnp.sum(o * o, axis=-1, keepdims=True) * (1.0 / V_DIM) + EPS)
    out_ref[...] = (o * r * g_ref[...]).astype(BF16)


def _unabsorb(o_lat_h, wuv_h, g_h):
    n = o_lat_h.shape[1]
    return pl.pallas_call(
        _unabsorb_body,
        grid=(N_HEADS,),
        in_specs=[pl.BlockSpec((None, n, KV_LORA), lambda h: (h, 0, 0)),
                  pl.BlockSpec((None, KV_LORA, HEAD_PAD), lambda h: (h, 0, 0)),
                  pl.BlockSpec((None, 1, HEAD_PAD), lambda h: (h, 0, 0))],
        out_specs=pl.BlockSpec((None, n, HEAD_PAD), lambda h: (h, 0, 0)),
        out_shape=jax.ShapeDtypeStruct((N_HEADS, n, HEAD_PAD), BF16),
        compiler_params=_cparams(("arbitrary",)),
        name="unabsorb",
    )(o_lat_h, wuv_h, g_h)


def _post_mix_body(x_ref, o_ref, z_ref, gt1_ref, sc_ref, sh_ref, gt2_ref, wo_ref, g2_ref,
                   wr_ref, eb_ref, wsgu_ref, wsd_ref, tri_ref, cin_ref,
                   xs_ref, h2_ref, idx_ref, gw_ref, rank_ref, cout_ref, cnt_ref, *, tm):
    i = pl.program_id(0)

    @pl.when(i == 0)
    def _():
        cnt_ref[...] = cin_ref[...]

    wo = wo_ref[...]
    mix = _dot(o_ref[...], wo[:ATTN_WIDTH, :]) + _dot(z_ref[...], wo[ATTN_WIDTH:, :])
    x1 = x_ref[...] + gt1_ref[...] * mix
    h2 = _rms(x1, g2_ref[...]) * (1.0 + sc_ref[...]) + sh_ref[...]
    hb = h2.astype(BF16)
    h2_ref[...] = _pack_pair(hb)

    gu = _dot(hb, wsgu_ref[...])
    gate, up = gu[:, :SHARED_FF], gu[:, SHARED_FF:]
    shared = _dot((gate * jax.nn.sigmoid(gate) * up).astype(BF16), wsd_ref[...])
    xs_ref[...] = x1 + gt2_ref[...] * shared

    s = jax.nn.sigmoid(_dot_nt(wr_ref[...], hb))
    biased = s + eb_ref[...]
    ninf = -jnp.inf
    gi = lax.broadcasted_iota(jnp.int32, (GROUP_SIZE, tm), 0).astype(F32)
    gs = []
    for g in range(N_GROUPS):
        blk = biased[g * GROUP_SIZE:(g + 1) * GROUP_SIZE, :]
        m1 = jnp.max(blk, axis=0, keepdims=True)
        i1 = jnp.min(jnp.where(blk == m1, gi, float(GROUP_SIZE)), axis=0, keepdims=True)
        m2 = jnp.max(jnp.where(gi == i1, ninf, blk), axis=0, keepdims=True)
        gs.append(m1 + m2)
    gscore = jnp.concatenate(gs, axis=0)
    giota = lax.broadcasted_iota(jnp.int32, (N_GROUPS, tm), 0).astype(F32)
    gsel = jnp.zeros((N_GROUPS, tm), F32)
    for _ in range(TOPK_GROUPS):
        gm = jnp.max(gscore, axis=0, keepdims=True)
        gidx = jnp.min(jnp.where(gscore == gm, giota, float(N_GROUPS)), axis=0, keepdims=True)
        hit = giota == gidx
        gsel = jnp.where(hit, 1.0, gsel)
        gscore = jnp.where(hit, ninf, gscore)
    masked = jnp.concatenate(
        [jnp.where(gsel[g:g + 1, :] > 0.0, biased[g * GROUP_SIZE:(g + 1) * GROUP_SIZE, :], ninf)
         for g in range(N_GROUPS)], axis=0)
    eiota = lax.broadcasted_iota(jnp.int32, (N_EXPERTS, tm), 0).astype(F32)
    idxs, ws = [], []
    selall = jnp.zeros((N_EXPERTS, tm), F32)
    for _ in range(TOP_K):
        mx = jnp.max(masked, axis=0, keepdims=True)
        ei = jnp.min(jnp.where(masked == mx, eiota, float(N_EXPERTS)), axis=0, keepdims=True)
        hit = eiota == ei
        ws.append(jnp.sum(jnp.where(hit, s, 0.0), axis=0, keepdims=True))
        idxs.append(ei)
        selall = jnp.where(hit, 1.0, selall)
        masked = jnp.where(hit, ninf, masked)
    wsum = ws[0]
    for w in ws[1:]:
        wsum = wsum + w
    gw_ref[...] = jnp.concatenate(ws, axis=0) / wsum * ROUTED_SCALE
    idx_ref[...] = jnp.concatenate(idxs, axis=0).astype(jnp.int32)

    before = _dot(selall.astype(BF16), tri_ref[...]) + cnt_ref[:, 0:1]
    ranks = [jnp.sum(jnp.where(eiota == ei, before, 0.0), axis=0, keepdims=True) for ei in idxs]
    rank_ref[...] = jnp.concatenate(ranks, axis=0).astype(jnp.int32)
    cnt_ref[...] = cnt_ref[...] + jnp.sum(selall, axis=1, keepdims=True)
    cout_ref[...] = cnt_ref[...]


def _post_mix(x2d, o, z, gt1, sc2, sh2, gt2, p, cnt_in, h2_buf, *, prompt, seq_len, row_off, n_total):
    n = x2d.shape[0]
    tm = TILE_PROMPT if prompt else TILE_SAMPLE
    nt = n // tm
    full = lambda a: pl.BlockSpec(a.shape, lambda i: (0,) * a.ndim)
    row_tile = lambda w: pl.BlockSpec((tm, w), lambda i: (i, 0))
    if prompt:
        tps = seq_len // tm
        mod_spec = pl.BlockSpec((None, 1, D_MODEL), lambda i: (i // tps, 0, 0))
    else:
        mod_spec = row_tile(D_MODEL)
    tri = p["tri_p"] if prompt else p["tri_s"]
    boff = row_off // tm
    ins = [x2d, o, z, gt1, sc2, sh2, gt2, p["w_out"], p["g2"], p["wr_t"], p["e_bias"], p["ws_gu"], p["ws_d"],
           tri, cnt_in]
    in_specs = [row_tile(D_MODEL), row_tile(ATTN_WIDTH), row_tile(CONV_DIM), mod_spec, mod_spec, mod_spec,
                mod_spec, full(p["w_out"]), full(p["g2"]), full(p["wr_t"]), full(p["e_bias"]), full(p["ws_gu"]),
                full(p["ws_d"]), full(tri), full(cnt_in)]
    out_shape = [jax.ShapeDtypeStruct((n, D_MODEL), F32), jax.ShapeDtypeStruct((n_total, PACK_W), jnp.uint32),
                 jax.ShapeDtypeStruct((TOP_K, n), jnp.int32), jax.ShapeDtypeStruct((TOP_K, n), F32),
                 jax.ShapeDtypeStruct((TOP_K, n), jnp.int32), jax.ShapeDtypeStruct((N_EXPERTS, LANES), F32)]
    col_tile = pl.BlockSpec((TOP_K, tm), lambda i: (0, i))
    out_specs = [row_tile(D_MODEL), pl.BlockSpec((tm, PACK_W), lambda i: (i + boff, 0)),
                 col_tile, col_tile, col_tile, pl.BlockSpec((N_EXPERTS, LANES), lambda i: (0, 0))]
    aliases = {}
    if h2_buf is not None:
        ins.append(h2_buf)
        in_specs.append(pl.BlockSpec(memory_space=pl.ANY))
        aliases = {len(ins) - 1: 1}
        body = lambda *refs: _post_mix_body(*refs[:15], *refs[16:], tm=tm)
    else:
        body = functools.partial(_post_mix_body, tm=tm)
    return pl.pallas_call(
        body, grid=(nt,), in_specs=in_specs, out_specs=out_specs, out_shape=out_shape,
        scratch_shapes=[pltpu.VMEM((N_EXPERTS, LANES), F32)],
        input_output_aliases=aliases,
        compiler_params=_cparams(("arbitrary",)),
        name="post_mix_prompt" if prompt else "post_mix_sample",
    )(*ins)


def _dest_body(idx_ref, rank_ref, ps_ref, d_ref, *, tm):
    eiota = lax.broadcasted_iota(jnp.int32, (N_EXPERTS, tm), 0)
    ps = ps_ref[...]
    idx = idx_ref[...]
    starts = [jnp.sum(jnp.where(eiota == idx[k:k + 1, :], ps, 0.0), axis=0, keepdims=True) for k in range(TOP_K)]
    d_ref[...] = jnp.concatenate(starts, axis=0).astype(jnp.int32) + rank_ref[...]


def _dest(idx, rank, pstart):
    n = idx.shape[1]
    tm = 512
    tile = pl.BlockSpec((TOP_K, tm), lambda i: (0, i))
    return pl.pallas_call(
        functools.partial(_dest_body, tm=tm),
        grid=(n // tm,),
        in_specs=[tile, tile, pl.BlockSpec((N_EXPERTS, 1), lambda i: (0, 0))],
        out_specs=tile,
        out_shape=jax.ShapeDtypeStruct((TOP_K, n), jnp.int32),
        compiler_params=_cparams(("arbitrary",)),
        name="dest_rows",
    )(idx, rank, pstart.astype(F32).reshape(N_EXPERTS, 1))


def _experts_body(be_ref, nv_ref, nu_ref, nx_ref, x_ref, wg_ref, wu_ref, wd_ref, y_ref,
                  sg, su, sd, wgub, wdb, sem, *, ch):
    c = pl.program_id(0)

    def copies(e):
        return (pltpu.make_async_copy(wg_ref.at[e], sg, sem.at[0]),
                pltpu.make_async_copy(wu_ref.at[e], su, sem.at[1]),
                pltpu.make_async_copy(wd_ref.at[e], sd, sem.at[2]))

    @pl.when(c == 0)
    def _():
        for cp in copies(be_ref[0]):
            cp.start()

    nxt = nx_ref[c]

    @pl.when(nxt >= -1)
    def _():
        for cp in copies(0):
            cp.wait()
        wgub[:, :E_FF] = sg[...].astype(BF16)
        wgub[:, E_FF:] = su[...].astype(BF16)
        wdb[...] = sd[...].astype(BF16)

        @pl.when(nxt >= 0)
        def _():
            for cp in copies(nxt):
                cp.start()

    nv = jnp.where(c < nu_ref[0], nv_ref[c], 0)

    def run(rows):
        row = lax.broadcasted_iota(jnp.int32, (rows, 1), 0)
        w = jnp.where(row < nv, x_ref[:rows, :], jnp.uint32(0))
        lo, hi = _unpack_pair(w)
        lo, hi = lo.astype(BF16), hi.astype(BF16)
        gu = _dot(lo, wgub[:PACK_W, :]) + _dot(hi, wgub[PACK_W:, :])
        gate, up = gu[:, :E_FF], gu[:, E_FF:]
        act = (gate * jax.nn.sigmoid(gate) * up).astype(BF16)
        y_ref[:rows, :] = _pack_pair(_dot(act, wdb[...]).astype(BF16))

    sizes = [ch // 4, ch // 2, ch]
    for lo_n, hi_n in zip([0] + sizes[:-1], sizes):
        pl.when(jnp.logical_and(nv > lo_n, nv <= hi_n))(functools.partial(run, hi_n))


def _experts(block_e, n_valid, n_used, next_e, x_sorted, w_e_gate, w_e_up, w_e_down):
    ch = ROUTE_CHUNK
    nch = x_sorted.shape[0] // ch
    last = lambda c, nu: jnp.minimum(c, jnp.maximum(nu[0] - 1, 0))
    grid_spec = pltpu.PrefetchScalarGridSpec(
        num_scalar_prefetch=4,
        grid=(nch,),
        in_specs=[pl.BlockSpec((ch, PACK_W), lambda c, be, nv, nu, nx: (last(c, nu), 0)),
                  pl.BlockSpec(memory_space=pl.ANY), pl.BlockSpec(memory_space=pl.ANY),
                  pl.BlockSpec(memory_space=pl.ANY)],
        out_specs=pl.BlockSpec((ch, PACK_W), lambda c, be, nv, nu, nx: (last(c, nu), 0)),
        scratch_shapes=[pltpu.VMEM((D_MODEL, E_FF), F32), pltpu.VMEM((D_MODEL, E_FF), F32),
                        pltpu.VMEM((E_FF, D_MODEL), F32),
                        pltpu.VMEM((D_MODEL, 2 * E_FF), BF16),
                        pltpu.VMEM((E_FF, D_MODEL), BF16), pltpu.SemaphoreType.DMA((3,))],
    )
    return pl.pallas_call(
        functools.partial(_experts_body, ch=ch),
        grid_spec=grid_spec,
        out_shape=jax.ShapeDtypeStruct((nch * ch, PACK_W), jnp.uint32),
        compiler_params=_cparams(("arbitrary",)),
        name="experts",
    )(block_e, n_valid, n_used, next_e, x_sorted, w_e_gate, w_e_up, w_e_down)


SC_CORES = 2
SC_SUBCORES = 16
SC_WORKERS = SC_CORES * SC_SUBCORES
SCATTER_WIN = 24
GATHER_WIN = 64


def _sc_mesh():
    return plsc.VectorSubcoreMesh(core_axis_name="c", subcore_axis_name="s")


def _sc_worker():
    return lax.axis_index("s") * SC_CORES + lax.axis_index("c")


def _sc_scatter_rows(rows, idx, n_out):
    n, width = rows.shape
    nw, nwin_k, win = idx.shape
    n_win = nwin_k // TOP_K
    per_w = n // nw

    assert n_win % 2 == 0

    def body(rows_hbm, idx_hbm, out_hbm, idx_v, buf, sem_in, sem_out):
        wid = _sc_worker()
        base = wid * per_w
        pltpu.sync_copy(idx_hbm.at[wid], idx_v)

        def load(j, b):
            return pltpu.make_async_copy(rows_hbm.at[pl.ds(base + j * win, win)], buf.at[b], sem_in.at[b])

        def scatters(j, b):
            return [pltpu.make_async_copy(buf.at[b], out_hbm.at[idx_v.at[j * TOP_K + k]], sem_out.at[b])
                    for k in range(TOP_K)]

        load(0, 0).start()

        @pl.loop(0, n_win, step=2)
        def _(j):
            for b in range(2):
                jj = j + b
                load(jj, b).wait()
                for cp in scatters(jj, b):
                    cp.start()

                @pl.when(jj >= 1)
                def _():
                    for cp in scatters(jj - 1, 1 - b):
                        cp.wait()

                @pl.when(jj + 1 < n_win)
                def _():
                    load(jj + 1, 1 - b).start()

        for cp in scatters(n_win - 1, 1):
            cp.wait()

    return pl.kernel(
        body, out_type=jax.ShapeDtypeStruct((n_out, width), rows.dtype), mesh=_sc_mesh(),
        scratch_types=[pltpu.VMEM((nwin_k, win), jnp.int32), pltpu.VMEM((2, win, width), rows.dtype),
                       pltpu.SemaphoreType.DMA((2,)), pltpu.SemaphoreType.DMA((2,))],
        name="sc_scatter_rows",
    )(rows, idx)


def _sc_gather_rows(table, idx):
    nw, n_win, win = idx.shape
    width = table.shape[1]
    per_w = n_win * win

    assert n_win % 2 == 0

    def body(table_hbm, idx_hbm, out_hbm, idx_v, buf, sem):
        wid = _sc_worker()
        base = wid * per_w
        pltpu.sync_copy(idx_hbm.at[wid], idx_v)

        def gather(j, b):
            return pltpu.make_async_copy(table_hbm.at[idx_v.at[j]], buf.at[b], sem.at[b])

        gather(0, 0).start()

        @pl.loop(0, n_win, step=2)
        def _(j):
            for b in range(2):
                @pl.when(j + b + 1 < n_win)
                def _():
                    gather(j + b + 1, 1 - b).start()
                gather(j + b, b).wait()
                pltpu.sync_copy(buf.at[b], out_hbm.at[pl.ds(base + (j + b) * win, win)])

    return pl.kernel(
        body, out_type=jax.ShapeDtypeStruct((nw * per_w, width), table.dtype), mesh=_sc_mesh(),
        scratch_types=[pltpu.VMEM((n_win, win), jnp.int32), pltpu.VMEM((2, win, width), table.dtype),
                       pltpu.SemaphoreType.DMA((2,))],
        name="sc_gather_rows",
    )(table, idx)


def _finish_body(y_ref, xs_ref, gw_ref, gt2_ref, gf_ref, fsc_ref, fsh_ref, out_ref, *, tm):
    gw = gw_ref[...]
    f_lo = jnp.zeros((tm, PACK_W), F32)
    f_hi = jnp.zeros((tm, PACK_W), F32)
    for k in range(TOP_K):
        lo, hi = _unpack_pair(y_ref[k * tm:(k + 1) * tm, :])
        f_lo = f_lo + gw[:, k:k + 1] * lo
        f_hi = f_hi + gw[:, k:k + 1] * hi
    f = jnp.concatenate([f_lo, f_hi], axis=1)
    x2 = xs_ref[...] + gt2_ref[...] * f
    out_ref[...] = _rms(x2, gf_ref[...]) * (1.0 + fsc_ref[...]) + fsh_ref[...]


def _finish(y_part, xs, gw, gt2, g_final, fsc, fsh, *, prompt, seq_len, tile_lo, n_tiles, y_tile_off, out_buf=None):
    n = xs.shape[0]
    tm = COMBINE_TILE
    row_tile = lambda w: pl.BlockSpec((tm, w), lambda i: (i + tile_lo, 0))
    if prompt:
        tps = seq_len // tm
        mod_spec = pl.BlockSpec((None, 1, D_MODEL), lambda i: ((i + tile_lo) // tps, 0, 0))
    else:
        mod_spec = row_tile(D_MODEL)
    ins = [y_part, xs, gw, gt2, g_final, fsc, fsh]
    in_specs = [pl.BlockSpec((TOP_K * tm, PACK_W), lambda i: (i + y_tile_off, 0)),
                row_tile(D_MODEL), row_tile(TOP_K), mod_spec,
                pl.BlockSpec((1, D_MODEL), lambda i: (0, 0)), mod_spec, mod_spec]
    aliases = {}
    body = functools.partial(_finish_body, tm=tm)
    if out_buf is not None:
        ins.append(out_buf)
        in_specs.append(pl.BlockSpec(memory_space=pl.ANY))
        aliases = {len(ins) - 1: 0}
        body = lambda *refs: _finish_body(*refs[:7], *refs[8:], tm=tm)
    return pl.pallas_call(
        body,
        grid=(n_tiles,),
        in_specs=in_specs,
        out_specs=row_tile(D_MODEL),
        out_shape=jax.ShapeDtypeStruct((n, D_MODEL), F32),
        input_output_aliases=aliases,
        compiler_params=_cparams(("arbitrary",)),
        name="finish_prompt" if prompt else "finish_sample",
    )(*ins)


def _rope_tables(pos):
    inv = 1.0 / (ROPE_THETA ** (jnp.arange(PE_HALF, dtype=F32) / PE_HALF))
    ang = pos.astype(F32)[:, None] * inv[None, :]
    cos, sin = jnp.cos(ang), jnp.sin(ang)
    n = pos.shape[0]
    ones = jnp.ones((n, PE_LO), F32)
    zeros = jnp.zeros((n, PE_LO), F32)
    zh = jnp.zeros((n, PE_HALF), F32)
    tail1 = jnp.ones((n, LANES - PE_LO - ROPE_DIM), F32)
    tail0 = jnp.zeros((n, LANES - PE_LO - ROPE_DIM), F32)
    cos_t = jnp.concatenate([ones, cos, cos, tail1], axis=1)
    sin_up = jnp.concatenate([zeros, zh, sin, tail0], axis=1)
    sin_dn = jnp.concatenate([zeros, -sin, zh, tail0], axis=1)
    return cos_t, sin_up, sin_dn


def _pad_heads(w, width):
    pad = [(0, 0)] * (w.ndim - 1) + [(0, HEAD_PAD - width)]
    w = jnp.pad(w, pad)
    return w.reshape(w.shape[:-2] + (N_HEADS * HEAD_PAD,))


def _block_diag_ones(n, blk):
    r = jnp.arange(n) // blk
    return (r[:, None] == r[None, :]).astype(BF16)


def _prep_weights(w_in, g_attn_norm, g_q, w_q_up, g_kv, w_uk, w_uv, w_conv, g_attn_out, g_conv_out, w_out,
                  g_ffn_norm, w_router, e_bias, w_s_gate, w_s_up, w_s_down):
    o0 = Q_LORA
    o1 = o0 + KV_LORA
    o2 = o1 + ROPE_DIM
    kpe_cols = jnp.pad(w_in[:, o1:o2], ((0, 0), (PE_LO, HEAD_PAD - PE_LO - ROPE_DIM)))
    w_in_p = jnp.concatenate([w_in[:, :o1], kpe_cols, w_in[:, o2:]], axis=1).astype(BF16)
    wq = _pad_heads(w_q_up.reshape(Q_LORA, N_HEADS, NOPE_DIM + ROPE_DIM), NOPE_DIM + ROPE_DIM).astype(BF16)
    wuk = _pad_heads(w_uk, NOPE_DIM).astype(BF16)
    wuv = _pad_heads(w_uv, V_DIM).astype(BF16)
    wuk_t = jnp.pad(jnp.transpose(w_uk, (1, 2, 0)), ((0, 0), (0, HEAD_PAD - NOPE_DIM), (0, 0))).astype(BF16)
    wuv_h = jnp.pad(jnp.transpose(w_uv, (1, 0, 2)), ((0, 0), (0, 0), (0, HEAD_PAD - V_DIM))).astype(BF16)
    g_attn_h = jnp.pad(g_attn_out.reshape(N_HEADS, 1, V_DIM), ((0, 0), (0, 0), (0, HEAD_PAD - V_DIM)))
    tri = lambda t: (jnp.arange(t)[:, None] < jnp.arange(t)[None, :]).astype(BF16)
    return {
        "g1": g_attn_norm.reshape(1, -1), "w_in": w_in_p, "g_q": g_q.reshape(1, -1), "wq": wq,
        "g_kv": g_kv.reshape(1, -1), "wuk": wuk, "wuv": wuv, "wuk_t": wuk_t, "wuv_h": wuv_h,
        "w_conv": w_conv, "g_conv": g_conv_out.reshape(1, -1), "gmat": _block_diag_ones(CONV_DIM, CONV_GROUP_DIM),
        "g_attn": g_attn_out.reshape(1, -1), "g_attn_h": g_attn_h, "gm_head": _block_diag_ones(LANES, V_DIM),
        "w_out": w_out.astype(BF16), "g2": g_ffn_norm.reshape(1, -1), "wr_t": w_router.T.astype(BF16),
        "e_bias": e_bias.reshape(-1, 1), "ws_gu": jnp.concatenate([w_s_gate, w_s_up], axis=1).astype(BF16),
        "ws_d": w_s_down.astype(BF16), "tri_p": tri(TILE_PROMPT), "tri_s": tri(TILE_SAMPLE),
    }


def kernel(x_prompt, x_sample, c_prompt, c_sample, cache_ckv, cache_kpe, state_conv, page_table, w_ada, b_ada, g_attn_norm, w_in, g_q, w_q_up, g_kv, w_uk, w_uv, w_conv, g_attn_out, g_conv_out, w_out, g_ffn_norm, w_router, e_bias, w_e_gate, w_e_up, w_e_down, w_s_gate, w_s_up, w_s_down, w_ada_final, b_ada_final, g_final):
    assert w_ada.shape[0] == 1, "one layer"
    bsz, seq, d = x_prompt.shape
    nseq, t_new, _ = x_sample.shape
    n_p, n_s = bsz * seq, nseq * t_new
    n_tot = n_p + n_s
    past = page_table.shape[1] * PAGE_SIZE

    p = _prep_weights(w_in[0], g_attn_norm[0], g_q[0], w_q_up[0], g_kv[0], w_uk[0], w_uv[0], w_conv[0],
                      g_attn_out[0], g_conv_out[0], w_out[0], g_ffn_norm[0], w_router[0], e_bias[0],
                      w_s_gate[0], w_s_up[0], w_s_down[0])

    c_all = jnp.concatenate([c_prompt, c_sample], axis=0)
    mod = _ada(c_all, w_ada[0], b_ada[0])
    modf = _ada(c_all, w_ada_final, b_ada_final)
    sh1, sc1, gt1, sh2, sc2, gt2 = [mod[:, j * d:(j + 1) * d] for j in range(6)]
    fsh, fsc = modf[:, :d], modf[:, d:]
    per_batch = lambda a: a[:bsz].reshape(bsz, 1, d)
    per_token = lambda a: jnp.repeat(a[bsz:], t_new, axis=0)

    xp = x_prompt.reshape(n_p, d)
    xs_in = x_sample.reshape(n_s, d)
    tabs_p = _rope_tables(jnp.arange(seq, dtype=jnp.int32))
    tabs_s = tuple(jnp.tile(t, (nseq, 1)) for t in _rope_tables(past + jnp.arange(t_new, dtype=jnp.int32)))

    q_p, k_p, v_p, ckv_p, kpe_p, z_p, conv_p = _mix_in(
        xp, per_batch(sc1), per_batch(sh1), p, tabs_p, prompt=True, seq_len=seq)
    o_p = _flash(q_p, k_p, v_p, p["g_attn"], p["gm_head"], bsz, seq)

    prev = state_conv[0]
    prev_a = jnp.repeat(prev[:, 1, :], t_new, axis=0)
    prev_b = jnp.repeat(prev[:, 0, :], t_new, axis=0)
    q_s, ckv_s, kpe_s, z_s, u_s = _mix_in(
        xs_in, per_token(sc1), per_token(sh1), p, tabs_s, prompt=False, prev_a=prev_a, prev_b=prev_b,
        t_new=t_new)
    ql_h, qp_h = _absorb(q_s, p["wuk_t"])
    rows = N_HEADS * t_new
    to_seq = lambda a: jnp.transpose(a.reshape(N_HEADS, nseq, t_new, a.shape[-1]), (1, 0, 2, 3)).reshape(
        nseq, rows, a.shape[-1])
    o_lat = _paged(page_table, to_seq(ql_h), to_seq(qp_h), ckv_s.reshape(nseq, t_new, KV_LORA),
                   kpe_s.reshape(nseq, t_new, ROPE_DIM), cache_ckv, jnp.swapaxes(cache_kpe, 2, 3))
    o_lat_h = jnp.transpose(o_lat.reshape(nseq, N_HEADS, t_new, KV_LORA), (1, 0, 2, 3)).reshape(
        N_HEADS, n_s, KV_LORA)
    o_s_h = _unabsorb(o_lat_h, p["wuv_h"], p["g_attn_h"])
    o_s = jnp.transpose(o_s_h[:, :, :V_DIM], (1, 0, 2)).reshape(n_s, ATTN_WIDTH)

    cnt0 = jnp.zeros((N_EXPERTS, LANES), F32)
    xs_p, h2, idx_p, gw_p, rank_p, cnt_p = _post_mix(
        xp, o_p, z_p, per_batch(gt1), per_batch(sc2), per_batch(sh2), per_batch(gt2), p, cnt0, None,
        prompt=True, seq_len=seq, row_off=0, n_total=n_tot)
    xs_s, h2, idx_s, gw_s, rank_s, cnt_s = _post_mix(
        xs_in, o_s, z_s, per_token(gt1), per_token(sc2), per_token(sh2), per_token(gt2), p, cnt_p, h2,
        prompt=False, seq_len=None, row_off=n_p, n_total=n_tot)

    ch = ROUTE_CHUNK
    counts = cnt_s[:, 0].astype(jnp.int32)
    padded = (counts + ch - 1) // ch * ch
    pend = jnp.cumsum(padded)
    pstart = pend - padded
    idx_all = jnp.concatenate([idx_p, idx_s], axis=1)
    rank_all = jnp.concatenate([rank_p, rank_s], axis=1)
    eids = jnp.arange(N_EXPERTS, dtype=jnp.int32)
    lookup = lambda table, keys: jnp.sum(jnp.where(keys[..., None] == eids, table, 0), axis=-1)
    dest = _dest(idx_all, rank_all, pstart)
    nch = -(-(n_tot * TOP_K) // ch) + N_EXPERTS
    chunk_start = jnp.arange(nch, dtype=jnp.int32) * ch
    block_e = jnp.minimum(jnp.sum((pend[None, :] <= chunk_start[:, None]).astype(jnp.int32), axis=1),
                          N_EXPERTS - 1)
    n_valid = jnp.clip(lookup(pstart + counts, block_e) - chunk_start, 0, ch).astype(jnp.int32)
    n_used_s = (pend[-1] // ch).astype(jnp.int32)
    n_used = n_used_s.reshape(1)
    cidx = jnp.arange(nch, dtype=jnp.int32)
    run_end = lookup(pend, block_e) // ch
    after = jnp.sum(jnp.where(run_end[:, None] == cidx[None, :], block_e[None, :], 0), axis=1)
    is_first = (cidx == lookup(pstart, block_e) // ch) & (cidx < n_used_s)
    next_e = jnp.where(is_first, jnp.where(run_end < n_used_s, after, -1), -2).astype(jnp.int32)

    tmc = COMBINE_TILE

    def dest_tiles(dst):
        n = dst.shape[1]
        return jnp.transpose(dst.reshape(TOP_K, n // tmc, tmc), (1, 0, 2)).reshape(n // tmc, 1, TOP_K * tmc)

    per_w = n_tot // SC_WORKERS
    idx_sc = jnp.transpose(dest.reshape(TOP_K, SC_WORKERS, per_w // SCATTER_WIN, SCATTER_WIN),
                           (1, 2, 0, 3)).reshape(SC_WORKERS, (per_w // SCATTER_WIN) * TOP_K, SCATTER_WIN)
    x_sorted = _sc_scatter_rows(h2, idx_sc, nch * ch)
    y_sorted = _experts(block_e, n_valid, n_used, next_e, x_sorted, w_e_gate[0], w_e_up[0], w_e_down[0])

    idx_tiles = dest_tiles(dest)
    tiles_p = n_p // tmc
    tiles_a = tiles_p // 2
    sc_idx = lambda part: part.reshape(SC_WORKERS, -1, GATHER_WIN)
    y_a = _sc_gather_rows(y_sorted, sc_idx(idx_tiles[:tiles_a]))
    y_b = _sc_gather_rows(y_sorted, sc_idx(idx_tiles[tiles_a:]))

    gfin = g_final.reshape(1, d)
    fin_p = functools.partial(_finish, xs=xs_p, gw=gw_p.T, gt2=per_batch(gt2), g_final=gfin, fsc=per_batch(fsc),
                              fsh=per_batch(fsh), prompt=True, seq_len=seq)
    y_p = fin_p(y_a, tile_lo=0, n_tiles=tiles_a, y_tile_off=0)
    y_p = fin_p(y_b, tile_lo=tiles_a, n_tiles=tiles_p - tiles_a, y_tile_off=0, out_buf=y_p)
    y_s = _finish(y_b, xs_s, gw_s.T, per_token(gt2), gfin, per_token(fsc), per_token(fsh), prompt=False,
                  seq_len=None, tile_lo=0, n_tiles=n_s // tmc, y_tile_off=tiles_p - tiles_a)

    return (y_p.reshape(bsz, seq, d), y_s.reshape(nseq, t_new, d),
            ckv_p.reshape(1, bsz, seq, KV_LORA), jnp.swapaxes(kpe_p, 1, 2)[None],
            conv_p.reshape(1, bsz, CONV_W - 1, CONV_DIM),
            ckv_s.reshape(1, nseq, t_new, KV_LORA), kpe_s.reshape(1, nseq, t_new, ROPE_DIM),
            u_s.reshape(nseq, t_new, CONV_DIM)[:, t_new - (CONV_W - 1):, :].reshape(1, nseq, CONV_W - 1, CONV_DIM))
```

```python
import functools
import math

import jax
import jax.numpy as jnp
from jax import lax
from jax.experimental import pallas as pl
from jax.experimental.pallas import tpu as pltpu
from jax.experimental.pallas import tpu_sc as plsc

F32 = jnp.float32
BF16 = jnp.bfloat16

D_MODEL = 1024
N_HEADS = 8
NOPE_DIM = 64
ROPE_DIM = 32
V_DIM = 64
Q_LORA = 384
KV_LORA = 256
ATTN_WIDTH = N_HEADS * V_DIM
CONV_DIM = 512
CONV_GROUPS = 8
CONV_GROUP_DIM = CONV_DIM // CONV_GROUPS
CONV_W = 3
ROPE_THETA = 10000.0
PAGE_SIZE = 128
N_EXPERTS = 256
TOP_K = 8
N_GROUPS = 8
GROUP_SIZE = N_EXPERTS // N_GROUPS
TOPK_GROUPS = 4
E_FF = 256
SHARED_FF = 256
ROUTED_SCALE = 2.5
EPS = 1e-6

LANES = 128
HEAD_PAD = LANES
QK_WIDTH = N_HEADS * HEAD_PAD
PE_LO = NOPE_DIM
PE_HALF = ROPE_DIM // 2
IN_PAD_COLS = Q_LORA + KV_LORA + HEAD_PAD + 3 * CONV_DIM
VMEM_LIMIT = 48 * 1024 * 1024
Q_SCALE = (NOPE_DIM + ROPE_DIM) ** -0.5 * math.log2(math.e)
PACK_W = D_MODEL // 2

TILE_PROMPT = 256
TILE_SAMPLE = 128
TQ = 512
TK = 256
STRIP = 128
ROUTE_CHUNK = 512
ROW_GRAN = 128
COMBINE_TILE = 256


def _cparams(sem, vmem=VMEM_LIMIT):
    return pltpu.CompilerParams(dimension_semantics=sem, vmem_limit_bytes=vmem)


def _dot(a, b):
    return jnp.dot(a, b, preferred_element_type=F32)


def _dot_nt(a, b):
    return lax.dot_general(a, b, (((1,), (1,)), ((), ())), preferred_element_type=F32)


def _rms(x, g):
    r = lax.rsqrt(jnp.mean(x * x, axis=-1, keepdims=True) + EPS)
    return (x * r) * g


def _pack_pair(xb):
    lo = lax.bitcast_convert_type(xb[:, :PACK_W].astype(F32), jnp.uint32) >> 16
    hi = lax.bitcast_convert_type(xb[:, PACK_W:].astype(F32), jnp.uint32) & jnp.uint32(0xFFFF0000)
    return lo | hi


def _unpack_pair(w):
    lo = lax.bitcast_convert_type(w << 16, F32)
    hi = lax.bitcast_convert_type(w & jnp.uint32(0xFFFF0000), F32)
    return lo, hi


def _ada_body(c_ref, w_ref, b_ref, o_ref):
    c = c_ref[...]
    a = (c * jax.nn.sigmoid(c)).astype(BF16)
    o_ref[...] = _dot(a, w_ref[...].astype(BF16)) + b_ref[...]


def _ada(c_all, w, b):
    m, d = c_all.shape
    n = w.shape[1]
    tn = 1024
    return pl.pallas_call(
        _ada_body,
        grid=(n // tn,),
        in_specs=[pl.BlockSpec((m, d), lambda j: (0, 0)),
                  pl.BlockSpec((d, tn), lambda j: (0, j)),
                  pl.BlockSpec((1, tn), lambda j: (0, j))],
        out_specs=pl.BlockSpec((m, tn), lambda j: (0, j)),
        out_shape=jax.ShapeDtypeStruct((m, n), F32),
        compiler_params=_cparams(("arbitrary",)),
        name="ada",
    )(c_all, w, b.reshape(1, n))


def _rope_lanes(x, cos, sin_up, sin_dn):
    w = x.shape[1]
    up = pltpu.roll(x, PE_HALF, 1)
    dn = pltpu.roll(x, w - PE_HALF, 1)
    return x * cos + up * sin_up + dn * sin_dn


def _mix_in_body(*refs, prompt, tm, tiles_per_seq, t_new):
    if prompt:
        (x_ref, sc_ref, sh_ref, g1_ref, win_ref, gq_ref, wq_ref, gkv_ref, wuk_ref, wuv_ref,
         cos_ref, sup_ref, sdn_ref, wconv_ref, gconv_ref, gmat_ref,
         q_ref, k_ref, v_ref, ckv_ref, kpe_ref, z_ref, cst_ref, carry_ref) = refs
    else:
        (x_ref, sc_ref, sh_ref, g1_ref, win_ref, gq_ref, wq_ref, gkv_ref,
         cos_ref, sup_ref, sdn_ref, wconv_ref, gconv_ref, gmat_ref, pa_ref, pb_ref,
         q_ref, ckv_ref, kpe_ref, z_ref, u_ref) = refs

    x = x_ref[...]
    h = _rms(x, g1_ref[...]) * (1.0 + sc_ref[...]) + sh_ref[...]
    proj = _dot(h.astype(BF16), win_ref[...])
    o0 = Q_LORA
    o1 = o0 + KV_LORA
    o2 = o1 + HEAD_PAD
    o3 = o2 + CONV_DIM
    o4 = o3 + CONV_DIM
    q_a, kv_a, kpe_blk = proj[:, :o0], proj[:, o0:o1], proj[:, o1:o2]
    b_g, c_g, u_in = proj[:, o2:o3], proj[:, o3:o4], proj[:, o4:]

    cos, sup, sdn = cos_ref[...], sup_ref[...], sdn_ref[...]
    cos8 = jnp.concatenate([cos] * N_HEADS, axis=1)
    sup8 = jnp.concatenate([sup] * N_HEADS, axis=1)
    sdn8 = jnp.concatenate([sdn] * N_HEADS, axis=1)

    qn = _rms(q_a, gq_ref[...]).astype(BF16)
    q = _dot(qn, wq_ref[...]) * Q_SCALE
    q_ref[...] = _rope_lanes(q, cos8, sup8, sdn8).astype(BF16)

    ckv = _rms(kv_a, gkv_ref[...])
    ckv_ref[...] = ckv
    kpe = _rope_lanes(kpe_blk, cos, sup, sdn)
    if prompt:
        kpe_ref[...] = kpe.T[PE_LO:PE_LO + ROPE_DIM, :]
    else:
        kpe_ref[...] = kpe[:, PE_LO:PE_LO + ROPE_DIM]

    if prompt:
        ckvb = ckv.astype(BF16)
        k = _dot(ckvb, wuk_ref[...]) + jnp.concatenate([kpe] * N_HEADS, axis=1)
        k_ref[...] = k.astype(BF16)
        lane = lax.broadcasted_iota(jnp.int32, (1, QK_WIDTH), 1)
        ones_hi = jnp.where(lane % HEAD_PAD >= V_DIM, 1.0, 0.0)
        v_ref[...] = (_dot(ckvb, wuv_ref[...]) + ones_hi).astype(BF16)

    u = c_g * u_in
    row = lax.broadcasted_iota(jnp.int32, (tm, 1), 0)
    r1 = pltpu.roll(u, 1, 0)
    r2 = pltpu.roll(u, 2, 0)
    if prompt:
        @pl.when(pl.program_id(0) % tiles_per_seq == 0)
        def _():
            carry_ref[...] = jnp.zeros_like(carry_ref)
        c6 = carry_ref[6:7, :]
        c7 = carry_ref[7:8, :]
        um1 = jnp.where(row == 0, c7, r1)
        um2 = jnp.where(row == 0, c6, jnp.where(row == 1, c7, r2))
        carry_ref[...] = u[tm - 8:, :]
        cst_ref[...] = u[tm - (CONV_W - 1):, :]
    else:
        t = row % t_new
        um1 = jnp.where(t == 0, pa_ref[...], r1)
        um2 = jnp.where(t == 0, pb_ref[...], jnp.where(t == 1, pa_ref[...], r2))
        u_ref[...] = u
    wc = wconv_ref[...]
    y = um2 * wc[0:1, :] + um1 * wc[1:2, :] + u * wc[2:3, :]
    zz = b_g * y
    ms = _dot((zz * zz).astype(BF16), gmat_ref[...]) * (1.0 / CONV_GROUP_DIM)
    z_ref[...] = (zz * lax.rsqrt(ms + EPS) * gconv_ref[...]).astype(BF16)


def _mix_in(x2d, sc, sh, p, rope_tabs, *, prompt, seq_len=None, prev_a=None, prev_b=None, t_new=1):
    n = x2d.shape[0]
    tm = TILE_PROMPT if prompt else TILE_SAMPLE
    nt = n // tm
    cos_t, sup_t, sdn_t = rope_tabs
    full = lambda a: pl.BlockSpec(a.shape, lambda i: (0,) * a.ndim)
    row_tile = lambda w: pl.BlockSpec((tm, w), lambda i: (i, 0))
    if prompt:
        tps = seq_len // tm
        mod_spec = pl.BlockSpec((None, 1, D_MODEL), lambda i: (i // tps, 0, 0))
        tab_spec = pl.BlockSpec((tm, LANES), lambda i: (i % tps, 0))
        nb = n // seq_len
        ins = [x2d, sc, sh, p["g1"], p["w_in"], p["g_q"], p["wq"], p["g_kv"], p["wuk"], p["wuv"],
               cos_t, sup_t, sdn_t, p["w_conv"], p["g_conv"], p["gmat"]]
        in_specs = [row_tile(D_MODEL), mod_spec, mod_spec, full(p["g1"]), full(p["w_in"]), full(p["g_q"]),
                    full(p["wq"]), full(p["g_kv"]), full(p["wuk"]), full(p["wuv"]),
                    tab_spec, tab_spec, tab_spec, full(p["w_conv"]), full(p["g_conv"]), full(p["gmat"])]
        out_shape = [jax.ShapeDtypeStruct((n, QK_WIDTH), BF16), jax.ShapeDtypeStruct((n, QK_WIDTH), BF16),
                     jax.ShapeDtypeStruct((n, QK_WIDTH), BF16), jax.ShapeDtypeStruct((n, KV_LORA), F32),
                     jax.ShapeDtypeStruct((nb, ROPE_DIM, seq_len), F32), jax.ShapeDtypeStruct((n, CONV_DIM), BF16),
                     jax.ShapeDtypeStruct((nb, CONV_W - 1, CONV_DIM), F32)]
        out_specs = [row_tile(QK_WIDTH), row_tile(QK_WIDTH), row_tile(QK_WIDTH), row_tile(KV_LORA),
                     pl.BlockSpec((None, ROPE_DIM, tm), lambda i: (i // tps, 0, i % tps)), row_tile(CONV_DIM),
                     pl.BlockSpec((None, CONV_W - 1, CONV_DIM), lambda i: (i // tps, 0, 0))]
        scratch = [pltpu.VMEM((8, CONV_DIM), F32)]
    else:
        tps = 1
        ins = [x2d, sc, sh, p["g1"], p["w_in"], p["g_q"], p["wq"], p["g_kv"],
               cos_t, sup_t, sdn_t, p["w_conv"], p["g_conv"], p["gmat"], prev_a, prev_b]
        in_specs = [row_tile(D_MODEL), row_tile(D_MODEL), row_tile(D_MODEL), full(p["g1"]), full(p["w_in"]),
                    full(p["g_q"]), full(p["wq"]), full(p["g_kv"]),
                    row_tile(LANES), row_tile(LANES), row_tile(LANES), full(p["w_conv"]), full(p["g_conv"]),
                    full(p["gmat"]), row_tile(CONV_DIM), row_tile(CONV_DIM)]
        out_shape = [jax.ShapeDtypeStruct((n, QK_WIDTH), BF16), jax.ShapeDtypeStruct((n, KV_LORA), F32),
                     jax.ShapeDtypeStruct((n, ROPE_DIM), F32), jax.ShapeDtypeStruct((n, CONV_DIM), BF16),
                     jax.ShapeDtypeStruct((n, CONV_DIM), F32)]
        out_specs = [row_tile(QK_WIDTH), row_tile(KV_LORA), row_tile(ROPE_DIM), row_tile(CONV_DIM),
                     row_tile(CONV_DIM)]
        scratch = []
    return pl.pallas_call(
        functools.partial(_mix_in_body, prompt=prompt, tm=tm, tiles_per_seq=tps, t_new=t_new),
        grid=(nt,), in_specs=in_specs, out_specs=out_specs, out_shape=out_shape,
        scratch_shapes=scratch, compiler_params=_cparams(("arbitrary",)),
        name="mix_in_prompt" if prompt else "mix_in_sample",
    )(*ins)


def _flash_body(q_ref, k_ref, v_ref, g_ref, gm_ref, o_ref, m_sc, acc_sc, s_sc, p_sc, *, tq, tk):
    assert tq % tk == 0
    i = pl.program_id(2)
    m_sc[...] = jnp.full(m_sc.shape, -jnp.inf, F32)
    acc_sc[...] = jnp.zeros(acc_sc.shape, F32)
    row_g = i * tq + lax.broadcasted_iota(jnp.int32, (STRIP, tk), 0)
    col_l = lax.broadcasted_iota(jnp.int32, (STRIP, tk), 1)
    heads = [slice(hh * HEAD_PAD, (hh + 1) * HEAD_PAD) for hh in range(2)]

    def logits(t, slot, row_lo=0):
        ks = pl.multiple_of(t * tk, tk)
        for hh in range(2):
            s_sc[slot, hh, row_lo:, :] = _dot_nt(q_ref[row_lo:, heads[hh]],
                                                 k_ref[pl.ds(ks, tk), heads[hh]])

    def consume(t, slot, masked, row_lo=0):
        ks = pl.multiple_of(t * tk, tk)
        for hh in range(2):
            for r0 in range(row_lo, tq, STRIP):
                rows = slice(r0, r0 + STRIP)
                s = s_sc[slot, hh, rows, :]
                if masked:
                    s = jnp.where(ks + col_l <= row_g + r0, s, -jnp.inf)
                m_prev = m_sc[hh, rows, :]
                m_new = jnp.maximum(m_prev, jnp.max(s, axis=1, keepdims=True))
                alpha = jnp.exp2(m_prev - m_new)
                pr = jnp.exp2(s - jnp.concatenate([m_new] * (tk // LANES), axis=1))
                p_sc[hh, rows, :] = pr.astype(BF16)
                acc_sc[hh, rows, :] = alpha * acc_sc[hh, rows, :]
                m_sc[hh, rows, :] = m_new
            acc_sc[hh, row_lo:, :] = acc_sc[hh, row_lo:, :] + _dot(p_sc[hh, row_lo:, :],
                                                                   v_ref[pl.ds(ks, tk), heads[hh]])

    n_diag = tq // tk
    n_full = i * n_diag
    logits(0, 0)
    if n_diag % 2 == 0:
        def tile_pair(p, _):
            logits(2 * p + 1, 1)
            consume(2 * p, 0, False)
            logits(2 * p + 2, 0)
            consume(2 * p + 1, 1, False)
            return 0
        lax.fori_loop(0, n_full // 2, tile_pair, 0)
    else:
        def tile(t, _):
            logits(t + 1, (t + 1) % 2)
            consume(t, t % 2, False)
            return 0
        lax.fori_loop(0, n_full, tile, 0)
    for jj in range(n_diag):
        t = n_full + jj
        slot = jj % 2 if n_diag % 2 == 0 else t % 2
        if jj + 1 < n_diag:
            logits(t + 1, 1 - slot, row_lo=(jj + 1) * tk)
        consume(t, slot, True, row_lo=jj * tk)

    outs = []
    for hh in range(2):
        acc = acc_sc[hh]
        outs.append(acc * pltpu.roll(1.0 / acc, V_DIM, 1))
    lane = lax.broadcasted_iota(jnp.int32, (tq, LANES), 1)
    o = jnp.where(lane < V_DIM, outs[0], pltpu.roll(outs[1], V_DIM, 1))
    ms = _dot((o * o).astype(BF16), gm_ref[...]) * (1.0 / V_DIM)
    o_ref[...] = (o * lax.rsqrt(ms + EPS) * g_ref[...]).astype(BF16)


def _flash(q, k, v, g_attn, gm_head, batch, seq_len):
    n = q.shape[0]
    nq = seq_len // TQ
    pair = pl.BlockSpec((seq_len, 2 * HEAD_PAD), lambda b, p, i: (b, p))
    return pl.pallas_call(
        functools.partial(_flash_body, tq=TQ, tk=TK),
        grid=(batch, N_HEADS // 2, nq),
        in_specs=[pl.BlockSpec((TQ, 2 * HEAD_PAD), lambda b, p, i: (b * nq + i, p)),
                  pair, pair,
                  pl.BlockSpec((1, 2 * V_DIM), lambda b, p, i: (0, p)),
                  pl.BlockSpec((LANES, LANES), lambda b, p, i: (0, 0))],
        out_specs=pl.BlockSpec((TQ, 2 * V_DIM), lambda b, p, i: (b * nq + i, p)),
        out_shape=jax.ShapeDtypeStruct((n, ATTN_WIDTH), BF16),
        scratch_shapes=[pltpu.VMEM((2, TQ, LANES), F32), pltpu.VMEM((2, TQ, LANES), F32),
                        pltpu.VMEM((2, 2, TQ, TK), F32), pltpu.VMEM((2, TQ, TK), BF16)],
        compiler_params=_cparams(("arbitrary", "arbitrary", "arbitrary")),
        name="flash_prompt",
    )(q, k, v, g_attn, gm_head)


def _absorb_body(q_ref, w_ref, ql_ref, qp_ref):
    qb = q_ref[...]
    ql_ref[...] = _dot(qb, w_ref[...]).astype(BF16)
    qp_ref[...] = qb[:, PE_LO:PE_LO + ROPE_DIM]


def _absorb(q_s, wuk_t):
    n = q_s.shape[0]
    return pl.pallas_call(
        _absorb_body,
        grid=(N_HEADS,),
        in_specs=[pl.BlockSpec((n, HEAD_PAD), lambda h: (0, h)),
                  pl.BlockSpec((None, HEAD_PAD, KV_LORA), lambda h: (h, 0, 0))],
        out_specs=[pl.BlockSpec((None, n, KV_LORA), lambda h: (h, 0, 0)),
                   pl.BlockSpec((None, n, ROPE_DIM), lambda h: (h, 0, 0))],
        out_shape=[jax.ShapeDtypeStruct((N_HEADS, n, KV_LORA), BF16),
                   jax.ShapeDtypeStruct((N_HEADS, n, ROPE_DIM), BF16)],
        compiler_params=_cparams(("arbitrary",)),
        name="absorb",
    )(q_s, wuk_t)


PAGES_PER_CHUNK = 8


def _paged_body(pt_ref, ql_ref, qp_ref, cn_ref, kn_ref, cckv_ref, ckpe_ref, o_ref,
                ckv_buf, kpe_buf, s_all, kcb, sems, *, n_pages, n_seq, t_new):
    b = pl.program_id(0)
    slot = b % 2

    def fetch(seq, sl):
        def body(g, _):
            for u in range(PAGES_PER_CHUNK):
                pg = g * PAGES_PER_CHUNK + u
                page = pt_ref[seq, pg]
                pltpu.make_async_copy(cckv_ref.at[0, page], ckv_buf.at[sl, pg], sems.at[0, sl]).start()
                pltpu.make_async_copy(ckpe_ref.at[0, page], kpe_buf.at[sl, pg],
                                      sems.at[1, sl]).start(priority=1)
            return 0
        lax.fori_loop(0, n_pages // PAGES_PER_CHUNK, body, 0)

    @pl.when(b == 0)
    def _():
        fetch(0, 0)

    @pl.when(b + 1 < n_seq)
    def _():
        fetch(b + 1, 1 - slot)

    pltpu.make_async_copy(cckv_ref.at[0, pl.ds(0, n_pages)], ckv_buf.at[slot], sems.at[0, slot]).wait()
    pltpu.make_async_copy(ckpe_ref.at[0, pl.ds(0, n_pages)], kpe_buf.at[slot], sems.at[1, slot]).wait()

    ql = ql_ref[...]
    qp = qp_ref[...]
    rows = ql.shape[0]
    ck = PAGES_PER_CHUNK * PAGE_SIZE

    n_chunks = n_pages // PAGES_PER_CHUNK

    m = jnp.full((rows, 1), -jnp.inf, F32)
    for c in range(n_chunks):
        pages = slice(c * PAGES_PER_CHUNK, (c + 1) * PAGES_PER_CHUNK)
        kc = ckv_buf[slot, pages].reshape(ck, KV_LORA).astype(BF16)
        kcb[c] = kc
        pc = kpe_buf[slot, pages].astype(BF16)
        s_pe = jnp.concatenate([_dot(qp, pc[j]) for j in range(PAGES_PER_CHUNK)], axis=1)
        s = _dot_nt(ql, kc) + s_pe
        s_all[c] = s
        m = jnp.maximum(m, jnp.max(s, axis=1, keepdims=True))

    qlf = ql.astype(F32)
    qpf = qp.astype(F32)
    cn = cn_ref[...]
    kn = kn_ref[...]
    trow = lax.broadcasted_iota(jnp.int32, (rows, 1), 0) % t_new
    s_new = []
    for j in range(t_new):
        sj = (jnp.sum(qlf * cn[j:j + 1, :], axis=1, keepdims=True)
              + jnp.sum(qpf * kn[j:j + 1, :], axis=1, keepdims=True))
        sj = jnp.where(trow >= j, sj, -jnp.inf)
        s_new.append(sj)
        m = jnp.maximum(m, sj)

    l = jnp.zeros((rows, 1), F32)
    acc = jnp.zeros((rows, KV_LORA), F32)
    for c in range(n_chunks):
        pr = jnp.exp2(s_all[c] - m)
        l = l + jnp.sum(pr, axis=1, keepdims=True)
        acc = acc + _dot(pr.astype(BF16), kcb[c])
    for j in range(t_new):
        pj = jnp.exp2(s_new[j] - m)
        l = l + pj
        acc = acc + pj * cn[j:j + 1, :]
    o_ref[...] = acc / l


def _paged(page_table, ql, qp, ckv_new, kpe_new, cache_ckv, cache_kpe):
    n_seq, n_pages = page_table.shape
    rows = ql.shape[1]
    t_new = ckv_new.shape[1]
    grid_spec = pltpu.PrefetchScalarGridSpec(
        num_scalar_prefetch=1,
        grid=(n_seq,),
        in_specs=[pl.BlockSpec((None, rows, KV_LORA), lambda b, pt: (b, 0, 0)),
                  pl.BlockSpec((None, rows, ROPE_DIM), lambda b, pt: (b, 0, 0)),
                  pl.BlockSpec((None, t_new, KV_LORA), lambda b, pt: (b, 0, 0)),
                  pl.BlockSpec((None, t_new, ROPE_DIM), lambda b, pt: (b, 0, 0)),
                  pl.BlockSpec(memory_space=pl.ANY),
                  pl.BlockSpec(memory_space=pl.ANY)],
        out_specs=pl.BlockSpec((None, rows, KV_LORA), lambda b, pt: (b, 0, 0)),
        scratch_shapes=[pltpu.VMEM((2, n_pages, PAGE_SIZE, KV_LORA), F32),
                        pltpu.VMEM((2, n_pages, ROPE_DIM, PAGE_SIZE), F32),
                        pltpu.VMEM((n_pages // PAGES_PER_CHUNK, rows, PAGES_PER_CHUNK * PAGE_SIZE), F32),
                        pltpu.VMEM((n_pages // PAGES_PER_CHUNK, PAGES_PER_CHUNK * PAGE_SIZE, KV_LORA), BF16),
                        pltpu.SemaphoreType.DMA((2, 2))],
    )
    return pl.pallas_call(
        functools.partial(_paged_body, n_pages=n_pages, n_seq=n_seq, t_new=t_new),
        grid_spec=grid_spec,
        out_shape=jax.ShapeDtypeStruct((n_seq, rows, KV_LORA), F32),
        compiler_params=_cparams(("arbitrary",)),
        name="paged_attn",
    )(page_table, ql, qp, ckv_new, kpe_new, cache_ckv, cache_kpe)


def _unabsorb_body(o_ref, w_ref, g_ref, out_ref):
    o = _dot(o_ref[...].astype(BF16), w_ref[...])
    r = lax.rsqrt(jnp.sum(o * o, axis=-1, keepdims=True) * (1.0 / V_DIM) + EPS)
    out_ref[...] = (o * r * g_ref[...]).astype(BF16)


def _unabsorb(o_lat_h, wuv_h, g_h):
    n = o_lat_h.shape[1]
    return pl.pallas_call(
        _unabsorb_body,
        grid=(N_HEADS,),
        in_specs=[pl.BlockSpec((None, n, KV_LORA), lambda h: (h, 0, 0)),
                  pl.BlockSpec((None, KV_LORA, HEAD_PAD), lambda h: (h, 0, 0)),
                  pl.BlockSpec((None, 1, HEAD_PAD), lambda h: (h, 0, 0))],
        out_specs=pl.BlockSpec((None, n, HEAD_PAD), lambda h: (h, 0, 0)),
        out_shape=jax.ShapeDtypeStruct((N_HEADS, n, HEAD_PAD), BF16),
        compiler_params=_cparams(("arbitrary",)),
        name="unabsorb",
    )(o_lat_h, wuv_h, g_h)


def _post_mix_body(x_ref, o_ref, z_ref, gt1_ref, sc_ref, sh_ref, gt2_ref, wo_ref, g2_ref,
                   wr_ref, eb_ref, wsgu_ref, wsd_ref, tri_ref, cin_ref,
                   xs_ref, h2_ref, idx_ref, gw_ref, rank_ref, cout_ref, cnt_ref, *, tm):
    i = pl.program_id(0)

    @pl.when(i == 0)
    def _():
        cnt_ref[...] = cin_ref[...]

    wo = wo_ref[...]
    mix = _dot(o_ref[...], wo[:ATTN_WIDTH, :]) + _dot(z_ref[...], wo[ATTN_WIDTH:, :])
    x1 = x_ref[...] + gt1_ref[...] * mix
    h2 = _rms(x1, g2_ref[...]) * (1.0 + sc_ref[...]) + sh_ref[...]
    hb = h2.astype(BF16)
    h2_ref[...] = _pack_pair(hb)

    gu = _dot(hb, wsgu_ref[...])
    gate, up = gu[:, :SHARED_FF], gu[:, SHARED_FF:]
    shared = _dot((gate * jax.nn.sigmoid(gate) * up).astype(BF16), wsd_ref[...])
    xs_ref[...] = x1 + gt2_ref[...] * shared

    s = jax.nn.sigmoid(_dot_nt(wr_ref[...], hb))
    biased = s + eb_ref[...]
    ninf = -jnp.inf
    gi = lax.broadcasted_iota(jnp.int32, (GROUP_SIZE, tm), 0).astype(F32)
    gs = []
    for g in range(N_GROUPS):
        blk = biased[g * GROUP_SIZE:(g + 1) * GROUP_SIZE, :]
        m1 = jnp.max(blk, axis=0, keepdims=True)
        i1 = jnp.min(jnp.where(blk == m1, gi, float(GROUP_SIZE)), axis=0, keepdims=True)
        m2 = jnp.max(jnp.where(gi == i1, ninf, blk), axis=0, keepdims=True)
        gs.append(m1 + m2)
    gscore = jnp.concatenate(gs, axis=0)
    giota = lax.broadcasted_iota(jnp.int32, (N_GROUPS, tm), 0).astype(F32)
    gsel = jnp.zeros((N_GROUPS, tm), F32)
    for _ in range(TOPK_GROUPS):
        gm = jnp.max(gscore, axis=0, keepdims=True)
        gidx = jnp.min(jnp.where(gscore == gm, giota, float(N_GROUPS)), axis=0, keepdims=True)
        hit = giota == gidx
        gsel = jnp.where(hit, 1.0, gsel)
        gscore = jnp.where(hit, ninf, gscore)
    masked = jnp.concatenate(
        [jnp.where(gsel[g:g + 1, :] > 0.0, biased[g * GROUP_SIZE:(g + 1) * GROUP_SIZE, :], ninf)
         for g in range(N_GROUPS)], axis=0)
    eiota = lax.broadcasted_iota(jnp.int32, (N_EXPERTS, tm), 0).astype(F32)
    idxs, ws = [], []
    selall = jnp.zeros((N_EXPERTS, tm), F32)
    for _ in range(TOP_K):
        mx = jnp.max(masked, axis=0, keepdims=True)
        ei = jnp.min(jnp.where(masked == mx, eiota, float(N_EXPERTS)), axis=0, keepdims=True)
        hit = eiota == ei
        ws.append(jnp.sum(jnp.where(hit, s, 0.0), axis=0, keepdims=True))
        idxs.append(ei)
        selall = jnp.where(hit, 1.0, selall)
        masked = jnp.where(hit, ninf, masked)
    wsum = ws[0]
    for w in ws[1:]:
        wsum = wsum + w
    gw_ref[...] = jnp.concatenate(ws, axis=0) / wsum * ROUTED_SCALE
    idx_ref[...] = jnp.concatenate(idxs, axis=0).astype(jnp.int32)

    before = _dot(selall.astype(BF16), tri_ref[...]) + cnt_ref[:, 0:1]
    ranks = [jnp.sum(jnp.where(eiota == ei, before, 0.0), axis=0, keepdims=True) for ei in idxs]
    rank_ref[...] = jnp.concatenate(ranks, axis=0).astype(jnp.int32)
    cnt_ref[...] = cnt_ref[...] + jnp.sum(selall, axis=1, keepdims=True)
    cout_ref[...] = cnt_ref[...]


def _post_mix(x2d, o, z, gt1, sc2, sh2, gt2, p, cnt_in, h2_buf, *, prompt, seq_len, row_off, n_total):
    n = x2d.shape[0]
    tm = TILE_PROMPT if prompt else TILE_SAMPLE
    nt = n // tm
    full = lambda a: pl.BlockSpec(a.shape, lambda i: (0,) * a.ndim)
    row_tile = lambda w: pl.BlockSpec((tm, w), lambda i: (i, 0))
    if prompt:
        tps = seq_len // tm
        mod_spec = pl.BlockSpec((None, 1, D_MODEL), lambda i: (i // tps, 0, 0))
    else:
        mod_spec = row_tile(D_MODEL)
    tri = p["tri_p"] if prompt else p["tri_s"]
    boff = row_off // tm
    ins = [x2d, o, z, gt1, sc2, sh2, gt2, p["w_out"], p["g2"], p["wr_t"], p["e_bias"], p["ws_gu"], p["ws_d"],
           tri, cnt_in]
    in_specs = [row_tile(D_MODEL), row_tile(ATTN_WIDTH), row_tile(CONV_DIM), mod_spec, mod_spec, mod_spec,
                mod_spec, full(p["w_out"]), full(p["g2"]), full(p["wr_t"]), full(p["e_bias"]), full(p["ws_gu"]),
                full(p["ws_d"]), full(tri), full(cnt_in)]
    out_shape = [jax.ShapeDtypeStruct((n, D_MODEL), F32), jax.ShapeDtypeStruct((n_total, PACK_W), jnp.uint32),
                 jax.ShapeDtypeStruct((TOP_K, n), jnp.int32), jax.ShapeDtypeStruct((TOP_K, n), F32),
                 jax.ShapeDtypeStruct((TOP_K, n), jnp.int32), jax.ShapeDtypeStruct((N_EXPERTS, LANES), F32)]
    col_tile = pl.BlockSpec((TOP_K, tm), lambda i: (0, i))
    out_specs = [row_tile(D_MODEL), pl.BlockSpec((tm, PACK_W), lambda i: (i + boff, 0)),
                 col_tile, col_tile, col_tile, pl.BlockSpec((N_EXPERTS, LANES), lambda i: (0, 0))]
    aliases = {}
    if h2_buf is not None:
        ins.append(h2_buf)
        in_specs.append(pl.BlockSpec(memory_space=pl.ANY))
        aliases = {len(ins) - 1: 1}
        body = lambda *refs: _post_mix_body(*refs[:15], *refs[16:], tm=tm)
    else:
        body = functools.partial(_post_mix_body, tm=tm)
    return pl.pallas_call(
        body, grid=(nt,), in_specs=in_specs, out_specs=out_specs, out_shape=out_shape,
        scratch_shapes=[pltpu.VMEM((N_EXPERTS, LANES), F32)],
        input_output_aliases=aliases,
        compiler_params=_cparams(("arbitrary",)),
        name="post_mix_prompt" if prompt else "post_mix_sample",
    )(*ins)


def _dest_body(idx_ref, rank_ref, ps_ref, d_ref, *, tm):
    eiota = lax.broadcasted_iota(jnp.int32, (N_EXPERTS, tm), 0)
    ps = ps_ref[...]
    idx = idx_ref[...]
    starts = [jnp.sum(jnp.where(eiota == idx[k:k + 1, :], ps, 0.0), axis=0, keepdims=True) for k in range(TOP_K)]
    d_ref[...] = jnp.concatenate(starts, axis=0).astype(jnp.int32) + rank_ref[...]


def _dest(idx, rank, pstart):
    n = idx.shape[1]
    tm = 512
    tile = pl.BlockSpec((TOP_K, tm), lambda i: (0, i))
    return pl.pallas_call(
        functools.partial(_dest_body, tm=tm),
        grid=(n // tm,),
        in_specs=[tile, tile, pl.BlockSpec((N_EXPERTS, 1), lambda i: (0, 0))],
        out_specs=tile,
        out_shape=jax.ShapeDtypeStruct((TOP_K, n), jnp.int32),
        compiler_params=_cparams(("arbitrary",)),
        name="dest_rows",
    )(idx, rank, pstart.astype(F32).reshape(N_EXPERTS, 1))


def _experts_body(e0_ref, rs_ref, sz_ref, nv_ref, nx_ref, x_ref, wg_ref, wu_ref, wd_ref, y_ref,
                  xbuf, ybuf, sg, su, sd, wgub, wdb, sem, sem_x, sem_y, *, ch, gran):
    c = pl.program_id(0)
    slot = c % 2
    classes = list(range(gran, ch + 1, gran))

    def x_copy(cc, sl, rows):
        start = pl.multiple_of(rs_ref[cc], gran)
        return pltpu.make_async_copy(x_ref.at[pl.ds(start, rows)], xbuf.at[sl, pl.ds(0, rows)], sem_x.at[sl])

    def y_copy(cc, sl, rows):
        start = pl.multiple_of(rs_ref[cc], gran)
        return pltpu.make_async_copy(ybuf.at[sl, pl.ds(0, rows)], y_ref.at[pl.ds(start, rows)], sem_y.at[sl])

    def for_class(cc, fn):
        for r in classes:
            pl.when(sz_ref[cc] == r // gran)(functools.partial(fn, r))

    def copies(e):
        return (pltpu.make_async_copy(wg_ref.at[e], sg, sem.at[0]),
                pltpu.make_async_copy(wu_ref.at[e], su, sem.at[1]),
                pltpu.make_async_copy(wd_ref.at[e], sd, sem.at[2]))

    @pl.when(c == 0)
    def _():
        for cp in copies(e0_ref[0]):
            cp.start()
        for_class(0, lambda r: x_copy(0, 0, r).start())

    for_class(c + 1, lambda r: x_copy(c + 1, 1 - slot, r).start())

    nxt = nx_ref[c]

    @pl.when(nxt >= -1)
    def _():
        for cp in copies(0):
            cp.wait()
        wgub[:, :E_FF] = sg[...].astype(BF16)
        wgub[:, E_FF:] = su[...].astype(BF16)
        wdb[...] = sd[...].astype(BF16)

        @pl.when(nxt >= 0)
        def _():
            for cp in copies(nxt):
                cp.start()

    @pl.when(c >= 2)
    def _():
        for_class(c - 2, lambda r: y_copy(c - 2, slot, r).wait())

    nv = nv_ref[c]

    def run(rows):
        x_copy(c, slot, rows).wait()
        row = lax.broadcasted_iota(jnp.int32, (rows, 1), 0)
        w = jnp.where(row < nv, xbuf[slot, :rows, :], jnp.uint32(0))
        lo, hi = _unpack_pair(w)
        lo, hi = lo.astype(BF16), hi.astype(BF16)
        gu = _dot(lo, wgub[:PACK_W, :]) + _dot(hi, wgub[PACK_W:, :])
        gate, up = gu[:, :E_FF], gu[:, E_FF:]
        act = (gate * jax.nn.sigmoid(gate) * up).astype(BF16)
        ybuf[slot, :rows, :] = _pack_pair(_dot(act, wdb[...]).astype(BF16))
        y_copy(c, slot, rows).start()

    for_class(c, run)


def _experts(e_first, row_start, size_class, n_valid, next_e, x_sorted, w_e_gate, w_e_up, w_e_down):
    ch, gran = ROUTE_CHUNK, ROW_GRAN
    steps = row_start.shape[0] - 1
    any_spec = pl.BlockSpec(memory_space=pl.ANY)
    grid_spec = pltpu.PrefetchScalarGridSpec(
        num_scalar_prefetch=5,
        grid=(steps,),
        in_specs=[any_spec, any_spec, any_spec, any_spec],
        out_specs=any_spec,
        scratch_shapes=[pltpu.VMEM((2, ch, PACK_W), jnp.uint32), pltpu.VMEM((2, ch, PACK_W), jnp.uint32),
                        pltpu.VMEM((D_MODEL, E_FF), F32), pltpu.VMEM((D_MODEL, E_FF), F32),
                        pltpu.VMEM((E_FF, D_MODEL), F32),
                        pltpu.VMEM((D_MODEL, 2 * E_FF), BF16),
                        pltpu.VMEM((E_FF, D_MODEL), BF16), pltpu.SemaphoreType.DMA((3,)),
                        pltpu.SemaphoreType.DMA((2,)), pltpu.SemaphoreType.DMA((2,))],
    )
    return pl.pallas_call(
        functools.partial(_experts_body, ch=ch, gran=gran),
        grid_spec=grid_spec,
        out_shape=jax.ShapeDtypeStruct(x_sorted.shape, jnp.uint32),
        compiler_params=_cparams(("arbitrary",)),
        name="experts",
    )(e_first, row_start, size_class, n_valid, next_e, x_sorted, w_e_gate, w_e_up, w_e_down)


SC_CORES = 2
SC_SUBCORES = 16
SC_WORKERS = SC_CORES * SC_SUBCORES
SCATTER_WIN = 24
GATHER_WIN = 64


def _sc_mesh():
    return plsc.VectorSubcoreMesh(core_axis_name="c", subcore_axis_name="s")


def _sc_worker():
    return lax.axis_index("s") * SC_CORES + lax.axis_index("c")


def _sc_scatter_rows(rows, idx, n_out):
    n, width = rows.shape
    nw, nwin_k, win = idx.shape
    n_win = nwin_k // TOP_K
    per_w = n // nw

    assert n_win % 2 == 0

    def body(rows_hbm, idx_hbm, out_hbm, idx_v, buf, sem_in, sem_out):
        wid = _sc_worker()
        base = wid * per_w
        pltpu.sync_copy(idx_hbm.at[wid], idx_v)

        def load(j, b):
            return pltpu.make_async_copy(rows_hbm.at[pl.ds(base + j * win, win)], buf.at[b], sem_in.at[b])

        def scatters(j, b):
            return [pltpu.make_async_copy(buf.at[b], out_hbm.at[idx_v.at[j * TOP_K + k]], sem_out.at[b])
                    for k in range(TOP_K)]

        load(0, 0).start()

        @pl.loop(0, n_win, step=2)
        def _(j):
            for b in range(2):
                jj = j + b
                load(jj, b).wait()
                for cp in scatters(jj, b):
                    cp.start()

                @pl.when(jj >= 1)
                def _():
                    for cp in scatters(jj - 1, 1 - b):
                        cp.wait()

                @pl.when(jj + 1 < n_win)
                def _():
                    load(jj + 1, 1 - b).start()

        for cp in scatters(n_win - 1, 1):
            cp.wait()

    return pl.kernel(
        body, out_type=jax.ShapeDtypeStruct((n_out, width), rows.dtype), mesh=_sc_mesh(),
        scratch_types=[pltpu.VMEM((nwin_k, win), jnp.int32), pltpu.VMEM((2, win, width), rows.dtype),
                       pltpu.SemaphoreType.DMA((2,)), pltpu.SemaphoreType.DMA((2,))],
        name="sc_scatter_rows",
    )(rows, idx)


def _sc_gather_rows(table, idx):
    nw, n_win, win = idx.shape
    width = table.shape[1]
    per_w = n_win * win

    assert n_win % 2 == 0

    def body(table_hbm, idx_hbm, out_hbm, idx_v, buf, sem):
        wid = _sc_worker()
        base = wid * per_w
        pltpu.sync_copy(idx_hbm.at[wid], idx_v)

        def gather(j, b):
            return pltpu.make_async_copy(table_hbm.at[idx_v.at[j]], buf.at[b], sem.at[b])

        gather(0, 0).start()

        @pl.loop(0, n_win, step=2)
        def _(j):
            for b in range(2):
                @pl.when(j + b + 1 < n_win)
                def _():
                    gather(j + b + 1, 1 - b).start()
                gather(j + b, b).wait()
                pltpu.sync_copy(buf.at[b], out_hbm.at[pl.ds(base + (j + b) * win, win)])

    return pl.kernel(
        body, out_type=jax.ShapeDtypeStruct((nw * per_w, width), table.dtype), mesh=_sc_mesh(),
        scratch_types=[pltpu.VMEM((n_win, win), jnp.int32), pltpu.VMEM((2, win, width), table.dtype),
                       pltpu.SemaphoreType.DMA((2,))],
        name="sc_gather_rows",
    )(table, idx)


def _finish_body(y_ref, xs_ref, gw_ref, gt2_ref, gf_ref, fsc_ref, fsh_ref, out_ref, *, tm):
    gw = gw_ref[...]
    f_lo = jnp.zeros((tm, PACK_W), F32)
    f_hi = jnp.zeros((tm, PACK_W), F32)
    for k in range(TOP_K):
        lo, hi = _unpack_pair(y_ref[k * tm:(k + 1) * tm, :])
        f_lo = f_lo + gw[:, k:k + 1] * lo
        f_hi = f_hi + gw[:, k:k + 1] * hi
    f = jnp.concatenate([f_lo, f_hi], axis=1)
    x2 = xs_ref[...] + gt2_ref[...] * f
    out_ref[...] = _rms(x2, gf_ref[...]) * (1.0 + fsc_ref[...]) + fsh_ref[...]


def _finish(y_part, xs, gw, gt2, g_final, fsc, fsh, *, prompt, seq_len, tile_lo, n_tiles, y_tile_off, out_buf=None):
    n = xs.shape[0]
    tm = COMBINE_TILE
    row_tile = lambda w: pl.BlockSpec((tm, w), lambda i: (i + tile_lo, 0))
    if prompt:
        tps = seq_len // tm
        mod_spec = pl.BlockSpec((None, 1, D_MODEL), lambda i: ((i + tile_lo) // tps, 0, 0))
    else:
        mod_spec = row_tile(D_MODEL)
    ins = [y_part, xs, gw, gt2, g_final, fsc, fsh]
    in_specs = [pl.BlockSpec((TOP_K * tm, PACK_W), lambda i: (i + y_tile_off, 0)),
                row_tile(D_MODEL), row_tile(TOP_K), mod_spec,
                pl.BlockSpec((1, D_MODEL), lambda i: (0, 0)), mod_spec, mod_spec]
    aliases = {}
    body = functools.partial(_finish_body, tm=tm)
    if out_buf is not None:
        ins.append(out_buf)
        in_specs.append(pl.BlockSpec(memory_space=pl.ANY))
        aliases = {len(ins) - 1: 0}
        body = lambda *refs: _finish_body(*refs[:7], *refs[8:], tm=tm)
    return pl.pallas_call(
        body,
        grid=(n_tiles,),
        in_specs=in_specs,
        out_specs=row_tile(D_MODEL),
        out_shape=jax.ShapeDtypeStruct((n, D_MODEL), F32),
        input_output_aliases=aliases,
        compiler_params=_cparams(("arbitrary",)),
        name="finish_prompt" if prompt else "finish_sample",
    )(*ins)


def _rope_tables(pos):
    inv = 1.0 / (ROPE_THETA ** (jnp.arange(PE_HALF, dtype=F32) / PE_HALF))
    ang = pos.astype(F32)[:, None] * inv[None, :]
    cos, sin = jnp.cos(ang), jnp.sin(ang)
    n = pos.shape[0]
    ones = jnp.ones((n, PE_LO), F32)
    zeros = jnp.zeros((n, PE_LO), F32)
    zh = jnp.zeros((n, PE_HALF), F32)
    tail1 = jnp.ones((n, LANES - PE_LO - ROPE_DIM), F32)
    tail0 = jnp.zeros((n, LANES - PE_LO - ROPE_DIM), F32)
    cos_t = jnp.concatenate([ones, cos, cos, tail1], axis=1)
    sin_up = jnp.concatenate([zeros, zh, sin, tail0], axis=1)
    sin_dn = jnp.concatenate([zeros, -sin, zh, tail0], axis=1)
    return cos_t, sin_up, sin_dn


def _pad_heads(w, width):
    pad = [(0, 0)] * (w.ndim - 1) + [(0, HEAD_PAD - width)]
    w = jnp.pad(w, pad)
    return w.reshape(w.shape[:-2] + (N_HEADS * HEAD_PAD,))


def _block_diag_ones(n, blk):
    r = jnp.arange(n) // blk
    return (r[:, None] == r[None, :]).astype(BF16)


def _prep_weights(w_in, g_attn_norm, g_q, w_q_up, g_kv, w_uk, w_uv, w_conv, g_attn_out, g_conv_out, w_out,
                  g_ffn_norm, w_router, e_bias, w_s_gate, w_s_up, w_s_down):
    o0 = Q_LORA
    o1 = o0 + KV_LORA
    o2 = o1 + ROPE_DIM
    kpe_cols = jnp.pad(w_in[:, o1:o2], ((0, 0), (PE_LO, HEAD_PAD - PE_LO - ROPE_DIM)))
    w_in_p = jnp.concatenate([w_in[:, :o1], kpe_cols, w_in[:, o2:]], axis=1).astype(BF16)
    wq = _pad_heads(w_q_up.reshape(Q_LORA, N_HEADS, NOPE_DIM + ROPE_DIM), NOPE_DIM + ROPE_DIM).astype(BF16)
    wuk = _pad_heads(w_uk, NOPE_DIM).astype(BF16)
    wuv = _pad_heads(w_uv, V_DIM).astype(BF16)
    wuk_t = jnp.pad(jnp.transpose(w_uk, (1, 2, 0)), ((0, 0), (0, HEAD_PAD - NOPE_DIM), (0, 0))).astype(BF16)
    wuv_h = jnp.pad(jnp.transpose(w_uv, (1, 0, 2)), ((0, 0), (0, 0), (0, HEAD_PAD - V_DIM))).astype(BF16)
    g_attn_h = jnp.pad(g_attn_out.reshape(N_HEADS, 1, V_DIM), ((0, 0), (0, 0), (0, HEAD_PAD - V_DIM)))
    tri = lambda t: (jnp.arange(t)[:, None] < jnp.arange(t)[None, :]).astype(BF16)
    return {
        "g1": g_attn_norm.reshape(1, -1), "w_in": w_in_p, "g_q": g_q.reshape(1, -1), "wq": wq,
        "g_kv": g_kv.reshape(1, -1), "wuk": wuk, "wuv": wuv, "wuk_t": wuk_t, "wuv_h": wuv_h,
        "w_conv": w_conv, "g_conv": g_conv_out.reshape(1, -1), "gmat": _block_diag_ones(CONV_DIM, CONV_GROUP_DIM),
        "g_attn": g_attn_out.reshape(1, -1), "g_attn_h": g_attn_h, "gm_head": _block_diag_ones(LANES, V_DIM),
        "w_out": w_out.astype(BF16), "g2": g_ffn_norm.reshape(1, -1), "wr_t": w_router.T.astype(BF16),
        "e_bias": e_bias.reshape(-1, 1), "ws_gu": jnp.concatenate([w_s_gate, w_s_up], axis=1).astype(BF16),
        "ws_d": w_s_down.astype(BF16), "tri_p": tri(TILE_PROMPT), "tri_s": tri(TILE_SAMPLE),
    }


def kernel(x_prompt, x_sample, c_prompt, c_sample, cache_ckv, cache_kpe, state_conv, page_table, w_ada, b_ada, g_attn_norm, w_in, g_q, w_q_up, g_kv, w_uk, w_uv, w_conv, g_attn_out, g_conv_out, w_out, g_ffn_norm, w_router, e_bias, w_e_gate, w_e_up, w_e_down, w_s_gate, w_s_up, w_s_down, w_ada_final, b_ada_final, g_final):
    assert w_ada.shape[0] == 1, "one layer"
    bsz, seq, d = x_prompt.shape
    nseq, t_new, _ = x_sample.shape
    n_p, n_s = bsz * seq, nseq * t_new
    n_tot = n_p + n_s
    past = page_table.shape[1] * PAGE_SIZE

    p = _prep_weights(w_in[0], g_attn_norm[0], g_q[0], w_q_up[0], g_kv[0], w_uk[0], w_uv[0], w_conv[0],
                      g_attn_out[0], g_conv_out[0], w_out[0], g_ffn_norm[0], w_router[0], e_bias[0],
                      w_s_gate[0], w_s_up[0], w_s_down[0])

    c_all = jnp.concatenate([c_prompt, c_sample], axis=0)
    mod = _ada(c_all, w_ada[0], b_ada[0])
    modf = _ada(c_all, w_ada_final, b_ada_final)
    sh1, sc1, gt1, sh2, sc2, gt2 = [mod[:, j * d:(j + 1) * d] for j in range(6)]
    fsh, fsc = modf[:, :d], modf[:, d:]
    per_batch = lambda a: a[:bsz].reshape(bsz, 1, d)
    per_token = lambda a: jnp.repeat(a[bsz:], t_new, axis=0)

    xp = x_prompt.reshape(n_p, d)
    xs_in = x_sample.reshape(n_s, d)
    tabs_p = _rope_tables(jnp.arange(seq, dtype=jnp.int32))
    tabs_s = tuple(jnp.tile(t, (nseq, 1)) for t in _rope_tables(past + jnp.arange(t_new, dtype=jnp.int32)))

    q_p, k_p, v_p, ckv_p, kpe_p, z_p, conv_p = _mix_in(
        xp, per_batch(sc1), per_batch(sh1), p, tabs_p, prompt=True, seq_len=seq)
    o_p = _flash(q_p, k_p, v_p, p["g_attn"], p["gm_head"], bsz, seq)

    prev = state_conv[0]
    prev_a = jnp.repeat(prev[:, 1, :], t_new, axis=0)
    prev_b = jnp.repeat(prev[:, 0, :], t_new, axis=0)
    q_s, ckv_s, kpe_s, z_s, u_s = _mix_in(
        xs_in, per_token(sc1), per_token(sh1), p, tabs_s, prompt=False, prev_a=prev_a, prev_b=prev_b,
        t_new=t_new)
    ql_h, qp_h = _absorb(q_s, p["wuk_t"])
    rows = N_HEADS * t_new
    to_seq = lambda a: jnp.transpose(a.reshape(N_HEADS, nseq, t_new, a.shape[-1]), (1, 0, 2, 3)).reshape(
        nseq, rows, a.shape[-1])
    o_lat = _paged(page_table, to_seq(ql_h), to_seq(qp_h), ckv_s.reshape(nseq, t_new, KV_LORA),
                   kpe_s.reshape(nseq, t_new, ROPE_DIM), cache_ckv, jnp.swapaxes(cache_kpe, 2, 3))
    o_lat_h = jnp.transpose(o_lat.reshape(nseq, N_HEADS, t_new, KV_LORA), (1, 0, 2, 3)).reshape(
        N_HEADS, n_s, KV_LORA)
    o_s_h = _unabsorb(o_lat_h, p["wuv_h"], p["g_attn_h"])
    o_s = jnp.transpose(o_s_h[:, :, :V_DIM], (1, 0, 2)).reshape(n_s, ATTN_WIDTH)

    cnt0 = jnp.zeros((N_EXPERTS, LANES), F32)
    xs_p, h2, idx_p, gw_p, rank_p, cnt_p = _post_mix(
        xp, o_p, z_p, per_batch(gt1), per_batch(sc2), per_batch(sh2), per_batch(gt2), p, cnt0, None,
        prompt=True, seq_len=seq, row_off=0, n_total=n_tot)
    xs_s, h2, idx_s, gw_s, rank_s, cnt_s = _post_mix(
        xs_in, o_s, z_s, per_token(gt1), per_token(sc2), per_token(sh2), per_token(gt2), p, cnt_p, h2,
        prompt=False, seq_len=None, row_off=n_p, n_total=n_tot)

    ch, gran = ROUTE_CHUNK, ROW_GRAN
    counts = cnt_s[:, 0].astype(jnp.int32)
    padded = (counts + gran - 1) // gran * gran
    pend = jnp.cumsum(padded)
    pstart = pend - padded
    idx_all = jnp.concatenate([idx_p, idx_s], axis=1)
    rank_all = jnp.concatenate([rank_p, rank_s], axis=1)
    eids = jnp.arange(N_EXPERTS, dtype=jnp.int32)
    lookup = lambda table, keys: jnp.sum(jnp.where(keys[..., None] == eids, table, 0), axis=-1)
    dest = _dest(idx_all, rank_all, pstart)
    n_rows = n_tot * TOP_K + N_EXPERTS * gran
    nck = (padded + ch - 1) // ch
    cend = jnp.cumsum(nck)
    cstart = cend - nck
    n_used = cend[-1]
    n_meta = -(-(n_tot * TOP_K) // ch) + N_EXPERTS + 3
    cidx = jnp.arange(n_meta, dtype=jnp.int32)
    chunk_e = jnp.minimum(jnp.sum((cend[None, :] <= cidx[:, None]).astype(jnp.int32), axis=1), N_EXPERTS - 1)
    used = cidx < n_used
    off = (cidx - lookup(cstart, chunk_e)) * ch
    row_start = jnp.where(used, lookup(pstart, chunk_e) + off, 0).astype(jnp.int32)
    size_class = jnp.where(used, jnp.minimum(lookup(padded, chunk_e) - off, ch) // gran, 0).astype(jnp.int32)
    n_valid = jnp.where(used, jnp.clip(lookup(counts, chunk_e) - off, 0, ch), 0).astype(jnp.int32)
    run_end = lookup(cend, chunk_e)
    after = jnp.sum(jnp.where(run_end[:, None] == cidx[None, :], chunk_e[None, :], 0), axis=1)
    is_first = used & (off == 0)
    next_e = jnp.where(is_first, jnp.where(run_end < n_used, after, -1), -2).astype(jnp.int32)

    tmc = COMBINE_TILE

    def dest_tiles(dst):
        n = dst.shape[1]
        return jnp.transpose(dst.reshape(TOP_K, n // tmc, tmc), (1, 0, 2)).reshape(n // tmc, 1, TOP_K * tmc)

    per_w = n_tot // SC_WORKERS
    idx_sc = jnp.transpose(dest.reshape(TOP_K, SC_WORKERS, per_w // SCATTER_WIN, SCATTER_WIN),
                           (1, 2, 0, 3)).reshape(SC_WORKERS, (per_w // SCATTER_WIN) * TOP_K, SCATTER_WIN)
    x_sorted = _sc_scatter_rows(h2, idx_sc, n_rows)
    y_sorted = _experts(chunk_e[:1], row_start, size_class, n_valid, next_e, x_sorted,
                        w_e_gate[0], w_e_up[0], w_e_down[0])

    idx_tiles = dest_tiles(dest)
    tiles_p = n_p // tmc
    tiles_a = tiles_p // 2
    sc_idx = lambda part: part.reshape(SC_WORKERS, -1, GATHER_WIN)
    y_a = _sc_gather_rows(y_sorted, sc_idx(idx_tiles[:tiles_a]))
    y_b = _sc_gather_rows(y_sorted, sc_idx(idx_tiles[tiles_a:]))

    gfin = g_final.reshape(1, d)
    fin_p = functools.partial(_finish, xs=xs_p, gw=gw_p.T, gt2=per_batch(gt2), g_final=gfin, fsc=per_batch(fsc),
                              fsh=per_batch(fsh), prompt=True, seq_len=seq)
    y_p = fin_p(y_a, tile_lo=0, n_tiles=tiles_a, y_tile_off=0)
    y_p = fin_p(y_b, tile_lo=tiles_a, n_tiles=tiles_p - tiles_a, y_tile_off=0, out_buf=y_p)
    y_s = _finish(y_b, xs_s, gw_s.T, per_token(gt2), gfin, per_token(fsc), per_token(fsh), prompt=False,
                  seq_len=None, tile_lo=0, n_tiles=n_s // tmc, y_tile_off=tiles_p - tiles_a)

    return (y_p.reshape(bsz, seq, d), y_s.reshape(nseq, t_new, d),
            ckv_p.reshape(1, bsz, seq, KV_LORA), jnp.swapaxes(kpe_p, 1, 2)[None],
            conv_p.reshape(1, bsz, CONV_W - 1, CONV_DIM),
            ckv_s.reshape(1, nseq, t_new, KV_LORA), kpe_s.reshape(1, nseq, t_new, ROPE_DIM),
            u_s.reshape(nseq, t_new, CONV_DIM)[:, t_new - (CONV_W - 1):, :].reshape(1, nseq, CONV_W - 1, CONV_DIM))
```

```python
import functools
import math

import jax
import jax.numpy as jnp
import numpy as np
from jax import lax
from jax.experimental import pallas as pl
from jax.experimental.pallas import tpu as pltpu
from jax.experimental.pallas import tpu_sc as plsc

F32 = jnp.float32
BF16 = jnp.bfloat16

D_MODEL = 1024
N_HEADS = 8
NOPE_DIM = 64
ROPE_DIM = 32
V_DIM = 64
Q_LORA = 384
KV_LORA = 256
ATTN_WIDTH = N_HEADS * V_DIM
CONV_DIM = 512
CONV_GROUPS = 8
CONV_GROUP_DIM = CONV_DIM // CONV_GROUPS
CONV_W = 3
ROPE_THETA = 10000.0
PAGE_SIZE = 128
N_EXPERTS = 256
TOP_K = 8
N_GROUPS = 8
GROUP_SIZE = N_EXPERTS // N_GROUPS
TOPK_GROUPS = 4
E_FF = 256
SHARED_FF = 256
ROUTED_SCALE = 2.5
EPS = 1e-6

LANES = 128
HEAD_PAD = LANES
QK_WIDTH = N_HEADS * HEAD_PAD
PE_LO = NOPE_DIM
PE_HALF = ROPE_DIM // 2
IN_PAD_COLS = Q_LORA + KV_LORA + HEAD_PAD + 3 * CONV_DIM
VMEM_LIMIT = 48 * 1024 * 1024
Q_SCALE = (NOPE_DIM + ROPE_DIM) ** -0.5 * math.log2(math.e)
PACK_W = D_MODEL // 2

TILE_PROMPT = 512
TILE_SAMPLE = 256
TQ = 512
TK = 256
STRIP = 128
ROUTE_CHUNK = 768
ROW_GRAN = 128
COMBINE_TILE = 256


def _cparams(sem, vmem=VMEM_LIMIT):
    return pltpu.CompilerParams(dimension_semantics=sem, vmem_limit_bytes=vmem)


def _dot(a, b):
    return jnp.dot(a, b, preferred_element_type=F32)


def _dot_nt(a, b):
    return lax.dot_general(a, b, (((1,), (1,)), ((), ())), preferred_element_type=F32)


def _rms(x, g):
    r = lax.rsqrt(jnp.mean(x * x, axis=-1, keepdims=True) + EPS)
    return (x * r) * g


def _pack_pair(xb):
    lo = lax.bitcast_convert_type(xb[:, :PACK_W].astype(F32), jnp.uint32) >> 16
    hi = lax.bitcast_convert_type(xb[:, PACK_W:].astype(F32), jnp.uint32) & jnp.uint32(0xFFFF0000)
    return lo | hi


def _unpack_pair(w):
    lo = lax.bitcast_convert_type(w << 16, F32)
    hi = lax.bitcast_convert_type(w & jnp.uint32(0xFFFF0000), F32)
    return lo, hi


def _ada_body(c_ref, w_ref, b_ref, o_ref):
    c = c_ref[...]
    a = (c * jax.nn.sigmoid(c)).astype(BF16)
    o_ref[...] = _dot(a, w_ref[...].astype(BF16)) + b_ref[...]


def _ada(c_all, w, b):
    m, d = c_all.shape
    n = w.shape[1]
    tn = 1024
    return pl.pallas_call(
        _ada_body,
        grid=(n // tn,),
        in_specs=[pl.BlockSpec((m, d), lambda j: (0, 0)),
                  pl.BlockSpec((d, tn), lambda j: (0, j)),
                  pl.BlockSpec((1, tn), lambda j: (0, j))],
        out_specs=pl.BlockSpec((m, tn), lambda j: (0, j)),
        out_shape=jax.ShapeDtypeStruct((m, n), F32),
        compiler_params=_cparams(("arbitrary",)),
        name="ada",
    )(c_all, w, b.reshape(1, n))


def _rope_lanes(x, cos, sin_up, sin_dn):
    w = x.shape[1]
    up = pltpu.roll(x, PE_HALF, 1)
    dn = pltpu.roll(x, w - PE_HALF, 1)
    return x * cos + up * sin_up + dn * sin_dn


def _mix_in_body(*refs, prompt, tm, tiles_per_seq, t_new):
    if prompt:
        (x_ref, sc_ref, sh_ref, g1_ref, win_ref, gq_ref, wq_ref, gkv_ref, wuk_ref, wuv_ref,
         cos_ref, sup_ref, sdn_ref, wconv_ref, gconv_ref, gmat_ref,
         q_ref, k_ref, v_ref, ckv_ref, kpe_ref, z_ref, cst_ref, carry_ref) = refs
    else:
        (x_ref, sc_ref, sh_ref, g1_ref, win_ref, gq_ref, wq_ref, gkv_ref,
         cos_ref, sup_ref, sdn_ref, wconv_ref, gconv_ref, gmat_ref, pa_ref, pb_ref,
         q_ref, ckv_ref, kpe_ref, z_ref, u_ref) = refs

    x = x_ref[...]
    h = _rms(x, g1_ref[...]) * (1.0 + sc_ref[...]) + sh_ref[...]
    proj = _dot(h.astype(BF16), win_ref[...])
    o0 = Q_LORA
    o1 = o0 + KV_LORA
    o2 = o1 + HEAD_PAD
    o3 = o2 + CONV_DIM
    o4 = o3 + CONV_DIM
    q_a, kv_a, kpe_blk = proj[:, :o0], proj[:, o0:o1], proj[:, o1:o2]
    b_g, c_g, u_in = proj[:, o2:o3], proj[:, o3:o4], proj[:, o4:]

    cos, sup, sdn = cos_ref[...], sup_ref[...], sdn_ref[...]
    cos8 = jnp.concatenate([cos] * N_HEADS, axis=1)
    sup8 = jnp.concatenate([sup] * N_HEADS, axis=1)
    sdn8 = jnp.concatenate([sdn] * N_HEADS, axis=1)

    qn = _rms(q_a, gq_ref[...]).astype(BF16)
    q = _dot(qn, wq_ref[...]) * Q_SCALE
    q_ref[...] = _rope_lanes(q, cos8, sup8, sdn8).astype(BF16)

    ckv = _rms(kv_a, gkv_ref[...])
    ckv_ref[...] = ckv
    kpe = _rope_lanes(kpe_blk, cos, sup, sdn)
    if prompt:
        kpe_ref[...] = kpe.T[PE_LO:PE_LO + ROPE_DIM, :]
    else:
        kpe_ref[...] = kpe[:, PE_LO:PE_LO + ROPE_DIM]

    if prompt:
        ckvb = ckv.astype(BF16)
        k = _dot(ckvb, wuk_ref[...]) + jnp.concatenate([kpe] * N_HEADS, axis=1)
        k_ref[...] = k.astype(BF16)
        lane = lax.broadcasted_iota(jnp.int32, (1, QK_WIDTH), 1)
        ones_hi = jnp.where(lane % HEAD_PAD >= V_DIM, 1.0, 0.0)
        v_ref[...] = (_dot(ckvb, wuv_ref[...]) + ones_hi).astype(BF16)

    u = c_g * u_in
    row = lax.broadcasted_iota(jnp.int32, (tm, 1), 0)
    r1 = pltpu.roll(u, 1, 0)
    r2 = pltpu.roll(u, 2, 0)
    if prompt:
        @pl.when(pl.program_id(0) % tiles_per_seq == 0)
        def _():
            carry_ref[...] = jnp.zeros_like(carry_ref)
        c6 = carry_ref[6:7, :]
        c7 = carry_ref[7:8, :]
        um1 = jnp.where(row == 0, c7, r1)
        um2 = jnp.where(row == 0, c6, jnp.where(row == 1, c7, r2))
        carry_ref[...] = u[tm - 8:, :]
        cst_ref[...] = u[tm - (CONV_W - 1):, :]
    else:
        t = row % t_new
        um1 = jnp.where(t == 0, pa_ref[...], r1)
        um2 = jnp.where(t == 0, pb_ref[...], jnp.where(t == 1, pa_ref[...], r2))
        u_ref[...] = u
    wc = wconv_ref[...]
    y = um2 * wc[0:1, :] + um1 * wc[1:2, :] + u * wc[2:3, :]
    zz = b_g * y
    ms = _dot((zz * zz).astype(BF16), gmat_ref[...]) * (1.0 / CONV_GROUP_DIM)
    z_ref[...] = (zz * lax.rsqrt(ms + EPS) * gconv_ref[...]).astype(BF16)


def _mix_in(x2d, sc, sh, p, rope_tabs, *, prompt, seq_len=None, prev_a=None, prev_b=None, t_new=1):
    n = x2d.shape[0]
    tm = TILE_PROMPT if prompt else TILE_SAMPLE
    nt = n // tm
    cos_t, sup_t, sdn_t = rope_tabs
    full = lambda a: pl.BlockSpec(a.shape, lambda i: (0,) * a.ndim)
    row_tile = lambda w: pl.BlockSpec((tm, w), lambda i: (i, 0))
    if prompt:
        tps = seq_len // tm
        mod_spec = pl.BlockSpec((None, 1, D_MODEL), lambda i: (i // tps, 0, 0))
        tab_spec = pl.BlockSpec((tm, LANES), lambda i: (i % tps, 0))
        nb = n // seq_len
        ins = [x2d, sc, sh, p["g1"], p["w_in"], p["g_q"], p["wq"], p["g_kv"], p["wuk"], p["wuv"],
               cos_t, sup_t, sdn_t, p["w_conv"], p["g_conv"], p["gmat"]]
        in_specs = [row_tile(D_MODEL), mod_spec, mod_spec, full(p["g1"]), full(p["w_in"]), full(p["g_q"]),
                    full(p["wq"]), full(p["g_kv"]), full(p["wuk"]), full(p["wuv"]),
                    tab_spec, tab_spec, tab_spec, full(p["w_conv"]), full(p["g_conv"]), full(p["gmat"])]
        out_shape = [jax.ShapeDtypeStruct((n, QK_WIDTH), BF16), jax.ShapeDtypeStruct((n, QK_WIDTH), BF16),
                     jax.ShapeDtypeStruct((n, QK_WIDTH), BF16), jax.ShapeDtypeStruct((n, KV_LORA), F32),
                     jax.ShapeDtypeStruct((nb, ROPE_DIM, seq_len), F32), jax.ShapeDtypeStruct((n, CONV_DIM), BF16),
                     jax.ShapeDtypeStruct((nb, CONV_W - 1, CONV_DIM), F32)]
        out_specs = [row_tile(QK_WIDTH), row_tile(QK_WIDTH), row_tile(QK_WIDTH), row_tile(KV_LORA),
                     pl.BlockSpec((None, ROPE_DIM, tm), lambda i: (i // tps, 0, i % tps)), row_tile(CONV_DIM),
                     pl.BlockSpec((None, CONV_W - 1, CONV_DIM), lambda i: (i // tps, 0, 0))]
        scratch = [pltpu.VMEM((8, CONV_DIM), F32)]
    else:
        tps = 1
        ins = [x2d, sc, sh, p["g1"], p["w_in"], p["g_q"], p["wq"], p["g_kv"],
               cos_t, sup_t, sdn_t, p["w_conv"], p["g_conv"], p["gmat"], prev_a, prev_b]
        in_specs = [row_tile(D_MODEL), row_tile(D_MODEL), row_tile(D_MODEL), full(p["g1"]), full(p["w_in"]),
                    full(p["g_q"]), full(p["wq"]), full(p["g_kv"]),
                    row_tile(LANES), row_tile(LANES), row_tile(LANES), full(p["w_conv"]), full(p["g_conv"]),
                    full(p["gmat"]), row_tile(CONV_DIM), row_tile(CONV_DIM)]
        out_shape = [jax.ShapeDtypeStruct((n, QK_WIDTH), BF16), jax.ShapeDtypeStruct((n, KV_LORA), F32),
                     jax.ShapeDtypeStruct((n, ROPE_DIM), F32), jax.ShapeDtypeStruct((n, CONV_DIM), BF16),
                     jax.ShapeDtypeStruct((n, CONV_DIM), F32)]
        out_specs = [row_tile(QK_WIDTH), row_tile(KV_LORA), row_tile(ROPE_DIM), row_tile(CONV_DIM),
                     row_tile(CONV_DIM)]
        scratch = []
    return pl.pallas_call(
        functools.partial(_mix_in_body, prompt=prompt, tm=tm, tiles_per_seq=tps, t_new=t_new),
        grid=(nt,), in_specs=in_specs, out_specs=out_specs, out_shape=out_shape,
        scratch_shapes=scratch, compiler_params=_cparams(("arbitrary",)),
        name="mix_in_prompt" if prompt else "mix_in_sample",
    )(*ins)


def _flash_body(q_ref, k_ref, v_ref, g_ref, gm_ref, o_ref, m_sc, acc_sc, s_sc, p_sc, *, tq, tk):
    assert tq % tk == 0
    i = pl.program_id(2)
    m_sc[...] = jnp.full(m_sc.shape, -jnp.inf, F32)
    acc_sc[...] = jnp.zeros(acc_sc.shape, F32)
    row_g = i * tq + lax.broadcasted_iota(jnp.int32, (STRIP, tk), 0)
    col_l = lax.broadcasted_iota(jnp.int32, (STRIP, tk), 1)
    heads = [slice(hh * HEAD_PAD, (hh + 1) * HEAD_PAD) for hh in range(2)]

    def logits(t, slot, row_lo=0):
        ks = pl.multiple_of(t * tk, tk)
        for hh in range(2):
            s_sc[slot, hh, row_lo:, :] = _dot_nt(q_ref[row_lo:, heads[hh]],
                                                 k_ref[pl.ds(ks, tk), heads[hh]])

    def consume(t, slot, masked, row_lo=0):
        ks = pl.multiple_of(t * tk, tk)
        for hh in range(2):
            for r0 in range(row_lo, tq, STRIP):
                rows = slice(r0, r0 + STRIP)
                s = s_sc[slot, hh, rows, :]
                if masked:
                    s = jnp.where(ks + col_l <= row_g + r0, s, -jnp.inf)
                m_prev = m_sc[hh, rows, :]
                m_new = jnp.maximum(m_prev, jnp.max(s, axis=1, keepdims=True))
                alpha = jnp.exp2(m_prev - m_new)
                pr = jnp.exp2(s - jnp.concatenate([m_new] * (tk // LANES), axis=1))
                p_sc[hh, rows, :] = pr.astype(BF16)
                acc_sc[hh, rows, :] = alpha * acc_sc[hh, rows, :]
                m_sc[hh, rows, :] = m_new
            acc_sc[hh, row_lo:, :] = acc_sc[hh, row_lo:, :] + _dot(p_sc[hh, row_lo:, :],
                                                                   v_ref[pl.ds(ks, tk), heads[hh]])

    n_diag = tq // tk
    n_full = i * n_diag
    logits(0, 0)
    if n_diag % 2 == 0:
        def tile_pair(p, _):
            logits(2 * p + 1, 1)
            consume(2 * p, 0, False)
            logits(2 * p + 2, 0)
            consume(2 * p + 1, 1, False)
            return 0
        lax.fori_loop(0, n_full // 2, tile_pair, 0)
    else:
        def tile(t, _):
            logits(t + 1, (t + 1) % 2)
            consume(t, t % 2, False)
            return 0
        lax.fori_loop(0, n_full, tile, 0)
    for jj in range(n_diag):
        t = n_full + jj
        slot = jj % 2 if n_diag % 2 == 0 else t % 2
        if jj + 1 < n_diag:
            logits(t + 1, 1 - slot, row_lo=(jj + 1) * tk)
        consume(t, slot, True, row_lo=jj * tk)

    outs = []
    for hh in range(2):
        acc = acc_sc[hh]
        outs.append(acc * pltpu.roll(1.0 / acc, V_DIM, 1))
    lane = lax.broadcasted_iota(jnp.int32, (tq, LANES), 1)
    o = jnp.where(lane < V_DIM, outs[0], pltpu.roll(outs[1], V_DIM, 1))
    ms = _dot((o * o).astype(BF16), gm_ref[...]) * (1.0 / V_DIM)
    o_ref[...] = (o * lax.rsqrt(ms + EPS) * g_ref[...]).astype(BF16)


def _flash(q, k, v, g_attn, gm_head, batch, seq_len):
    n = q.shape[0]
    nq = seq_len // TQ
    pair = pl.BlockSpec((seq_len, 2 * HEAD_PAD), lambda b, p, i: (b, p))
    return pl.pallas_call(
        functools.partial(_flash_body, tq=TQ, tk=TK),
        grid=(batch, N_HEADS // 2, nq),
        in_specs=[pl.BlockSpec((TQ, 2 * HEAD_PAD), lambda b, p, i: (b * nq + i, p)),
                  pair, pair,
                  pl.BlockSpec((1, 2 * V_DIM), lambda b, p, i: (0, p)),
                  pl.BlockSpec((LANES, LANES), lambda b, p, i: (0, 0))],
        out_specs=pl.BlockSpec((TQ, 2 * V_DIM), lambda b, p, i: (b * nq + i, p)),
        out_shape=jax.ShapeDtypeStruct((n, ATTN_WIDTH), BF16),
        scratch_shapes=[pltpu.VMEM((2, TQ, LANES), F32), pltpu.VMEM((2, TQ, LANES), F32),
                        pltpu.VMEM((2, 2, TQ, TK), F32), pltpu.VMEM((2, TQ, TK), BF16)],
        compiler_params=_cparams(("arbitrary", "arbitrary", "arbitrary")),
        name="flash_prompt",
    )(q, k, v, g_attn, gm_head)


def _absorb_body(q_ref, w_ref, ql_ref, qp_ref):
    qb = q_ref[...]
    ql_ref[...] = _dot(qb, w_ref[...]).astype(BF16)
    qp_ref[...] = qb[:, PE_LO:PE_LO + ROPE_DIM]


def _absorb(q_s, wuk_t):
    n = q_s.shape[0]
    return pl.pallas_call(
        _absorb_body,
        grid=(N_HEADS,),
        in_specs=[pl.BlockSpec((n, HEAD_PAD), lambda h: (0, h)),
                  pl.BlockSpec((None, HEAD_PAD, KV_LORA), lambda h: (h, 0, 0))],
        out_specs=[pl.BlockSpec((None, n, KV_LORA), lambda h: (h, 0, 0)),
                   pl.BlockSpec((None, n, ROPE_DIM), lambda h: (h, 0, 0))],
        out_shape=[jax.ShapeDtypeStruct((N_HEADS, n, KV_LORA), BF16),
                   jax.ShapeDtypeStruct((N_HEADS, n, ROPE_DIM), BF16)],
        compiler_params=_cparams(("arbitrary",)),
        name="absorb",
    )(q_s, wuk_t)


PAGES_PER_CHUNK = 8


def _paged_body(pt_ref, ql_ref, qp_ref, cn_ref, kn_ref, cckv_ref, ckpe_ref, o_ref,
                ckv_buf, kpe_buf, s_all, kcb, sems, *, n_pages, n_seq, t_new):
    b = pl.program_id(0)
    slot = b % 2

    def fetch(seq, sl):
        def body(g, _):
            for u in range(PAGES_PER_CHUNK):
                pg = g * PAGES_PER_CHUNK + u
                page = pt_ref[seq, pg]
                pltpu.make_async_copy(cckv_ref.at[0, page], ckv_buf.at[sl, pg], sems.at[0, sl]).start()
                pltpu.make_async_copy(ckpe_ref.at[0, page], kpe_buf.at[sl, pg],
                                      sems.at[1, sl]).start(priority=1)
            return 0
        lax.fori_loop(0, n_pages // PAGES_PER_CHUNK, body, 0)

    @pl.when(b == 0)
    def _():
        fetch(0, 0)

    @pl.when(b + 1 < n_seq)
    def _():
        fetch(b + 1, 1 - slot)

    pltpu.make_async_copy(cckv_ref.at[0, pl.ds(0, n_pages)], ckv_buf.at[slot], sems.at[0, slot]).wait()
    pltpu.make_async_copy(ckpe_ref.at[0, pl.ds(0, n_pages)], kpe_buf.at[slot], sems.at[1, slot]).wait()

    ql = ql_ref[...]
    qp = qp_ref[...]
    rows = ql.shape[0]
    ck = PAGES_PER_CHUNK * PAGE_SIZE

    n_chunks = n_pages // PAGES_PER_CHUNK

    m = jnp.full((rows, 1), -jnp.inf, F32)
    for c in range(n_chunks):
        pages = slice(c * PAGES_PER_CHUNK, (c + 1) * PAGES_PER_CHUNK)
        kc = ckv_buf[slot, pages].reshape(ck, KV_LORA).astype(BF16)
        kcb[c] = kc
        pc = kpe_buf[slot, pages].astype(BF16)
        s_pe = jnp.concatenate([_dot(qp, pc[j]) for j in range(PAGES_PER_CHUNK)], axis=1)
        s = _dot_nt(ql, kc) + s_pe
        s_all[c] = s
        m = jnp.maximum(m, jnp.max(s, axis=1, keepdims=True))

    qlf = ql.astype(F32)
    qpf = qp.astype(F32)
    cn = cn_ref[...]
    kn = kn_ref[...]
    trow = lax.broadcasted_iota(jnp.int32, (rows, 1), 0) % t_new
    s_new = []
    for j in range(t_new):
        sj = (jnp.sum(qlf * cn[j:j + 1, :], axis=1, keepdims=True)
              + jnp.sum(qpf * kn[j:j + 1, :], axis=1, keepdims=True))
        sj = jnp.where(trow >= j, sj, -jnp.inf)
        s_new.append(sj)
        m = jnp.maximum(m, sj)

    l = jnp.zeros((rows, 1), F32)
    acc = jnp.zeros((rows, KV_LORA), F32)
    for c in range(n_chunks):
        pr = jnp.exp2(s_all[c] - m)
        l = l + jnp.sum(pr, axis=1, keepdims=True)
        acc = acc + _dot(pr.astype(BF16), kcb[c])
    for j in range(t_new):
        pj = jnp.exp2(s_new[j] - m)
        l = l + pj
        acc = acc + pj * cn[j:j + 1, :]
    o_ref[...] = acc / l


def _paged(page_table, ql, qp, ckv_new, kpe_new, cache_ckv, cache_kpe):
    n_seq, n_pages = page_table.shape
    rows = ql.shape[1]
    t_new = ckv_new.shape[1]
    grid_spec = pltpu.PrefetchScalarGridSpec(
        num_scalar_prefetch=1,
        grid=(n_seq,),
        in_specs=[pl.BlockSpec((None, rows, KV_LORA), lambda b, pt: (b, 0, 0)),
                  pl.BlockSpec((None, rows, ROPE_DIM), lambda b, pt: (b, 0, 0)),
                  pl.BlockSpec((None, t_new, KV_LORA), lambda b, pt: (b, 0, 0)),
                  pl.BlockSpec((None, t_new, ROPE_DIM), lambda b, pt: (b, 0, 0)),
                  pl.BlockSpec(memory_space=pl.ANY),
                  pl.BlockSpec(memory_space=pl.ANY)],
        out_specs=pl.BlockSpec((None, rows, KV_LORA), lambda b, pt: (b, 0, 0)),
        scratch_shapes=[pltpu.VMEM((2, n_pages, PAGE_SIZE, KV_LORA), F32),
                        pltpu.VMEM((2, n_pages, ROPE_DIM, PAGE_SIZE), F32),
                        pltpu.VMEM((n_pages // PAGES_PER_CHUNK, rows, PAGES_PER_CHUNK * PAGE_SIZE), F32),
                        pltpu.VMEM((n_pages // PAGES_PER_CHUNK, PAGES_PER_CHUNK * PAGE_SIZE, KV_LORA), BF16),
                        pltpu.SemaphoreType.DMA((2, 2))],
    )
    return pl.pallas_call(
        functools.partial(_paged_body, n_pages=n_pages, n_seq=n_seq, t_new=t_new),
        grid_spec=grid_spec,
        out_shape=jax.ShapeDtypeStruct((n_seq, rows, KV_LORA), F32),
        compiler_params=_cparams(("arbitrary",)),
        name="paged_attn",
    )(page_table, ql, qp, ckv_new, kpe_new, cache_ckv, cache_kpe)


def _unabsorb_body(o_ref, w_ref, g_ref, out_ref):
    o = _dot(o_ref[...].astype(BF16), w_ref[...])
    r = lax.rsqrt(jnp.sum(o * o, axis=-1, keepdims=True) * (1.0 / V_DIM) + EPS)
    out_ref[...] = (o * r * g_ref[...]).astype(BF16)


def _unabsorb(o_lat_h, wuv_h, g_h):
    n = o_lat_h.shape[1]
    return pl.pallas_call(
        _unabsorb_body,
        grid=(N_HEADS,),
        in_specs=[pl.BlockSpec((None, n, KV_LORA), lambda h: (h, 0, 0)),
                  pl.BlockSpec((None, KV_LORA, HEAD_PAD), lambda h: (h, 0, 0)),
                  pl.BlockSpec((None, 1, HEAD_PAD), lambda h: (h, 0, 0))],
        out_specs=pl.BlockSpec((None, n, HEAD_PAD), lambda h: (h, 0, 0)),
        out_shape=jax.ShapeDtypeStruct((N_HEADS, n, HEAD_PAD), BF16),
        compiler_params=_cparams(("arbitrary",)),
        name="unabsorb",
    )(o_lat_h, wuv_h, g_h)


def _post_mix_body(x_ref, o_ref, z_ref, gt1_ref, sc_ref, sh_ref, gt2_ref, wo_ref, g2_ref,
                   wr_ref, eb_ref, wsgu_ref, wsd_ref, tri_ref, cin_ref,
                   xs_ref, h2_ref, idx_ref, gw_ref, rank_ref, cout_ref, cnt_ref, *, tm):
    i = pl.program_id(0)

    @pl.when(i == 0)
    def _():
        cnt_ref[...] = cin_ref[...]

    wo = wo_ref[...]
    mix = _dot(o_ref[...], wo[:ATTN_WIDTH, :]) + _dot(z_ref[...], wo[ATTN_WIDTH:, :])
    x1 = x_ref[...] + gt1_ref[...] * mix
    h2 = _rms(x1, g2_ref[...]) * (1.0 + sc_ref[...]) + sh_ref[...]
    hb = h2.astype(BF16)
    h2_ref[...] = _pack_pair(hb)

    gu = _dot(hb, wsgu_ref[...])
    gate, up = gu[:, :SHARED_FF], gu[:, SHARED_FF:]
    shared = _dot((gate * jax.nn.sigmoid(gate) * up).astype(BF16), wsd_ref[...])
    xs_ref[...] = x1 + gt2_ref[...] * shared

    s = jax.nn.sigmoid(_dot_nt(wr_ref[...], hb))
    biased = s + eb_ref[...]
    ninf = -jnp.inf
    gi = lax.broadcasted_iota(jnp.int32, (GROUP_SIZE, tm), 0).astype(F32)
    gs = []
    for g in range(N_GROUPS):
        blk = biased[g * GROUP_SIZE:(g + 1) * GROUP_SIZE, :]
        m1 = jnp.max(blk, axis=0, keepdims=True)
        i1 = jnp.min(jnp.where(blk == m1, gi, float(GROUP_SIZE)), axis=0, keepdims=True)
        m2 = jnp.max(jnp.where(gi == i1, ninf, blk), axis=0, keepdims=True)
        gs.append(m1 + m2)
    gscore = jnp.concatenate(gs, axis=0)
    giota = lax.broadcasted_iota(jnp.int32, (N_GROUPS, tm), 0).astype(F32)
    gsel = jnp.zeros((N_GROUPS, tm), F32)
    for _ in range(TOPK_GROUPS):
        gm = jnp.max(gscore, axis=0, keepdims=True)
        gidx = jnp.min(jnp.where(gscore == gm, giota, float(N_GROUPS)), axis=0, keepdims=True)
        hit = giota == gidx
        gsel = jnp.where(hit, 1.0, gsel)
        gscore = jnp.where(hit, ninf, gscore)
    masked = jnp.concatenate(
        [jnp.where(gsel[g:g + 1, :] > 0.0, biased[g * GROUP_SIZE:(g + 1) * GROUP_SIZE, :], ninf)
         for g in range(N_GROUPS)], axis=0)
    eiota = lax.broadcasted_iota(jnp.int32, (N_EXPERTS, tm), 0).astype(F32)
    idxs, ws = [], []
    selall = jnp.zeros((N_EXPERTS, tm), F32)
    for _ in range(TOP_K):
        mx = jnp.max(masked, axis=0, keepdims=True)
        ei = jnp.min(jnp.where(masked == mx, eiota, float(N_EXPERTS)), axis=0, keepdims=True)
        hit = eiota == ei
        ws.append(jnp.sum(jnp.where(hit, s, 0.0), axis=0, keepdims=True))
        idxs.append(ei)
        selall = jnp.where(hit, 1.0, selall)
        masked = jnp.where(hit, ninf, masked)
    wsum = ws[0]
    for w in ws[1:]:
        wsum = wsum + w
    gw_ref[...] = jnp.concatenate(ws, axis=0) / wsum * ROUTED_SCALE
    idx_ref[...] = jnp.concatenate(idxs, axis=0).astype(jnp.int32)

    before = _dot(selall.astype(BF16), tri_ref[...]) + cnt_ref[:, 0:1]
    ranks = [jnp.sum(jnp.where(eiota == ei, before, 0.0), axis=0, keepdims=True) for ei in idxs]
    rank_ref[...] = jnp.concatenate(ranks, axis=0).astype(jnp.int32)
    cnt_ref[...] = cnt_ref[...] + jnp.sum(selall, axis=1, keepdims=True)
    cout_ref[...] = cnt_ref[...]


def _post_mix(x2d, o, z, gt1, sc2, sh2, gt2, p, cnt_in, h2_buf, *, prompt, seq_len, row_off, n_total):
    n = x2d.shape[0]
    tm = TILE_PROMPT if prompt else TILE_SAMPLE
    nt = n // tm
    full = lambda a: pl.BlockSpec(a.shape, lambda i: (0,) * a.ndim)
    row_tile = lambda w: pl.BlockSpec((tm, w), lambda i: (i, 0))
    if prompt:
        tps = seq_len // tm
        mod_spec = pl.BlockSpec((None, 1, D_MODEL), lambda i: (i // tps, 0, 0))
    else:
        mod_spec = row_tile(D_MODEL)
    tri = p["tri_p"] if prompt else p["tri_s"]
    boff = row_off // tm
    ins = [x2d, o, z, gt1, sc2, sh2, gt2, p["w_out"], p["g2"], p["wr_t"], p["e_bias"], p["ws_gu"], p["ws_d"],
           tri, cnt_in]
    in_specs = [row_tile(D_MODEL), row_tile(ATTN_WIDTH), row_tile(CONV_DIM), mod_spec, mod_spec, mod_spec,
                mod_spec, full(p["w_out"]), full(p["g2"]), full(p["wr_t"]), full(p["e_bias"]), full(p["ws_gu"]),
                full(p["ws_d"]), full(tri), full(cnt_in)]
    out_shape = [jax.ShapeDtypeStruct((n, D_MODEL), F32), jax.ShapeDtypeStruct((n_total, PACK_W), jnp.uint32),
                 jax.ShapeDtypeStruct((TOP_K, n), jnp.int32), jax.ShapeDtypeStruct((TOP_K, n), F32),
                 jax.ShapeDtypeStruct((TOP_K, n), jnp.int32), jax.ShapeDtypeStruct((N_EXPERTS, LANES), F32)]
    col_tile = pl.BlockSpec((TOP_K, tm), lambda i: (0, i))
    out_specs = [row_tile(D_MODEL), pl.BlockSpec((tm, PACK_W), lambda i: (i + boff, 0)),
                 col_tile, col_tile, col_tile, pl.BlockSpec((N_EXPERTS, LANES), lambda i: (0, 0))]
    aliases = {}
    if h2_buf is not None:
        ins.append(h2_buf)
        in_specs.append(pl.BlockSpec(memory_space=pl.ANY))
        aliases = {len(ins) - 1: 1}
        body = lambda *refs: _post_mix_body(*refs[:15], *refs[16:], tm=tm)
    else:
        body = functools.partial(_post_mix_body, tm=tm)
    return pl.pallas_call(
        body, grid=(nt,), in_specs=in_specs, out_specs=out_specs, out_shape=out_shape,
        scratch_shapes=[pltpu.VMEM((N_EXPERTS, LANES), F32)],
        input_output_aliases=aliases,
        compiler_params=_cparams(("arbitrary",)),
        name="post_mix_prompt" if prompt else "post_mix_sample",
    )(*ins)


def _dest_body(idx_ref, rank_ref, ps_ref, d_ref, *, tm):
    eiota = lax.broadcasted_iota(jnp.int32, (N_EXPERTS, tm), 0)
    ps = ps_ref[...]
    idx = idx_ref[...]
    starts = [jnp.sum(jnp.where(eiota == idx[k:k + 1, :], ps, 0.0), axis=0, keepdims=True) for k in range(TOP_K)]
    d_ref[...] = jnp.concatenate(starts, axis=0).astype(jnp.int32) + rank_ref[...]


def _dest(idx, rank, pstart):
    n = idx.shape[1]
    tm = 512
    tile = pl.BlockSpec((TOP_K, tm), lambda i: (0, i))
    return pl.pallas_call(
        functools.partial(_dest_body, tm=tm),
        grid=(n // tm,),
        in_specs=[tile, tile, pl.BlockSpec((N_EXPERTS, 1), lambda i: (0, 0))],
        out_specs=tile,
        out_shape=jax.ShapeDtypeStruct((TOP_K, n), jnp.int32),
        compiler_params=_cparams(("arbitrary",)),
        name="dest_rows",
    )(idx, rank, pstart.astype(F32).reshape(N_EXPERTS, 1))


def _experts_body(e0_ref, rs_ref, sz_ref, nv_ref, nx_ref, ws_ref, x_ref, wg_ref, wu_ref, wd_ref, y_ref,
                  xbuf, ybuf, sg, su, sd, wgub, wdb, sem, sem_x, sem_y, *, ch, gran):
    c = pl.program_id(0)
    slot = c % 2
    classes = list(range(gran, ch + 1, gran))

    def x_copy(cc, sl, rows):
        start = pl.multiple_of(rs_ref[cc], gran)
        return pltpu.make_async_copy(x_ref.at[pl.ds(start, rows)], xbuf.at[sl, pl.ds(0, rows)], sem_x.at[sl])

    def y_copy(cc, sl, rows):
        start = pl.multiple_of(rs_ref[cc], gran)
        return pltpu.make_async_copy(ybuf.at[sl, pl.ds(0, rows)], y_ref.at[pl.ds(start, rows)], sem_y.at[sl])

    def for_class(cc, fn):
        for r in classes:
            pl.when(sz_ref[cc] == r // gran)(functools.partial(fn, r))

    def copies(e, ws):
        return (pltpu.make_async_copy(wg_ref.at[e], sg.at[ws], sem.at[ws, 0]),
                pltpu.make_async_copy(wu_ref.at[e], su.at[ws], sem.at[ws, 1]),
                pltpu.make_async_copy(wd_ref.at[e], sd.at[ws], sem.at[ws, 2]))

    @pl.when(c == 0)
    def _():
        for cp in copies(e0_ref[0], 0):
            cp.start()

        @pl.when(e0_ref[1] >= 0)
        def _():
            for cp in copies(e0_ref[1], 1):
                cp.start()
        for_class(0, lambda r: x_copy(0, 0, r).start())

    for_class(c + 1, lambda r: x_copy(c + 1, 1 - slot, r).start())

    nxt = nx_ref[c]

    @pl.when(nxt >= -1)
    def _():
        ws = ws_ref[c]
        for cp in copies(0, ws):
            cp.wait()
        wgub[:, :E_FF] = sg[ws].astype(BF16)
        wgub[:, E_FF:] = su[ws].astype(BF16)
        wdb[...] = sd[ws].astype(BF16)

        @pl.when(nxt >= 0)
        def _():
            for cp in copies(nxt, ws):
                cp.start()

    @pl.when(c >= 2)
    def _():
        for_class(c - 2, lambda r: y_copy(c - 2, slot, r).wait())

    nv = nv_ref[c]

    def run(rows):
        x_copy(c, slot, rows).wait()
        row = lax.broadcasted_iota(jnp.int32, (rows, 1), 0)
        w = jnp.where(row < nv, xbuf[slot, :rows, :], jnp.uint32(0))
        lo, hi = _unpack_pair(w)
        lo, hi = lo.astype(BF16), hi.astype(BF16)
        gu = _dot(lo, wgub[:PACK_W, :]) + _dot(hi, wgub[PACK_W:, :])
        gate, up = gu[:, :E_FF], gu[:, E_FF:]
        act = (gate * jax.nn.sigmoid(gate) * up).astype(BF16)
        ybuf[slot, :rows, :] = _pack_pair(_dot(act, wdb[...]).astype(BF16))
        y_copy(c, slot, rows).start()

    for_class(c, run)


def _experts(e_first, row_start, size_class, n_valid, next_e, w_slot, x_sorted, w_e_gate, w_e_up, w_e_down):
    ch, gran = ROUTE_CHUNK, ROW_GRAN
    steps = row_start.shape[0] - 1
    any_spec = pl.BlockSpec(memory_space=pl.ANY)
    grid_spec = pltpu.PrefetchScalarGridSpec(
        num_scalar_prefetch=6,
        grid=(steps,),
        in_specs=[any_spec, any_spec, any_spec, any_spec],
        out_specs=any_spec,
        scratch_shapes=[pltpu.VMEM((2, ch, PACK_W), jnp.uint32), pltpu.VMEM((2, ch, PACK_W), jnp.uint32),
                        pltpu.VMEM((2, D_MODEL, E_FF), F32), pltpu.VMEM((2, D_MODEL, E_FF), F32),
                        pltpu.VMEM((2, E_FF, D_MODEL), F32),
                        pltpu.VMEM((D_MODEL, 2 * E_FF), BF16),
                        pltpu.VMEM((E_FF, D_MODEL), BF16), pltpu.SemaphoreType.DMA((2, 3)),
                        pltpu.SemaphoreType.DMA((2,)), pltpu.SemaphoreType.DMA((2,))],
    )
    return pl.pallas_call(
        functools.partial(_experts_body, ch=ch, gran=gran),
        grid_spec=grid_spec,
        out_shape=jax.ShapeDtypeStruct(x_sorted.shape, jnp.uint32),
        compiler_params=_cparams(("arbitrary",)),
        name="experts",
    )(e_first, row_start, size_class, n_valid, next_e, w_slot, x_sorted, w_e_gate, w_e_up, w_e_down)


SC_CORES = 2
SC_SUBCORES = 16
SC_WORKERS = SC_CORES * SC_SUBCORES
SCATTER_WIN = 24
GATHER_WIN = 64
GATHER_PARTS = 2


def _sc_mesh():
    return plsc.VectorSubcoreMesh(core_axis_name="c", subcore_axis_name="s")


def _sc_worker():
    return lax.axis_index("s") * SC_CORES + lax.axis_index("c")


def _sc_scatter_rows(rows, idx, n_out):
    n, width = rows.shape
    nw, nwin_k, win = idx.shape
    n_win = nwin_k // TOP_K
    per_w = n // nw

    assert n_win % 2 == 0

    def body(rows_hbm, idx_hbm, out_hbm, idx_v, buf, sem_in, sem_out):
        wid = _sc_worker()
        base = wid * per_w
        pltpu.sync_copy(idx_hbm.at[wid], idx_v)

        def load(j, b):
            return pltpu.make_async_copy(rows_hbm.at[pl.ds(base + j * win, win)], buf.at[b], sem_in.at[b])

        def scatters(j, b):
            return [pltpu.make_async_copy(buf.at[b], out_hbm.at[idx_v.at[j * TOP_K + k]], sem_out.at[b])
                    for k in range(TOP_K)]

        load(0, 0).start()

        @pl.loop(0, n_win, step=2)
        def _(j):
            for b in range(2):
                jj = j + b
                load(jj, b).wait()
                for cp in scatters(jj, b):
                    cp.start()

                @pl.when(jj >= 1)
                def _():
                    for cp in scatters(jj - 1, 1 - b):
                        cp.wait()

                @pl.when(jj + 1 < n_win)
                def _():
                    load(jj + 1, 1 - b).start()

        for cp in scatters(n_win - 1, 1):
            cp.wait()

    return pl.kernel(
        body, out_type=jax.ShapeDtypeStruct((n_out, width), rows.dtype), mesh=_sc_mesh(),
        scratch_types=[pltpu.VMEM((nwin_k, win), jnp.int32), pltpu.VMEM((2, win, width), rows.dtype),
                       pltpu.SemaphoreType.DMA((2,)), pltpu.SemaphoreType.DMA((2,))],
        name="sc_scatter_rows",
    )(rows, idx)


def _sc_gather_rows(table, idx):
    nw, n_win, win = idx.shape
    width = table.shape[1]
    per_w = n_win * win

    assert n_win % 2 == 0

    def body(table_hbm, idx_hbm, out_hbm, idx_v, buf, sem):
        wid = _sc_worker()
        base = wid * per_w
        pltpu.sync_copy(idx_hbm.at[wid], idx_v)

        def gather(j, b):
            return pltpu.make_async_copy(table_hbm.at[idx_v.at[j]], buf.at[b], sem.at[b])

        gather(0, 0).start()

        @pl.loop(0, n_win, step=2)
        def _(j):
            for b in range(2):
                @pl.when(j + b + 1 < n_win)
                def _():
                    gather(j + b + 1, 1 - b).start()
                gather(j + b, b).wait()
                pltpu.sync_copy(buf.at[b], out_hbm.at[pl.ds(base + (j + b) * win, win)])

    return pl.kernel(
        body, out_type=jax.ShapeDtypeStruct((nw * per_w, width), table.dtype), mesh=_sc_mesh(),
        scratch_types=[pltpu.VMEM((n_win, win), jnp.int32), pltpu.VMEM((2, win, width), table.dtype),
                       pltpu.SemaphoreType.DMA((2,))],
        name="sc_gather_rows",
    )(table, idx)


def _finish_body(y_ref, xs_ref, gw_ref, gt2_ref, gf_ref, fsc_ref, fsh_ref, out_ref, *, tm):
    gw = gw_ref[...]
    f_lo = jnp.zeros((tm, PACK_W), F32)
    f_hi = jnp.zeros((tm, PACK_W), F32)
    for k in range(TOP_K):
        lo, hi = _unpack_pair(y_ref[k * tm:(k + 1) * tm, :])
        f_lo = f_lo + gw[:, k:k + 1] * lo
        f_hi = f_hi + gw[:, k:k + 1] * hi
    f = jnp.concatenate([f_lo, f_hi], axis=1)
    x2 = xs_ref[...] + gt2_ref[...] * f
    out_ref[...] = _rms(x2, gf_ref[...]) * (1.0 + fsc_ref[...]) + fsh_ref[...]


def _finish(y_part, xs, gw, gt2, g_final, fsc, fsh, *, prompt, seq_len, tile_lo, n_tiles, y_tile_off, out_buf=None):
    n = xs.shape[0]
    tm = COMBINE_TILE
    row_tile = lambda w: pl.BlockSpec((tm, w), lambda i: (i + tile_lo, 0))
    if prompt:
        tps = seq_len // tm
        mod_spec = pl.BlockSpec((None, 1, D_MODEL), lambda i: ((i + tile_lo) // tps, 0, 0))
    else:
        mod_spec = row_tile(D_MODEL)
    ins = [y_part, xs, gw, gt2, g_final, fsc, fsh]
    in_specs = [pl.BlockSpec((TOP_K * tm, PACK_W), lambda i: (i + y_tile_off, 0)),
                row_tile(D_MODEL), row_tile(TOP_K), mod_spec,
                pl.BlockSpec((1, D_MODEL), lambda i: (0, 0)), mod_spec, mod_spec]
    aliases = {}
    body = functools.partial(_finish_body, tm=tm)
    if out_buf is not None:
        ins.append(out_buf)
        in_specs.append(pl.BlockSpec(memory_space=pl.ANY))
        aliases = {len(ins) - 1: 0}
        body = lambda *refs: _finish_body(*refs[:7], *refs[8:], tm=tm)
    return pl.pallas_call(
        body,
        grid=(n_tiles,),
        in_specs=in_specs,
        out_specs=row_tile(D_MODEL),
        out_shape=jax.ShapeDtypeStruct((n, D_MODEL), F32),
        input_output_aliases=aliases,
        compiler_params=_cparams(("arbitrary",)),
        name="finish_prompt" if prompt else "finish_sample",
    )(*ins)


def _rope_tables(pos):
    pos = np.asarray(pos, np.float32)
    inv = np.float32(1.0) / (np.float32(ROPE_THETA) ** (np.arange(PE_HALF, dtype=np.float32) / np.float32(PE_HALF)))
    ang = pos[:, None] * inv[None, :]
    cos, sin = np.cos(ang), np.sin(ang)
    n = pos.shape[0]
    ones = np.ones((n, PE_LO), np.float32)
    zeros = np.zeros((n, PE_LO), np.float32)
    zh = np.zeros((n, PE_HALF), np.float32)
    tail1 = np.ones((n, LANES - PE_LO - ROPE_DIM), np.float32)
    tail0 = np.zeros((n, LANES - PE_LO - ROPE_DIM), np.float32)
    cos_t = np.concatenate([ones, cos, cos, tail1], axis=1)
    sin_up = np.concatenate([zeros, zh, sin, tail0], axis=1)
    sin_dn = np.concatenate([zeros, -sin, zh, tail0], axis=1)
    return tuple(t.astype(np.float32) for t in (cos_t, sin_up, sin_dn))


def _pad_heads(w, width):
    pad = [(0, 0)] * (w.ndim - 1) + [(0, HEAD_PAD - width)]
    w = jnp.pad(w, pad)
    return w.reshape(w.shape[:-2] + (N_HEADS * HEAD_PAD,))


def _block_diag_ones(n, blk):
    r = np.arange(n) // blk
    return jnp.asarray(r[:, None] == r[None, :], dtype=BF16)


def _prep_weights(w_in, g_attn_norm, g_q, w_q_up, g_kv, w_uk, w_uv, w_conv, g_attn_out, g_conv_out, w_out,
                  g_ffn_norm, w_router, e_bias, w_s_gate, w_s_up, w_s_down):
    o0 = Q_LORA
    o1 = o0 + KV_LORA
    o2 = o1 + ROPE_DIM
    kpe_cols = jnp.pad(w_in[:, o1:o2], ((0, 0), (PE_LO, HEAD_PAD - PE_LO - ROPE_DIM)))
    w_in_p = jnp.concatenate([w_in[:, :o1], kpe_cols, w_in[:, o2:]], axis=1).astype(BF16)
    wq = _pad_heads(w_q_up.reshape(Q_LORA, N_HEADS, NOPE_DIM + ROPE_DIM), NOPE_DIM + ROPE_DIM).astype(BF16)
    wuk = _pad_heads(w_uk, NOPE_DIM).astype(BF16)
    wuv = _pad_heads(w_uv, V_DIM).astype(BF16)
    wuk_t = jnp.pad(jnp.transpose(w_uk, (1, 2, 0)), ((0, 0), (0, HEAD_PAD - NOPE_DIM), (0, 0))).astype(BF16)
    wuv_h = jnp.pad(jnp.transpose(w_uv, (1, 0, 2)), ((0, 0), (0, 0), (0, HEAD_PAD - V_DIM))).astype(BF16)
    g_attn_h = jnp.pad(g_attn_out.reshape(N_HEADS, 1, V_DIM), ((0, 0), (0, 0), (0, HEAD_PAD - V_DIM)))
    tri = lambda t: jnp.asarray(np.arange(t)[:, None] < np.arange(t)[None, :], dtype=BF16)
    return {
        "g1": g_attn_norm.reshape(1, -1), "w_in": w_in_p, "g_q": g_q.reshape(1, -1), "wq": wq,
        "g_kv": g_kv.reshape(1, -1), "wuk": wuk, "wuv": wuv, "wuk_t": wuk_t, "wuv_h": wuv_h,
        "w_conv": w_conv, "g_conv": g_conv_out.reshape(1, -1), "gmat": _block_diag_ones(CONV_DIM, CONV_GROUP_DIM),
        "g_attn": g_attn_out.reshape(1, -1), "g_attn_h": g_attn_h, "gm_head": _block_diag_ones(LANES, V_DIM),
        "w_out": w_out.astype(BF16), "g2": g_ffn_norm.reshape(1, -1), "wr_t": w_router.T.astype(BF16),
        "e_bias": e_bias.reshape(-1, 1), "ws_gu": jnp.concatenate([w_s_gate, w_s_up], axis=1).astype(BF16),
        "ws_d": w_s_down.astype(BF16), "tri_p": tri(TILE_PROMPT), "tri_s": tri(TILE_SAMPLE),
    }


def kernel(x_prompt, x_sample, c_prompt, c_sample, cache_ckv, cache_kpe, state_conv, page_table, w_ada, b_ada, g_attn_norm, w_in, g_q, w_q_up, g_kv, w_uk, w_uv, w_conv, g_attn_out, g_conv_out, w_out, g_ffn_norm, w_router, e_bias, w_e_gate, w_e_up, w_e_down, w_s_gate, w_s_up, w_s_down, w_ada_final, b_ada_final, g_final):
    assert w_ada.shape[0] == 1, "one layer"
    bsz, seq, d = x_prompt.shape
    nseq, t_new, _ = x_sample.shape
    n_p, n_s = bsz * seq, nseq * t_new
    n_tot = n_p + n_s
    past = page_table.shape[1] * PAGE_SIZE

    p = _prep_weights(w_in[0], g_attn_norm[0], g_q[0], w_q_up[0], g_kv[0], w_uk[0], w_uv[0], w_conv[0],
                      g_attn_out[0], g_conv_out[0], w_out[0], g_ffn_norm[0], w_router[0], e_bias[0],
                      w_s_gate[0], w_s_up[0], w_s_down[0])

    c_all = jnp.concatenate([c_prompt, c_sample], axis=0)
    mod = _ada(c_all, w_ada[0], b_ada[0])
    modf = _ada(c_all, w_ada_final, b_ada_final)
    sh1, sc1, gt1, sh2, sc2, gt2 = [mod[:, j * d:(j + 1) * d] for j in range(6)]
    fsh, fsc = modf[:, :d], modf[:, d:]
    per_batch = lambda a: a[:bsz].reshape(bsz, 1, d)
    per_token = lambda a: jnp.repeat(a[bsz:], t_new, axis=0)

    xp = x_prompt.reshape(n_p, d)
    xs_in = x_sample.reshape(n_s, d)
    tabs_p = tuple(jnp.asarray(t) for t in _rope_tables(np.arange(seq)))
    tabs_s = tuple(jnp.asarray(np.tile(t, (nseq, 1))) for t in _rope_tables(past + np.arange(t_new)))

    q_p, k_p, v_p, ckv_p, kpe_p, z_p, conv_p = _mix_in(
        xp, per_batch(sc1), per_batch(sh1), p, tabs_p, prompt=True, seq_len=seq)
    o_p = _flash(q_p, k_p, v_p, p["g_attn"], p["gm_head"], bsz, seq)

    prev = state_conv[0]
    prev_a = jnp.repeat(prev[:, 1, :], t_new, axis=0)
    prev_b = jnp.repeat(prev[:, 0, :], t_new, axis=0)
    q_s, ckv_s, kpe_s, z_s, u_s = _mix_in(
        xs_in, per_token(sc1), per_token(sh1), p, tabs_s, prompt=False, prev_a=prev_a, prev_b=prev_b,
        t_new=t_new)
    ql_h, qp_h = _absorb(q_s, p["wuk_t"])
    rows = N_HEADS * t_new
    to_seq = lambda a: jnp.transpose(a.reshape(N_HEADS, nseq, t_new, a.shape[-1]), (1, 0, 2, 3)).reshape(
        nseq, rows, a.shape[-1])
    o_lat = _paged(page_table, to_seq(ql_h), to_seq(qp_h), ckv_s.reshape(nseq, t_new, KV_LORA),
                   kpe_s.reshape(nseq, t_new, ROPE_DIM), cache_ckv, jnp.swapaxes(cache_kpe, 2, 3))
    o_lat_h = jnp.transpose(o_lat.reshape(nseq, N_HEADS, t_new, KV_LORA), (1, 0, 2, 3)).reshape(
        N_HEADS, n_s, KV_LORA)
    o_s_h = _unabsorb(o_lat_h, p["wuv_h"], p["g_attn_h"])
    o_s = jnp.transpose(o_s_h[:, :, :V_DIM], (1, 0, 2)).reshape(n_s, ATTN_WIDTH)

    cnt0 = jnp.zeros((N_EXPERTS, LANES), F32)
    xs_p, h2, idx_p, gw_p, rank_p, cnt_p = _post_mix(
        xp, o_p, z_p, per_batch(gt1), per_batch(sc2), per_batch(sh2), per_batch(gt2), p, cnt0, None,
        prompt=True, seq_len=seq, row_off=0, n_total=n_tot)
    xs_s, h2, idx_s, gw_s, rank_s, cnt_s = _post_mix(
        xs_in, o_s, z_s, per_token(gt1), per_token(sc2), per_token(sh2), per_token(gt2), p, cnt_p, h2,
        prompt=False, seq_len=None, row_off=n_p, n_total=n_tot)

    ch, gran = ROUTE_CHUNK, ROW_GRAN
    counts = cnt_s[:, 0].astype(jnp.int32)
    padded = (counts + gran - 1) // gran * gran
    pend = jnp.cumsum(padded)
    pstart = pend - padded
    idx_all = jnp.concatenate([idx_p, idx_s], axis=1)
    rank_all = jnp.concatenate([rank_p, rank_s], axis=1)
    eids = jnp.arange(N_EXPERTS, dtype=jnp.int32)
    lookup = lambda table, keys: jnp.sum(jnp.where(keys[..., None] == eids, table, 0), axis=-1)
    dest = _dest(idx_all, rank_all, pstart)
    n_rows = n_tot * TOP_K + N_EXPERTS * gran
    nck = (padded + ch - 1) // ch
    cend = jnp.cumsum(nck)
    cstart = cend - nck
    n_used = cend[-1]
    n_meta = -(-(n_tot * TOP_K) // ch) + N_EXPERTS + 3
    cidx = jnp.arange(n_meta, dtype=jnp.int32)
    chunk_e = jnp.minimum(jnp.sum((cend[None, :] <= cidx[:, None]).astype(jnp.int32), axis=1), N_EXPERTS - 1)
    used = cidx < n_used
    off = (cidx - lookup(cstart, chunk_e)) * ch
    row_start = jnp.where(used, lookup(pstart, chunk_e) + off, 0).astype(jnp.int32)
    size_class = jnp.where(used, jnp.minimum(lookup(padded, chunk_e) - off, ch) // gran, 0).astype(jnp.int32)
    n_valid = jnp.where(used, jnp.clip(lookup(counts, chunk_e) - off, 0, ch), 0).astype(jnp.int32)
    has_rows = nck > 0
    run_of = jnp.cumsum(has_rows.astype(jnp.int32)) - 1
    n_runs = run_of[-1] + 1
    expert_of_run = lambda r: jnp.sum(jnp.where(has_rows[None, :] & (run_of[None, :] == r[:, None]), eids, 0), axis=1)
    chunk_run = lookup(run_of, chunk_e)
    is_first = used & (off == 0)
    ahead = chunk_run + 2
    next_e = jnp.where(is_first, jnp.where(ahead < n_runs, expert_of_run(ahead), -1), -2).astype(jnp.int32)
    w_slot = (chunk_run % 2).astype(jnp.int32)
    first_two = jnp.arange(2, dtype=jnp.int32)
    e_first = jnp.where(first_two < n_runs, expert_of_run(first_two), -1).astype(jnp.int32)

    tmc = COMBINE_TILE

    def dest_tiles(dst):
        n = dst.shape[1]
        return jnp.transpose(dst.reshape(TOP_K, n // tmc, tmc), (1, 0, 2)).reshape(n // tmc, 1, TOP_K * tmc)

    per_w = n_tot // SC_WORKERS
    idx_sc = jnp.transpose(dest.reshape(TOP_K, SC_WORKERS, per_w // SCATTER_WIN, SCATTER_WIN),
                           (1, 2, 0, 3)).reshape(SC_WORKERS, (per_w // SCATTER_WIN) * TOP_K, SCATTER_WIN)
    x_sorted = _sc_scatter_rows(h2, idx_sc, n_rows)
    y_sorted = _experts(e_first, row_start, size_class, n_valid, next_e, w_slot, x_sorted,
                        w_e_gate[0], w_e_up[0], w_e_down[0])

    idx_tiles = dest_tiles(dest)
    tiles_p = n_p // tmc
    part = tiles_p // GATHER_PARTS
    sc_idx = lambda a: a.reshape(SC_WORKERS, -1, GATHER_WIN)
    bounds = [(j * part, (j + 1) * part if j + 1 < GATHER_PARTS else idx_tiles.shape[0])
              for j in range(GATHER_PARTS)]
    y_parts = [_sc_gather_rows(y_sorted, sc_idx(idx_tiles[lo:hi])) for lo, hi in bounds]

    gfin = g_final.reshape(1, d)
    fin_p = functools.partial(_finish, xs=xs_p, gw=gw_p.T, gt2=per_batch(gt2), g_final=gfin, fsc=per_batch(fsc),
                              fsh=per_batch(fsh), prompt=True, seq_len=seq)
    y_p = None
    for j, y_part in enumerate(y_parts):
        y_p = fin_p(y_part, tile_lo=j * part, n_tiles=part, y_tile_off=0, out_buf=y_p)
    y_s = _finish(y_parts[-1], xs_s, gw_s.T, per_token(gt2), gfin, per_token(fsc), per_token(fsh), prompt=False,
                  seq_len=None, tile_lo=0, n_tiles=n_s // tmc, y_tile_off=part)

    return (y_p.reshape(bsz, seq, d), y_s.reshape(nseq, t_new, d),
            ckv_p.reshape(1, bsz, seq, KV_LORA), jnp.swapaxes(kpe_p, 1, 2)[None],
            conv_p.reshape(1, bsz, CONV_W - 1, CONV_DIM),
            ckv_s.reshape(1, nseq, t_new, KV_LORA), kpe_s.reshape(1, nseq, t_new, ROPE_DIM),
            u_s.reshape(nseq, t_new, CONV_DIM)[:, t_new - (CONV_W - 1):, :].reshape(1, nseq, CONV_W - 1, CONV_DIM))
```

```python
import functools
import math

import jax
import jax.numpy as jnp
import numpy as np
from jax import lax
from jax.experimental import pallas as pl
from jax.experimental.pallas import tpu as pltpu
from jax.experimental.pallas import tpu_sc as plsc

F32 = jnp.float32
BF16 = jnp.bfloat16

D_MODEL = 1024
N_HEADS = 8
NOPE_DIM = 64
ROPE_DIM = 32
V_DIM = 64
Q_LORA = 384
KV_LORA = 256
ATTN_WIDTH = N_HEADS * V_DIM
CONV_DIM = 512
CONV_GROUPS = 8
CONV_GROUP_DIM = CONV_DIM // CONV_GROUPS
CONV_W = 3
ROPE_THETA = 10000.0
PAGE_SIZE = 128
N_EXPERTS = 256
TOP_K = 8
N_GROUPS = 8
GROUP_SIZE = N_EXPERTS // N_GROUPS
TOPK_GROUPS = 4
E_FF = 256
SHARED_FF = 256
ROUTED_SCALE = 2.5
EPS = 1e-6

LANES = 128
HEAD_PAD = LANES
QK_WIDTH = N_HEADS * HEAD_PAD
PE_LO = NOPE_DIM
PE_HALF = ROPE_DIM // 2
IN_PAD_COLS = Q_LORA + KV_LORA + HEAD_PAD + 3 * CONV_DIM
VMEM_LIMIT = 48 * 1024 * 1024
Q_SCALE = (NOPE_DIM + ROPE_DIM) ** -0.5 * math.log2(math.e)
PACK_W = D_MODEL // 2

TILE_PROMPT = 512
TILE_SAMPLE = 256
TQ = 512
TK = 256
STRIP = 128
ROUTE_CHUNK = 768
ROW_GRAN = 128
COMBINE_TILE = 256


def _cparams(sem, vmem=VMEM_LIMIT):
    return pltpu.CompilerParams(dimension_semantics=sem, vmem_limit_bytes=vmem)


def _dot(a, b):
    return jnp.dot(a, b, preferred_element_type=F32)


def _dot_nt(a, b):
    return lax.dot_general(a, b, (((1,), (1,)), ((), ())), preferred_element_type=F32)


def _rms(x, g):
    r = lax.rsqrt(jnp.mean(x * x, axis=-1, keepdims=True) + EPS)
    return (x * r) * g


def _pack_pair(xb):
    lo = lax.bitcast_convert_type(xb[:, :PACK_W].astype(F32), jnp.uint32) >> 16
    hi = lax.bitcast_convert_type(xb[:, PACK_W:].astype(F32), jnp.uint32) & jnp.uint32(0xFFFF0000)
    return lo | hi


def _unpack_pair(w):
    lo = lax.bitcast_convert_type(w << 16, F32)
    hi = lax.bitcast_convert_type(w & jnp.uint32(0xFFFF0000), F32)
    return lo, hi


def _ada_body(c_ref, w_ref, b_ref, o_ref):
    c = c_ref[...]
    a = (c * jax.nn.sigmoid(c)).astype(BF16)
    o_ref[...] = _dot(a, w_ref[...].astype(BF16)) + b_ref[...]


def _ada(c_all, w, b):
    m, d = c_all.shape
    n = w.shape[1]
    tn = 1024
    return pl.pallas_call(
        _ada_body,
        grid=(n // tn,),
        in_specs=[pl.BlockSpec((m, d), lambda j: (0, 0)),
                  pl.BlockSpec((d, tn), lambda j: (0, j)),
                  pl.BlockSpec((1, tn), lambda j: (0, j))],
        out_specs=pl.BlockSpec((m, tn), lambda j: (0, j)),
        out_shape=jax.ShapeDtypeStruct((m, n), F32),
        compiler_params=_cparams(("arbitrary",)),
        name="ada",
    )(c_all, w, b.reshape(1, n))


def _rope_lanes(x, cos, sin_up, sin_dn):
    w = x.shape[1]
    up = pltpu.roll(x, PE_HALF, 1)
    dn = pltpu.roll(x, w - PE_HALF, 1)
    return x * cos + up * sin_up + dn * sin_dn


def _mix_in_body(*refs, prompt, tm, tiles_per_seq, t_new):
    if prompt:
        (x_ref, sc_ref, sh_ref, g1_ref, win_ref, gq_ref, wq_ref, gkv_ref, wuk_ref, wuv_ref,
         cos_ref, sup_ref, sdn_ref, wconv_ref, gconv_ref, gmat_ref,
         q_ref, k_ref, v_ref, ckv_ref, kpe_ref, z_ref, cst_ref, carry_ref) = refs
    else:
        (x_ref, sc_ref, sh_ref, g1_ref, win_ref, gq_ref, wq_ref, gkv_ref,
         cos_ref, sup_ref, sdn_ref, wconv_ref, gconv_ref, gmat_ref, pa_ref, pb_ref,
         q_ref, ckv_ref, kpe_ref, z_ref, u_ref) = refs

    x = x_ref[...]
    h = _rms(x, g1_ref[...]) * (1.0 + sc_ref[...]) + sh_ref[...]
    proj = _dot(h.astype(BF16), win_ref[...])
    o0 = Q_LORA
    o1 = o0 + KV_LORA
    o2 = o1 + HEAD_PAD
    o3 = o2 + CONV_DIM
    o4 = o3 + CONV_DIM
    q_a, kv_a, kpe_blk = proj[:, :o0], proj[:, o0:o1], proj[:, o1:o2]
    b_g, c_g, u_in = proj[:, o2:o3], proj[:, o3:o4], proj[:, o4:]

    cos, sup, sdn = cos_ref[...], sup_ref[...], sdn_ref[...]
    cos8 = jnp.concatenate([cos] * N_HEADS, axis=1)
    sup8 = jnp.concatenate([sup] * N_HEADS, axis=1)
    sdn8 = jnp.concatenate([sdn] * N_HEADS, axis=1)

    qn = _rms(q_a, gq_ref[...]).astype(BF16)
    q = _dot(qn, wq_ref[...]) * Q_SCALE
    q_ref[...] = _rope_lanes(q, cos8, sup8, sdn8).astype(BF16)

    ckv = _rms(kv_a, gkv_ref[...])
    ckv_ref[...] = ckv
    kpe = _rope_lanes(kpe_blk, cos, sup, sdn)
    if prompt:
        kpe_ref[...] = kpe.T[PE_LO:PE_LO + ROPE_DIM, :]
    else:
        kpe_ref[...] = kpe[:, PE_LO:PE_LO + ROPE_DIM]

    if prompt:
        ckvb = ckv.astype(BF16)
        k = _dot(ckvb, wuk_ref[...]) + jnp.concatenate([kpe] * N_HEADS, axis=1)
        k_ref[...] = k.astype(BF16)
        lane = lax.broadcasted_iota(jnp.int32, (1, QK_WIDTH), 1)
        ones_hi = jnp.where(lane % HEAD_PAD >= V_DIM, 1.0, 0.0)
        v_ref[...] = (_dot(ckvb, wuv_ref[...]) + ones_hi).astype(BF16)

    u = c_g * u_in
    row = lax.broadcasted_iota(jnp.int32, (tm, 1), 0)
    r1 = pltpu.roll(u, 1, 0)
    r2 = pltpu.roll(u, 2, 0)
    if prompt:
        @pl.when(pl.program_id(0) % tiles_per_seq == 0)
        def _():
            carry_ref[...] = jnp.zeros_like(carry_ref)
        c6 = carry_ref[6:7, :]
        c7 = carry_ref[7:8, :]
        um1 = jnp.where(row == 0, c7, r1)
        um2 = jnp.where(row == 0, c6, jnp.where(row == 1, c7, r2))
        carry_ref[...] = u[tm - 8:, :]
        cst_ref[...] = u[tm - (CONV_W - 1):, :]
    else:
        t = row % t_new
        um1 = jnp.where(t == 0, pa_ref[...], r1)
        um2 = jnp.where(t == 0, pb_ref[...], jnp.where(t == 1, pa_ref[...], r2))
        u_ref[...] = u
    wc = wconv_ref[...]
    y = um2 * wc[0:1, :] + um1 * wc[1:2, :] + u * wc[2:3, :]
    zz = b_g * y
    ms = _dot((zz * zz).astype(BF16), gmat_ref[...]) * (1.0 / CONV_GROUP_DIM)
    z_ref[...] = (zz * lax.rsqrt(ms + EPS) * gconv_ref[...]).astype(BF16)


def _mix_in(x2d, sc, sh, p, rope_tabs, *, prompt, seq_len=None, prev_a=None, prev_b=None, t_new=1):
    n = x2d.shape[0]
    tm = TILE_PROMPT if prompt else TILE_SAMPLE
    nt = n // tm
    cos_t, sup_t, sdn_t = rope_tabs
    full = lambda a: pl.BlockSpec(a.shape, lambda i: (0,) * a.ndim)
    row_tile = lambda w: pl.BlockSpec((tm, w), lambda i: (i, 0))
    if prompt:
        tps = seq_len // tm
        mod_spec = pl.BlockSpec((None, 1, D_MODEL), lambda i: (i // tps, 0, 0))
        tab_spec = pl.BlockSpec((tm, LANES), lambda i: (i % tps, 0))
        nb = n // seq_len
        ins = [x2d, sc, sh, p["g1"], p["w_in"], p["g_q"], p["wq"], p["g_kv"], p["wuk"], p["wuv"],
               cos_t, sup_t, sdn_t, p["w_conv"], p["g_conv"], p["gmat"]]
        in_specs = [row_tile(D_MODEL), mod_spec, mod_spec, full(p["g1"]), full(p["w_in"]), full(p["g_q"]),
                    full(p["wq"]), full(p["g_kv"]), full(p["wuk"]), full(p["wuv"]),
                    tab_spec, tab_spec, tab_spec, full(p["w_conv"]), full(p["g_conv"]), full(p["gmat"])]
        out_shape = [jax.ShapeDtypeStruct((n, QK_WIDTH), BF16), jax.ShapeDtypeStruct((n, QK_WIDTH), BF16),
                     jax.ShapeDtypeStruct((n, QK_WIDTH), BF16), jax.ShapeDtypeStruct((n, KV_LORA), F32),
                     jax.ShapeDtypeStruct((nb, ROPE_DIM, seq_len), F32), jax.ShapeDtypeStruct((n, CONV_DIM), BF16),
                     jax.ShapeDtypeStruct((nb, CONV_W - 1, CONV_DIM), F32)]
        out_specs = [row_tile(QK_WIDTH), row_tile(QK_WIDTH), row_tile(QK_WIDTH), row_tile(KV_LORA),
                     pl.BlockSpec((None, ROPE_DIM, tm), lambda i: (i // tps, 0, i % tps)), row_tile(CONV_DIM),
                     pl.BlockSpec((None, CONV_W - 1, CONV_DIM), lambda i: (i // tps, 0, 0))]
        scratch = [pltpu.VMEM((8, CONV_DIM), F32)]
    else:
        tps = 1
        ins = [x2d, sc, sh, p["g1"], p["w_in"], p["g_q"], p["wq"], p["g_kv"],
               cos_t, sup_t, sdn_t, p["w_conv"], p["g_conv"], p["gmat"], prev_a, prev_b]
        in_specs = [row_tile(D_MODEL), row_tile(D_MODEL), row_tile(D_MODEL), full(p["g1"]), full(p["w_in"]),
                    full(p["g_q"]), full(p["wq"]), full(p["g_kv"]),
                    row_tile(LANES), row_tile(LANES), row_tile(LANES), full(p["w_conv"]), full(p["g_conv"]),
                    full(p["gmat"]), row_tile(CONV_DIM), row_tile(CONV_DIM)]
        out_shape = [jax.ShapeDtypeStruct((n, QK_WIDTH), BF16), jax.ShapeDtypeStruct((n, KV_LORA), F32),
                     jax.ShapeDtypeStruct((n, ROPE_DIM), F32), jax.ShapeDtypeStruct((n, CONV_DIM), BF16),
                     jax.ShapeDtypeStruct((n, CONV_DIM), F32)]
        out_specs = [row_tile(QK_WIDTH), row_tile(KV_LORA), row_tile(ROPE_DIM), row_tile(CONV_DIM),
                     row_tile(CONV_DIM)]
        scratch = []
    return pl.pallas_call(
        functools.partial(_mix_in_body, prompt=prompt, tm=tm, tiles_per_seq=tps, t_new=t_new),
        grid=(nt,), in_specs=in_specs, out_specs=out_specs, out_shape=out_shape,
        scratch_shapes=scratch, compiler_params=_cparams(("arbitrary",)),
        name="mix_in_prompt" if prompt else "mix_in_sample",
    )(*ins)


def _flash_body(q_ref, k_ref, v_ref, g_ref, gm_ref, o_ref, m_sc, acc_sc, s_sc, p_sc, *, tq, tk):
    assert tq % tk == 0
    i = pl.program_id(2)
    m_sc[...] = jnp.full(m_sc.shape, -jnp.inf, F32)
    acc_sc[...] = jnp.zeros(acc_sc.shape, F32)
    row_g = i * tq + lax.broadcasted_iota(jnp.int32, (STRIP, tk), 0)
    col_l = lax.broadcasted_iota(jnp.int32, (STRIP, tk), 1)
    heads = [slice(hh * HEAD_PAD, (hh + 1) * HEAD_PAD) for hh in range(2)]

    def logits(t, slot, row_lo=0):
        ks = pl.multiple_of(t * tk, tk)
        for hh in range(2):
            s_sc[slot, hh, row_lo:, :] = _dot_nt(q_ref[row_lo:, heads[hh]],
                                                 k_ref[pl.ds(ks, tk), heads[hh]])

    def consume(t, slot, masked, row_lo=0):
        ks = pl.multiple_of(t * tk, tk)
        for hh in range(2):
            for r0 in range(row_lo, tq, STRIP):
                rows = slice(r0, r0 + STRIP)
                s = s_sc[slot, hh, rows, :]
                if masked and r0 < row_lo + tk:
                    s = jnp.where(ks + col_l <= row_g + r0, s, -jnp.inf)
                m_prev = m_sc[hh, rows, :]
                m_new = jnp.maximum(m_prev, jnp.max(s, axis=1, keepdims=True))
                alpha = jnp.exp2(m_prev - m_new)
                pr = jnp.exp2(s - jnp.concatenate([m_new] * (tk // LANES), axis=1))
                p_sc[hh, rows, :] = pr.astype(BF16)
                acc_sc[hh, rows, :] = alpha * acc_sc[hh, rows, :]
                m_sc[hh, rows, :] = m_new
            acc_sc[hh, row_lo:, :] = acc_sc[hh, row_lo:, :] + _dot(p_sc[hh, row_lo:, :],
                                                                   v_ref[pl.ds(ks, tk), heads[hh]])

    n_diag = tq // tk
    n_full = i * n_diag
    logits(0, 0)
    if n_diag % 2 == 0:
        def tile_pair(p, _):
            logits(2 * p + 1, 1)
            consume(2 * p, 0, False)
            logits(2 * p + 2, 0)
            consume(2 * p + 1, 1, False)
            return 0
        lax.fori_loop(0, n_full // 2, tile_pair, 0)
    else:
        def tile(t, _):
            logits(t + 1, (t + 1) % 2)
            consume(t, t % 2, False)
            return 0
        lax.fori_loop(0, n_full, tile, 0)
    for jj in range(n_diag):
        t = n_full + jj
        slot = jj % 2 if n_diag % 2 == 0 else t % 2
        if jj + 1 < n_diag:
            logits(t + 1, 1 - slot, row_lo=(jj + 1) * tk)
        consume(t, slot, True, row_lo=jj * tk)

    outs = []
    for hh in range(2):
        acc = acc_sc[hh]
        outs.append(acc * pltpu.roll(1.0 / acc, V_DIM, 1))
    lane = lax.broadcasted_iota(jnp.int32, (tq, LANES), 1)
    o = jnp.where(lane < V_DIM, outs[0], pltpu.roll(outs[1], V_DIM, 1))
    ms = _dot((o * o).astype(BF16), gm_ref[...]) * (1.0 / V_DIM)
    o_ref[...] = (o * lax.rsqrt(ms + EPS) * g_ref[...]).astype(BF16)


def _flash(q, k, v, g_attn, gm_head, batch, seq_len):
    n = q.shape[0]
    nq = seq_len // TQ
    pair = pl.BlockSpec((seq_len, 2 * HEAD_PAD), lambda b, p, i: (b, p))
    return pl.pallas_call(
        functools.partial(_flash_body, tq=TQ, tk=TK),
        grid=(batch, N_HEADS // 2, nq),
        in_specs=[pl.BlockSpec((TQ, 2 * HEAD_PAD), lambda b, p, i: (b * nq + i, p)),
                  pair, pair,
                  pl.BlockSpec((1, 2 * V_DIM), lambda b, p, i: (0, p)),
                  pl.BlockSpec((LANES, LANES), lambda b, p, i: (0, 0))],
        out_specs=pl.BlockSpec((TQ, 2 * V_DIM), lambda b, p, i: (b * nq + i, p)),
        out_shape=jax.ShapeDtypeStruct((n, ATTN_WIDTH), BF16),
        scratch_shapes=[pltpu.VMEM((2, TQ, LANES), F32), pltpu.VMEM((2, TQ, LANES), F32),
                        pltpu.VMEM((2, 2, TQ, TK), F32), pltpu.VMEM((2, TQ, TK), BF16)],
        compiler_params=_cparams(("arbitrary", "arbitrary", "arbitrary")),
        name="flash_prompt",
    )(q, k, v, g_attn, gm_head)


def _absorb_body(q_ref, w_ref, ql_ref, qp_ref):
    qb = q_ref[...]
    ql_ref[...] = _dot(qb, w_ref[...]).astype(BF16)
    qp_ref[...] = qb[:, PE_LO:PE_LO + ROPE_DIM]


def _absorb(q_s, wuk_t):
    n = q_s.shape[0]
    return pl.pallas_call(
        _absorb_body,
        grid=(N_HEADS,),
        in_specs=[pl.BlockSpec((n, HEAD_PAD), lambda h: (0, h)),
                  pl.BlockSpec((None, HEAD_PAD, KV_LORA), lambda h: (h, 0, 0))],
        out_specs=[pl.BlockSpec((None, n, KV_LORA), lambda h: (h, 0, 0)),
                   pl.BlockSpec((None, n, ROPE_DIM), lambda h: (h, 0, 0))],
        out_shape=[jax.ShapeDtypeStruct((N_HEADS, n, KV_LORA), BF16),
                   jax.ShapeDtypeStruct((N_HEADS, n, ROPE_DIM), BF16)],
        compiler_params=_cparams(("arbitrary",)),
        name="absorb",
    )(q_s, wuk_t)


PAGES_PER_CHUNK = 8


def _paged_body(pt_ref, ql_ref, qp_ref, cn_ref, kn_ref, cckv_ref, ckpe_ref, o_ref,
                ckv_buf, kpe_buf, s_all, kcb, sems, *, n_pages, n_seq, t_new):
    b = pl.program_id(0)
    slot = b % 2

    def fetch(seq, sl):
        def body(g, _):
            for u in range(PAGES_PER_CHUNK):
                pg = g * PAGES_PER_CHUNK + u
                page = pt_ref[seq, pg]
                pltpu.make_async_copy(cckv_ref.at[0, page], ckv_buf.at[sl, pg], sems.at[0, sl]).start()
                pltpu.make_async_copy(ckpe_ref.at[0, page], kpe_buf.at[sl, pg],
                                      sems.at[1, sl]).start(priority=1)
            return 0
        lax.fori_loop(0, n_pages // PAGES_PER_CHUNK, body, 0)

    @pl.when(b == 0)
    def _():
        fetch(0, 0)

    @pl.when(b + 1 < n_seq)
    def _():
        fetch(b + 1, 1 - slot)

    pltpu.make_async_copy(cckv_ref.at[0, pl.ds(0, n_pages)], ckv_buf.at[slot], sems.at[0, slot]).wait()
    pltpu.make_async_copy(ckpe_ref.at[0, pl.ds(0, n_pages)], kpe_buf.at[slot], sems.at[1, slot]).wait()

    ql = ql_ref[...]
    qp = qp_ref[...]
    rows = ql.shape[0]
    ck = PAGES_PER_CHUNK * PAGE_SIZE

    n_chunks = n_pages // PAGES_PER_CHUNK

    m = jnp.full((rows, 1), -jnp.inf, F32)
    for c in range(n_chunks):
        pages = slice(c * PAGES_PER_CHUNK, (c + 1) * PAGES_PER_CHUNK)
        kc = ckv_buf[slot, pages].reshape(ck, KV_LORA).astype(BF16)
        kcb[c] = kc
        pc = kpe_buf[slot, pages].astype(BF16)
        s_pe = jnp.concatenate([_dot(qp, pc[j]) for j in range(PAGES_PER_CHUNK)], axis=1)
        s = _dot_nt(ql, kc) + s_pe
        s_all[c] = s
        m = jnp.maximum(m, jnp.max(s, axis=1, keepdims=True))

    qlf = ql.astype(F32)
    qpf = qp.astype(F32)
    cn = cn_ref[...]
    kn = kn_ref[...]
    trow = lax.broadcasted_iota(jnp.int32, (rows, 1), 0) % t_new
    s_new = []
    for j in range(t_new):
        sj = (jnp.sum(qlf * cn[j:j + 1, :], axis=1, keepdims=True)
              + jnp.sum(qpf * kn[j:j + 1, :], axis=1, keepdims=True))
        sj = jnp.where(trow >= j, sj, -jnp.inf)
        s_new.append(sj)
        m = jnp.maximum(m, sj)

    l = jnp.zeros((rows, 1), F32)
    acc = jnp.zeros((rows, KV_LORA), F32)
    for c in range(n_chunks):
        pr = jnp.exp2(s_all[c] - m)
        l = l + jnp.sum(pr, axis=1, keepdims=True)
        acc = acc + _dot(pr.astype(BF16), kcb[c])
    for j in range(t_new):
        pj = jnp.exp2(s_new[j] - m)
        l = l + pj
        acc = acc + pj * cn[j:j + 1, :]
    o_ref[...] = acc / l


def _paged(page_table, ql, qp, ckv_new, kpe_new, cache_ckv, cache_kpe):
    n_seq, n_pages = page_table.shape
    rows = ql.shape[1]
    t_new = ckv_new.shape[1]
    grid_spec = pltpu.PrefetchScalarGridSpec(
        num_scalar_prefetch=1,
        grid=(n_seq,),
        in_specs=[pl.BlockSpec((None, rows, KV_LORA), lambda b, pt: (b, 0, 0)),
                  pl.BlockSpec((None, rows, ROPE_DIM), lambda b, pt: (b, 0, 0)),
                  pl.BlockSpec((None, t_new, KV_LORA), lambda b, pt: (b, 0, 0)),
                  pl.BlockSpec((None, t_new, ROPE_DIM), lambda b, pt: (b, 0, 0)),
                  pl.BlockSpec(memory_space=pl.ANY),
                  pl.BlockSpec(memory_space=pl.ANY)],
        out_specs=pl.BlockSpec((None, rows, KV_LORA), lambda b, pt: (b, 0, 0)),
        scratch_shapes=[pltpu.VMEM((2, n_pages, PAGE_SIZE, KV_LORA), F32),
                        pltpu.VMEM((2, n_pages, ROPE_DIM, PAGE_SIZE), F32),
                        pltpu.VMEM((n_pages // PAGES_PER_CHUNK, rows, PAGES_PER_CHUNK * PAGE_SIZE), F32),
                        pltpu.VMEM((n_pages // PAGES_PER_CHUNK, PAGES_PER_CHUNK * PAGE_SIZE, KV_LORA), BF16),
                        pltpu.SemaphoreType.DMA((2, 2))],
    )
    return pl.pallas_call(
        functools.partial(_paged_body, n_pages=n_pages, n_seq=n_seq, t_new=t_new),
        grid_spec=grid_spec,
        out_shape=jax.ShapeDtypeStruct((n_seq, rows, KV_LORA), F32),
        compiler_params=_cparams(("arbitrary",)),
        name="paged_attn",
    )(page_table, ql, qp, ckv_new, kpe_new, cache_ckv, cache_kpe)


def _unabsorb_body(o_ref, w_ref, g_ref, out_ref):
    o = _dot(o_ref[...].astype(BF16), w_ref[...])
    r = lax.rsqrt(jnp.sum(o * o, axis=-1, keepdims=True) * (1.0 / V_DIM) + EPS)
    out_ref[...] = (o * r * g_ref[...]).astype(BF16)


def _unabsorb(o_lat_h, wuv_h, g_h):
    n = o_lat_h.shape[1]
    return pl.pallas_call(
        _unabsorb_body,
        grid=(N_HEADS,),
        in_specs=[pl.BlockSpec((None, n, KV_LORA), lambda h: (h, 0, 0)),
                  pl.BlockSpec((None, KV_LORA, HEAD_PAD), lambda h: (h, 0, 0)),
                  pl.BlockSpec((None, 1, HEAD_PAD), lambda h: (h, 0, 0))],
        out_specs=pl.BlockSpec((None, n, HEAD_PAD), lambda h: (h, 0, 0)),
        out_shape=jax.ShapeDtypeStruct((N_HEADS, n, HEAD_PAD), BF16),
        compiler_params=_cparams(("arbitrary",)),
        name="unabsorb",
    )(o_lat_h, wuv_h, g_h)


def _post_mix_body(x_ref, o_ref, z_ref, gt1_ref, sc_ref, sh_ref, gt2_ref, wo_ref, g2_ref,
                   wr_ref, eb_ref, wsgu_ref, wsd_ref, tri_ref, cin_ref,
                   xs_ref, h2_ref, idx_ref, gw_ref, rank_ref, cout_ref, cnt_ref, *, tm):
    i = pl.program_id(0)

    @pl.when(i == 0)
    def _():
        cnt_ref[...] = cin_ref[...]

    wo = wo_ref[...]
    mix = _dot(o_ref[...], wo[:ATTN_WIDTH, :]) + _dot(z_ref[...], wo[ATTN_WIDTH:, :])
    x1 = x_ref[...] + gt1_ref[...] * mix
    h2 = _rms(x1, g2_ref[...]) * (1.0 + sc_ref[...]) + sh_ref[...]
    hb = h2.astype(BF16)
    h2_ref[...] = _pack_pair(hb)

    gu = _dot(hb, wsgu_ref[...])
    gate, up = gu[:, :SHARED_FF], gu[:, SHARED_FF:]
    shared = _dot((gate * jax.nn.sigmoid(gate) * up).astype(BF16), wsd_ref[...])
    xs_ref[...] = x1 + gt2_ref[...] * shared

    s = jax.nn.sigmoid(_dot_nt(wr_ref[...], hb))
    biased = s + eb_ref[...]
    ninf = -jnp.inf
    gi = lax.broadcasted_iota(jnp.int32, (GROUP_SIZE, tm), 0).astype(F32)
    gs = []
    for g in range(N_GROUPS):
        blk = biased[g * GROUP_SIZE:(g + 1) * GROUP_SIZE, :]
        m1 = jnp.max(blk, axis=0, keepdims=True)
        i1 = jnp.min(jnp.where(blk == m1, gi, float(GROUP_SIZE)), axis=0, keepdims=True)
        m2 = jnp.max(jnp.where(gi == i1, ninf, blk), axis=0, keepdims=True)
        gs.append(m1 + m2)
    gscore = jnp.concatenate(gs, axis=0)
    giota = lax.broadcasted_iota(jnp.int32, (N_GROUPS, tm), 0).astype(F32)
    gsel = jnp.zeros((N_GROUPS, tm), F32)
    for _ in range(TOPK_GROUPS):
        gm = jnp.max(gscore, axis=0, keepdims=True)
        gidx = jnp.min(jnp.where(gscore == gm, giota, float(N_GROUPS)), axis=0, keepdims=True)
        hit = giota == gidx
        gsel = jnp.where(hit, 1.0, gsel)
        gscore = jnp.where(hit, ninf, gscore)
    masked = jnp.concatenate(
        [jnp.where(gsel[g:g + 1, :] > 0.0, biased[g * GROUP_SIZE:(g + 1) * GROUP_SIZE, :], ninf)
         for g in range(N_GROUPS)], axis=0)
    eiota = lax.broadcasted_iota(jnp.int32, (N_EXPERTS, tm), 0).astype(F32)
    idxs, ws = [], []
    selall = jnp.zeros((N_EXPERTS, tm), F32)
    for _ in range(TOP_K):
        mx = jnp.max(masked, axis=0, keepdims=True)
        ei = jnp.min(jnp.where(masked == mx, eiota, float(N_EXPERTS)), axis=0, keepdims=True)
        hit = eiota == ei
        ws.append(jnp.sum(jnp.where(hit, s, 0.0), axis=0, keepdims=True))
        idxs.append(ei)
        selall = jnp.where(hit, 1.0, selall)
        masked = jnp.where(hit, ninf, masked)
    wsum = ws[0]
    for w in ws[1:]:
        wsum = wsum + w
    gw_ref[...] = jnp.concatenate(ws, axis=0) / wsum * ROUTED_SCALE
    idx_ref[...] = jnp.concatenate(idxs, axis=0).astype(jnp.int32)

    before = _dot(selall.astype(BF16), tri_ref[...]) + cnt_ref[:, 0:1]
    ranks = [jnp.sum(jnp.where(eiota == ei, before, 0.0), axis=0, keepdims=True) for ei in idxs]
    rank_ref[...] = jnp.concatenate(ranks, axis=0).astype(jnp.int32)
    cnt_ref[...] = cnt_ref[...] + jnp.sum(selall, axis=1, keepdims=True)
    cout_ref[...] = cnt_ref[...]


def _post_mix(x2d, o, z, gt1, sc2, sh2, gt2, p, cnt_in, h2_buf, *, prompt, seq_len, row_off, n_total):
    n = x2d.shape[0]
    tm = TILE_PROMPT if prompt else TILE_SAMPLE
    nt = n // tm
    full = lambda a: pl.BlockSpec(a.shape, lambda i: (0,) * a.ndim)
    row_tile = lambda w: pl.BlockSpec((tm, w), lambda i: (i, 0))
    if prompt:
        tps = seq_len // tm
        mod_spec = pl.BlockSpec((None, 1, D_MODEL), lambda i: (i // tps, 0, 0))
    else:
        mod_spec = row_tile(D_MODEL)
    tri = p["tri_p"] if prompt else p["tri_s"]
    boff = row_off // tm
    ins = [x2d, o, z, gt1, sc2, sh2, gt2, p["w_out"], p["g2"], p["wr_t"], p["e_bias"], p["ws_gu"], p["ws_d"],
           tri, cnt_in]
    in_specs = [row_tile(D_MODEL), row_tile(ATTN_WIDTH), row_tile(CONV_DIM), mod_spec, mod_spec, mod_spec,
                mod_spec, full(p["w_out"]), full(p["g2"]), full(p["wr_t"]), full(p["e_bias"]), full(p["ws_gu"]),
                full(p["ws_d"]), full(tri), full(cnt_in)]
    out_shape = [jax.ShapeDtypeStruct((n, D_MODEL), F32), jax.ShapeDtypeStruct((n_total, PACK_W), jnp.uint32),
                 jax.ShapeDtypeStruct((TOP_K, n), jnp.int32), jax.ShapeDtypeStruct((TOP_K, n), F32),
                 jax.ShapeDtypeStruct((TOP_K, n), jnp.int32), jax.ShapeDtypeStruct((N_EXPERTS, LANES), F32)]
    col_tile = pl.BlockSpec((TOP_K, tm), lambda i: (0, i))
    out_specs = [row_tile(D_MODEL), pl.BlockSpec((tm, PACK_W), lambda i: (i + boff, 0)),
                 col_tile, col_tile, col_tile, pl.BlockSpec((N_EXPERTS, LANES), lambda i: (0, 0))]
    aliases = {}
    if h2_buf is not None:
        ins.append(h2_buf)
        in_specs.append(pl.BlockSpec(memory_space=pl.ANY))
        aliases = {len(ins) - 1: 1}
        body = lambda *refs: _post_mix_body(*refs[:15], *refs[16:], tm=tm)
    else:
        body = functools.partial(_post_mix_body, tm=tm)
    return pl.pallas_call(
        body, grid=(nt,), in_specs=in_specs, out_specs=out_specs, out_shape=out_shape,
        scratch_shapes=[pltpu.VMEM((N_EXPERTS, LANES), F32)],
        input_output_aliases=aliases,
        compiler_params=_cparams(("arbitrary",)),
        name="post_mix_prompt" if prompt else "post_mix_sample",
    )(*ins)


def _dest_body(idx_ref, rank_ref, ps_ref, d_ref, *, tm):
    eiota = lax.broadcasted_iota(jnp.int32, (N_EXPERTS, tm), 0)
    ps = ps_ref[...]
    idx = idx_ref[...]
    starts = [jnp.sum(jnp.where(eiota == idx[k:k + 1, :], ps, 0.0), axis=0, keepdims=True) for k in range(TOP_K)]
    d_ref[...] = jnp.concatenate(starts, axis=0).astype(jnp.int32) + rank_ref[...]


def _dest(idx, rank, pstart):
    n = idx.shape[1]
    tm = 512
    tile = pl.BlockSpec((TOP_K, tm), lambda i: (0, i))
    return pl.pallas_call(
        functools.partial(_dest_body, tm=tm),
        grid=(n // tm,),
        in_specs=[tile, tile, pl.BlockSpec((N_EXPERTS, 1), lambda i: (0, 0))],
        out_specs=tile,
        out_shape=jax.ShapeDtypeStruct((TOP_K, n), jnp.int32),
        compiler_params=_cparams(("arbitrary",)),
        name="dest_rows",
    )(idx, rank, pstart.astype(F32).reshape(N_EXPERTS, 1))


def _experts_body(e0_ref, rs_ref, sz_ref, nv_ref, nx_ref, ws_ref, x_ref, wg_ref, wu_ref, wd_ref, y_ref,
                  xbuf, ybuf, sg, su, sd, wgub, wdb, sem, sem_x, sem_y, *, ch, gran):
    c = pl.program_id(0)
    slot = c % 2
    classes = list(range(gran, ch + 1, gran))

    def x_copy(cc, sl, rows):
        start = pl.multiple_of(rs_ref[cc], gran)
        return pltpu.make_async_copy(x_ref.at[pl.ds(start, rows)], xbuf.at[sl, pl.ds(0, rows)], sem_x.at[sl])

    def y_copy(cc, sl, rows):
        start = pl.multiple_of(rs_ref[cc], gran)
        return pltpu.make_async_copy(ybuf.at[sl, pl.ds(0, rows)], y_ref.at[pl.ds(start, rows)], sem_y.at[sl])

    def for_class(cc, fn):
        for r in classes:
            pl.when(sz_ref[cc] == r // gran)(functools.partial(fn, r))

    def copies(e, ws):
        return (pltpu.make_async_copy(wg_ref.at[e], sg.at[ws], sem.at[ws, 0]),
                pltpu.make_async_copy(wu_ref.at[e], su.at[ws], sem.at[ws, 1]),
                pltpu.make_async_copy(wd_ref.at[e], sd.at[ws], sem.at[ws, 2]))

    @pl.when(c == 0)
    def _():
        for cp in copies(e0_ref[0], 0):
            cp.start(priority=1)

        @pl.when(e0_ref[1] >= 0)
        def _():
            for cp in copies(e0_ref[1], 1):
                cp.start(priority=1)
        for_class(0, lambda r: x_copy(0, 0, r).start())

    for_class(c + 1, lambda r: x_copy(c + 1, 1 - slot, r).start())

    nxt = nx_ref[c]

    @pl.when(nxt >= -1)
    def _():
        ws = ws_ref[c]
        for cp in copies(0, ws):
            cp.wait()
        wgub[:, :E_FF] = sg[ws].astype(BF16)
        wgub[:, E_FF:] = su[ws].astype(BF16)
        wdb[...] = sd[ws].astype(BF16)

        @pl.when(nxt >= 0)
        def _():
            for cp in copies(nxt, ws):
                cp.start(priority=1)

    @pl.when(c >= 2)
    def _():
        for_class(c - 2, lambda r: y_copy(c - 2, slot, r).wait())

    nv = nv_ref[c]

    def run(rows):
        x_copy(c, slot, rows).wait()
        row = lax.broadcasted_iota(jnp.int32, (rows, 1), 0)
        w = jnp.where(row < nv, xbuf[slot, :rows, :], jnp.uint32(0))
        lo, hi = _unpack_pair(w)
        lo, hi = lo.astype(BF16), hi.astype(BF16)
        gu = _dot(lo, wgub[:PACK_W, :]) + _dot(hi, wgub[PACK_W:, :])
        gate, up = gu[:, :E_FF], gu[:, E_FF:]
        act = (gate * jax.nn.sigmoid(gate) * up).astype(BF16)
        ybuf[slot, :rows, :] = _pack_pair(_dot(act, wdb[...]).astype(BF16))
        y_copy(c, slot, rows).start()

    for_class(c, run)


def _experts(e_first, row_start, size_class, n_valid, next_e, w_slot, x_sorted, w_e_gate, w_e_up, w_e_down):
    ch, gran = ROUTE_CHUNK, ROW_GRAN
    steps = row_start.shape[0] - 1
    any_spec = pl.BlockSpec(memory_space=pl.ANY)
    grid_spec = pltpu.PrefetchScalarGridSpec(
        num_scalar_prefetch=6,
        grid=(steps,),
        in_specs=[any_spec, any_spec, any_spec, any_spec],
        out_specs=any_spec,
        scratch_shapes=[pltpu.VMEM((2, ch, PACK_W), jnp.uint32), pltpu.VMEM((2, ch, PACK_W), jnp.uint32),
                        pltpu.VMEM((2, D_MODEL, E_FF), F32), pltpu.VMEM((2, D_MODEL, E_FF), F32),
                        pltpu.VMEM((2, E_FF, D_MODEL), F32),
                        pltpu.VMEM((D_MODEL, 2 * E_FF), BF16),
                        pltpu.VMEM((E_FF, D_MODEL), BF16), pltpu.SemaphoreType.DMA((2, 3)),
                        pltpu.SemaphoreType.DMA((2,)), pltpu.SemaphoreType.DMA((2,))],
    )
    return pl.pallas_call(
        functools.partial(_experts_body, ch=ch, gran=gran),
        grid_spec=grid_spec,
        out_shape=jax.ShapeDtypeStruct(x_sorted.shape, jnp.uint32),
        compiler_params=_cparams(("arbitrary",)),
        name="experts",
    )(e_first, row_start, size_class, n_valid, next_e, w_slot, x_sorted, w_e_gate, w_e_up, w_e_down)


SC_CORES = 2
SC_SUBCORES = 16
SC_WORKERS = SC_CORES * SC_SUBCORES
SCATTER_WIN = 24
GATHER_WIN = 64
GATHER_PARTS = 2


def _sc_mesh():
    return plsc.VectorSubcoreMesh(core_axis_name="c", subcore_axis_name="s")


def _sc_worker():
    return lax.axis_index("s") * SC_CORES + lax.axis_index("c")


def _sc_scatter_rows(rows, idx, n_out):
    n, width = rows.shape
    nw, nwin_k, win = idx.shape
    n_win = nwin_k // TOP_K
    per_w = n // nw

    assert n_win % 2 == 0

    def body(rows_hbm, idx_hbm, out_hbm, idx_v, buf, sem_in, sem_out):
        wid = _sc_worker()
        base = wid * per_w
        pltpu.sync_copy(idx_hbm.at[wid], idx_v)

        def load(j, b):
            return pltpu.make_async_copy(rows_hbm.at[pl.ds(base + j * win, win)], buf.at[b], sem_in.at[b])

        def scatters(j, b):
            return [pltpu.make_async_copy(buf.at[b], out_hbm.at[idx_v.at[j * TOP_K + k]], sem_out.at[b])
                    for k in range(TOP_K)]

        load(0, 0).start()

        @pl.loop(0, n_win, step=2)
        def _(j):
            for b in range(2):
                jj = j + b
                load(jj, b).wait()
                for cp in scatters(jj, b):
                    cp.start()

                @pl.when(jj >= 1)
                def _():
                    for cp in scatters(jj - 1, 1 - b):
                        cp.wait()

                @pl.when(jj + 1 < n_win)
                def _():
                    load(jj + 1, 1 - b).start()

        for cp in scatters(n_win - 1, 1):
            cp.wait()

    return pl.kernel(
        body, out_type=jax.ShapeDtypeStruct((n_out, width), rows.dtype), mesh=_sc_mesh(),
        scratch_types=[pltpu.VMEM((nwin_k, win), jnp.int32), pltpu.VMEM((2, win, width), rows.dtype),
                       pltpu.SemaphoreType.DMA((2,)), pltpu.SemaphoreType.DMA((2,))],
        name="sc_scatter_rows",
    )(rows, idx)


def _sc_gather_rows(table, idx):
    nw, n_win, win = idx.shape
    width = table.shape[1]
    per_w = n_win * win

    assert n_win % 2 == 0

    def body(table_hbm, idx_hbm, out_hbm, idx_v, buf, sem):
        wid = _sc_worker()
        base = wid * per_w
        pltpu.sync_copy(idx_hbm.at[wid], idx_v)

        def gather(j, b):
            return pltpu.make_async_copy(table_hbm.at[idx_v.at[j]], buf.at[b], sem.at[b])

        gather(0, 0).start()

        @pl.loop(0, n_win, step=2)
        def _(j):
            for b in range(2):
                @pl.when(j + b + 1 < n_win)
                def _():
                    gather(j + b + 1, 1 - b).start()
                gather(j + b, b).wait()
                pltpu.sync_copy(buf.at[b], out_hbm.at[pl.ds(base + (j + b) * win, win)])

    return pl.kernel(
        body, out_type=jax.ShapeDtypeStruct((nw * per_w, width), table.dtype), mesh=_sc_mesh(),
        scratch_types=[pltpu.VMEM((n_win, win), jnp.int32), pltpu.VMEM((2, win, width), table.dtype),
                       pltpu.SemaphoreType.DMA((2,))],
        name="sc_gather_rows",
    )(table, idx)


def _finish_body(y_ref, xs_ref, gw_ref, gt2_ref, gf_ref, fsc_ref, fsh_ref, out_ref, *, tm):
    gw = gw_ref[...]
    f_lo = jnp.zeros((tm, PACK_W), F32)
    f_hi = jnp.zeros((tm, PACK_W), F32)
    for k in range(TOP_K):
        lo, hi = _unpack_pair(y_ref[k * tm:(k + 1) * tm, :])
        f_lo = f_lo + gw[:, k:k + 1] * lo
        f_hi = f_hi + gw[:, k:k + 1] * hi
    f = jnp.concatenate([f_lo, f_hi], axis=1)
    x2 = xs_ref[...] + gt2_ref[...] * f
    out_ref[...] = _rms(x2, gf_ref[...]) * (1.0 + fsc_ref[...]) + fsh_ref[...]


def _finish(y_part, xs, gw, gt2, g_final, fsc, fsh, *, prompt, seq_len, tile_lo, n_tiles, y_tile_off, out_buf=None):
    n = xs.shape[0]
    tm = COMBINE_TILE
    row_tile = lambda w: pl.BlockSpec((tm, w), lambda i: (i + tile_lo, 0))
    if prompt:
        tps = seq_len // tm
        mod_spec = pl.BlockSpec((None, 1, D_MODEL), lambda i: ((i + tile_lo) // tps, 0, 0))
    else:
        mod_spec = row_tile(D_MODEL)
    ins = [y_part, xs, gw, gt2, g_final, fsc, fsh]
    in_specs = [pl.BlockSpec((TOP_K * tm, PACK_W), lambda i: (i + y_tile_off, 0)),
                row_tile(D_MODEL), row_tile(TOP_K), mod_spec,
                pl.BlockSpec((1, D_MODEL), lambda i: (0, 0)), mod_spec, mod_spec]
    aliases = {}
    body = functools.partial(_finish_body, tm=tm)
    if out_buf is not None:
        ins.append(out_buf)
        in_specs.append(pl.BlockSpec(memory_space=pl.ANY))
        aliases = {len(ins) - 1: 0}
        body = lambda *refs: _finish_body(*refs[:7], *refs[8:], tm=tm)
    return pl.pallas_call(
        body,
        grid=(n_tiles,),
        in_specs=in_specs,
        out_specs=row_tile(D_MODEL),
        out_shape=jax.ShapeDtypeStruct((n, D_MODEL), F32),
        input_output_aliases=aliases,
        compiler_params=_cparams(("arbitrary",)),
        name="finish_prompt" if prompt else "finish_sample",
    )(*ins)


def _rope_tables(pos):
    pos = np.asarray(pos, np.float32)
    inv = np.float32(1.0) / (np.float32(ROPE_THETA) ** (np.arange(PE_HALF, dtype=np.float32) / np.float32(PE_HALF)))
    ang = pos[:, None] * inv[None, :]
    cos, sin = np.cos(ang), np.sin(ang)
    n = pos.shape[0]
    ones = np.ones((n, PE_LO), np.float32)
    zeros = np.zeros((n, PE_LO), np.float32)
    zh = np.zeros((n, PE_HALF), np.float32)
    tail1 = np.ones((n, LANES - PE_LO - ROPE_DIM), np.float32)
    tail0 = np.zeros((n, LANES - PE_LO - ROPE_DIM), np.float32)
    cos_t = np.concatenate([ones, cos, cos, tail1], axis=1)
    sin_up = np.concatenate([zeros, zh, sin, tail0], axis=1)
    sin_dn = np.concatenate([zeros, -sin, zh, tail0], axis=1)
    return tuple(t.astype(np.float32) for t in (cos_t, sin_up, sin_dn))


def _pad_heads(w, width):
    pad = [(0, 0)] * (w.ndim - 1) + [(0, HEAD_PAD - width)]
    w = jnp.pad(w, pad)
    return w.reshape(w.shape[:-2] + (N_HEADS * HEAD_PAD,))


def _block_diag_ones(n, blk):
    r = np.arange(n) // blk
    return jnp.asarray(r[:, None] == r[None, :], dtype=BF16)


def _prep_weights(w_in, g_attn_norm, g_q, w_q_up, g_kv, w_uk, w_uv, w_conv, g_attn_out, g_conv_out, w_out,
                  g_ffn_norm, w_router, e_bias, w_s_gate, w_s_up, w_s_down):
    o0 = Q_LORA
    o1 = o0 + KV_LORA
    o2 = o1 + ROPE_DIM
    kpe_cols = jnp.pad(w_in[:, o1:o2], ((0, 0), (PE_LO, HEAD_PAD - PE_LO - ROPE_DIM)))
    w_in_p = jnp.concatenate([w_in[:, :o1], kpe_cols, w_in[:, o2:]], axis=1).astype(BF16)
    wq = _pad_heads(w_q_up.reshape(Q_LORA, N_HEADS, NOPE_DIM + ROPE_DIM), NOPE_DIM + ROPE_DIM).astype(BF16)
    wuk = _pad_heads(w_uk, NOPE_DIM).astype(BF16)
    wuv = _pad_heads(w_uv, V_DIM).astype(BF16)
    wuk_t = jnp.pad(jnp.transpose(w_uk, (1, 2, 0)), ((0, 0), (0, HEAD_PAD - NOPE_DIM), (0, 0))).astype(BF16)
    wuv_h = jnp.pad(jnp.transpose(w_uv, (1, 0, 2)), ((0, 0), (0, 0), (0, HEAD_PAD - V_DIM))).astype(BF16)
    g_attn_h = jnp.pad(g_attn_out.reshape(N_HEADS, 1, V_DIM), ((0, 0), (0, 0), (0, HEAD_PAD - V_DIM)))
    tri = lambda t: jnp.asarray(np.arange(t)[:, None] < np.arange(t)[None, :], dtype=BF16)
    return {
        "g1": g_attn_norm.reshape(1, -1), "w_in": w_in_p, "g_q": g_q.reshape(1, -1), "wq": wq,
        "g_kv": g_kv.reshape(1, -1), "wuk": wuk, "wuv": wuv, "wuk_t": wuk_t, "wuv_h": wuv_h,
        "w_conv": w_conv, "g_conv": g_conv_out.reshape(1, -1), "gmat": _block_diag_ones(CONV_DIM, CONV_GROUP_DIM),
        "g_attn": g_attn_out.reshape(1, -1), "g_attn_h": g_attn_h, "gm_head": _block_diag_ones(LANES, V_DIM),
        "w_out": w_out.astype(BF16), "g2": g_ffn_norm.reshape(1, -1), "wr_t": w_router.T.astype(BF16),
        "e_bias": e_bias.reshape(-1, 1), "ws_gu": jnp.concatenate([w_s_gate, w_s_up], axis=1).astype(BF16),
        "ws_d": w_s_down.astype(BF16), "tri_p": tri(TILE_PROMPT), "tri_s": tri(TILE_SAMPLE),
    }


def kernel(x_prompt, x_sample, c_prompt, c_sample, cache_ckv, cache_kpe, state_conv, page_table, w_ada, b_ada, g_attn_norm, w_in, g_q, w_q_up, g_kv, w_uk, w_uv, w_conv, g_attn_out, g_conv_out, w_out, g_ffn_norm, w_router, e_bias, w_e_gate, w_e_up, w_e_down, w_s_gate, w_s_up, w_s_down, w_ada_final, b_ada_final, g_final):
    assert w_ada.shape[0] == 1, "one layer"
    bsz, seq, d = x_prompt.shape
    nseq, t_new, _ = x_sample.shape
    n_p, n_s = bsz * seq, nseq * t_new
    n_tot = n_p + n_s
    past = page_table.shape[1] * PAGE_SIZE

    p = _prep_weights(w_in[0], g_attn_norm[0], g_q[0], w_q_up[0], g_kv[0], w_uk[0], w_uv[0], w_conv[0],
                      g_attn_out[0], g_conv_out[0], w_out[0], g_ffn_norm[0], w_router[0], e_bias[0],
                      w_s_gate[0], w_s_up[0], w_s_down[0])

    c_all = jnp.concatenate([c_prompt, c_sample], axis=0)
    mod = _ada(c_all, w_ada[0], b_ada[0])
    modf = _ada(c_all, w_ada_final, b_ada_final)
    sh1, sc1, gt1, sh2, sc2, gt2 = [mod[:, j * d:(j + 1) * d] for j in range(6)]
    fsh, fsc = modf[:, :d], modf[:, d:]
    per_batch = lambda a: a[:bsz].reshape(bsz, 1, d)
    per_token = lambda a: jnp.repeat(a[bsz:], t_new, axis=0)

    xp = x_prompt.reshape(n_p, d)
    xs_in = x_sample.reshape(n_s, d)
    tabs_p = tuple(jnp.asarray(t) for t in _rope_tables(np.arange(seq)))
    tabs_s = tuple(jnp.asarray(np.tile(t, (nseq, 1))) for t in _rope_tables(past + np.arange(t_new)))

    q_p, k_p, v_p, ckv_p, kpe_p, z_p, conv_p = _mix_in(
        xp, per_batch(sc1), per_batch(sh1), p, tabs_p, prompt=True, seq_len=seq)
    o_p = _flash(q_p, k_p, v_p, p["g_attn"], p["gm_head"], bsz, seq)

    prev = state_conv[0]
    prev_a = jnp.repeat(prev[:, 1, :], t_new, axis=0)
    prev_b = jnp.repeat(prev[:, 0, :], t_new, axis=0)
    q_s, ckv_s, kpe_s, z_s, u_s = _mix_in(
        xs_in, per_token(sc1), per_token(sh1), p, tabs_s, prompt=False, prev_a=prev_a, prev_b=prev_b,
        t_new=t_new)
    ql_h, qp_h = _absorb(q_s, p["wuk_t"])
    rows = N_HEADS * t_new
    to_seq = lambda a: jnp.transpose(a.reshape(N_HEADS, nseq, t_new, a.shape[-1]), (1, 0, 2, 3)).reshape(
        nseq, rows, a.shape[-1])
    o_lat = _paged(page_table, to_seq(ql_h), to_seq(qp_h), ckv_s.reshape(nseq, t_new, KV_LORA),
                   kpe_s.reshape(nseq, t_new, ROPE_DIM), cache_ckv, jnp.swapaxes(cache_kpe, 2, 3))
    o_lat_h = jnp.transpose(o_lat.reshape(nseq, N_HEADS, t_new, KV_LORA), (1, 0, 2, 3)).reshape(
        N_HEADS, n_s, KV_LORA)
    o_s_h = _unabsorb(o_lat_h, p["wuv_h"], p["g_attn_h"])
    o_s = jnp.transpose(o_s_h[:, :, :V_DIM], (1, 0, 2)).reshape(n_s, ATTN_WIDTH)

    cnt0 = jnp.zeros((N_EXPERTS, LANES), F32)
    xs_p, h2, idx_p, gw_p, rank_p, cnt_p = _post_mix(
        xp, o_p, z_p, per_batch(gt1), per_batch(sc2), per_batch(sh2), per_batch(gt2), p, cnt0, None,
        prompt=True, seq_len=seq, row_off=0, n_total=n_tot)
    xs_s, h2, idx_s, gw_s, rank_s, cnt_s = _post_mix(
        xs_in, o_s, z_s, per_token(gt1), per_token(sc2), per_token(sh2), per_token(gt2), p, cnt_p, h2,
        prompt=False, seq_len=None, row_off=n_p, n_total=n_tot)

    ch, gran = ROUTE_CHUNK, ROW_GRAN
    counts = cnt_s[:, 0].astype(jnp.int32)
    padded = (counts + gran - 1) // gran * gran
    pend = jnp.cumsum(padded)
    pstart = pend - padded
    idx_all = jnp.concatenate([idx_p, idx_s], axis=1)
    rank_all = jnp.concatenate([rank_p, rank_s], axis=1)
    eids = jnp.arange(N_EXPERTS, dtype=jnp.int32)
    lookup = lambda table, keys: jnp.sum(jnp.where(keys[..., None] == eids, table, 0), axis=-1)
    dest = _dest(idx_all, rank_all, pstart)
    n_rows = n_tot * TOP_K + N_EXPERTS * gran
    nck = (padded + ch - 1) // ch
    cend = jnp.cumsum(nck)
    cstart = cend - nck
    n_used = cend[-1]
    n_meta = -(-(n_tot * TOP_K) // ch) + N_EXPERTS + 3
    cidx = jnp.arange(n_meta, dtype=jnp.int32)
    chunk_e = jnp.minimum(jnp.sum((cend[None, :] <= cidx[:, None]).astype(jnp.int32), axis=1), N_EXPERTS - 1)
    used = cidx < n_used
    off = (cidx - lookup(cstart, chunk_e)) * ch
    row_start = jnp.where(used, lookup(pstart, chunk_e) + off, 0).astype(jnp.int32)
    size_class = jnp.where(used, jnp.minimum(lookup(padded, chunk_e) - off, ch) // gran, 0).astype(jnp.int32)
    n_valid = jnp.where(used, jnp.clip(lookup(counts, chunk_e) - off, 0, ch), 0).astype(jnp.int32)
    has_rows = nck > 0
    run_of = jnp.cumsum(has_rows.astype(jnp.int32)) - 1
    n_runs = run_of[-1] + 1
    expert_of_run = lambda r: jnp.sum(jnp.where(has_rows[None, :] & (run_of[None, :] == r[:, None]), eids, 0), axis=1)
    chunk_run = lookup(run_of, chunk_e)
    is_first = used & (off == 0)
    ahead = chunk_run + 2
    next_e = jnp.where(is_first, jnp.where(ahead < n_runs, expert_of_run(ahead), -1), -2).astype(jnp.int32)
    w_slot = (chunk_run % 2).astype(jnp.int32)
    first_two = jnp.arange(2, dtype=jnp.int32)
    e_first = jnp.where(first_two < n_runs, expert_of_run(first_two), -1).astype(jnp.int32)

    tmc = COMBINE_TILE

    def dest_tiles(dst):
        n = dst.shape[1]
        return jnp.transpose(dst.reshape(TOP_K, n // tmc, tmc), (1, 0, 2)).reshape(n // tmc, 1, TOP_K * tmc)

    per_w = n_tot // SC_WORKERS
    idx_sc = jnp.transpose(dest.reshape(TOP_K, SC_WORKERS, per_w // SCATTER_WIN, SCATTER_WIN),
                           (1, 2, 0, 3)).reshape(SC_WORKERS, (per_w // SCATTER_WIN) * TOP_K, SCATTER_WIN)
    x_sorted = _sc_scatter_rows(h2, idx_sc, n_rows)
    y_sorted = _experts(e_first, row_start, size_class, n_valid, next_e, w_slot, x_sorted,
                        w_e_gate[0], w_e_up[0], w_e_down[0])

    idx_tiles = dest_tiles(dest)
    tiles_p = n_p // tmc
    part = tiles_p // GATHER_PARTS
    sc_idx = lambda a: a.reshape(SC_WORKERS, -1, GATHER_WIN)
    bounds = [(j * part, (j + 1) * part if j + 1 < GATHER_PARTS else idx_tiles.shape[0])
              for j in range(GATHER_PARTS)]
    y_parts = [_sc_gather_rows(y_sorted, sc_idx(idx_tiles[lo:hi])) for lo, hi in bounds]

    gfin = g_final.reshape(1, d)
    fin_p = functools.partial(_finish, xs=xs_p, gw=gw_p.T, gt2=per_batch(gt2), g_final=gfin, fsc=per_batch(fsc),
                              fsh=per_batch(fsh), prompt=True, seq_len=seq)
    y_p = None
    for j, y_part in enumerate(y_parts):
        y_p = fin_p(y_part, tile_lo=j * part, n_tiles=part, y_tile_off=0, out_buf=y_p)
    y_s = _finish(y_parts[-1], xs_s, gw_s.T, per_token(gt2), gfin, per_token(fsc), per_token(fsh), prompt=False,
                  seq_len=None, tile_lo=0, n_tiles=n_s // tmc, y_tile_off=part)

    return (y_p.reshape(bsz, seq, d), y_s.reshape(nseq, t_new, d),
            ckv_p.reshape(1, bsz, seq, KV_LORA), jnp.swapaxes(kpe_p, 1, 2)[None],
            conv_p.reshape(1, bsz, CONV_W - 1, CONV_DIM),
            ckv_s.reshape(1, nseq, t_new, KV_LORA), kpe_s.reshape(1, nseq, t_new, ROPE_DIM),
            u_s.reshape(nseq, t_new, CONV_DIM)[:, t_new - (CONV_W - 1):, :].reshape(1, nseq, CONV_W - 1, CONV_DIM))
```
